```python
import math
import jax, jax.numpy as jnp
from jax import lax
import numpy as np

D_MODEL = 2048
BATCH = 8
SEQ = 8192
DEPTH = 1

MIX_WIDTH = D_MODEL
SSM_WIDTH = MIX_WIDTH // 2
SSM_GROUP = 16
SSM_GROUPS = SSM_WIDTH // SSM_GROUP
SSM_STATE = 64
SGU_WIDTH = MIX_WIDTH - SSM_WIDTH
SGU_HEADS = 8
SGU_HEAD_DIM = SGU_WIDTH // SGU_HEADS
SGU_CHUNK = 128
IN_WIDTH = SSM_WIDTH + 2 * SGU_WIDTH
D_FF = 4 * D_MODEL
EPS = 1e-6
DT_MIN = 1e-3
DT_MAX = 1e-1

kernel_name = "hymba_style_s5_sgu_hybrid_layer"


def rmsnorm(x, g):
    xf = x.astype(jnp.float32)
    xf = xf * lax.rsqrt(jnp.mean(xf * xf, axis=-1, keepdims=True) + EPS)
    return (xf * g.astype(jnp.float32)).astype(x.dtype)


def layernorm(x, g, b):
    xf = x.astype(jnp.float32)
    mu = jnp.mean(xf, axis=-1, keepdims=True)
    var = jnp.mean(jnp.square(xf - mu), axis=-1, keepdims=True)
    y = (xf - mu) * lax.rsqrt(var + EPS) * g.astype(jnp.float32) + b.astype(jnp.float32)
    return y.astype(x.dtype)


def _scan_combine(left, right):
    a_l, b_l = left
    a_r, b_r = right
    return a_l * a_r, a_r * b_l + b_r


def s5_mixer(u, a_re, a_im, b_re, b_im, c_re, c_im, d, log_dt, glu_w, glu_b):
    dtype = u.dtype
    bsz, seq, _ = u.shape
    uf = u.astype(jnp.float32).reshape(bsz, seq, SSM_GROUPS, SSM_GROUP)
    lam = lax.complex(a_re.astype(jnp.float32), a_im.astype(jnp.float32))
    dt = jnp.exp(log_dt.astype(jnp.float32))[:, None]
    a_bar = jnp.exp(lam * dt)
    b_mat = lax.complex(b_re.astype(jnp.float32), b_im.astype(jnp.float32))
    b_bar = ((a_bar - 1.0) / lam)[..., None] * b_mat
    bu = jnp.einsum('bsgh,gph->bsgp', uf.astype(jnp.complex64), b_bar)
    a_seq = jnp.broadcast_to(a_bar, bu.shape)
    _, states = lax.associative_scan(_scan_combine, (a_seq, bu), axis=1)
    c_mat = lax.complex(c_re.astype(jnp.float32), c_im.astype(jnp.float32))
    y = jnp.einsum('bsgp,ghp->bsgh', states, c_mat).real + d.astype(jnp.float32) * uf
    y = jax.nn.gelu(y.reshape(bsz, seq, SSM_WIDTH))
    gate = jax.nn.sigmoid(y @ glu_w.astype(jnp.float32) + glu_b.astype(jnp.float32))
    return (y * gate).astype(dtype)


def sgu_mixer(z, ln_g, ln_b, w_s, b_s):
    bsz, seq, _ = z.shape
    u, v = jnp.split(jax.nn.gelu(z), 2, axis=-1)
    v = layernorm(v, ln_g, ln_b)
    v = v.reshape(bsz, seq // SGU_CHUNK, SGU_CHUNK, SGU_HEADS, SGU_HEAD_DIM)
    causal = jnp.tril(jnp.ones((SGU_CHUNK, SGU_CHUNK), dtype=bool))
    w = jnp.where(causal[None], w_s, jnp.zeros_like(w_s))
    mixed = jnp.einsum('hts,bcshd->bcthd', w, v) + b_s.T[:, :, None]
    return u * mixed.reshape(bsz, seq, SGU_WIDTH)


def _fwd_setup_inputs(seed: int = 0) -> dict:
    key = jax.random.key(seed)
    ks = jax.random.split(key, 24)
    L = DEPTH
    G, P, H = SSM_GROUPS, SSM_STATE, SSM_GROUP
    f32 = jnp.float32
    nrm = lambda k, shape: jax.random.normal(k, shape, f32)
    x = nrm(ks[0], (BATCH, SEQ, D_MODEL))
    norm_mix_g = 1.0 + 0.02 * nrm(ks[1], (L, D_MODEL))
    w_in = nrm(ks[2], (L, D_MODEL, IN_WIDTH)) * D_MODEL ** -0.5
    n = jnp.arange(P, dtype=f32)
    ssm_a_re = -0.5 + 0.01 * nrm(ks[3], (L, G, P))
    ssm_a_im = math.pi * n + 0.01 * nrm(ks[4], (L, G, P))
    ssm_b_re = nrm(ks[5], (L, G, P, H)) * (2.0 * H) ** -0.5
    ssm_b_im = nrm(ks[6], (L, G, P, H)) * (2.0 * H) ** -0.5
    ssm_c_re = nrm(ks[7], (L, G, H, P)) * (2.0 * P) ** -0.5
    ssm_c_im = nrm(ks[8], (L, G, H, P)) * (2.0 * P) ** -0.5
    ssm_d = nrm(ks[9], (L, G, H))
    ssm_log_dt = jax.random.uniform(ks[10], (L, G), f32, math.log(DT_MIN), math.log(DT_MAX))
    ssm_glu_w = nrm(ks[11], (L, SSM_WIDTH, SSM_WIDTH)) * SSM_WIDTH ** -0.5
    ssm_glu_b = 0.01 * nrm(ks[12], (L, SSM_WIDTH))
    sgu_ln_g = 1.0 + 0.02 * nrm(ks[13], (L, SGU_WIDTH))
    sgu_ln_b = 0.01 * nrm(ks[14], (L, SGU_WIDTH))
    sgu_w = nrm(ks[15], (L, SGU_HEADS, SGU_CHUNK, SGU_CHUNK)) * 0.5 * SGU_CHUNK ** -0.5
    sgu_b = 1.0 + 0.01 * nrm(ks[16], (L, SGU_HEADS, SGU_CHUNK))
    out_norm_ssm_g = 1.0 + 0.02 * nrm(ks[17], (L, SSM_WIDTH))
    out_norm_sgu_g = 1.0 + 0.02 * nrm(ks[18], (L, SGU_WIDTH))
    w_out = nrm(ks[19], (L, MIX_WIDTH, D_MODEL)) * MIX_WIDTH ** -0.5
    norm_mlp_g = 1.0 + 0.02 * nrm(ks[20], (L, D_MODEL))
    w_up = nrm(ks[21], (L, D_MODEL, D_FF)) * D_MODEL ** -0.5
    w_down = nrm(ks[22], (L, D_FF, D_MODEL)) * D_FF ** -0.5
    norm_final_g = 1.0 + 0.02 * nrm(ks[23], (D_MODEL,))
    return {"x": x, "norm_mix_g": norm_mix_g, "w_in": w_in,
            "ssm_a_re": ssm_a_re, "ssm_a_im": ssm_a_im,
            "ssm_b_re": ssm_b_re, "ssm_b_im": ssm_b_im,
            "ssm_c_re": ssm_c_re, "ssm_c_im": ssm_c_im,
            "ssm_d": ssm_d, "ssm_log_dt": ssm_log_dt,
            "ssm_glu_w": ssm_glu_w, "ssm_glu_b": ssm_glu_b,
            "sgu_ln_g": sgu_ln_g, "sgu_ln_b": sgu_ln_b,
            "sgu_w": sgu_w, "sgu_b": sgu_b,
            "out_norm_ssm_g": out_norm_ssm_g, "out_norm_sgu_g": out_norm_sgu_g,
            "w_out": w_out, "norm_mlp_g": norm_mlp_g,
            "w_up": w_up, "w_down": w_down, "norm_final_g": norm_final_g}


def _fwd_reference(x, norm_mix_g, w_in, ssm_a_re, ssm_a_im, ssm_b_re, ssm_b_im,
              ssm_c_re, ssm_c_im, ssm_d, ssm_log_dt, ssm_glu_w, ssm_glu_b,
              sgu_ln_g, sgu_ln_b, sgu_w, sgu_b, out_norm_ssm_g, out_norm_sgu_g,
              w_out, norm_mlp_g, w_up, w_down, norm_final_g):
    for l in range(DEPTH):
        h = rmsnorm(x, norm_mix_g[l])
        z = h @ w_in[l]
        z_ssm = z[..., :SSM_WIDTH]
        z_sgu = z[..., SSM_WIDTH:]
        y_ssm = s5_mixer(z_ssm, ssm_a_re[l], ssm_a_im[l], ssm_b_re[l], ssm_b_im[l],
                         ssm_c_re[l], ssm_c_im[l], ssm_d[l], ssm_log_dt[l],
                         ssm_glu_w[l], ssm_glu_b[l])
        y_sgu = sgu_mixer(z_sgu, sgu_ln_g[l], sgu_ln_b[l], sgu_w[l], sgu_b[l])
        mixed = jnp.concatenate([rmsnorm(y_ssm, out_norm_ssm_g[l]),
                                 rmsnorm(y_sgu, out_norm_sgu_g[l])], axis=-1)
        x = x + mixed @ w_out[l]
        h = rmsnorm(x, norm_mlp_g[l])
        x = x + jnp.square(jax.nn.relu(h @ w_up[l])) @ w_down[l]
    return rmsnorm(x, norm_final_g)


import jax as _jax
import jax.numpy as _jnp

TWIN_FORMAT = 'train_step'
FWD_PARAMS = ['x', 'norm_mix_g', 'w_in', 'ssm_a_re', 'ssm_a_im', 'ssm_b_re', 'ssm_b_im', 'ssm_c_re', 'ssm_c_im', 'ssm_d', 'ssm_log_dt', 'ssm_glu_w', 'ssm_glu_b', 'sgu_ln_g', 'sgu_ln_b', 'sgu_w', 'sgu_b', 'out_norm_ssm_g', 'out_norm_sgu_g', 'w_out', 'norm_mlp_g', 'w_up', 'w_down', 'norm_final_g']
TWIN_WEIGHTS = ['norm_mix_g', 'w_in', 'ssm_a_re', 'ssm_a_im', 'ssm_b_re', 'ssm_b_im', 'ssm_c_re', 'ssm_c_im', 'ssm_d', 'ssm_log_dt', 'ssm_glu_w', 'ssm_glu_b', 'sgu_ln_g', 'sgu_ln_b', 'sgu_w', 'sgu_b', 'out_norm_ssm_g', 'out_norm_sgu_g', 'w_out', 'norm_mlp_g', 'w_up', 'w_down', 'norm_final_g']
TWIN_DIFF_INPUT = 'x'
TWIN_INPUTS = ['x', 'norm_mix_g', 'w_in', 'ssm_a_re', 'ssm_a_im', 'ssm_b_re', 'ssm_b_im', 'ssm_c_re', 'ssm_c_im', 'ssm_d', 'ssm_log_dt', 'ssm_glu_w', 'ssm_glu_b', 'sgu_ln_g', 'sgu_ln_b', 'sgu_w', 'sgu_b', 'out_norm_ssm_g', 'out_norm_sgu_g', 'w_out', 'norm_mlp_g', 'w_up', 'w_down', 'norm_final_g', 'loss_target', 'm_norm_mix_g', 'm_w_in', 'm_ssm_a_re', 'm_ssm_a_im', 'm_ssm_b_re', 'm_ssm_b_im', 'm_ssm_c_re', 'm_ssm_c_im', 'm_ssm_d', 'm_ssm_log_dt', 'm_ssm_glu_w', 'm_ssm_glu_b', 'm_sgu_ln_g', 'm_sgu_ln_b', 'm_sgu_w', 'm_sgu_b', 'm_out_norm_ssm_g', 'm_out_norm_sgu_g', 'm_w_out', 'm_norm_mlp_g', 'm_w_up', 'm_w_down', 'm_norm_final_g', 'v_norm_mix_g', 'v_w_in', 'v_ssm_a_re', 'v_ssm_a_im', 'v_ssm_b_re', 'v_ssm_b_im', 'v_ssm_c_re', 'v_ssm_c_im', 'v_ssm_d', 'v_ssm_log_dt', 'v_ssm_glu_w', 'v_ssm_glu_b', 'v_sgu_ln_g', 'v_sgu_ln_b', 'v_sgu_w', 'v_sgu_b', 'v_out_norm_ssm_g', 'v_out_norm_sgu_g', 'v_w_out', 'v_norm_mlp_g', 'v_w_up', 'v_w_down', 'v_norm_final_g']
TWIN_OUTPUTS = ['loss', 'grad_x', 'grad_norm_mix_g', 'grad_w_in', 'grad_ssm_a_re', 'grad_ssm_a_im', 'grad_ssm_b_re', 'grad_ssm_b_im', 'grad_ssm_c_re', 'grad_ssm_c_im', 'grad_ssm_d', 'grad_ssm_log_dt', 'grad_ssm_glu_w', 'grad_ssm_glu_b', 'grad_sgu_ln_g', 'grad_sgu_ln_b', 'grad_sgu_w', 'grad_sgu_b', 'grad_out_norm_ssm_g', 'grad_out_norm_sgu_g', 'grad_w_out', 'grad_norm_mlp_g', 'grad_w_up', 'grad_w_down', 'grad_norm_final_g', 'delta_norm_mix_g', 'delta_w_in', 'delta_ssm_a_re', 'delta_ssm_a_im', 'delta_ssm_b_re', 'delta_ssm_b_im', 'delta_ssm_c_re', 'delta_ssm_c_im', 'delta_ssm_d', 'delta_ssm_log_dt', 'delta_ssm_glu_w', 'delta_ssm_glu_b', 'delta_sgu_ln_g', 'delta_sgu_ln_b', 'delta_sgu_w', 'delta_sgu_b', 'delta_out_norm_ssm_g', 'delta_out_norm_sgu_g', 'delta_w_out', 'delta_norm_mlp_g', 'delta_w_up', 'delta_w_down', 'delta_norm_final_g', 'new_m_norm_mix_g', 'new_m_w_in', 'new_m_ssm_a_re', 'new_m_ssm_a_im', 'new_m_ssm_b_re', 'new_m_ssm_b_im', 'new_m_ssm_c_re', 'new_m_ssm_c_im', 'new_m_ssm_d', 'new_m_ssm_log_dt', 'new_m_ssm_glu_w', 'new_m_ssm_glu_b', 'new_m_sgu_ln_g', 'new_m_sgu_ln_b', 'new_m_sgu_w', 'new_m_sgu_b', 'new_m_out_norm_ssm_g', 'new_m_out_norm_sgu_g', 'new_m_w_out', 'new_m_norm_mlp_g', 'new_m_w_up', 'new_m_w_down', 'new_m_norm_final_g', 'new_v_norm_mix_g', 'new_v_w_in', 'new_v_ssm_a_re', 'new_v_ssm_a_im', 'new_v_ssm_b_re', 'new_v_ssm_b_im', 'new_v_ssm_c_re', 'new_v_ssm_c_im', 'new_v_ssm_d', 'new_v_ssm_log_dt', 'new_v_ssm_glu_w', 'new_v_ssm_glu_b', 'new_v_sgu_ln_g', 'new_v_sgu_ln_b', 'new_v_sgu_w', 'new_v_sgu_b', 'new_v_out_norm_ssm_g', 'new_v_out_norm_sgu_g', 'new_v_w_out', 'new_v_norm_mlp_g', 'new_v_w_up', 'new_v_w_down', 'new_v_norm_final_g']
TWIN_LEAF_KINDS = {'loss': 'loss', 'grad_x': 'grad_x', 'grad_norm_mix_g': 'grad_w', 'grad_w_in': 'grad_w', 'grad_ssm_a_re': 'grad_w', 'grad_ssm_a_im': 'grad_w', 'grad_ssm_b_re': 'grad_w', 'grad_ssm_b_im': 'grad_w', 'grad_ssm_c_re': 'grad_w', 'grad_ssm_c_im': 'grad_w', 'grad_ssm_d': 'grad_w', 'grad_ssm_log_dt': 'grad_w', 'grad_ssm_glu_w': 'grad_w', 'grad_ssm_glu_b': 'grad_w', 'grad_sgu_ln_g': 'grad_w', 'grad_sgu_ln_b': 'grad_w', 'grad_sgu_w': 'grad_w', 'grad_sgu_b': 'grad_w', 'grad_out_norm_ssm_g': 'grad_w', 'grad_out_norm_sgu_g': 'grad_w', 'grad_w_out': 'grad_w', 'grad_norm_mlp_g': 'grad_w', 'grad_w_up': 'grad_w', 'grad_w_down': 'grad_w', 'grad_norm_final_g': 'grad_w', 'delta_norm_mix_g': 'delta_w', 'delta_w_in': 'delta_w', 'delta_ssm_a_re': 'delta_w', 'delta_ssm_a_im': 'delta_w', 'delta_ssm_b_re': 'delta_w', 'delta_ssm_b_im': 'delta_w', 'delta_ssm_c_re': 'delta_w', 'delta_ssm_c_im': 'delta_w', 'delta_ssm_d': 'delta_w', 'delta_ssm_log_dt': 'delta_w', 'delta_ssm_glu_w': 'delta_w', 'delta_ssm_glu_b': 'delta_w', 'delta_sgu_ln_g': 'delta_w', 'delta_sgu_ln_b': 'delta_w', 'delta_sgu_w': 'delta_w', 'delta_sgu_b': 'delta_w', 'delta_out_norm_ssm_g': 'delta_w', 'delta_out_norm_sgu_g': 'delta_w', 'delta_w_out': 'delta_w', 'delta_norm_mlp_g': 'delta_w', 'delta_w_up': 'delta_w', 'delta_w_down': 'delta_w', 'delta_norm_final_g': 'delta_w', 'new_m_norm_mix_g': 'new_m', 'new_m_w_in': 'new_m', 'new_m_ssm_a_re': 'new_m', 'new_m_ssm_a_im': 'new_m', 'new_m_ssm_b_re': 'new_m', 'new_m_ssm_b_im': 'new_m', 'new_m_ssm_c_re': 'new_m', 'new_m_ssm_c_im': 'new_m', 'new_m_ssm_d': 'new_m', 'new_m_ssm_log_dt': 'new_m', 'new_m_ssm_glu_w': 'new_m', 'new_m_ssm_glu_b': 'new_m', 'new_m_sgu_ln_g': 'new_m', 'new_m_sgu_ln_b': 'new_m', 'new_m_sgu_w': 'new_m', 'new_m_sgu_b': 'new_m', 'new_m_out_norm_ssm_g': 'new_m', 'new_m_out_norm_sgu_g': 'new_m', 'new_m_w_out': 'new_m', 'new_m_norm_mlp_g': 'new_m', 'new_m_w_up': 'new_m', 'new_m_w_down': 'new_m', 'new_m_norm_final_g': 'new_m', 'new_v_norm_mix_g': 'new_v', 'new_v_w_in': 'new_v', 'new_v_ssm_a_re': 'new_v', 'new_v_ssm_a_im': 'new_v', 'new_v_ssm_b_re': 'new_v', 'new_v_ssm_b_im': 'new_v', 'new_v_ssm_c_re': 'new_v', 'new_v_ssm_c_im': 'new_v', 'new_v_ssm_d': 'new_v', 'new_v_ssm_log_dt': 'new_v', 'new_v_ssm_glu_w': 'new_v', 'new_v_ssm_glu_b': 'new_v', 'new_v_sgu_ln_g': 'new_v', 'new_v_sgu_ln_b': 'new_v', 'new_v_sgu_w': 'new_v', 'new_v_sgu_b': 'new_v', 'new_v_out_norm_ssm_g': 'new_v', 'new_v_out_norm_sgu_g': 'new_v', 'new_v_w_out': 'new_v', 'new_v_norm_mlp_g': 'new_v', 'new_v_w_up': 'new_v', 'new_v_w_down': 'new_v', 'new_v_norm_final_g': 'new_v'}


def _forward(args):
    return _fwd_reference(*[args[k] for k in FWD_PARAMS])


def _output_shape():
    def fwd():
        inp = _fwd_setup_inputs(0)
        return _fwd_reference(*[inp[k] for k in FWD_PARAMS])
    out = _jax.eval_shape(fwd)
    return out.shape, out.dtype

N_MICROBATCH = 1
ADAM_LR = 0.001
ADAM_B1 = 0.9
ADAM_B2 = 0.999
ADAM_EPS = 1e-08
ADAM_WD = 0.01
ADAM_STEP = 10
PER_EXAMPLE_BATCH_AXIS = {'x': 0, 'loss_target': 0}
SHARED_INPUTS = []
_WEIGHT_DTYPES = {'norm_mix_g': _jnp.float32, 'w_in': _jnp.float32, 'ssm_a_re': _jnp.float32, 'ssm_a_im': _jnp.float32, 'ssm_b_re': _jnp.float32, 'ssm_b_im': _jnp.float32, 'ssm_c_re': _jnp.float32, 'ssm_c_im': _jnp.float32, 'ssm_d': _jnp.float32, 'ssm_log_dt': _jnp.float32, 'ssm_glu_w': _jnp.float32, 'ssm_glu_b': _jnp.float32, 'sgu_ln_g': _jnp.float32, 'sgu_ln_b': _jnp.float32, 'sgu_w': _jnp.float32, 'sgu_b': _jnp.float32, 'out_norm_ssm_g': _jnp.float32, 'out_norm_sgu_g': _jnp.float32, 'w_out': _jnp.float32, 'norm_mlp_g': _jnp.float32, 'w_up': _jnp.float32, 'w_down': _jnp.float32, 'norm_final_g': _jnp.float32}
MOMENT_SCALE = {'norm_mix_g': 1.050999e-01, 'w_in': 8.447992e-02, 'ssm_a_re': 4.901754e-03, 'ssm_a_im': 4.965527e-03, 'ssm_b_re': 3.311534e-03, 'ssm_b_im': 3.375628e-03, 'ssm_c_re': 6.723710e-03, 'ssm_c_im': 6.557203e-03, 'ssm_d': 1.348681e-01, 'ssm_log_dt': 3.844053e+00, 'ssm_glu_w': 2.961098e-02, 'ssm_glu_b': 5.562666e-02, 'sgu_ln_g': 3.048552e-02, 'sgu_ln_b': 3.109219e-02, 'sgu_w': 6.214653e-02, 'sgu_b': 8.258287e-02, 'out_norm_ssm_g': 1.082335e-01, 'out_norm_sgu_g': 1.226938e-01, 'w_out': 1.195666e-01, 'norm_mlp_g': 1.043248e-01, 'w_up': 5.112953e-02, 'w_down': 1.163991e-01, 'norm_final_g': 3.259292e+01}


def _to_microbatches(a, axis):
    t = _jnp.moveaxis(a, axis, 0)
    t = t.reshape((N_MICROBATCH, t.shape[0] // N_MICROBATCH) + t.shape[1:])
    return _jnp.moveaxis(t, 1, axis + 1)


def setup_inputs(seed: int = 0) -> dict:
    inp = _fwd_setup_inputs(seed)
    key = _jax.random.fold_in(_jax.random.key(seed), 7919)
    shape, _ = _output_shape()
    out = dict(inp)
    out["loss_target"] = _jax.random.normal(_jax.random.fold_in(key, 0), shape, _jnp.float32)
    for i, name in enumerate(TWIN_WEIGHTS):
        w = inp[name].astype(_jnp.float32)
        if MOMENT_SCALE is None:
            s = _jnp.sqrt(_jnp.mean(_jnp.square(w)) + 1e-30)
        else:
            s = MOMENT_SCALE[name]
        km, kv = _jax.random.split(_jax.random.fold_in(key, i + 1))
        out[name] = w
        out["m_" + name] = s * _jax.random.normal(km, w.shape, _jnp.float32)
        out["v_" + name] = (s * s) * _jax.random.uniform(kv, w.shape, _jnp.float32, 0.5, 1.5)
    if N_MICROBATCH > 1:
        for name, axis in PER_EXAMPLE_BATCH_AXIS.items():
            out[name] = _to_microbatches(out[name], axis)
    return {'x': out['x'], 'norm_mix_g': out['norm_mix_g'], 'w_in': out['w_in'], 'ssm_a_re': out['ssm_a_re'], 'ssm_a_im': out['ssm_a_im'], 'ssm_b_re': out['ssm_b_re'], 'ssm_b_im': out['ssm_b_im'], 'ssm_c_re': out['ssm_c_re'], 'ssm_c_im': out['ssm_c_im'], 'ssm_d': out['ssm_d'], 'ssm_log_dt': out['ssm_log_dt'], 'ssm_glu_w': out['ssm_glu_w'], 'ssm_glu_b': out['ssm_glu_b'], 'sgu_ln_g': out['sgu_ln_g'], 'sgu_ln_b': out['sgu_ln_b'], 'sgu_w': out['sgu_w'], 'sgu_b': out['sgu_b'], 'out_norm_ssm_g': out['out_norm_ssm_g'], 'out_norm_sgu_g': out['out_norm_sgu_g'], 'w_out': out['w_out'], 'norm_mlp_g': out['norm_mlp_g'], 'w_up': out['w_up'], 'w_down': out['w_down'], 'norm_final_g': out['norm_final_g'], 'loss_target': out['loss_target'], 'm_norm_mix_g': out['m_norm_mix_g'], 'm_w_in': out['m_w_in'], 'm_ssm_a_re': out['m_ssm_a_re'], 'm_ssm_a_im': out['m_ssm_a_im'], 'm_ssm_b_re': out['m_ssm_b_re'], 'm_ssm_b_im': out['m_ssm_b_im'], 'm_ssm_c_re': out['m_ssm_c_re'], 'm_ssm_c_im': out['m_ssm_c_im'], 'm_ssm_d': out['m_ssm_d'], 'm_ssm_log_dt': out['m_ssm_log_dt'], 'm_ssm_glu_w': out['m_ssm_glu_w'], 'm_ssm_glu_b': out['m_ssm_glu_b'], 'm_sgu_ln_g': out['m_sgu_ln_g'], 'm_sgu_ln_b': out['m_sgu_ln_b'], 'm_sgu_w': out['m_sgu_w'], 'm_sgu_b': out['m_sgu_b'], 'm_out_norm_ssm_g': out['m_out_norm_ssm_g'], 'm_out_norm_sgu_g': out['m_out_norm_sgu_g'], 'm_w_out': out['m_w_out'], 'm_norm_mlp_g': out['m_norm_mlp_g'], 'm_w_up': out['m_w_up'], 'm_w_down': out['m_w_down'], 'm_norm_final_g': out['m_norm_final_g'], 'v_norm_mix_g': out['v_norm_mix_g'], 'v_w_in': out['v_w_in'], 'v_ssm_a_re': out['v_ssm_a_re'], 'v_ssm_a_im': out['v_ssm_a_im'], 'v_ssm_b_re': out['v_ssm_b_re'], 'v_ssm_b_im': out['v_ssm_b_im'], 'v_ssm_c_re': out['v_ssm_c_re'], 'v_ssm_c_im': out['v_ssm_c_im'], 'v_ssm_d': out['v_ssm_d'], 'v_ssm_log_dt': out['v_ssm_log_dt'], 'v_ssm_glu_w': out['v_ssm_glu_w'], 'v_ssm_glu_b': out['v_ssm_glu_b'], 'v_sgu_ln_g': out['v_sgu_ln_g'], 'v_sgu_ln_b': out['v_sgu_ln_b'], 'v_sgu_w': out['v_sgu_w'], 'v_sgu_b': out['v_sgu_b'], 'v_out_norm_ssm_g': out['v_out_norm_ssm_g'], 'v_out_norm_sgu_g': out['v_out_norm_sgu_g'], 'v_w_out': out['v_w_out'], 'v_norm_mlp_g': out['v_norm_mlp_g'], 'v_w_up': out['v_w_up'], 'v_w_down': out['v_w_down'], 'v_norm_final_g': out['v_norm_final_g']}


def _loss(weights, diff, rest, loss_target):
    with _jax.named_scope("forward"):
        args = {**rest, TWIN_DIFF_INPUT: diff, **{k: w.astype(_WEIGHT_DTYPES[k]) for k, w in weights.items()}}
        y = _forward(args)
    with _jax.named_scope("loss_head"):
        err = _jnp.square(y.astype(_jnp.float32) - loss_target)
        return 0.5 * _jnp.sum(_jnp.mean(err, axis=-1)) if err.ndim else 0.5 * err


def _adamw(w, g, m, v):
    m = ADAM_B1 * m + (1.0 - ADAM_B1) * g
    v = ADAM_B2 * v + (1.0 - ADAM_B2) * _jnp.square(g)
    m_hat = m / (1.0 - ADAM_B1 ** ADAM_STEP)
    v_hat = v / (1.0 - ADAM_B2 ** ADAM_STEP)
    delta = -ADAM_LR * (m_hat / (_jnp.sqrt(v_hat) + ADAM_EPS) + ADAM_WD * w)
    return delta, m, v


def reference(x, norm_mix_g, w_in, ssm_a_re, ssm_a_im, ssm_b_re, ssm_b_im, ssm_c_re, ssm_c_im, ssm_d, ssm_log_dt, ssm_glu_w, ssm_glu_b, sgu_ln_g, sgu_ln_b, sgu_w, sgu_b, out_norm_ssm_g, out_norm_sgu_g, w_out, norm_mlp_g, w_up, w_down, norm_final_g, loss_target, m_norm_mix_g, m_w_in, m_ssm_a_re, m_ssm_a_im, m_ssm_b_re, m_ssm_b_im, m_ssm_c_re, m_ssm_c_im, m_ssm_d, m_ssm_log_dt, m_ssm_glu_w, m_ssm_glu_b, m_sgu_ln_g, m_sgu_ln_b, m_sgu_w, m_sgu_b, m_out_norm_ssm_g, m_out_norm_sgu_g, m_w_out, m_norm_mlp_g, m_w_up, m_w_down, m_norm_final_g, v_norm_mix_g, v_w_in, v_ssm_a_re, v_ssm_a_im, v_ssm_b_re, v_ssm_b_im, v_ssm_c_re, v_ssm_c_im, v_ssm_d, v_ssm_log_dt, v_ssm_glu_w, v_ssm_glu_b, v_sgu_ln_g, v_sgu_ln_b, v_sgu_w, v_sgu_b, v_out_norm_ssm_g, v_out_norm_sgu_g, v_w_out, v_norm_mlp_g, v_w_up, v_w_down, v_norm_final_g):
    given = dict(x=x, norm_mix_g=norm_mix_g, w_in=w_in, ssm_a_re=ssm_a_re, ssm_a_im=ssm_a_im, ssm_b_re=ssm_b_re, ssm_b_im=ssm_b_im, ssm_c_re=ssm_c_re, ssm_c_im=ssm_c_im, ssm_d=ssm_d, ssm_log_dt=ssm_log_dt, ssm_glu_w=ssm_glu_w, ssm_glu_b=ssm_glu_b, sgu_ln_g=sgu_ln_g, sgu_ln_b=sgu_ln_b, sgu_w=sgu_w, sgu_b=sgu_b, out_norm_ssm_g=out_norm_ssm_g, out_norm_sgu_g=out_norm_sgu_g, w_out=w_out, norm_mlp_g=norm_mlp_g, w_up=w_up, w_down=w_down, norm_final_g=norm_final_g, loss_target=loss_target, m_norm_mix_g=m_norm_mix_g, m_w_in=m_w_in, m_ssm_a_re=m_ssm_a_re, m_ssm_a_im=m_ssm_a_im, m_ssm_b_re=m_ssm_b_re, m_ssm_b_im=m_ssm_b_im, m_ssm_c_re=m_ssm_c_re, m_ssm_c_im=m_ssm_c_im, m_ssm_d=m_ssm_d, m_ssm_log_dt=m_ssm_log_dt, m_ssm_glu_w=m_ssm_glu_w, m_ssm_glu_b=m_ssm_glu_b, m_sgu_ln_g=m_sgu_ln_g, m_sgu_ln_b=m_sgu_ln_b, m_sgu_w=m_sgu_w, m_sgu_b=m_sgu_b, m_out_norm_ssm_g=m_out_norm_ssm_g, m_out_norm_sgu_g=m_out_norm_sgu_g, m_w_out=m_w_out, m_norm_mlp_g=m_norm_mlp_g, m_w_up=m_w_up, m_w_down=m_w_down, m_norm_final_g=m_norm_final_g, v_norm_mix_g=v_norm_mix_g, v_w_in=v_w_in, v_ssm_a_re=v_ssm_a_re, v_ssm_a_im=v_ssm_a_im, v_ssm_b_re=v_ssm_b_re, v_ssm_b_im=v_ssm_b_im, v_ssm_c_re=v_ssm_c_re, v_ssm_c_im=v_ssm_c_im, v_ssm_d=v_ssm_d, v_ssm_log_dt=v_ssm_log_dt, v_ssm_glu_w=v_ssm_glu_w, v_ssm_glu_b=v_ssm_glu_b, v_sgu_ln_g=v_sgu_ln_g, v_sgu_ln_b=v_sgu_ln_b, v_sgu_w=v_sgu_w, v_sgu_b=v_sgu_b, v_out_norm_ssm_g=v_out_norm_ssm_g, v_out_norm_sgu_g=v_out_norm_sgu_g, v_w_out=v_w_out, v_norm_mlp_g=v_norm_mlp_g, v_w_up=v_w_up, v_w_down=v_w_down, v_norm_final_g=v_norm_final_g)
    weights = {n: given[n] for n in TWIN_WEIGHTS}
    shared = {n: given[n] for n in SHARED_INPUTS}
    per_example = {n: given[n] for n in ['x']}
    grad_fn = _jax.value_and_grad(_loss, argnums=(0, 1))

    def one_microbatch(ex, loss_target):
        ex = dict(ex)
        diff = ex.pop(TWIN_DIFF_INPUT)
        return grad_fn(weights, diff, {**shared, **ex}, loss_target)

    if N_MICROBATCH == 1:
        loss, (grad_w, grad_x) = one_microbatch(per_example, given["loss_target"])
    else:
        def body(carry, xs):
            loss_sum, grad_sum = carry
            l_k, (gw_k, gx_k) = one_microbatch(xs[0], xs[1])
            with _jax.named_scope("update"):
                return (loss_sum + l_k, _jax.tree.map(_jnp.add, grad_sum, gw_k)), gx_k

        init = (_jnp.zeros((), _jnp.float32), _jax.tree.map(_jnp.zeros_like, weights))
        (loss, grad_w), grad_x = _jax.lax.scan(body, init, (per_example, given["loss_target"]))
    with _jax.named_scope("update"):
        delta_w, new_m, new_v = {}, {}, {}
        for n in TWIN_WEIGHTS:
            delta_w[n], new_m[n], new_v[n] = _adamw(weights[n], grad_w[n], given["m_" + n], given["v_" + n])
    return (loss, grad_x, *[grad_w[n] for n in TWIN_WEIGHTS], *[delta_w[n] for n in TWIN_WEIGHTS],
            *[new_m[n] for n in TWIN_WEIGHTS], *[new_v[n] for n in TWIN_WEIGHTS])
```

```python
import functools
import math

import jax
import jax.numpy as jnp
from jax import lax
from jax.experimental import pallas as pl
from jax.experimental.pallas import tpu as pltpu

F32, BF16 = jnp.float32, jnp.bfloat16
EPS = 1e-6
N_DEV = 8
D_MODEL = 2048
SSM_WIDTH = 1024
SSM_GROUP = 16
SSM_STATE = 64
SGU_HEADS = 8
SGU_CHUNK = 128
LANES = 128
SUBLANES = 8
N_SLAB = SSM_WIDTH // LANES
SLAB_STATE = (LANES // SSM_GROUP) * SSM_STATE
SCAN_BLOCK = 128
VMEM_LIMIT = 48 * 1024 * 1024
ROW_BLOCK = 256
ADAM_ROWS = 128
MESH_AXES = ("x", "y", "c")

ADAM_LR, ADAM_B1, ADAM_B2, ADAM_EPS, ADAM_WD, ADAM_STEP = 0.001, 0.9, 0.999, 1e-08, 0.01, 10

_GELU_C0 = math.sqrt(2.0 / math.pi)
_GELU_C1 = 0.044715


def _gelu(v):
    return 0.5 * v * (1.0 + jnp.tanh(_GELU_C0 * (v + _GELU_C1 * v * v * v)))


def _gelu_grad(v):
    th = jnp.tanh(_GELU_C0 * (v + _GELU_C1 * v * v * v))
    return 0.5 * (1.0 + th) + 0.5 * v * (1.0 - th * th) * _GELU_C0 * (1.0 + 3.0 * _GELU_C1 * v * v)


def _sigmoid(v):
    return 1.0 / (1.0 + jnp.exp(-v))


def _params(sem=None):
    return pltpu.CompilerParams(dimension_semantics=sem, vmem_limit_bytes=VMEM_LIMIT)


def _dot(a, b, mode="nn"):
    dims = {"nn": ((1,), (0,)), "nt": ((1,), (1,)), "tn": ((0,), (0,))}[mode]
    return lax.dot_general(a, b, (dims, ((), ())), preferred_element_type=F32)


def _mm(name, mode, grid, a, b, outs, extras=(), epilogue=None):
    nk = grid[2]
    n_ex, n_out = len(extras), len(outs)
    acc_shape = tuple(d for d in outs[0][1].block_shape if d is not None)

    def body(*refs):
        a_ref, b_ref = refs[0], refs[1]
        ex = refs[2:2 + n_ex]
        out_refs = refs[2 + n_ex:2 + n_ex + n_out]
        acc = refs[-1]
        k = pl.program_id(2)

        @pl.when(k == 0)
        def _():
            acc[...] = jnp.zeros_like(acc)

        acc[...] += _dot(a_ref[...], b_ref[...], mode)

        @pl.when(k == nk - 1)
        def _():
            res = acc[...]
            res = (res,) if epilogue is None else epilogue(res, *[e[...] for e in ex])
            for o, r in zip(out_refs, res):
                o[...] = r.astype(o.dtype)

    res = pl.pallas_call(
        body, name=name, grid=grid,
        in_specs=[a[1], b[1]] + [e[1] for e in extras],
        out_specs=[o[1] for o in outs],
        out_shape=[o[0] for o in outs],
        scratch_shapes=[pltpu.VMEM(acc_shape, F32)],
        compiler_params=_params(("parallel", "parallel", "arbitrary")),
    )(a[0], b[0], *[e[0] for e in extras])
    return res


def _sds(shape, dtype):
    return jax.ShapeDtypeStruct(shape, dtype)


def _bs(shape, fn):
    return pl.BlockSpec(shape, fn)


def _rms_fwd(name, x, g):
    t, w = x.shape
    br = min(ROW_BLOCK, t)

    def body(x_ref, g_ref, o_ref):
        xv = x_ref[...]
        r = lax.rsqrt(jnp.mean(xv * xv, axis=-1, keepdims=True) + EPS)
        o_ref[...] = (xv * r * g_ref[...]).astype(BF16)

    return pl.pallas_call(
        body, name=name, grid=(t // br,),
        in_specs=[_bs((br, w), lambda i: (i, 0)), _bs((1, w), lambda i: (0, 0))],
        out_specs=_bs((br, w), lambda i: (i, 0)),
        out_shape=_sds((t, w), BF16),
        compiler_params=_params(("parallel",)),
    )(x, g)


def _mix_norm(ya, yb, ga, gb):
    t, w = ya.shape
    br = min(ROW_BLOCK, t)

    def body(a_ref, b_ref, ga_ref, gb_ref, o_ref):
        for src, g_ref, col in ((a_ref, ga_ref, 0), (b_ref, gb_ref, w)):
            v = src[...]
            r = lax.rsqrt(jnp.mean(v * v, axis=-1, keepdims=True) + EPS)
            o_ref[:, col:col + w] = (v * r * g_ref[...]).astype(BF16)

    row = _bs((br, w), lambda i: (i, 0))
    vec = _bs((1, w), lambda i: (0, 0))
    return pl.pallas_call(
        body, name="mix_norm", grid=(t // br,),
        in_specs=[row, row, vec, vec],
        out_specs=_bs((br, 2 * w), lambda i: (i, 0)),
        out_shape=_sds((t, 2 * w), BF16),
        compiler_params=_params(("parallel",)),
    )(ya, yb, ga, gb)


def _rms_bwd(name, dh, dh_col, x, g, dres=None, want_bf16=False):
    t, w = x.shape
    br = min(ROW_BLOCK, t)
    has_res = dres is not None

    def body(*refs):
        dh_ref, x_ref, g_ref = refs[0], refs[1], refs[2]
        pos = 3
        res_ref = refs[pos] if has_res else None
        pos += int(has_res)
        dx_ref = refs[pos]
        dxb_ref = refs[pos + 1] if want_bf16 else None
        dg_ref = refs[-1]
        xv, dy = x_ref[...], dh_ref[...]
        r = lax.rsqrt(jnp.mean(xv * xv, axis=-1, keepdims=True) + EPS)
        xhat = xv * r
        dxhat = dy * g_ref[...]
        dx = r * (dxhat - xhat * jnp.mean(dxhat * xhat, axis=-1, keepdims=True))
        if has_res:
            dx = dx + res_ref[...]
        dx_ref[...] = dx
        if want_bf16:
            dxb_ref[...] = dx.astype(BF16)

        @pl.when(pl.program_id(0) == 0)
        def _():
            dg_ref[...] = jnp.zeros_like(dg_ref)

        dg_ref[...] += jnp.sum(dy * xhat, axis=0, keepdims=True)

    row = _bs((br, w), lambda i: (i, 0))
    vec = _bs((1, w), lambda i: (0, 0))
    in_specs = [_bs((br, w), lambda i: (i, dh_col)), row, vec] + ([row] if has_res else [])
    out_specs = [row] + ([row] if want_bf16 else []) + [vec]
    out_shape = [_sds((t, w), F32)] + ([_sds((t, w), BF16)] if want_bf16 else []) + [_sds((1, w), F32)]
    args = [dh, x, g] + ([dres] if has_res else [])
    return pl.pallas_call(
        body, name=name, grid=(t // br,),
        in_specs=in_specs, out_specs=out_specs, out_shape=out_shape,
        compiler_params=_params(("arbitrary",)),
    )(*args)


def _final_loss(x3, target, g):
    t, w = x3.shape
    br = min(ROW_BLOCK, t)

    def body(x_ref, tg_ref, g_ref, dx_ref, dxb_ref, dg_ref, l_ref):
        xv = x_ref[...]
        r = lax.rsqrt(jnp.mean(xv * xv, axis=-1, keepdims=True) + EPS)
        xhat = xv * r
        err = xhat * g_ref[...] - tg_ref[...]
        dy = err * (1.0 / w)
        dxhat = dy * g_ref[...]
        dx = r * (dxhat - xhat * jnp.mean(dxhat * xhat, axis=-1, keepdims=True))
        dx_ref[...] = dx
        dxb_ref[...] = dx.astype(BF16)

        @pl.when(pl.program_id(0) == 0)
        def _():
            dg_ref[...] = jnp.zeros_like(dg_ref)
            l_ref[...] = jnp.zeros_like(l_ref)

        dg_ref[...] += jnp.sum(dy * xhat, axis=0, keepdims=True)
        l_ref[...] += jnp.sum(err * err, axis=0, keepdims=True)

    row = _bs((br, w), lambda i: (i, 0))
    vec = _bs((1, w), lambda i: (0, 0))
    return pl.pallas_call(
        body, name="final_loss", grid=(t // br,),
        in_specs=[row, row, vec], out_specs=[row, row, vec, vec],
        out_shape=[_sds((t, w), F32), _sds((t, w), BF16), _sds((1, w), F32), _sds((1, w), F32)],
        compiler_params=_params(("arbitrary",)),
    )(x3, target, g)


def _glu_pre_bwd(dy, ypre, gate):
    t, w = dy.shape
    br = min(ROW_BLOCK, t)

    def body(dy_ref, y_ref, gt_ref, dp_ref, db_ref):
        gt = gt_ref[...]
        dp = dy_ref[...] * _gelu(y_ref[...]) * gt * (1.0 - gt)
        dp_ref[...] = dp.astype(BF16)

        @pl.when(pl.program_id(0) == 0)
        def _():
            db_ref[...] = jnp.zeros_like(db_ref)

        db_ref[...] += jnp.sum(dp, axis=0, keepdims=True)

    row = _bs((br, w), lambda i: (i, 0))
    vec = _bs((1, w), lambda i: (0, 0))
    return pl.pallas_call(
        body, name="glu_pre_bwd", grid=(t // br,),
        in_specs=[row, row, row], out_specs=[row, vec],
        out_shape=[_sds((t, w), BF16), _sds((1, w), F32)],
        compiler_params=_params(("arbitrary",)),
    )(dy, ypre, gate)


def _prep_math(are, aim, ldt, bxr, bxi):
    dt = jnp.exp(ldt)
    er = jnp.exp(are * dt)
    th = aim * dt
    abr, abi = er * jnp.cos(th), er * jnp.sin(th)
    nr, ni = abr - 1.0, abi
    den = are * are + aim * aim
    cr = (nr * are + ni * aim) / den
    ci = (ni * are - nr * aim) / den
    bbr, bbi = [], []
    for j in range(N_SLAB):
        sl = slice(j * SLAB_STATE, (j + 1) * SLAB_STATE)
        bbr.append(cr[:, sl] * bxr[j] - ci[:, sl] * bxi[j])
        bbi.append(cr[:, sl] * bxi[j] + ci[:, sl] * bxr[j])
    return abr, abi, bbr, bbi


def _ssm_prep(are, aim, ldt, bxr, bxi, cxr, cxi):
    nst = N_SLAB * SLAB_STATE

    def body(are_r, aim_r, ldt_r, bxr_r, bxi_r, cxr_r, cxi_r,
             bre_o, bim_o, cre_o, cimn_o, apr_o, api_o, air_o, aii_o):
        abr, abi, bbr, bbi = _prep_math(are_r[...], aim_r[...], ldt_r[...],
                                        [bxr_r[j] for j in range(N_SLAB)], [bxi_r[j] for j in range(N_SLAB)])
        for j in range(N_SLAB):
            bre_o[j] = bbr[j].astype(BF16)
            bim_o[j] = bbi[j].astype(BF16)
        cre_o[...] = cxr_r[...].astype(BF16)
        cimn_o[...] = (-cxi_r[...]).astype(BF16)

        def step(k, cur):
            cr, ci = cur
            den = cr * cr + ci * ci
            apr_o[pl.ds(k, 1), :] = cr
            api_o[pl.ds(k, 1), :] = ci
            air_o[pl.ds(k, 1), :] = cr / den
            aii_o[pl.ds(k, 1), :] = -ci / den
            return cr * abr - ci * abi, cr * abi + ci * abr

        lax.fori_loop(0, SCAN_BLOCK, step, (jnp.ones((1, nst), F32), jnp.zeros((1, nst), F32)))

    tab = _sds((SCAN_BLOCK, nst), F32)
    return pl.pallas_call(
        body, name="ssm_prep",
        out_shape=[_sds(bxr.shape, BF16), _sds(bxr.shape, BF16), _sds(cxr.shape, BF16), _sds(cxr.shape, BF16),
                   tab, tab, tab, tab],
        compiler_params=_params(),
    )(are, aim, ldt, bxr, bxi, cxr, cxi)


def _ssm_prep_bwd(are, aim, ldt, bxr, bxi, dbre, dbim, qr, qi):
    def body(are_r, aim_r, ldt_r, bxr_r, bxi_r, dbre_r, dbim_r, qr_r, qi_r,
             dare_o, daim_o, dldt_o, dbxr_o, dbxi_o):
        prim = (are_r[...], aim_r[...], ldt_r[...],
                [bxr_r[j] for j in range(N_SLAB)], [bxi_r[j] for j in range(N_SLAB)])
        (abr, abi, _, _), vjp = jax.vjp(_prep_math, *prim)
        den = abr * abr + abi * abi
        q_r, q_i = qr_r[...], qi_r[...]
        gar = (q_r * abr - q_i * abi) / den
        gai = (q_r * abi + q_i * abr) / den
        ct = (gar, gai, [dbre_r[j] for j in range(N_SLAB)], [dbim_r[j] for j in range(N_SLAB)])
        dare, daim, dldt, dbxr, dbxi = vjp(ct)
        dare_o[...] = dare
        daim_o[...] = daim
        dldt_o[...] = dldt
        for j in range(N_SLAB):
            dbxr_o[j] = dbxr[j]
            dbxi_o[j] = dbxi[j]

    row = _sds(are.shape, F32)
    return pl.pallas_call(
        body, name="ssm_prep_bwd",
        out_shape=[row, row, row, _sds(bxr.shape, F32), _sds(bxr.shape, F32)],
        compiler_params=_params(),
    )(are, aim, ldt, bxr, bxi, dbre, dbim, qr, qi)


def _tri(lower):
    r = lax.broadcasted_iota(jnp.int32, (SCAN_BLOCK, SCAN_BLOCK), 0)
    c = lax.broadcasted_iota(jnp.int32, (SCAN_BLOCK, SCAN_BLOCK), 1)
    return jnp.where((r >= c) if lower else (r <= c), 1.0, 0.0).astype(BF16)


def _cumsum_mxu(tri, v):
    hi = v.astype(BF16)
    lo = (v - hi.astype(F32)).astype(BF16)
    return _dot(tri, hi) + _dot(tri, lo)


def _ssm_specs(t):
    nt = t // SCAN_BLOCK
    tab = _bs((SCAN_BLOCK, SLAB_STATE), lambda j, i: (0, j))
    bmat = _bs((None, LANES, SLAB_STATE), lambda j, i: (j, 0, 0))
    cmat = _bs((None, SLAB_STATE, LANES), lambda j, i: (j, 0, 0))
    return nt, tab, bmat, cmat


def _ssm_fwd(z, dvec, bre, bim, cre, cimn, apr, api, air, aii):
    t = z.shape[0]
    nt, tab, bmat, cmat = _ssm_specs(t)
    nst = N_SLAB * SLAB_STATE
    last = SCAN_BLOCK - 1

    def body(u_ref, d_ref, bre_r, bim_r, cre_r, cimn_r, apr_r, api_r, air_r, aii_r,
             y_ref, yg_ref, pr_ref, pi_ref, car_r, car_i):
        @pl.when(pl.program_id(1) == 0)
        def _():
            car_r[...] = jnp.zeros_like(car_r)
            car_i[...] = jnp.zeros_like(car_i)

        u = u_ref[...]
        ub = u.astype(BF16)
        bur, bui = _dot(ub, bre_r[...]), _dot(ub, bim_r[...])
        ir, ii = air_r[...], aii_r[...]
        tri = _tri(True)
        csr = _cumsum_mxu(tri, ir * bur - ii * bui)
        csi = _cumsum_mxu(tri, ir * bui + ii * bur)
        pr, pi = apr_r[...], api_r[...]
        a_r, a_i = apr_r[1:2, :], api_r[1:2, :]
        c_r, c_i = car_r[...], car_i[...]
        wr = csr + (a_r * c_r - a_i * c_i)
        wi = csi + (a_r * c_i + a_i * c_r)
        sr = pr * wr - pi * wi
        si = pr * wi + pi * wr
        car_r[...] = sr[last:last + 1, :]
        car_i[...] = si[last:last + 1, :]
        pr_ref[...] = (sr - bur).astype(BF16)
        pi_ref[...] = (si - bui).astype(BF16)
        y = _dot(sr.astype(BF16), cre_r[...]) + _dot(si.astype(BF16), cimn_r[...]) + d_ref[...] * u
        y_ref[...] = y
        yg_ref[...] = _gelu(y).astype(BF16)

    ublk = _bs((SCAN_BLOCK, LANES), lambda j, i: (i, j))
    sblk = _bs((SCAN_BLOCK, SLAB_STATE), lambda j, i: (i, j))
    return pl.pallas_call(
        body, name="ssm_fwd", grid=(N_SLAB, nt),
        in_specs=[ublk, _bs((1, LANES), lambda j, i: (0, j)), bmat, bmat, cmat, cmat, tab, tab, tab, tab],
        out_specs=[ublk, ublk, sblk, sblk],
        out_shape=[_sds((t, SSM_WIDTH), F32), _sds((t, SSM_WIDTH), BF16),
                   _sds((t, nst), BF16), _sds((t, nst), BF16)],
        scratch_shapes=[pltpu.VMEM((1, SLAB_STATE), F32), pltpu.VMEM((1, SLAB_STATE), F32)],
        compiler_params=_params(("parallel", "arbitrary")),
    )(z, dvec, bre, bim, cre, cimn, apr, api, air, aii)


def _ssm_bwd(gy, z, p_re, p_im, dvec, bre, bim, cre, cimn, apr, api, air, aii):
    t = z.shape[0]
    nt, tab, bmat, cmat = _ssm_specs(t)
    last = SCAN_BLOCK - 1

    def fold(v):
        return v.reshape(SCAN_BLOCK // SUBLANES, SUBLANES, v.shape[-1]).sum(axis=0)

    def body(g_ref, u_ref, pr_ref, pi_ref, d_ref, bre_r, bim_r, cre_r, cimn_r, apr_r, api_r, air_r, aii_r,
             du_ref, dbre_o, dbim_o, dcre_o, dcimn_o, qr_o, qi_o, dd_o, car_r, car_i, qacc_r, qacc_i, dacc):
        i = pl.program_id(1)

        @pl.when(i == 0)
        def _():
            for ref in (car_r, car_i, qacc_r, qacc_i, dacc, dbre_o, dbim_o, dcre_o, dcimn_o):
                ref[...] = jnp.zeros_like(ref)

        g = g_ref[...]
        gb = g.astype(BF16)
        u = u_ref[...]
        ub = u.astype(BF16)
        bur, bui = _dot(ub, bre_r[...]), _dot(ub, bim_r[...])
        p_r, p_i = pr_ref[...].astype(F32), pi_ref[...].astype(F32)
        srb, sib = (p_r + bur).astype(BF16), (p_i + bui).astype(BF16)
        dcre_o[...] += _dot(srb, gb, "tn")
        dcimn_o[...] += _dot(sib, gb, "tn")
        dsr, dsi = _dot(gb, cre_r[...], "nt"), _dot(gb, cimn_r[...], "nt")
        pr, pi = apr_r[...], api_r[...]
        tri = _tri(False)
        csr = _cumsum_mxu(tri, pr * dsr + pi * dsi)
        csi = _cumsum_mxu(tri, pr * dsi - pi * dsr)
        al_r, al_i = apr_r[last:last + 1, :], api_r[last:last + 1, :]
        c_r, c_i = car_r[...], car_i[...]
        wr = csr + (al_r * c_r + al_i * c_i)
        wi = csi + (al_r * c_i - al_i * c_r)
        ir, ii = air_r[...], aii_r[...]
        lr = ir * wr + ii * wi
        li = ir * wi - ii * wr
        a_r, a_i = apr_r[1:2, :], api_r[1:2, :]
        car_r[...] = a_r * lr[0:1, :] + a_i * li[0:1, :]
        car_i[...] = a_r * li[0:1, :] - a_i * lr[0:1, :]
        lrb, lib = lr.astype(BF16), li.astype(BF16)
        dbre_o[...] += _dot(ub, lrb, "tn")
        dbim_o[...] += _dot(ub, lib, "tn")
        du = d_ref[...] * g + _dot(lrb, bre_r[...], "nt") + _dot(lib, bim_r[...], "nt")
        du_ref[...] = du.astype(BF16)
        qacc_r[...] += fold(lr * p_r + li * p_i)
        qacc_i[...] += fold(li * p_r - lr * p_i)
        dacc[...] += fold(g * u)

        @pl.when(i == nt - 1)
        def _():
            qr_o[...] = jnp.sum(qacc_r[...], axis=0, keepdims=True)
            qi_o[...] = jnp.sum(qacc_i[...], axis=0, keepdims=True)
            dd_o[...] = jnp.sum(dacc[...], axis=0, keepdims=True)

    rev = lambda j, i: (nt - 1 - i, j)
    ublk = _bs((SCAN_BLOCK, LANES), rev)
    sblk = _bs((SCAN_BLOCK, SLAB_STATE), rev)
    qrow = _bs((None, 1, SLAB_STATE), lambda j, i: (j, 0, 0))
    return pl.pallas_call(
        body, name="ssm_bwd", grid=(N_SLAB, nt),
        in_specs=[ublk, ublk, sblk, sblk, _bs((1, LANES), lambda j, i: (0, j)),
                  bmat, bmat, cmat, cmat, tab, tab, tab, tab],
        out_specs=[ublk, bmat, bmat, cmat, cmat, qrow, qrow, _bs((None, 1, LANES), lambda j, i: (j, 0, 0))],
        out_shape=[_sds((t, SSM_WIDTH), BF16),
                   _sds((N_SLAB, LANES, SLAB_STATE), F32), _sds((N_SLAB, LANES, SLAB_STATE), F32),
                   _sds((N_SLAB, SLAB_STATE, LANES), F32), _sds((N_SLAB, SLAB_STATE, LANES), F32),
                   _sds((N_SLAB, 1, SLAB_STATE), F32), _sds((N_SLAB, 1, SLAB_STATE), F32),
                   _sds((N_SLAB, 1, LANES), F32)],
        scratch_shapes=[pltpu.VMEM((1, SLAB_STATE), F32), pltpu.VMEM((1, SLAB_STATE), F32),
                        pltpu.VMEM((SUBLANES, SLAB_STATE), F32), pltpu.VMEM((SUBLANES, SLAB_STATE), F32),
                        pltpu.VMEM((SUBLANES, LANES), F32)],
        compiler_params=_params(("parallel", "arbitrary")),
    )(gy, z, p_re, p_im, dvec, bre, bim, cre, cimn, apr, api, air, aii)


def _sgu_mask():
    r = lax.broadcasted_iota(jnp.int32, (SGU_CHUNK, SGU_CHUNK), 0)
    c = lax.broadcasted_iota(jnp.int32, (SGU_CHUNK, SGU_CHUNK), 1)
    return r >= c


def _sgu_common(zu, zv, lng, lnb):
    us, v = _gelu(zu), _gelu(zv)
    mu = jnp.mean(v, axis=-1, keepdims=True)
    vc = v - mu
    rstd = lax.rsqrt(jnp.mean(vc * vc, axis=-1, keepdims=True) + EPS)
    xhat = vc * rstd
    return us, xhat, rstd, xhat * lng + lnb


def _sgu_fwd(z, lng, lnb, w, bexp):
    t = z.shape[0]
    hd = SGU_CHUNK

    def body(zu_ref, zv_ref, lng_ref, lnb_ref, w_ref, b_ref, y_ref):
        us, _, _, vn = _sgu_common(zu_ref[...], zv_ref[...], lng_ref[...], lnb_ref[...])
        vnb = vn.astype(BF16)
        mask = _sgu_mask()
        for h in range(SGU_HEADS):
            sl = slice(h * hd, (h + 1) * hd)
            wt = jnp.where(mask, w_ref[h], 0.0).astype(BF16)
            y_ref[:, sl] = us[:, sl] * (_dot(wt, vnb[:, sl]) + b_ref[h])

    row = lambda c: _bs((SGU_CHUNK, SSM_WIDTH), lambda i: (i, c))
    vec = _bs((1, SSM_WIDTH), lambda i: (0, 0))
    hmat = _bs((SGU_HEADS, hd, hd), lambda i: (0, 0, 0))
    return pl.pallas_call(
        body, name="sgu_fwd", grid=(t // SGU_CHUNK,),
        in_specs=[row(1), row(2), vec, vec, hmat, hmat],
        out_specs=row(0), out_shape=_sds((t, SSM_WIDTH), F32),
        compiler_params=_params(("parallel",)),
    )(z, z, lng, lnb, w, bexp)


def _sgu_bwd(dy, du_ssm, z, lng, lnb, w, bexp):
    t = z.shape[0]
    hd = SGU_CHUNK
    nc = t // SGU_CHUNK

    def body(dy_ref, dus_ref, zu_ref, zv_ref, lng_ref, lnb_ref, w_ref, b_ref,
             dz_ref, dlng_o, dlnb_o, dw_o, db_o):
        i = pl.program_id(0)

        @pl.when(i == 0)
        def _():
            for ref in (dlng_o, dlnb_o, dw_o, db_o):
                ref[...] = jnp.zeros_like(ref)

        zu, zv = zu_ref[...], zv_ref[...]
        lng = lng_ref[...]
        us, xhat, rstd, vn = _sgu_common(zu, zv, lng, lnb_ref[...])
        vnb = vn.astype(BF16)
        dyv = dy_ref[...]
        mask = _sgu_mask()
        dus_parts, dvn_parts = [], []
        for h in range(SGU_HEADS):
            sl = slice(h * hd, (h + 1) * hd)
            wt = jnp.where(mask, w_ref[h], 0.0).astype(BF16)
            mixed = _dot(wt, vnb[:, sl]) + b_ref[h]
            dus_parts.append(dyv[:, sl] * mixed)
            dmix = dyv[:, sl] * us[:, sl]
            dmb = dmix.astype(BF16)
            db_o[h] += dmix
            dw_o[h] += _dot(dmb, vnb[:, sl], "nt")
            dvn_parts.append(_dot(wt, dmb, "tn"))
        dus = jnp.concatenate(dus_parts, axis=1)
        dvn = jnp.concatenate(dvn_parts, axis=1)
        dlng_o[...] += jnp.sum(dvn * xhat, axis=0, keepdims=True)
        dlnb_o[...] += jnp.sum(dvn, axis=0, keepdims=True)
        dxh = dvn * lng
        dv = rstd * (dxh - jnp.mean(dxh, axis=-1, keepdims=True)
                     - xhat * jnp.mean(dxh * xhat, axis=-1, keepdims=True))
        dz_ref[:, 0:SSM_WIDTH] = dus_ref[...]
        dz_ref[:, SSM_WIDTH:2 * SSM_WIDTH] = (dus * _gelu_grad(zu)).astype(BF16)
        dz_ref[:, 2 * SSM_WIDTH:] = (dv * _gelu_grad(zv)).astype(BF16)

        @pl.when(i == nc - 1)
        def _():
            for h in range(SGU_HEADS):
                dw_o[h] = jnp.where(mask, dw_o[h], 0.0)
                db_o[h] = jnp.broadcast_to(jnp.sum(db_o[h], axis=1, keepdims=True), (hd, hd))

    row = lambda c: _bs((SGU_CHUNK, SSM_WIDTH), lambda i: (i, c))
    vec = _bs((1, SSM_WIDTH), lambda i: (0, 0))
    hmat = _bs((SGU_HEADS, hd, hd), lambda i: (0, 0, 0))
    return pl.pallas_call(
        body, name="sgu_bwd", grid=(nc,),
        in_specs=[row(0), row(0), row(1), row(2), vec, vec, hmat, hmat],
        out_specs=[_bs((SGU_CHUNK, 3 * SSM_WIDTH), lambda i: (i, 0)), vec, vec, hmat, hmat],
        out_shape=[_sds((t, 3 * SSM_WIDTH), BF16), _sds((1, SSM_WIDTH), F32), _sds((1, SSM_WIDTH), F32),
                   _sds((SGU_HEADS, hd, hd), F32), _sds((SGU_HEADS, hd, hd), F32)],
        compiler_params=_params(("arbitrary",)),
    )(dy, du_ssm, z, z, lng, lnb, w, bexp)


def _place():
    x, y, c = (lax.axis_index(a) for a in MESH_AXES)
    return x, y, c


def _index(p):
    return 4 * p[0] + 2 * p[1] + p[2]


def _any_spec():
    return pl.BlockSpec(memory_space=pl.ANY)


def _all_gather(name, shards):
    n = len(shards)

    def body(*refs):
        ins, outs = refs[:n], refs[n:2 * n]
        send, recv, loc = refs[2 * n:]
        x, y, c = _place()
        me, sib = (x, y, c), (x, y, 1 - c)
        chips = [(1 - x, y), (x, 1 - y), (1 - x, 1 - y)]

        def cp(w, k, block, to, src=None):
            dst = outs[w].at[_index(block)]
            return pltpu.make_async_remote_copy(
                src_ref=dst if src is None else src, dst_ref=dst,
                send_sem=send.at[w * 7 + k], recv_sem=recv.at[w * 7 + k],
                device_id=to, device_id_type=pl.DeviceIdType.MESH)

        mines, sends = [], []
        for w in range(n):
            m = pltpu.make_async_copy(ins[w], outs[w].at[_index(me)], loc.at[w])
            m.start()
            mines.append(m)
            first = [cp(w, 0, me, sib, src=ins[w])]
            first += [cp(w, 1 + j, me, (*chip, c), src=ins[w]) for j, chip in enumerate(chips)]
            for q in first:
                q.start()
            sends += first
        for j, chip in enumerate(chips):
            for w in range(n):
                cp(w, 1 + j, (*chip, c), me).wait_recv()
                q = cp(w, 4 + j, (*chip, c), sib)
                q.start()
                sends.append(q)
        for w in range(n):
            cp(w, 0, sib, me).wait_recv()
            for j, chip in enumerate(chips):
                cp(w, 4 + j, (*chip, 1 - c), me).wait_recv()
        for q in sends:
            q.wait_send()
        for m in mines:
            m.wait()

    return pl.pallas_call(
        body, name=name,
        in_specs=[_any_spec()] * n, out_specs=[_any_spec()] * n,
        out_shape=[_sds((N_DEV,) + s.shape, s.dtype) for s in shards],
        scratch_shapes=[pltpu.SemaphoreType.DMA((n * 7,)), pltpu.SemaphoreType.DMA((n * 7,)),
                        pltpu.SemaphoreType.DMA((n,))],
        compiler_params=pltpu.CompilerParams(has_side_effects=True),
    )(*shards)


def _exchange_shards(name, grads):
    n = len(grads)

    def body(*refs):
        ins, outs = refs[:n], refs[n:2 * n]
        send, recv, loc = refs[2 * n:]
        x, y, c = _place()
        me = _index((x, y, c))

        def peer(r):
            return ((1 - x) if r & 4 else x, (1 - y) if r & 2 else y, (1 - c) if r & 1 else c)

        def cp(w, r):
            p = peer(r)
            return pltpu.make_async_remote_copy(
                src_ref=ins[w].at[_index(p)], dst_ref=outs[w].at[me],
                send_sem=send.at[w * 7 + r - 1], recv_sem=recv.at[w * 7 + r - 1],
                device_id=p, device_id_type=pl.DeviceIdType.MESH)

        def landed(w, r):
            p = _index(peer(r))
            return pltpu.make_async_remote_copy(
                src_ref=ins[w].at[p], dst_ref=outs[w].at[p],
                send_sem=send.at[w * 7 + r - 1], recv_sem=recv.at[w * 7 + r - 1],
                device_id=peer(r), device_id_type=pl.DeviceIdType.MESH)

        mines = []
        for w in range(n):
            m = pltpu.make_async_copy(ins[w].at[me], outs[w].at[me], loc.at[w])
            m.start()
            mines.append(m)
        for w in range(n):
            for r in range(1, N_DEV):
                cp(w, r).start()
        for w in range(n):
            for r in range(1, N_DEV):
                landed(w, r).wait_recv()
        for w in range(n):
            for r in range(1, N_DEV):
                cp(w, r).wait_send()
        for m in mines:
            m.wait()

    return pl.pallas_call(
        body, name=name,
        in_specs=[_any_spec()] * n, out_specs=[_any_spec()] * n,
        out_shape=[_sds(g.shape, g.dtype) for g in grads],
        scratch_shapes=[pltpu.SemaphoreType.DMA((n * 7,)), pltpu.SemaphoreType.DMA((n * 7,)),
                        pltpu.SemaphoreType.DMA((n,))],
        compiler_params=pltpu.CompilerParams(has_side_effects=True),
    )(*grads)


def _adamw(name, parts, w, m, v):
    rows, cols = w.shape
    br = min(ADAM_ROWS, rows)
    c1 = 1.0 / (1.0 - ADAM_B1 ** ADAM_STEP)
    c2 = 1.0 / (1.0 - ADAM_B2 ** ADAM_STEP)

    def body(p_ref, w_ref, m_ref, v_ref, g_o, d_o, m_o, v_o):
        g = p_ref[0].astype(F32)
        for k in range(1, N_DEV):
            g = g + p_ref[k].astype(F32)
        mn = ADAM_B1 * m_ref[...] + (1.0 - ADAM_B1) * g
        vn = ADAM_B2 * v_ref[...] + (1.0 - ADAM_B2) * (g * g)
        g_o[...] = g
        m_o[...] = mn
        v_o[...] = vn
        d_o[...] = -ADAM_LR * ((mn * c1) / (jnp.sqrt(vn * c2) + ADAM_EPS) + ADAM_WD * w_ref[...])

    blk = _bs((br, cols), lambda i: (i, 0))
    out = _sds((rows, cols), F32)
    return pl.pallas_call(
        body, name=name, grid=(rows // br,),
        in_specs=[_bs((N_DEV, br, cols), lambda i: (0, i, 0)), blk, blk, blk],
        out_specs=[blk] * 4, out_shape=[out] * 4,
        compiler_params=_params(("parallel",)),
    )(parts, w, m, v)


def _pack(arrs):
    tile = SUBLANES * LANES
    flat = []
    for a in arrs:
        f = a.reshape(-1).astype(F32)
        pad = (-f.shape[0]) % tile
        flat.append(jnp.pad(f, (0, pad)) if pad else f)
    total = sum(f.shape[0] for f in flat)
    tail = (-total) % (ADAM_ROWS * LANES)
    if tail:
        flat.append(jnp.zeros((tail,), F32))
    return jnp.concatenate(flat).reshape(-1, LANES)


def _unpack(buf, like):
    tile = SUBLANES * LANES
    flat = buf.reshape(-1)
    out, off = [], 0
    for a in like:
        n = math.prod(a.shape)
        out.append(flat[off:off + n].reshape(a.shape))
        off += n + ((-n) % tile)
    return out


def _to_block_b(b):
    gl = LANES // SSM_GROUP
    tb = b.reshape(N_SLAB, gl, SSM_STATE, SSM_GROUP).transpose(0, 1, 3, 2)
    eye = jnp.eye(gl, dtype=F32)
    return (tb[:, :, :, None, :] * eye[None, :, None, :, None]).reshape(N_SLAB, LANES, SLAB_STATE)


def _from_block_b(bx):
    gl = LANES // SSM_GROUP
    d = jnp.einsum("jghgp->jgph", bx.reshape(N_SLAB, gl, SSM_GROUP, gl, SSM_STATE))
    return d.reshape(N_SLAB * gl, SSM_STATE, SSM_GROUP)


def _to_block_c(cm):
    gl = LANES // SSM_GROUP
    tc = cm.reshape(N_SLAB, gl, SSM_GROUP, SSM_STATE).transpose(0, 1, 3, 2)
    eye = jnp.eye(gl, dtype=F32)
    return (tc[:, :, :, None, :] * eye[None, :, None, :, None]).reshape(N_SLAB, SLAB_STATE, LANES)


def _from_block_c(cx):
    gl = LANES // SSM_GROUP
    d = jnp.einsum("jgpgh->jghp", cx.reshape(N_SLAB, gl, SSM_STATE, gl, SSM_GROUP))
    return d.reshape(N_SLAB * gl, SSM_GROUP, SSM_STATE)


def kernel(x, norm_mix_g, w_in, ssm_a_re, ssm_a_im, ssm_b_re, ssm_b_im, ssm_c_re, ssm_c_im, ssm_d, ssm_log_dt, ssm_glu_w, ssm_glu_b, sgu_ln_g, sgu_ln_b, sgu_w, sgu_b, out_norm_ssm_g, out_norm_sgu_g, w_out, norm_mlp_g, w_up, w_down, norm_final_g, loss_target, m_norm_mix_g, m_w_in, m_ssm_a_re, m_ssm_a_im, m_ssm_b_re, m_ssm_b_im, m_ssm_c_re, m_ssm_c_im, m_ssm_d, m_ssm_log_dt, m_ssm_glu_w, m_ssm_glu_b, m_sgu_ln_g, m_sgu_ln_b, m_sgu_w, m_sgu_b, m_out_norm_ssm_g, m_out_norm_sgu_g, m_w_out, m_norm_mlp_g, m_w_up, m_w_down, m_norm_final_g, v_norm_mix_g, v_w_in, v_ssm_a_re, v_ssm_a_im, v_ssm_b_re, v_ssm_b_im, v_ssm_c_re, v_ssm_c_im, v_ssm_d, v_ssm_log_dt, v_ssm_glu_w, v_ssm_glu_b, v_sgu_ln_g, v_sgu_ln_b, v_sgu_w, v_sgu_b, v_out_norm_ssm_g, v_out_norm_sgu_g, v_w_out, v_norm_mlp_g, v_w_up, v_w_down, v_norm_final_g):
    given = dict(locals())
    names = ["norm_mix_g", "w_in", "ssm_a_re", "ssm_a_im", "ssm_b_re", "ssm_b_im", "ssm_c_re", "ssm_c_im",
             "ssm_d", "ssm_log_dt", "ssm_glu_w", "ssm_glu_b", "sgu_ln_g", "sgu_ln_b", "sgu_w", "sgu_b",
             "out_norm_ssm_g", "out_norm_sgu_g", "w_out", "norm_mlp_g", "w_up", "w_down", "norm_final_g"]
    big = ["w_in", "ssm_glu_w", "w_out", "w_up", "w_down"]
    small = [n for n in names if n not in big]

    d = D_MODEL
    t = x.shape[1]
    tb = min(1024, t)
    xs = x[0]
    target = loss_target[0]
    nsh_in = w_in.shape[2]
    nsh_up = w_up.shape[2]
    d_ff = nsh_up * N_DEV
    n_in = nsh_in * N_DEV

    shards = [w_in[0].astype(BF16), ssm_glu_w[0].astype(BF16), w_out[0].astype(BF16),
              w_up[0].astype(BF16), w_down[0].astype(BF16)]
    wg_in, wg_glu, wg_out, wg_up, wg_down = _all_gather("gather_weights", shards)
    wg_glu = wg_glu.reshape(SSM_WIDTH, SSM_WIDTH)
    wg_out = wg_out.reshape(d, d)
    wg_down = wg_down.reshape(d_ff, d)

    nst = N_SLAB * SLAB_STATE
    are, aim = ssm_a_re.reshape(1, nst), ssm_a_im.reshape(1, nst)
    ldt = jnp.repeat(ssm_log_dt[0], SSM_STATE).reshape(1, nst)
    bxr, bxi = _to_block_b(ssm_b_re[0]), _to_block_b(ssm_b_im[0])
    cxr, cxi = _to_block_c(ssm_c_re[0]), _to_block_c(ssm_c_im[0])
    dvec = ssm_d.reshape(1, SSM_WIDTH)
    bre, bim, cre, cimn, apr, api, air, aii = _ssm_prep(are, aim, ldt, bxr, bxi, cxr, cxi)
    tabs = (bre, bim, cre, cimn, apr, api, air, aii)

    h1 = _rms_fwd("norm_mix", xs, norm_mix_g)
    (z,) = _mm("in_proj", "nn", (t // tb, N_DEV, 1),
               (h1, _bs((tb, d), lambda i, j, k: (i, 0))),
               (wg_in, _bs((None, d, nsh_in), lambda i, j, k: (j, 0, 0))),
               [(_sds((t, n_in), F32), _bs((tb, nsh_in), lambda i, j, k: (i, j)))])
    y_pre, yg_b, p_re, p_im = _ssm_fwd(z, dvec, *tabs)

    def glu_ep(acc, yp, b):
        gate = _sigmoid(acc + b)
        return _gelu(yp) * gate, gate

    hw = SSM_WIDTH // 2
    tile_g = _bs((tb, hw), lambda i, j, k: (i, j))
    y_ssm, gate = _mm("glu", "nn", (t // tb, 2, 1),
                      (yg_b, _bs((tb, SSM_WIDTH), lambda i, j, k: (i, 0))),
                      (wg_glu, _bs((SSM_WIDTH, hw), lambda i, j, k: (0, j))),
                      [(_sds((t, SSM_WIDTH), F32), tile_g), (_sds((t, SSM_WIDTH), F32), tile_g)],
                      extras=[(y_pre, tile_g), (ssm_glu_b, _bs((1, hw), lambda i, j, k: (0, j)))],
                      epilogue=glu_ep)

    sgu_bexp = jnp.broadcast_to(sgu_b[0][:, :, None], (SGU_HEADS, SGU_CHUNK, SGU_CHUNK))
    y_sgu = _sgu_fwd(z, sgu_ln_g, sgu_ln_b, sgu_w[0], sgu_bexp)
    mixed = _mix_norm(y_ssm, y_sgu, out_norm_ssm_g, out_norm_sgu_g)

    bn_o = 512
    tile_o = _bs((tb, bn_o), lambda i, j, k: (i, j))
    (x2,) = _mm("out_proj", "nn", (t // tb, d // bn_o, 1),
                (mixed, _bs((tb, d), lambda i, j, k: (i, 0))),
                (wg_out, _bs((d, bn_o), lambda i, j, k: (0, j))),
                [(_sds((t, d), F32), tile_o)],
                extras=[(xs, tile_o)], epilogue=lambda acc, r: (acc + r,))
    h2 = _rms_fwd("norm_mlp", x2, norm_mlp_g)

    def up_ep(acc):
        r = jnp.maximum(acc, 0.0)
        return r * r, r

    tile_f = _bs((tb, nsh_up), lambda i, j, k: (i, j))
    f_act, r_act = _mm("mlp_up", "nn", (t // tb, N_DEV, 1),
                       (h2, _bs((tb, d), lambda i, j, k: (i, 0))),
                       (wg_up, _bs((None, d, nsh_up), lambda i, j, k: (j, 0, 0))),
                       [(_sds((t, d_ff), BF16), tile_f), (_sds((t, d_ff), BF16), tile_f)],
                       epilogue=up_ep)
    bk_d = 2048
    (x3,) = _mm("mlp_down", "nn", (t // tb, d // bn_o, d_ff // bk_d),
                (f_act, _bs((tb, bk_d), lambda i, j, k: (i, k))),
                (wg_down, _bs((bk_d, bn_o), lambda i, j, k: (k, j))),
                [(_sds((t, d), F32), tile_o)],
                extras=[(x2, tile_o)], epilogue=lambda acc, r: (acc + r,))

    dx3, dx3_b, g_final, err2 = _final_loss(x3, target, norm_final_g.reshape(1, d))
    loss = lax.psum(0.5 * jnp.sum(err2) / d, MESH_AXES)

    bn_a = 1024
    tile_a = _bs((tb, bn_a), lambda i, j, k: (i, j))
    (da,) = _mm("mlp_down_dx", "nt", (t // tb, d_ff // bn_a, 1),
                (dx3_b, _bs((tb, d), lambda i, j, k: (i, 0))),
                (wg_down, _bs((bn_a, d), lambda i, j, k: (j, 0))),
                [(_sds((t, d_ff), BF16), tile_a)],
                extras=[(r_act, tile_a)], epilogue=lambda acc, r: (acc * (2.0 * r.astype(F32)),))
    sq = 1024
    (gw_down,) = _mm("mlp_down_dw", "tn", (d_ff // sq, d // sq, t // tb),
                     (f_act, _bs((tb, sq), lambda i, j, k: (k, i))),
                     (dx3_b, _bs((tb, sq), lambda i, j, k: (k, j))),
                     [(_sds((d_ff, d), BF16), _bs((sq, sq), lambda i, j, k: (i, j)))])
    (gw_up,) = _mm("mlp_up_dw", "tn", (d // sq, d_ff // nsh_up, t // tb),
                   (h2, _bs((tb, sq), lambda i, j, k: (k, i))),
                   (da, _bs((tb, nsh_up), lambda i, j, k: (k, j))),
                   [(_sds((N_DEV, d, nsh_up), BF16), _bs((None, sq, nsh_up), lambda i, j, k: (j, i, 0)))])
    (dh2,) = _mm("mlp_up_dx", "nt", (t // tb, d // sq, N_DEV),
                 (da, _bs((tb, nsh_up), lambda i, j, k: (i, k))),
                 (wg_up, _bs((None, sq, nsh_up), lambda i, j, k: (k, j, 0))),
                 [(_sds((t, d), F32), _bs((tb, sq), lambda i, j, k: (i, j)))])
    dx2, dx2_b, g_norm_mlp = _rms_bwd("norm_mlp_bwd", dh2, 0, x2, norm_mlp_g, dres=dx3, want_bf16=True)

    (dmixed,) = _mm("out_proj_dx", "nt", (t // tb, d // sq, 1),
                    (dx2_b, _bs((tb, d), lambda i, j, k: (i, 0))),
                    (wg_out, _bs((sq, d), lambda i, j, k: (j, 0))),
                    [(_sds((t, d), F32), _bs((tb, sq), lambda i, j, k: (i, j)))])
    (gw_out,) = _mm("out_proj_dw", "tn", (d // sq, d // sq, t // tb),
                    (mixed, _bs((tb, sq), lambda i, j, k: (k, i))),
                    (dx2_b, _bs((tb, sq), lambda i, j, k: (k, j))),
                    [(_sds((d, d), BF16), _bs((sq, sq), lambda i, j, k: (i, j)))])
    dy_ssm, g_onorm_ssm = _rms_bwd("out_norm_ssm_bwd", dmixed, 0, y_ssm, out_norm_ssm_g)
    dy_sgu, g_onorm_sgu = _rms_bwd("out_norm_sgu_bwd", dmixed, 1, y_sgu, out_norm_sgu_g)

    dpre_b, g_glu_b = _glu_pre_bwd(dy_ssm, y_pre, gate)
    (gw_glu,) = _mm("glu_dw", "tn", (1, 1, t // tb),
                    (yg_b, _bs((tb, SSM_WIDTH), lambda i, j, k: (k, 0))),
                    (dpre_b, _bs((tb, SSM_WIDTH), lambda i, j, k: (k, 0))),
                    [(_sds((SSM_WIDTH, SSM_WIDTH), BF16), _bs((SSM_WIDTH, SSM_WIDTH), lambda i, j, k: (0, 0)))])
    (dy_pre,) = _mm("glu_dx", "nt", (t // tb, 2, 1),
                    (dpre_b, _bs((tb, SSM_WIDTH), lambda i, j, k: (i, 0))),
                    (wg_glu, _bs((hw, SSM_WIDTH), lambda i, j, k: (j, 0))),
                    [(_sds((t, SSM_WIDTH), F32), tile_g)],
                    extras=[(dy_ssm, tile_g), (gate, tile_g), (y_pre, tile_g)],
                    epilogue=lambda acc, dy, gt, yp: ((dy * gt + acc) * _gelu_grad(yp),))
    du_b, dbre, dbim, dcre, dcimn, q_re, q_im, dd = _ssm_bwd(dy_pre, z, p_re, p_im, dvec, *tabs)
    dare, daim, dldt, dbxr, dbxi = _ssm_prep_bwd(are, aim, ldt, bxr, bxi, dbre, dbim,
                                                 q_re.reshape(1, nst), q_im.reshape(1, nst))

    dz_b, g_ln_g, g_ln_b, g_sgu_w, g_sgu_bx = _sgu_bwd(dy_sgu, du_b, z, sgu_ln_g, sgu_ln_b, sgu_w[0], sgu_bexp)
    (dh1,) = _mm("in_proj_dx", "nt", (t // tb, d // sq, N_DEV),
                 (dz_b, _bs((tb, nsh_in), lambda i, j, k: (i, k))),
                 (wg_in, _bs((None, sq, nsh_in), lambda i, j, k: (k, j, 0))),
                 [(_sds((t, d), F32), _bs((tb, sq), lambda i, j, k: (i, j)))])
    (gw_in,) = _mm("in_proj_dw", "tn", (d // sq, N_DEV, t // tb),
                   (h1, _bs((tb, sq), lambda i, j, k: (k, i))),
                   (dz_b, _bs((tb, nsh_in), lambda i, j, k: (k, j))),
                   [(_sds((N_DEV, d, nsh_in), BF16), _bs((None, sq, nsh_in), lambda i, j, k: (j, i, 0)))])
    grad_x, g_norm_mix = _rms_bwd("norm_mix_bwd", dh1, 0, xs, norm_mix_g, dres=dx2)

    full = {"w_in": gw_in, "ssm_glu_w": gw_glu.reshape(N_DEV, -1, SSM_WIDTH), "w_out": gw_out.reshape(N_DEV, -1, d),
            "w_up": gw_up, "w_down": gw_down.reshape(N_DEV, -1, d)}
    parts = dict(zip(big, _exchange_shards("exchange_grads", [full[n] for n in big])))

    local_small = {
        "norm_mix_g": g_norm_mix, "ssm_a_re": dare, "ssm_a_im": daim,
        "ssm_b_re": _from_block_b(dbxr), "ssm_b_im": _from_block_b(dbxi),
        "ssm_c_re": _from_block_c(dcre), "ssm_c_im": -_from_block_c(dcimn),
        "ssm_d": dd, "ssm_log_dt": dldt.reshape(-1, SSM_STATE).sum(axis=-1),
        "ssm_glu_b": g_glu_b, "sgu_ln_g": g_ln_g, "sgu_ln_b": g_ln_b, "sgu_w": g_sgu_w,
        "sgu_b": g_sgu_bx[:, :, 0], "out_norm_ssm_g": g_onorm_ssm, "out_norm_sgu_g": g_onorm_sgu,
        "norm_mlp_g": g_norm_mlp, "norm_final_g": g_final,
    }
    (small_parts,) = _all_gather("gather_small_grads", [_pack([local_small[n] for n in small])])

    grads, deltas, new_m, new_v = {}, {}, {}, {}
    for n in big:
        w2 = given[n][0]
        res = _adamw("adamw_" + n, parts[n], w2, given["m_" + n][0], given["v_" + n][0])
        grads[n], deltas[n], new_m[n], new_v[n] = [r.reshape(given[n].shape) for r in res]
    like = [given[n] for n in small]
    res = _adamw("adamw_small", small_parts, _pack(like), _pack([given["m_" + n] for n in small]),
                 _pack([given["v_" + n] for n in small]))
    for store, buf in zip((grads, deltas, new_m, new_v), res):
        for n, a in zip(small, _unpack(buf, like)):
            store[n] = a

    return (loss, grad_x.reshape(x.shape), *[grads[n] for n in names], *[deltas[n] for n in names],
            *[new_m[n] for n in names], *[new_v[n] for n in names])
```

```python
import functools
import math

import jax
import jax.numpy as jnp
from jax import lax
from jax.experimental import pallas as pl
from jax.experimental.pallas import tpu as pltpu

F32, BF16 = jnp.float32, jnp.bfloat16
EPS = 1e-6
N_DEV = 8
D_MODEL = 2048
SSM_WIDTH = 1024
SSM_GROUP = 16
SSM_STATE = 64
SGU_HEADS = 8
SGU_CHUNK = 128
LANES = 128
SUBLANES = 8
N_SLAB = SSM_WIDTH // LANES
SLAB_STATE = (LANES // SSM_GROUP) * SSM_STATE
SCAN_BLOCK = 128
VMEM_LIMIT = 48 * 1024 * 1024
ROW_BLOCK = 256
ADAM_ROWS = 128
MESH_AXES = ("x", "y", "c")

ADAM_LR, ADAM_B1, ADAM_B2, ADAM_EPS, ADAM_WD, ADAM_STEP = 0.001, 0.9, 0.999, 1e-08, 0.01, 10

_GELU_C0 = math.sqrt(2.0 / math.pi)
_GELU_C1 = 0.044715


def _gelu(v):
    return 0.5 * v * (1.0 + jnp.tanh(_GELU_C0 * (v + _GELU_C1 * v * v * v)))


def _gelu_grad(v):
    th = jnp.tanh(_GELU_C0 * (v + _GELU_C1 * v * v * v))
    return 0.5 * (1.0 + th) + 0.5 * v * (1.0 - th * th) * _GELU_C0 * (1.0 + 3.0 * _GELU_C1 * v * v)


def _sigmoid(v):
    return 1.0 / (1.0 + jnp.exp(-v))


def _params(sem=None):
    return pltpu.CompilerParams(dimension_semantics=sem, vmem_limit_bytes=VMEM_LIMIT)


def _dot(a, b, mode="nn"):
    dims = {"nn": ((1,), (0,)), "nt": ((1,), (1,)), "tn": ((0,), (0,))}[mode]
    return lax.dot_general(a, b, (dims, ((), ())), preferred_element_type=F32)


def _mm(name, mode, grid, a, b, outs, extras=(), epilogue=None, deps=()):
    nk = grid[2]
    n_ex, n_out, n_dep = len(extras), len(outs), len(deps)
    acc_shape = tuple(d for d in outs[0][1].block_shape if d is not None)

    def body(*refs):
        a_ref, b_ref = refs[0], refs[1]
        ex = refs[2:2 + n_ex]
        out_refs = refs[2 + n_ex + n_dep:2 + n_ex + n_dep + n_out]
        acc = refs[-1]
        k = pl.program_id(2)

        @pl.when(k == 0)
        def _():
            acc[...] = jnp.zeros_like(acc)

        acc[...] += _dot(a_ref[...], b_ref[...], mode)

        @pl.when(k == nk - 1)
        def _():
            res = acc[...]
            res = (res,) if epilogue is None else epilogue(res, *[e[...] for e in ex])
            for o, r in zip(out_refs, res):
                o[...] = r.astype(o.dtype)

    res = pl.pallas_call(
        body, name=name, grid=grid,
        in_specs=[a[1], b[1]] + [e[1] for e in extras] + [_any_spec()] * n_dep,
        out_specs=[o[1] for o in outs],
        out_shape=[o[0] for o in outs],
        scratch_shapes=[pltpu.VMEM(acc_shape, F32)],
        compiler_params=_params(("parallel", "parallel", "arbitrary")),
    )(a[0], b[0], *[e[0] for e in extras], *deps)
    return res


def _sds(shape, dtype):
    return jax.ShapeDtypeStruct(shape, dtype)


def _bs(shape, fn):
    return pl.BlockSpec(shape, fn)


def _rms_fwd(name, x, g, deps=()):
    t, w = x.shape
    br = min(ROW_BLOCK, t)

    def body(*refs):
        x_ref, g_ref, o_ref = refs[0], refs[1], refs[-1]
        xv = x_ref[...]
        r = lax.rsqrt(jnp.mean(xv * xv, axis=-1, keepdims=True) + EPS)
        o_ref[...] = (xv * r * g_ref[...]).astype(BF16)

    return pl.pallas_call(
        body, name=name, grid=(t // br,),
        in_specs=[_bs((br, w), lambda i: (i, 0)), _bs((1, w), lambda i: (0, 0))] + [_any_spec()] * len(deps),
        out_specs=_bs((br, w), lambda i: (i, 0)),
        out_shape=_sds((t, w), BF16),
        compiler_params=_params(("parallel",)),
    )(x, g, *deps)


def _mix_norm(ya, yb, ga, gb):
    t, w = ya.shape
    br = min(ROW_BLOCK, t)

    def body(a_ref, b_ref, ga_ref, gb_ref, o_ref):
        for src, g_ref, col in ((a_ref, ga_ref, 0), (b_ref, gb_ref, w)):
            v = src[...]
            r = lax.rsqrt(jnp.mean(v * v, axis=-1, keepdims=True) + EPS)
            o_ref[:, col:col + w] = (v * r * g_ref[...]).astype(BF16)

    row = _bs((br, w), lambda i: (i, 0))
    vec = _bs((1, w), lambda i: (0, 0))
    return pl.pallas_call(
        body, name="mix_norm", grid=(t // br,),
        in_specs=[row, row, vec, vec],
        out_specs=_bs((br, 2 * w), lambda i: (i, 0)),
        out_shape=_sds((t, 2 * w), BF16),
        compiler_params=_params(("parallel",)),
    )(ya, yb, ga, gb)


def _rms_bwd(name, dh, dh_col, x, g, dres=None, want_bf16=False, deps=()):
    t, w = x.shape
    br = min(ROW_BLOCK, t)
    has_res = dres is not None

    def body(*refs):
        dh_ref, x_ref, g_ref = refs[0], refs[1], refs[2]
        pos = 3
        res_ref = refs[pos] if has_res else None
        pos += int(has_res) + len(deps)
        dx_ref = refs[pos]
        dxb_ref = refs[pos + 1] if want_bf16 else None
        dg_ref = refs[-1]
        xv, dy = x_ref[...], dh_ref[...]
        r = lax.rsqrt(jnp.mean(xv * xv, axis=-1, keepdims=True) + EPS)
        xhat = xv * r
        dxhat = dy * g_ref[...]
        dx = r * (dxhat - xhat * jnp.mean(dxhat * xhat, axis=-1, keepdims=True))
        if has_res:
            dx = dx + res_ref[...]
        dx_ref[...] = dx
        if want_bf16:
            dxb_ref[...] = dx.astype(BF16)

        @pl.when(pl.program_id(0) == 0)
        def _():
            dg_ref[...] = jnp.zeros_like(dg_ref)

        dg_ref[...] += jnp.sum(dy * xhat, axis=0, keepdims=True)

    row = _bs((br, w), lambda i: (i, 0))
    vec = _bs((1, w), lambda i: (0, 0))
    in_specs = ([_bs((br, w), lambda i: (i, dh_col)), row, vec] + ([row] if has_res else [])
                + [_any_spec()] * len(deps))
    out_specs = [row] + ([row] if want_bf16 else []) + [vec]
    out_shape = [_sds((t, w), F32)] + ([_sds((t, w), BF16)] if want_bf16 else []) + [_sds((1, w), F32)]
    args = [dh, x, g] + ([dres] if has_res else []) + list(deps)
    return pl.pallas_call(
        body, name=name, grid=(t // br,),
        in_specs=in_specs, out_specs=out_specs, out_shape=out_shape,
        compiler_params=_params(("arbitrary",)),
    )(*args)


def _final_loss(x3, target, g):
    t, w = x3.shape
    br = min(ROW_BLOCK, t)

    def body(x_ref, tg_ref, g_ref, dx_ref, dxb_ref, dg_ref, l_ref):
        xv = x_ref[...]
        r = lax.rsqrt(jnp.mean(xv * xv, axis=-1, keepdims=True) + EPS)
        xhat = xv * r
        err = xhat * g_ref[...] - tg_ref[...]
        dy = err * (1.0 / w)
        dxhat = dy * g_ref[...]
        dx = r * (dxhat - xhat * jnp.mean(dxhat * xhat, axis=-1, keepdims=True))
        dx_ref[...] = dx
        dxb_ref[...] = dx.astype(BF16)

        @pl.when(pl.program_id(0) == 0)
        def _():
            dg_ref[...] = jnp.zeros_like(dg_ref)
            l_ref[...] = jnp.zeros_like(l_ref)

        dg_ref[...] += jnp.sum(dy * xhat, axis=0, keepdims=True)
        l_ref[...] += jnp.sum(err * err, axis=0, keepdims=True)

    row = _bs((br, w), lambda i: (i, 0))
    vec = _bs((1, w), lambda i: (0, 0))
    return pl.pallas_call(
        body, name="final_loss", grid=(t // br,),
        in_specs=[row, row, vec], out_specs=[row, row, vec, vec],
        out_shape=[_sds((t, w), F32), _sds((t, w), BF16), _sds((1, w), F32), _sds((1, w), F32)],
        compiler_params=_params(("arbitrary",)),
    )(x3, target, g)


def _glu_pre_bwd(dy, ypre, gate):
    t, w = dy.shape
    br = min(ROW_BLOCK, t)

    def body(dy_ref, y_ref, gt_ref, dp_ref, db_ref):
        gt = gt_ref[...]
        dp = dy_ref[...] * _gelu(y_ref[...]) * gt * (1.0 - gt)
        dp_ref[...] = dp.astype(BF16)

        @pl.when(pl.program_id(0) == 0)
        def _():
            db_ref[...] = jnp.zeros_like(db_ref)

        db_ref[...] += jnp.sum(dp, axis=0, keepdims=True)

    row = _bs((br, w), lambda i: (i, 0))
    vec = _bs((1, w), lambda i: (0, 0))
    return pl.pallas_call(
        body, name="glu_pre_bwd", grid=(t // br,),
        in_specs=[row, row, row], out_specs=[row, vec],
        out_shape=[_sds((t, w), BF16), _sds((1, w), F32)],
        compiler_params=_params(("arbitrary",)),
    )(dy, ypre, gate)


def _prep_math(are, aim, ldt, bxr, bxi):
    dt = jnp.exp(ldt)
    er = jnp.exp(are * dt)
    th = aim * dt
    abr, abi = er * jnp.cos(th), er * jnp.sin(th)
    nr, ni = abr - 1.0, abi
    den = are * are + aim * aim
    cr = (nr * are + ni * aim) / den
    ci = (ni * are - nr * aim) / den
    bbr, bbi = [], []
    for j in range(N_SLAB):
        sl = slice(j * SLAB_STATE, (j + 1) * SLAB_STATE)
        bbr.append(cr[:, sl] * bxr[j] - ci[:, sl] * bxi[j])
        bbi.append(cr[:, sl] * bxi[j] + ci[:, sl] * bxr[j])
    return abr, abi, bbr, bbi


def _ssm_prep(are, aim, ldt, bxr, bxi, cxr, cxi):
    nst = N_SLAB * SLAB_STATE

    def body(are_r, aim_r, ldt_r, bxr_r, bxi_r, cxr_r, cxi_r,
             bre_o, bim_o, cre_o, cimn_o, apr_o, api_o, air_o, aii_o):
        abr, abi, bbr, bbi = _prep_math(are_r[...], aim_r[...], ldt_r[...],
                                        [bxr_r[j] for j in range(N_SLAB)], [bxi_r[j] for j in range(N_SLAB)])
        for j in range(N_SLAB):
            bre_o[j] = bbr[j].astype(BF16)
            bim_o[j] = bbi[j].astype(BF16)
        cre_o[...] = cxr_r[...].astype(BF16)
        cimn_o[...] = (-cxi_r[...]).astype(BF16)

        def step(k, cur):
            cr, ci = cur
            den = cr * cr + ci * ci
            apr_o[pl.ds(k, 1), :] = cr
            api_o[pl.ds(k, 1), :] = ci
            air_o[pl.ds(k, 1), :] = cr / den
            aii_o[pl.ds(k, 1), :] = -ci / den
            return cr * abr - ci * abi, cr * abi + ci * abr

        lax.fori_loop(0, SCAN_BLOCK, step, (jnp.ones((1, nst), F32), jnp.zeros((1, nst), F32)))

    tab = _sds((SCAN_BLOCK, nst), F32)
    return pl.pallas_call(
        body, name="ssm_prep",
        out_shape=[_sds(bxr.shape, BF16), _sds(bxr.shape, BF16), _sds(cxr.shape, BF16), _sds(cxr.shape, BF16),
                   tab, tab, tab, tab],
        compiler_params=_params(),
    )(are, aim, ldt, bxr, bxi, cxr, cxi)


def _ssm_prep_bwd(are, aim, ldt, bxr, bxi, dbre, dbim, qr, qi):
    def body(are_r, aim_r, ldt_r, bxr_r, bxi_r, dbre_r, dbim_r, qr_r, qi_r,
             dare_o, daim_o, dldt_o, dbxr_o, dbxi_o):
        prim = (are_r[...], aim_r[...], ldt_r[...],
                [bxr_r[j] for j in range(N_SLAB)], [bxi_r[j] for j in range(N_SLAB)])
        (abr, abi, _, _), vjp = jax.vjp(_prep_math, *prim)
        den = abr * abr + abi * abi
        q_r, q_i = qr_r[...], qi_r[...]
        gar = (q_r * abr - q_i * abi) / den
        gai = (q_r * abi + q_i * abr) / den
        ct = (gar, gai, [dbre_r[j] for j in range(N_SLAB)], [dbim_r[j] for j in range(N_SLAB)])
        dare, daim, dldt, dbxr, dbxi = vjp(ct)
        dare_o[...] = dare
        daim_o[...] = daim
        dldt_o[...] = dldt
        for j in range(N_SLAB):
            dbxr_o[j] = dbxr[j]
            dbxi_o[j] = dbxi[j]

    row = _sds(are.shape, F32)
    return pl.pallas_call(
        body, name="ssm_prep_bwd",
        out_shape=[row, row, row, _sds(bxr.shape, F32), _sds(bxr.shape, F32)],
        compiler_params=_params(),
    )(are, aim, ldt, bxr, bxi, dbre, dbim, qr, qi)


def _tri(lower):
    r = lax.broadcasted_iota(jnp.int32, (SCAN_BLOCK, SCAN_BLOCK), 0)
    c = lax.broadcasted_iota(jnp.int32, (SCAN_BLOCK, SCAN_BLOCK), 1)
    return jnp.where((r >= c) if lower else (r <= c), 1.0, 0.0).astype(BF16)


def _cumsum_mxu(tri, v):
    hi = v.astype(BF16)
    lo = (v - hi.astype(F32)).astype(BF16)
    return _dot(tri, hi) + _dot(tri, lo)


def _ssm_specs(t):
    nt = t // SCAN_BLOCK
    tab = _bs((SCAN_BLOCK, SLAB_STATE), lambda j, i: (0, j))
    bmat = _bs((None, LANES, SLAB_STATE), lambda j, i: (j, 0, 0))
    cmat = _bs((None, SLAB_STATE, LANES), lambda j, i: (j, 0, 0))
    return nt, tab, bmat, cmat


def _ssm_fwd(z, dvec, bre, bim, cre, cimn, apr, api, air, aii):
    t = z.shape[0]
    nt, tab, bmat, cmat = _ssm_specs(t)
    nst = N_SLAB * SLAB_STATE
    last = SCAN_BLOCK - 1

    def body(u_ref, d_ref, bre_r, bim_r, cre_r, cimn_r, apr_r, api_r, air_r, aii_r,
             y_ref, yg_ref, pr_ref, pi_ref, car_r, car_i):
        @pl.when(pl.program_id(1) == 0)
        def _():
            car_r[...] = jnp.zeros_like(car_r)
            car_i[...] = jnp.zeros_like(car_i)

        u = u_ref[...]
        ub = u.astype(BF16)
        bur, bui = _dot(ub, bre_r[...]), _dot(ub, bim_r[...])
        ir, ii = air_r[...], aii_r[...]
        tri = _tri(True)
        csr = _cumsum_mxu(tri, ir * bur - ii * bui)
        csi = _cumsum_mxu(tri, ir * bui + ii * bur)
        pr, pi = apr_r[...], api_r[...]
        a_r, a_i = apr_r[1:2, :], api_r[1:2, :]
        c_r, c_i = car_r[...], car_i[...]
        wr = csr + (a_r * c_r - a_i * c_i)
        wi = csi + (a_r * c_i + a_i * c_r)
        sr = pr * wr - pi * wi
        si = pr * wi + pi * wr
        car_r[...] = sr[last:last + 1, :]
        car_i[...] = si[last:last + 1, :]
        pr_ref[...] = (sr - bur).astype(BF16)
        pi_ref[...] = (si - bui).astype(BF16)
        y = _dot(sr.astype(BF16), cre_r[...]) + _dot(si.astype(BF16), cimn_r[...]) + d_ref[...] * u
        y_ref[...] = y
        yg_ref[...] = _gelu(y).astype(BF16)

    ublk = _bs((SCAN_BLOCK, LANES), lambda j, i: (i, j))
    sblk = _bs((SCAN_BLOCK, SLAB_STATE), lambda j, i: (i, j))
    return pl.pallas_call(
        body, name="ssm_fwd", grid=(N_SLAB, nt),
        in_specs=[ublk, _bs((1, LANES), lambda j, i: (0, j)), bmat, bmat, cmat, cmat, tab, tab, tab, tab],
        out_specs=[ublk, ublk, sblk, sblk],
        out_shape=[_sds((t, SSM_WIDTH), F32), _sds((t, SSM_WIDTH), BF16),
                   _sds((t, nst), BF16), _sds((t, nst), BF16)],
        scratch_shapes=[pltpu.VMEM((1, SLAB_STATE), F32), pltpu.VMEM((1, SLAB_STATE), F32)],
        compiler_params=_params(("parallel", "arbitrary")),
    )(z, dvec, bre, bim, cre, cimn, apr, api, air, aii)


def _ssm_bwd(gy, z, p_re, p_im, dvec, bre, bim, cre, cimn, apr, api, air, aii):
    t = z.shape[0]
    nt, tab, bmat, cmat = _ssm_specs(t)
    last = SCAN_BLOCK - 1

    def fold(v):
        return v.reshape(SCAN_BLOCK // SUBLANES, SUBLANES, v.shape[-1]).sum(axis=0)

    def body(g_ref, u_ref, pr_ref, pi_ref, d_ref, bre_r, bim_r, cre_r, cimn_r, apr_r, api_r, air_r, aii_r,
             du_ref, dbre_o, dbim_o, dcre_o, dcimn_o, qr_o, qi_o, dd_o, car_r, car_i, qacc_r, qacc_i, dacc):
        i = pl.program_id(1)

        @pl.when(i == 0)
        def _():
            for ref in (car_r, car_i, qacc_r, qacc_i, dacc, dbre_o, dbim_o, dcre_o, dcimn_o):
                ref[...] = jnp.zeros_like(ref)

        g = g_ref[...]
        gb = g.astype(BF16)
        u = u_ref[...]
        ub = u.astype(BF16)
        bur, bui = _dot(ub, bre_r[...]), _dot(ub, bim_r[...])
        p_r, p_i = pr_ref[...].astype(F32), pi_ref[...].astype(F32)
        srb, sib = (p_r + bur).astype(BF16), (p_i + bui).astype(BF16)
        dcre_o[...] += _dot(srb, gb, "tn")
        dcimn_o[...] += _dot(sib, gb, "tn")
        dsr, dsi = _dot(gb, cre_r[...], "nt"), _dot(gb, cimn_r[...], "nt")
        pr, pi = apr_r[...], api_r[...]
        tri = _tri(False)
        csr = _cumsum_mxu(tri, pr * dsr + pi * dsi)
        csi = _cumsum_mxu(tri, pr * dsi - pi * dsr)
        al_r, al_i = apr_r[last:last + 1, :], api_r[last:last + 1, :]
        c_r, c_i = car_r[...], car_i[...]
        wr = csr + (al_r * c_r + al_i * c_i)
        wi = csi + (al_r * c_i - al_i * c_r)
        ir, ii = air_r[...], aii_r[...]
        lr = ir * wr + ii * wi
        li = ir * wi - ii * wr
        a_r, a_i = apr_r[1:2, :], api_r[1:2, :]
        car_r[...] = a_r * lr[0:1, :] + a_i * li[0:1, :]
        car_i[...] = a_r * li[0:1, :] - a_i * lr[0:1, :]
        lrb, lib = lr.astype(BF16), li.astype(BF16)
        dbre_o[...] += _dot(ub, lrb, "tn")
        dbim_o[...] += _dot(ub, lib, "tn")
        du = d_ref[...] * g + _dot(lrb, bre_r[...], "nt") + _dot(lib, bim_r[...], "nt")
        du_ref[...] = du.astype(BF16)
        qacc_r[...] += fold(lr * p_r + li * p_i)
        qacc_i[...] += fold(li * p_r - lr * p_i)
        dacc[...] += fold(g * u)

        @pl.when(i == nt - 1)
        def _():
            qr_o[...] = jnp.sum(qacc_r[...], axis=0, keepdims=True)
            qi_o[...] = jnp.sum(qacc_i[...], axis=0, keepdims=True)
            dd_o[...] = jnp.sum(dacc[...], axis=0, keepdims=True)

    rev = lambda j, i: (nt - 1 - i, j)
    ublk = _bs((SCAN_BLOCK, LANES), rev)
    sblk = _bs((SCAN_BLOCK, SLAB_STATE), rev)
    qrow = _bs((None, 1, SLAB_STATE), lambda j, i: (j, 0, 0))
    return pl.pallas_call(
        body, name="ssm_bwd", grid=(N_SLAB, nt),
        in_specs=[ublk, ublk, sblk, sblk, _bs((1, LANES), lambda j, i: (0, j)),
                  bmat, bmat, cmat, cmat, tab, tab, tab, tab],
        out_specs=[ublk, bmat, bmat, cmat, cmat, qrow, qrow, _bs((None, 1, LANES), lambda j, i: (j, 0, 0))],
        out_shape=[_sds((t, SSM_WIDTH), BF16),
                   _sds((N_SLAB, LANES, SLAB_STATE), F32), _sds((N_SLAB, LANES, SLAB_STATE), F32),
                   _sds((N_SLAB, SLAB_STATE, LANES), F32), _sds((N_SLAB, SLAB_STATE, LANES), F32),
                   _sds((N_SLAB, 1, SLAB_STATE), F32), _sds((N_SLAB, 1, SLAB_STATE), F32),
                   _sds((N_SLAB, 1, LANES), F32)],
        scratch_shapes=[pltpu.VMEM((1, SLAB_STATE), F32), pltpu.VMEM((1, SLAB_STATE), F32),
                        pltpu.VMEM((SUBLANES, SLAB_STATE), F32), pltpu.VMEM((SUBLANES, SLAB_STATE), F32),
                        pltpu.VMEM((SUBLANES, LANES), F32)],
        compiler_params=_params(("parallel", "arbitrary")),
    )(gy, z, p_re, p_im, dvec, bre, bim, cre, cimn, apr, api, air, aii)


def _sgu_mask():
    r = lax.broadcasted_iota(jnp.int32, (SGU_CHUNK, SGU_CHUNK), 0)
    c = lax.broadcasted_iota(jnp.int32, (SGU_CHUNK, SGU_CHUNK), 1)
    return r >= c


def _sgu_common(zu, zv, lng, lnb):
    us, v = _gelu(zu), _gelu(zv)
    mu = jnp.mean(v, axis=-1, keepdims=True)
    vc = v - mu
    rstd = lax.rsqrt(jnp.mean(vc * vc, axis=-1, keepdims=True) + EPS)
    xhat = vc * rstd
    return us, xhat, rstd, xhat * lng + lnb


def _sgu_fwd(z, lng, lnb, w, bexp):
    t = z.shape[0]
    hd = SGU_CHUNK

    def body(zu_ref, zv_ref, lng_ref, lnb_ref, w_ref, b_ref, y_ref):
        us, _, _, vn = _sgu_common(zu_ref[...], zv_ref[...], lng_ref[...], lnb_ref[...])
        vnb = vn.astype(BF16)
        mask = _sgu_mask()
        for h in range(SGU_HEADS):
            sl = slice(h * hd, (h + 1) * hd)
            wt = jnp.where(mask, w_ref[h], 0.0).astype(BF16)
            y_ref[:, sl] = us[:, sl] * (_dot(wt, vnb[:, sl]) + b_ref[h])

    row = lambda c: _bs((SGU_CHUNK, SSM_WIDTH), lambda i: (i, c))
    vec = _bs((1, SSM_WIDTH), lambda i: (0, 0))
    hmat = _bs((SGU_HEADS, hd, hd), lambda i: (0, 0, 0))
    return pl.pallas_call(
        body, name="sgu_fwd", grid=(t // SGU_CHUNK,),
        in_specs=[row(1), row(2), vec, vec, hmat, hmat],
        out_specs=row(0), out_shape=_sds((t, SSM_WIDTH), F32),
        compiler_params=_params(("parallel",)),
    )(z, z, lng, lnb, w, bexp)


def _sgu_bwd(dy, du_ssm, z, lng, lnb, w, bexp):
    t = z.shape[0]
    hd = SGU_CHUNK
    nc = t // SGU_CHUNK

    def body(dy_ref, dus_ref, zu_ref, zv_ref, lng_ref, lnb_ref, w_ref, b_ref,
             dz_ref, dlng_o, dlnb_o, dw_o, db_o):
        i = pl.program_id(0)

        @pl.when(i == 0)
        def _():
            for ref in (dlng_o, dlnb_o, dw_o, db_o):
                ref[...] = jnp.zeros_like(ref)

        zu, zv = zu_ref[...], zv_ref[...]
        lng = lng_ref[...]
        us, xhat, rstd, vn = _sgu_common(zu, zv, lng, lnb_ref[...])
        vnb = vn.astype(BF16)
        dyv = dy_ref[...]
        mask = _sgu_mask()
        dus_parts, dvn_parts = [], []
        for h in range(SGU_HEADS):
            sl = slice(h * hd, (h + 1) * hd)
            wt = jnp.where(mask, w_ref[h], 0.0).astype(BF16)
            mixed = _dot(wt, vnb[:, sl]) + b_ref[h]
            dus_parts.append(dyv[:, sl] * mixed)
            dmix = dyv[:, sl] * us[:, sl]
            dmb = dmix.astype(BF16)
            db_o[h] += dmix
            dw_o[h] += _dot(dmb, vnb[:, sl], "nt")
            dvn_parts.append(_dot(wt, dmb, "tn"))
        dus = jnp.concatenate(dus_parts, axis=1)
        dvn = jnp.concatenate(dvn_parts, axis=1)
        dlng_o[...] += jnp.sum(dvn * xhat, axis=0, keepdims=True)
        dlnb_o[...] += jnp.sum(dvn, axis=0, keepdims=True)
        dxh = dvn * lng
        dv = rstd * (dxh - jnp.mean(dxh, axis=-1, keepdims=True)
                     - xhat * jnp.mean(dxh * xhat, axis=-1, keepdims=True))
        dz_ref[:, 0:SSM_WIDTH] = dus_ref[...]
        dz_ref[:, SSM_WIDTH:2 * SSM_WIDTH] = (dus * _gelu_grad(zu)).astype(BF16)
        dz_ref[:, 2 * SSM_WIDTH:] = (dv * _gelu_grad(zv)).astype(BF16)

        @pl.when(i == nc - 1)
        def _():
            for h in range(SGU_HEADS):
                dw_o[h] = jnp.where(mask, dw_o[h], 0.0)
                db_o[h] = jnp.broadcast_to(jnp.sum(db_o[h], axis=1, keepdims=True), (hd, hd))

    row = lambda c: _bs((SGU_CHUNK, SSM_WIDTH), lambda i: (i, c))
    vec = _bs((1, SSM_WIDTH), lambda i: (0, 0))
    hmat = _bs((SGU_HEADS, hd, hd), lambda i: (0, 0, 0))
    return pl.pallas_call(
        body, name="sgu_bwd", grid=(nc,),
        in_specs=[row(0), row(0), row(1), row(2), vec, vec, hmat, hmat],
        out_specs=[_bs((SGU_CHUNK, 3 * SSM_WIDTH), lambda i: (i, 0)), vec, vec, hmat, hmat],
        out_shape=[_sds((t, 3 * SSM_WIDTH), BF16), _sds((1, SSM_WIDTH), F32), _sds((1, SSM_WIDTH), F32),
                   _sds((SGU_HEADS, hd, hd), F32), _sds((SGU_HEADS, hd, hd), F32)],
        compiler_params=_params(("arbitrary",)),
    )(dy, du_ssm, z, z, lng, lnb, w, bexp)


def _place():
    x, y, c = (lax.axis_index(a) for a in MESH_AXES)
    return x, y, c


def _index(p):
    return 4 * p[0] + 2 * p[1] + p[2]


def _any_spec():
    return pl.BlockSpec(memory_space=pl.ANY)


def _all_gather(name, shards):
    n = len(shards)

    def body(*refs):
        ins, outs = refs[:n], refs[n:2 * n]
        send, recv, loc = refs[2 * n:]
        x, y, c = _place()
        me, sib = (x, y, c), (x, y, 1 - c)
        chips = [(1 - x, y), (x, 1 - y), (1 - x, 1 - y)]

        def cp(w, k, block, to, src=None):
            dst = outs[w].at[_index(block)]
            return pltpu.make_async_remote_copy(
                src_ref=dst if src is None else src, dst_ref=dst,
                send_sem=send.at[w * 7 + k], recv_sem=recv.at[w * 7 + k],
                device_id=to, device_id_type=pl.DeviceIdType.MESH)

        mines, sends = [], []
        for w in range(n):
            m = pltpu.make_async_copy(ins[w], outs[w].at[_index(me)], loc.at[w])
            m.start()
            mines.append(m)
            first = [cp(w, 0, me, sib, src=ins[w])]
            first += [cp(w, 1 + j, me, (*chip, c), src=ins[w]) for j, chip in enumerate(chips)]
            for q in first:
                q.start()
            sends += first
        for j, chip in enumerate(chips):
            for w in range(n):
                cp(w, 1 + j, (*chip, c), me).wait_recv()
                q = cp(w, 4 + j, (*chip, c), sib)
                q.start()
                sends.append(q)
        for w in range(n):
            cp(w, 0, sib, me).wait_recv()
            for j, chip in enumerate(chips):
                cp(w, 4 + j, (*chip, 1 - c), me).wait_recv()
        for q in sends:
            q.wait_send()
        for m in mines:
            m.wait()

    return pl.pallas_call(
        body, name=name,
        in_specs=[_any_spec()] * n, out_specs=[_any_spec()] * n,
        out_shape=[_sds((N_DEV,) + s.shape, s.dtype) for s in shards],
        scratch_shapes=[pltpu.SemaphoreType.DMA((n * 7,)), pltpu.SemaphoreType.DMA((n * 7,)),
                        pltpu.SemaphoreType.DMA((n,))],
        compiler_params=pltpu.CompilerParams(has_side_effects=True),
    )(*shards)


def _peer(r, x, y, c):
    return ((1 - x) if r & 4 else x, (1 - y) if r & 2 else y, (1 - c) if r & 1 else c)


def _send_start(name, src, land, scatter):
    def body(src_ref, land_ref, send, recv, src_thru, land_thru, token):
        x, y, c = _place()
        me = _index((x, y, c))
        for r in range(1, N_DEV):
            p = _peer(r, x, y, c)
            pltpu.make_async_remote_copy(
                src_ref=src_ref.at[_index(p)] if scatter else src_ref, dst_ref=land_ref.at[me],
                send_sem=send.at[r - 1], recv_sem=recv.at[r - 1],
                device_id=p, device_id_type=pl.DeviceIdType.MESH).start()
        token[...] = jnp.zeros_like(token)

    hbm, sem = pl.BlockSpec(memory_space=pltpu.HBM), pl.BlockSpec(memory_space=pltpu.SEMAPHORE)
    return pl.pallas_call(
        body, name=name,
        out_shape=(pltpu.SemaphoreType.DMA((N_DEV - 1,)), pltpu.SemaphoreType.DMA((N_DEV - 1,)),
                   pltpu.HBM(src.shape, src.dtype), pltpu.HBM(land.shape, land.dtype),
                   _sds((SUBLANES, LANES), F32)),
        in_specs=(hbm, hbm), out_specs=(sem, sem, hbm, hbm, pl.BlockSpec(memory_space=pltpu.VMEM)),
        input_output_aliases={0: 2, 1: 3},
        compiler_params=pltpu.CompilerParams(has_side_effects=pltpu.SideEffectType.DATAFLOW_SIDE_EFFECTING),
    )(pltpu.with_memory_space_constraint(src, pltpu.HBM), pltpu.with_memory_space_constraint(land, pltpu.HBM))


def _send_wait(name, started, after, scatter):
    send, recv, src_thru, land_thru, _ = started

    def body(src_ref, land_ref, send_r, recv_r, after_ref, src_out, land_out):
        x, y, c = _place()
        for r in range(1, N_DEV):
            p = _peer(r, x, y, c)
            k = _index(p)
            cp = pltpu.make_async_remote_copy(
                src_ref=src_ref.at[k] if scatter else src_ref, dst_ref=land_ref.at[k],
                send_sem=send_r.at[r - 1], recv_sem=recv_r.at[r - 1],
                device_id=p, device_id_type=pl.DeviceIdType.MESH)
            cp.wait_send()
            cp.wait_recv()

    hbm, sem = pl.BlockSpec(memory_space=pltpu.HBM), pl.BlockSpec(memory_space=pltpu.SEMAPHORE)
    return pl.pallas_call(
        body, name=name,
        out_shape=(pltpu.HBM(src_thru.shape, src_thru.dtype), pltpu.HBM(land_thru.shape, land_thru.dtype)),
        in_specs=(hbm, hbm, sem, sem, _any_spec()), out_specs=(hbm, hbm),
        input_output_aliases={0: 0, 1: 1},
        compiler_params=pltpu.CompilerParams(has_side_effects=pltpu.SideEffectType.DATAFLOW_SIDE_EFFECTING),
    )(src_thru, land_thru, send, recv, after)


def _own_block(blocks, block, me):
    return lax.dynamic_update_index_in_dim(blocks, block, me, 0)


def _adamw(name, parts, w, m, v):
    rows, cols = w.shape
    br = min(ADAM_ROWS, rows)
    c1 = 1.0 / (1.0 - ADAM_B1 ** ADAM_STEP)
    c2 = 1.0 / (1.0 - ADAM_B2 ** ADAM_STEP)

    def body(p_ref, w_ref, m_ref, v_ref, g_o, d_o, m_o, v_o):
        g = p_ref[0].astype(F32)
        for k in range(1, N_DEV):
            g = g + p_ref[k].astype(F32)
        mn = ADAM_B1 * m_ref[...] + (1.0 - ADAM_B1) * g
        vn = ADAM_B2 * v_ref[...] + (1.0 - ADAM_B2) * (g * g)
        g_o[...] = g
        m_o[...] = mn
        v_o[...] = vn
        d_o[...] = -ADAM_LR * ((mn * c1) / (jnp.sqrt(vn * c2) + ADAM_EPS) + ADAM_WD * w_ref[...])

    blk = _bs((br, cols), lambda i: (i, 0))
    out = _sds((rows, cols), F32)
    return pl.pallas_call(
        body, name=name, grid=(rows // br,),
        in_specs=[_bs((N_DEV, br, cols), lambda i: (0, i, 0)), blk, blk, blk],
        out_specs=[blk] * 4, out_shape=[out] * 4,
        compiler_params=_params(("parallel",)),
    )(parts, w, m, v)


def _pack(arrs):
    tile = SUBLANES * LANES
    flat = []
    for a in arrs:
        f = a.reshape(-1).astype(F32)
        pad = (-f.shape[0]) % tile
        flat.append(jnp.pad(f, (0, pad)) if pad else f)
    total = sum(f.shape[0] for f in flat)
    tail = (-total) % (ADAM_ROWS * LANES)
    if tail:
        flat.append(jnp.zeros((tail,), F32))
    return jnp.concatenate(flat).reshape(-1, LANES)


def _unpack(buf, like):
    tile = SUBLANES * LANES
    flat = buf.reshape(-1)
    out, off = [], 0
    for a in like:
        n = math.prod(a.shape)
        out.append(flat[off:off + n].reshape(a.shape))
        off += n + ((-n) % tile)
    return out


def _to_block_b(b):
    gl = LANES // SSM_GROUP
    tb = b.reshape(N_SLAB, gl, SSM_STATE, SSM_GROUP).transpose(0, 1, 3, 2)
    eye = jnp.eye(gl, dtype=F32)
    return (tb[:, :, :, None, :] * eye[None, :, None, :, None]).reshape(N_SLAB, LANES, SLAB_STATE)


def _from_block_b(bx):
    gl = LANES // SSM_GROUP
    d = jnp.einsum("jghgp->jgph", bx.reshape(N_SLAB, gl, SSM_GROUP, gl, SSM_STATE))
    return d.reshape(N_SLAB * gl, SSM_STATE, SSM_GROUP)


def _to_block_c(cm):
    gl = LANES // SSM_GROUP
    tc = cm.reshape(N_SLAB, gl, SSM_GROUP, SSM_STATE).transpose(0, 1, 3, 2)
    eye = jnp.eye(gl, dtype=F32)
    return (tc[:, :, :, None, :] * eye[None, :, None, :, None]).reshape(N_SLAB, SLAB_STATE, LANES)


def _from_block_c(cx):
    gl = LANES // SSM_GROUP
    d = jnp.einsum("jgpgh->jghp", cx.reshape(N_SLAB, gl, SSM_STATE, gl, SSM_GROUP))
    return d.reshape(N_SLAB * gl, SSM_GROUP, SSM_STATE)


def kernel(x, norm_mix_g, w_in, ssm_a_re, ssm_a_im, ssm_b_re, ssm_b_im, ssm_c_re, ssm_c_im, ssm_d, ssm_log_dt, ssm_glu_w, ssm_glu_b, sgu_ln_g, sgu_ln_b, sgu_w, sgu_b, out_norm_ssm_g, out_norm_sgu_g, w_out, norm_mlp_g, w_up, w_down, norm_final_g, loss_target, m_norm_mix_g, m_w_in, m_ssm_a_re, m_ssm_a_im, m_ssm_b_re, m_ssm_b_im, m_ssm_c_re, m_ssm_c_im, m_ssm_d, m_ssm_log_dt, m_ssm_glu_w, m_ssm_glu_b, m_sgu_ln_g, m_sgu_ln_b, m_sgu_w, m_sgu_b, m_out_norm_ssm_g, m_out_norm_sgu_g, m_w_out, m_norm_mlp_g, m_w_up, m_w_down, m_norm_final_g, v_norm_mix_g, v_w_in, v_ssm_a_re, v_ssm_a_im, v_ssm_b_re, v_ssm_b_im, v_ssm_c_re, v_ssm_c_im, v_ssm_d, v_ssm_log_dt, v_ssm_glu_w, v_ssm_glu_b, v_sgu_ln_g, v_sgu_ln_b, v_sgu_w, v_sgu_b, v_out_norm_ssm_g, v_out_norm_sgu_g, v_w_out, v_norm_mlp_g, v_w_up, v_w_down, v_norm_final_g):
    given = dict(locals())
    names = ["norm_mix_g", "w_in", "ssm_a_re", "ssm_a_im", "ssm_b_re", "ssm_b_im", "ssm_c_re", "ssm_c_im",
             "ssm_d", "ssm_log_dt", "ssm_glu_w", "ssm_glu_b", "sgu_ln_g", "sgu_ln_b", "sgu_w", "sgu_b",
             "out_norm_ssm_g", "out_norm_sgu_g", "w_out", "norm_mlp_g", "w_up", "w_down", "norm_final_g"]
    big = ["w_in", "ssm_glu_w", "w_out", "w_up", "w_down"]
    small = [n for n in names if n not in big]

    d = D_MODEL
    t = x.shape[1]
    tb = min(1024, t)
    xs = x[0]
    target = loss_target[0]
    nsh_in = w_in.shape[2]
    nsh_up = w_up.shape[2]
    d_ff = nsh_up * N_DEV
    n_in = nsh_in * N_DEV

    me = _index(_place())
    gathers = {}
    for n in big:
        shard = given[n][0].astype(BF16)
        gathers[n] = _send_start("gather_start_" + n, shard, lax.empty((N_DEV,) + shard.shape, BF16), False)
    tokens = [gathers[n][4] for n in big]

    def gathered(n, after):
        shard, blocks = _send_wait("gather_wait_" + n, gathers[n], after, False)
        return _own_block(blocks, shard, me)

    nst = N_SLAB * SLAB_STATE
    are, aim = ssm_a_re.reshape(1, nst), ssm_a_im.reshape(1, nst)
    ldt = jnp.repeat(ssm_log_dt[0], SSM_STATE).reshape(1, nst)
    bxr, bxi = _to_block_b(ssm_b_re[0]), _to_block_b(ssm_b_im[0])
    cxr, cxi = _to_block_c(ssm_c_re[0]), _to_block_c(ssm_c_im[0])
    dvec = ssm_d.reshape(1, SSM_WIDTH)
    bre, bim, cre, cimn, apr, api, air, aii = _ssm_prep(are, aim, ldt, bxr, bxi, cxr, cxi)
    tabs = (bre, bim, cre, cimn, apr, api, air, aii)

    h1 = _rms_fwd("norm_mix", xs, norm_mix_g, deps=tokens)
    wg_in = gathered("w_in", h1)
    (z,) = _mm("in_proj", "nn", (t // tb, N_DEV, 1),
               (h1, _bs((tb, d), lambda i, j, k: (i, 0))),
               (wg_in, _bs((None, d, nsh_in), lambda i, j, k: (j, 0, 0))),
               [(_sds((t, n_in), F32), _bs((tb, nsh_in), lambda i, j, k: (i, j)))])
    y_pre, yg_b, p_re, p_im = _ssm_fwd(z, dvec, *tabs)

    def glu_ep(acc, yp, b):
        gate = _sigmoid(acc + b)
        return _gelu(yp) * gate, gate

    hw = SSM_WIDTH // 2
    wg_glu = gathered("ssm_glu_w", yg_b).reshape(SSM_WIDTH, SSM_WIDTH)
    tile_g = _bs((tb, hw), lambda i, j, k: (i, j))
    y_ssm, gate = _mm("glu", "nn", (t // tb, 2, 1),
                      (yg_b, _bs((tb, SSM_WIDTH), lambda i, j, k: (i, 0))),
                      (wg_glu, _bs((SSM_WIDTH, hw), lambda i, j, k: (0, j))),
                      [(_sds((t, SSM_WIDTH), F32), tile_g), (_sds((t, SSM_WIDTH), F32), tile_g)],
                      extras=[(y_pre, tile_g), (ssm_glu_b, _bs((1, hw), lambda i, j, k: (0, j)))],
                      epilogue=glu_ep)

    sgu_bexp = jnp.broadcast_to(sgu_b[0][:, :, None], (SGU_HEADS, SGU_CHUNK, SGU_CHUNK))
    y_sgu = _sgu_fwd(z, sgu_ln_g, sgu_ln_b, sgu_w[0], sgu_bexp)
    mixed = _mix_norm(y_ssm, y_sgu, out_norm_ssm_g, out_norm_sgu_g)

    bn_o = 512
    wg_out = gathered("w_out", mixed).reshape(d, d)
    tile_o = _bs((tb, bn_o), lambda i, j, k: (i, j))
    (x2,) = _mm("out_proj", "nn", (t // tb, d // bn_o, 1),
                (mixed, _bs((tb, d), lambda i, j, k: (i, 0))),
                (wg_out, _bs((d, bn_o), lambda i, j, k: (0, j))),
                [(_sds((t, d), F32), tile_o)],
                extras=[(xs, tile_o)], epilogue=lambda acc, r: (acc + r,))
    h2 = _rms_fwd("norm_mlp", x2, norm_mlp_g)

    def up_ep(acc):
        r = jnp.maximum(acc, 0.0)
        return r * r, r

    tile_f = _bs((tb, nsh_up), lambda i, j, k: (i, j))
    wg_up = gathered("w_up", h2)
    f_act, r_act = _mm("mlp_up", "nn", (t // tb, N_DEV, 1),
                       (h2, _bs((tb, d), lambda i, j, k: (i, 0))),
                       (wg_up, _bs((None, d, nsh_up), lambda i, j, k: (j, 0, 0))),
                       [(_sds((t, d_ff), BF16), tile_f), (_sds((t, d_ff), BF16), tile_f)],
                       epilogue=up_ep)
    bk_d = 2048
    wg_down = gathered("w_down", f_act).reshape(d_ff, d)
    (x3,) = _mm("mlp_down", "nn", (t // tb, d // bn_o, d_ff // bk_d),
                (f_act, _bs((tb, bk_d), lambda i, j, k: (i, k))),
                (wg_down, _bs((bk_d, bn_o), lambda i, j, k: (k, j))),
                [(_sds((t, d), F32), tile_o)],
                extras=[(x2, tile_o)], epilogue=lambda acc, r: (acc + r,))

    dx3, dx3_b, g_final, err2 = _final_loss(x3, target, norm_final_g.reshape(1, d))
    loss = lax.psum(0.5 * jnp.sum(err2) / d, MESH_AXES)

    sends = {}

    def send_grad(n, g):
        sends[n] = _send_start("grad_start_" + n, g, lax.empty(g.shape, BF16), True)
        return [sends[n][4]]

    bn_a = 1024
    tile_a = _bs((tb, bn_a), lambda i, j, k: (i, j))
    (da,) = _mm("mlp_down_dx", "nt", (t // tb, d_ff // bn_a, 1),
                (dx3_b, _bs((tb, d), lambda i, j, k: (i, 0))),
                (wg_down, _bs((bn_a, d), lambda i, j, k: (j, 0))),
                [(_sds((t, d_ff), BF16), tile_a)],
                extras=[(r_act, tile_a)], epilogue=lambda acc, r: (acc * (2.0 * r.astype(F32)),))
    sq = 1024
    (gw_down,) = _mm("mlp_down_dw", "tn", (d_ff // sq, d // sq, t // tb),
                     (f_act, _bs((tb, sq), lambda i, j, k: (k, i))),
                     (dx3_b, _bs((tb, sq), lambda i, j, k: (k, j))),
                     [(_sds((d_ff, d), BF16), _bs((sq, sq), lambda i, j, k: (i, j)))])
    sent = send_grad("w_down", gw_down.reshape(N_DEV, -1, d))
    (gw_up,) = _mm("mlp_up_dw", "tn", (d // sq, d_ff // nsh_up, t // tb),
                   (h2, _bs((tb, sq), lambda i, j, k: (k, i))),
                   (da, _bs((tb, nsh_up), lambda i, j, k: (k, j))),
                   [(_sds((N_DEV, d, nsh_up), BF16), _bs((None, sq, nsh_up), lambda i, j, k: (j, i, 0)))],
                   deps=sent)
    sent = send_grad("w_up", gw_up)
    (dh2,) = _mm("mlp_up_dx", "nt", (t // tb, d // sq, N_DEV),
                 (da, _bs((tb, nsh_up), lambda i, j, k: (i, k))),
                 (wg_up, _bs((None, sq, nsh_up), lambda i, j, k: (k, j, 0))),
                 [(_sds((t, d), F32), _bs((tb, sq), lambda i, j, k: (i, j)))], deps=sent)
    dx2, dx2_b, g_norm_mlp = _rms_bwd("norm_mlp_bwd", dh2, 0, x2, norm_mlp_g, dres=dx3, want_bf16=True)

    (dmixed,) = _mm("out_proj_dx", "nt", (t // tb, d // sq, 1),
                    (dx2_b, _bs((tb, d), lambda i, j, k: (i, 0))),
                    (wg_out, _bs((sq, d), lambda i, j, k: (j, 0))),
                    [(_sds((t, d), F32), _bs((tb, sq), lambda i, j, k: (i, j)))])
    (gw_out,) = _mm("out_proj_dw", "tn", (d // sq, d // sq, t // tb),
                    (mixed, _bs((tb, sq), lambda i, j, k: (k, i))),
                    (dx2_b, _bs((tb, sq), lambda i, j, k: (k, j))),
                    [(_sds((d, d), BF16), _bs((sq, sq), lambda i, j, k: (i, j)))])
    sent = send_grad("w_out", gw_out.reshape(N_DEV, -1, d))
    dy_ssm, g_onorm_ssm = _rms_bwd("out_norm_ssm_bwd", dmixed, 0, y_ssm, out_norm_ssm_g, deps=sent)
    dy_sgu, g_onorm_sgu = _rms_bwd("out_norm_sgu_bwd", dmixed, 1, y_sgu, out_norm_sgu_g)

    dpre_b, g_glu_b = _glu_pre_bwd(dy_ssm, y_pre, gate)
    (gw_glu,) = _mm("glu_dw", "tn", (1, 1, t // tb),
                    (yg_b, _bs((tb, SSM_WIDTH), lambda i, j, k: (k, 0))),
                    (dpre_b, _bs((tb, SSM_WIDTH), lambda i, j, k: (k, 0))),
                    [(_sds((SSM_WIDTH, SSM_WIDTH), BF16), _bs((SSM_WIDTH, SSM_WIDTH), lambda i, j, k: (0, 0)))])
    sent = send_grad("ssm_glu_w", gw_glu.reshape(N_DEV, -1, SSM_WIDTH))
    (dy_pre,) = _mm("glu_dx", "nt", (t // tb, 2, 1),
                    (dpre_b, _bs((tb, SSM_WIDTH), lambda i, j, k: (i, 0))),
                    (wg_glu, _bs((hw, SSM_WIDTH), lambda i, j, k: (j, 0))),
                    [(_sds((t, SSM_WIDTH), F32), tile_g)],
                    extras=[(dy_ssm, tile_g), (gate, tile_g), (y_pre, tile_g)],
                    epilogue=lambda acc, dy, gt, yp: ((dy * gt + acc) * _gelu_grad(yp),), deps=sent)
    du_b, dbre, dbim, dcre, dcimn, q_re, q_im, dd = _ssm_bwd(dy_pre, z, p_re, p_im, dvec, *tabs)
    dare, daim, dldt, dbxr, dbxi = _ssm_prep_bwd(are, aim, ldt, bxr, bxi, dbre, dbim,
                                                 q_re.reshape(1, nst), q_im.reshape(1, nst))

    dz_b, g_ln_g, g_ln_b, g_sgu_w, g_sgu_bx = _sgu_bwd(dy_sgu, du_b, z, sgu_ln_g, sgu_ln_b, sgu_w[0], sgu_bexp)
    (gw_in,) = _mm("in_proj_dw", "tn", (d // sq, N_DEV, t // tb),
                   (h1, _bs((tb, sq), lambda i, j, k: (k, i))),
                   (dz_b, _bs((tb, nsh_in), lambda i, j, k: (k, j))),
                   [(_sds((N_DEV, d, nsh_in), BF16), _bs((None, sq, nsh_in), lambda i, j, k: (j, i, 0)))])
    sent = send_grad("w_in", gw_in)
    (dh1,) = _mm("in_proj_dx", "nt", (t // tb, d // sq, N_DEV),
                 (dz_b, _bs((tb, nsh_in), lambda i, j, k: (i, k))),
                 (wg_in, _bs((None, sq, nsh_in), lambda i, j, k: (k, j, 0))),
                 [(_sds((t, d), F32), _bs((tb, sq), lambda i, j, k: (i, j)))], deps=sent)
    grad_x, g_norm_mix = _rms_bwd("norm_mix_bwd", dh1, 0, xs, norm_mix_g, dres=dx2)

    local_small = {
        "norm_mix_g": g_norm_mix, "ssm_a_re": dare, "ssm_a_im": daim,
        "ssm_b_re": _from_block_b(dbxr), "ssm_b_im": _from_block_b(dbxi),
        "ssm_c_re": _from_block_c(dcre), "ssm_c_im": -_from_block_c(dcimn),
        "ssm_d": dd, "ssm_log_dt": dldt.reshape(-1, SSM_STATE).sum(axis=-1),
        "ssm_glu_b": g_glu_b, "sgu_ln_g": g_ln_g, "sgu_ln_b": g_ln_b, "sgu_w": g_sgu_w,
        "sgu_b": g_sgu_bx[:, :, 0], "out_norm_ssm_g": g_onorm_ssm, "out_norm_sgu_g": g_onorm_sgu,
        "norm_mlp_g": g_norm_mlp, "norm_final_g": g_final,
    }
    (small_parts,) = _all_gather("gather_small_grads", [_pack([local_small[n] for n in small])])

    grads, deltas, new_m, new_v = {}, {}, {}, {}
    for n in big:
        w2 = given[n][0]
        sent_blocks, landed = _send_wait("grad_wait_" + n, sends[n], grad_x, True)
        parts = _own_block(landed, lax.dynamic_index_in_dim(sent_blocks, me, 0, keepdims=False), me)
        res = _adamw("adamw_" + n, parts, w2, given["m_" + n][0], given["v_" + n][0])
        grads[n], deltas[n], new_m[n], new_v[n] = [r.reshape(given[n].shape) for r in res]
    like = [given[n] for n in small]
    res = _adamw("adamw_small", small_parts, _pack(like), _pack([given["m_" + n] for n in small]),
                 _pack([given["v_" + n] for n in small]))
    for store, buf in zip((grads, deltas, new_m, new_v), res):
        for n, a in zip(small, _unpack(buf, like)):
            store[n] = a

    return (loss, grad_x.reshape(x.shape), *[grads[n] for n in names], *[deltas[n] for n in names],
            *[new_m[n] for n in names], *[new_v[n] for n in names])
```

```python
import functools
import math

import jax
import jax.numpy as jnp
from jax import lax
from jax.experimental import pallas as pl
from jax.experimental.pallas import tpu as pltpu

F32, BF16 = jnp.float32, jnp.bfloat16
EPS = 1e-6
N_DEV = 8
D_MODEL = 2048
SSM_WIDTH = 1024
SSM_GROUP = 16
SSM_STATE = 64
SGU_HEADS = 8
SGU_CHUNK = 128
LANES = 128
SUBLANES = 8
N_SLAB = SSM_WIDTH // LANES
SLAB_STATE = (LANES // SSM_GROUP) * SSM_STATE
SCAN_BLOCK = 128
SLABS_PER_STEP = 4
VMEM_LIMIT = 48 * 1024 * 1024
ROW_BLOCK = 256
ADAM_ROWS = 128
MESH_AXES = ("x", "y", "c")

ADAM_LR, ADAM_B1, ADAM_B2, ADAM_EPS, ADAM_WD, ADAM_STEP = 0.001, 0.9, 0.999, 1e-08, 0.01, 10

_GELU_C0 = math.sqrt(2.0 / math.pi)
_GELU_C1 = 0.044715


def _gelu(v):
    return 0.5 * v * (1.0 + jnp.tanh(_GELU_C0 * (v + _GELU_C1 * v * v * v)))


def _gelu_grad(v):
    th = jnp.tanh(_GELU_C0 * (v + _GELU_C1 * v * v * v))
    return 0.5 * (1.0 + th) + 0.5 * v * (1.0 - th * th) * _GELU_C0 * (1.0 + 3.0 * _GELU_C1 * v * v)


def _sigmoid(v):
    return 1.0 / (1.0 + jnp.exp(-v))


def _params(sem=None):
    return pltpu.CompilerParams(dimension_semantics=sem, vmem_limit_bytes=VMEM_LIMIT)


def _dot(a, b, mode="nn"):
    dims = {"nn": ((1,), (0,)), "nt": ((1,), (1,)), "tn": ((0,), (0,))}[mode]
    return lax.dot_general(a, b, (dims, ((), ())), preferred_element_type=F32)


def _mm(name, mode, grid, a, b, outs, extras=(), epilogue=None, deps=()):
    nk = grid[2]
    n_ex, n_out, n_dep = len(extras), len(outs), len(deps)
    acc_shape = tuple(d for d in outs[0][1].block_shape if d is not None)

    def body(*refs):
        a_ref, b_ref = refs[0], refs[1]
        ex = refs[2:2 + n_ex]
        out_refs = refs[2 + n_ex + n_dep:2 + n_ex + n_dep + n_out]
        acc = refs[-1]
        k = pl.program_id(2)

        @pl.when(k == 0)
        def _():
            acc[...] = jnp.zeros_like(acc)

        acc[...] += _dot(a_ref[...], b_ref[...], mode)

        @pl.when(k == nk - 1)
        def _():
            res = acc[...]
            res = (res,) if epilogue is None else epilogue(res, *[e[...] for e in ex])
            for o, r in zip(out_refs, res):
                o[...] = r.astype(o.dtype)

    res = pl.pallas_call(
        body, name=name, grid=grid,
        in_specs=[a[1], b[1]] + [e[1] for e in extras] + [_any_spec()] * n_dep,
        out_specs=[o[1] for o in outs],
        out_shape=[o[0] for o in outs],
        scratch_shapes=[pltpu.VMEM(acc_shape, F32)],
        compiler_params=_params(("parallel", "parallel", "arbitrary")),
    )(a[0], b[0], *[e[0] for e in extras], *deps)
    return res


def _sds(shape, dtype):
    return jax.ShapeDtypeStruct(shape, dtype)


def _bs(shape, fn):
    return pl.BlockSpec(shape, fn)


def _rms_fwd(name, x, g, deps=()):
    t, w = x.shape
    br = min(ROW_BLOCK, t)

    def body(*refs):
        x_ref, g_ref, o_ref = refs[0], refs[1], refs[-1]
        xv = x_ref[...]
        r = lax.rsqrt(jnp.mean(xv * xv, axis=-1, keepdims=True) + EPS)
        o_ref[...] = (xv * r * g_ref[...]).astype(BF16)

    return pl.pallas_call(
        body, name=name, grid=(t // br,),
        in_specs=[_bs((br, w), lambda i: (i, 0)), _bs((1, w), lambda i: (0, 0))] + [_any_spec()] * len(deps),
        out_specs=_bs((br, w), lambda i: (i, 0)),
        out_shape=_sds((t, w), BF16),
        compiler_params=_params(("parallel",)),
    )(x, g, *deps)


def _mix_norm(ya, yb, ga, gb):
    t, w = ya.shape
    br = min(ROW_BLOCK, t)

    def body(a_ref, b_ref, ga_ref, gb_ref, o_ref):
        for src, g_ref, col in ((a_ref, ga_ref, 0), (b_ref, gb_ref, w)):
            v = src[...]
            r = lax.rsqrt(jnp.mean(v * v, axis=-1, keepdims=True) + EPS)
            o_ref[:, col:col + w] = (v * r * g_ref[...]).astype(BF16)

    row = _bs((br, w), lambda i: (i, 0))
    vec = _bs((1, w), lambda i: (0, 0))
    return pl.pallas_call(
        body, name="mix_norm", grid=(t // br,),
        in_specs=[row, row, vec, vec],
        out_specs=_bs((br, 2 * w), lambda i: (i, 0)),
        out_shape=_sds((t, 2 * w), BF16),
        compiler_params=_params(("parallel",)),
    )(ya, yb, ga, gb)


def _rms_bwd(name, dh, dh_col, x, g, dres=None, want_bf16=False, deps=()):
    t, w = x.shape
    br = min(ROW_BLOCK, t)
    has_res = dres is not None

    def body(*refs):
        dh_ref, x_ref, g_ref = refs[0], refs[1], refs[2]
        pos = 3
        res_ref = refs[pos] if has_res else None
        pos += int(has_res) + len(deps)
        dx_ref = refs[pos]
        dxb_ref = refs[pos + 1] if want_bf16 else None
        dg_ref = refs[-1]
        xv, dy = x_ref[...], dh_ref[...]
        r = lax.rsqrt(jnp.mean(xv * xv, axis=-1, keepdims=True) + EPS)
        xhat = xv * r
        dxhat = dy * g_ref[...]
        dx = r * (dxhat - xhat * jnp.mean(dxhat * xhat, axis=-1, keepdims=True))
        if has_res:
            dx = dx + res_ref[...]
        dx_ref[...] = dx
        if want_bf16:
            dxb_ref[...] = dx.astype(BF16)

        @pl.when(pl.program_id(0) == 0)
        def _():
            dg_ref[...] = jnp.zeros_like(dg_ref)

        dg_ref[...] += jnp.sum(dy * xhat, axis=0, keepdims=True)

    row = _bs((br, w), lambda i: (i, 0))
    vec = _bs((1, w), lambda i: (0, 0))
    in_specs = ([_bs((br, w), lambda i: (i, dh_col)), row, vec] + ([row] if has_res else [])
                + [_any_spec()] * len(deps))
    out_specs = [row] + ([row] if want_bf16 else []) + [vec]
    out_shape = [_sds((t, w), F32)] + ([_sds((t, w), BF16)] if want_bf16 else []) + [_sds((1, w), F32)]
    args = [dh, x, g] + ([dres] if has_res else []) + list(deps)
    return pl.pallas_call(
        body, name=name, grid=(t // br,),
        in_specs=in_specs, out_specs=out_specs, out_shape=out_shape,
        compiler_params=_params(("arbitrary",)),
    )(*args)


def _final_loss(x3, target, g):
    t, w = x3.shape
    br = min(ROW_BLOCK, t)

    def body(x_ref, tg_ref, g_ref, dx_ref, dxb_ref, dg_ref, l_ref):
        xv = x_ref[...]
        r = lax.rsqrt(jnp.mean(xv * xv, axis=-1, keepdims=True) + EPS)
        xhat = xv * r
        err = xhat * g_ref[...] - tg_ref[...]
        dy = err * (1.0 / w)
        dxhat = dy * g_ref[...]
        dx = r * (dxhat - xhat * jnp.mean(dxhat * xhat, axis=-1, keepdims=True))
        dx_ref[...] = dx
        dxb_ref[...] = dx.astype(BF16)

        @pl.when(pl.program_id(0) == 0)
        def _():
            dg_ref[...] = jnp.zeros_like(dg_ref)
            l_ref[...] = jnp.zeros_like(l_ref)

        dg_ref[...] += jnp.sum(dy * xhat, axis=0, keepdims=True)
        l_ref[...] += jnp.sum(err * err, axis=0, keepdims=True)

    row = _bs((br, w), lambda i: (i, 0))
    vec = _bs((1, w), lambda i: (0, 0))
    return pl.pallas_call(
        body, name="final_loss", grid=(t // br,),
        in_specs=[row, row, vec], out_specs=[row, row, vec, vec],
        out_shape=[_sds((t, w), F32), _sds((t, w), BF16), _sds((1, w), F32), _sds((1, w), F32)],
        compiler_params=_params(("arbitrary",)),
    )(x3, target, g)


def _glu_pre_bwd(dy, ypre, gate):
    t, w = dy.shape
    br = min(ROW_BLOCK, t)

    def body(dy_ref, y_ref, gt_ref, dp_ref, db_ref):
        gt = gt_ref[...]
        dp = dy_ref[...] * _gelu(y_ref[...]) * gt * (1.0 - gt)
        dp_ref[...] = dp.astype(BF16)

        @pl.when(pl.program_id(0) == 0)
        def _():
            db_ref[...] = jnp.zeros_like(db_ref)

        db_ref[...] += jnp.sum(dp, axis=0, keepdims=True)

    row = _bs((br, w), lambda i: (i, 0))
    vec = _bs((1, w), lambda i: (0, 0))
    return pl.pallas_call(
        body, name="glu_pre_bwd", grid=(t // br,),
        in_specs=[row, row, row], out_specs=[row, vec],
        out_shape=[_sds((t, w), BF16), _sds((1, w), F32)],
        compiler_params=_params(("arbitrary",)),
    )(dy, ypre, gate)


def _prep_math(are, aim, ldt, bxr, bxi):
    dt = jnp.exp(ldt)
    er = jnp.exp(are * dt)
    th = aim * dt
    abr, abi = er * jnp.cos(th), er * jnp.sin(th)
    nr, ni = abr - 1.0, abi
    den = are * are + aim * aim
    cr = (nr * are + ni * aim) / den
    ci = (ni * are - nr * aim) / den
    bbr, bbi = [], []
    for j in range(N_SLAB):
        sl = slice(j * SLAB_STATE, (j + 1) * SLAB_STATE)
        bbr.append(cr[:, sl] * bxr[j] - ci[:, sl] * bxi[j])
        bbi.append(cr[:, sl] * bxi[j] + ci[:, sl] * bxr[j])
    return abr, abi, bbr, bbi


def _ssm_prep(are, aim, ldt, bxr, bxi, cxr, cxi):
    nst = N_SLAB * SLAB_STATE

    def body(are_r, aim_r, ldt_r, bxr_r, bxi_r, cxr_r, cxi_r,
             bre_o, bim_o, cre_o, cimn_o, apr_o, api_o, air_o, aii_o):
        abr, abi, bbr, bbi = _prep_math(are_r[...], aim_r[...], ldt_r[...],
                                        [bxr_r[j] for j in range(N_SLAB)], [bxi_r[j] for j in range(N_SLAB)])
        for j in range(N_SLAB):
            bre_o[j] = bbr[j].astype(BF16)
            bim_o[j] = bbi[j].astype(BF16)
        cre_o[...] = cxr_r[...].astype(BF16)
        cimn_o[...] = (-cxi_r[...]).astype(BF16)

        def step(k, cur):
            cr, ci = cur
            den = cr * cr + ci * ci
            apr_o[pl.ds(k, 1), :] = cr
            api_o[pl.ds(k, 1), :] = ci
            air_o[pl.ds(k, 1), :] = cr / den
            aii_o[pl.ds(k, 1), :] = -ci / den
            return cr * abr - ci * abi, cr * abi + ci * abr

        lax.fori_loop(0, SCAN_BLOCK, step, (jnp.ones((1, nst), F32), jnp.zeros((1, nst), F32)))

    tab = _sds((SCAN_BLOCK, nst), F32)
    return pl.pallas_call(
        body, name="ssm_prep",
        out_shape=[_sds(bxr.shape, BF16), _sds(bxr.shape, BF16), _sds(cxr.shape, BF16), _sds(cxr.shape, BF16),
                   tab, tab, tab, tab],
        compiler_params=_params(),
    )(are, aim, ldt, bxr, bxi, cxr, cxi)


def _ssm_prep_bwd(are, aim, ldt, bxr, bxi, dbre, dbim, qr, qi):
    def body(are_r, aim_r, ldt_r, bxr_r, bxi_r, dbre_r, dbim_r, qr_r, qi_r,
             dare_o, daim_o, dldt_o, dbxr_o, dbxi_o):
        prim = (are_r[...], aim_r[...], ldt_r[...],
                [bxr_r[j] for j in range(N_SLAB)], [bxi_r[j] for j in range(N_SLAB)])
        (abr, abi, _, _), vjp = jax.vjp(_prep_math, *prim)
        den = abr * abr + abi * abi
        q_r, q_i = qr_r[...], qi_r[...]
        gar = (q_r * abr - q_i * abi) / den
        gai = (q_r * abi + q_i * abr) / den
        ct = (gar, gai, [dbre_r[j] for j in range(N_SLAB)], [dbim_r[j] for j in range(N_SLAB)])
        dare, daim, dldt, dbxr, dbxi = vjp(ct)
        dare_o[...] = dare
        daim_o[...] = daim
        dldt_o[...] = dldt
        for j in range(N_SLAB):
            dbxr_o[j] = dbxr[j]
            dbxi_o[j] = dbxi[j]

    row = _sds(are.shape, F32)
    return pl.pallas_call(
        body, name="ssm_prep_bwd",
        out_shape=[row, row, row, _sds(bxr.shape, F32), _sds(bxr.shape, F32)],
        compiler_params=_params(),
    )(are, aim, ldt, bxr, bxi, dbre, dbim, qr, qi)


def _tri(lower):
    r = lax.broadcasted_iota(jnp.int32, (SCAN_BLOCK, SCAN_BLOCK), 0)
    c = lax.broadcasted_iota(jnp.int32, (SCAN_BLOCK, SCAN_BLOCK), 1)
    return jnp.where((r >= c) if lower else (r <= c), 1.0, 0.0).astype(BF16)


def _cumsum_mxu(tri, v):
    hi = v.astype(BF16)
    lo = (v - hi.astype(F32)).astype(BF16)
    return _dot(tri, hi) + _dot(tri, lo)


def _ssm_specs(t):
    nt = t // SCAN_BLOCK
    sps = SLABS_PER_STEP
    tab = _bs((SCAN_BLOCK, sps * SLAB_STATE), lambda j, i: (0, j))
    bmat = _bs((sps, LANES, SLAB_STATE), lambda j, i: (j, 0, 0))
    cmat = _bs((sps, SLAB_STATE, LANES), lambda j, i: (j, 0, 0))
    return nt, tab, bmat, cmat


def _slab_slices(s):
    return slice(s * LANES, (s + 1) * LANES), slice(s * SLAB_STATE, (s + 1) * SLAB_STATE)


def _ssm_fwd(z, dvec, bre, bim, cre, cimn, apr, api, air, aii, deps=()):
    t = z.shape[0]
    nt, tab, bmat, cmat = _ssm_specs(t)
    nst = N_SLAB * SLAB_STATE
    last = SCAN_BLOCK - 1

    sps = SLABS_PER_STEP

    def body(*refs):
        u_ref, d_ref, bre_r, bim_r, cre_r, cimn_r, apr_r, api_r, air_r, aii_r = refs[:10]
        y_ref, yg_ref, pr_ref, pi_ref, car_r, car_i = refs[10 + len(deps):]

        @pl.when(pl.program_id(1) == 0)
        def _():
            car_r[...] = jnp.zeros_like(car_r)
            car_i[...] = jnp.zeros_like(car_i)

        tri = _tri(True)
        for s in range(sps):
            ul, sl = _slab_slices(s)
            u = u_ref[:, ul]
            ub = u.astype(BF16)
            bur, bui = _dot(ub, bre_r[s]), _dot(ub, bim_r[s])
            ir, ii = air_r[:, sl], aii_r[:, sl]
            csr = _cumsum_mxu(tri, ir * bur - ii * bui)
            csi = _cumsum_mxu(tri, ir * bui + ii * bur)
            pr, pi = apr_r[:, sl], api_r[:, sl]
            a_r, a_i = apr_r[1:2, sl], api_r[1:2, sl]
            c_r, c_i = car_r[:, sl], car_i[:, sl]
            wr = csr + (a_r * c_r - a_i * c_i)
            wi = csi + (a_r * c_i + a_i * c_r)
            sr = pr * wr - pi * wi
            si = pr * wi + pi * wr
            car_r[:, sl] = sr[last:last + 1, :]
            car_i[:, sl] = si[last:last + 1, :]
            pr_ref[:, sl] = (sr - bur).astype(BF16)
            pi_ref[:, sl] = (si - bui).astype(BF16)
            y = _dot(sr.astype(BF16), cre_r[s]) + _dot(si.astype(BF16), cimn_r[s]) + d_ref[:, ul] * u
            y_ref[:, ul] = y
            yg_ref[:, ul] = _gelu(y).astype(BF16)

    ublk = _bs((SCAN_BLOCK, sps * LANES), lambda j, i: (i, j))
    sblk = _bs((SCAN_BLOCK, sps * SLAB_STATE), lambda j, i: (i, j))
    return pl.pallas_call(
        body, name="ssm_fwd", grid=(N_SLAB // sps, nt),
        in_specs=[ublk, _bs((1, sps * LANES), lambda j, i: (0, j)), bmat, bmat, cmat, cmat, tab, tab, tab, tab]
        + [_any_spec()] * len(deps),
        out_specs=[ublk, ublk, sblk, sblk],
        out_shape=[_sds((t, SSM_WIDTH), F32), _sds((t, SSM_WIDTH), BF16),
                   _sds((t, nst), BF16), _sds((t, nst), BF16)],
        scratch_shapes=[pltpu.VMEM((1, sps * SLAB_STATE), F32), pltpu.VMEM((1, sps * SLAB_STATE), F32)],
        compiler_params=_params(("parallel", "arbitrary")),
    )(z, dvec, bre, bim, cre, cimn, apr, api, air, aii, *deps)


def _ssm_bwd(gy, z, p_re, p_im, dvec, bre, bim, cre, cimn, apr, api, air, aii):
    t = z.shape[0]
    nt, tab, bmat, cmat = _ssm_specs(t)
    last = SCAN_BLOCK - 1

    def fold(v):
        return v.reshape(SCAN_BLOCK // SUBLANES, SUBLANES, v.shape[-1]).sum(axis=0)

    sps = SLABS_PER_STEP

    def body(g_ref, u_ref, pr_ref, pi_ref, d_ref, bre_r, bim_r, cre_r, cimn_r, apr_r, api_r, air_r, aii_r,
             du_ref, dbre_o, dbim_o, dcre_o, dcimn_o, qr_o, qi_o, dd_o, car_r, car_i, qacc_r, qacc_i, dacc):
        i = pl.program_id(1)

        @pl.when(i == 0)
        def _():
            for ref in (car_r, car_i, qacc_r, qacc_i, dacc, dbre_o, dbim_o, dcre_o, dcimn_o):
                ref[...] = jnp.zeros_like(ref)

        tri = _tri(False)
        for s in range(sps):
            ul, sl = _slab_slices(s)
            g = g_ref[:, ul]
            gb = g.astype(BF16)
            u = u_ref[:, ul]
            ub = u.astype(BF16)
            bur, bui = _dot(ub, bre_r[s]), _dot(ub, bim_r[s])
            p_r, p_i = pr_ref[:, sl].astype(F32), pi_ref[:, sl].astype(F32)
            srb, sib = (p_r + bur).astype(BF16), (p_i + bui).astype(BF16)
            dcre_o[s] += _dot(srb, gb, "tn")
            dcimn_o[s] += _dot(sib, gb, "tn")
            dsr, dsi = _dot(gb, cre_r[s], "nt"), _dot(gb, cimn_r[s], "nt")
            pr, pi = apr_r[:, sl], api_r[:, sl]
            csr = _cumsum_mxu(tri, pr * dsr + pi * dsi)
            csi = _cumsum_mxu(tri, pr * dsi - pi * dsr)
            al_r, al_i = apr_r[last:last + 1, sl], api_r[last:last + 1, sl]
            c_r, c_i = car_r[:, sl], car_i[:, sl]
            wr = csr + (al_r * c_r + al_i * c_i)
            wi = csi + (al_r * c_i - al_i * c_r)
            ir, ii = air_r[:, sl], aii_r[:, sl]
            lr = ir * wr + ii * wi
            li = ir * wi - ii * wr
            a_r, a_i = apr_r[1:2, sl], api_r[1:2, sl]
            car_r[:, sl] = a_r * lr[0:1, :] + a_i * li[0:1, :]
            car_i[:, sl] = a_r * li[0:1, :] - a_i * lr[0:1, :]
            lrb, lib = lr.astype(BF16), li.astype(BF16)
            dbre_o[s] += _dot(ub, lrb, "tn")
            dbim_o[s] += _dot(ub, lib, "tn")
            du = d_ref[:, ul] * g + _dot(lrb, bre_r[s], "nt") + _dot(lib, bim_r[s], "nt")
            du_ref[:, ul] = du.astype(BF16)
            qacc_r[:, sl] += fold(lr * p_r + li * p_i)
            qacc_i[:, sl] += fold(li * p_r - lr * p_i)
            dacc[:, ul] += fold(g * u)

        @pl.when(i == nt - 1)
        def _():
            qr_o[...] = jnp.sum(qacc_r[...], axis=0, keepdims=True)
            qi_o[...] = jnp.sum(qacc_i[...], axis=0, keepdims=True)
            dd_o[...] = jnp.sum(dacc[...], axis=0, keepdims=True)

    rev = lambda j, i: (nt - 1 - i, j)
    ublk = _bs((SCAN_BLOCK, sps * LANES), rev)
    sblk = _bs((SCAN_BLOCK, sps * SLAB_STATE), rev)
    qrow = _bs((1, sps * SLAB_STATE), lambda j, i: (0, j))
    urow = _bs((1, sps * LANES), lambda j, i: (0, j))
    nst = N_SLAB * SLAB_STATE
    return pl.pallas_call(
        body, name="ssm_bwd", grid=(N_SLAB // sps, nt),
        in_specs=[ublk, ublk, sblk, sblk, urow, bmat, bmat, cmat, cmat, tab, tab, tab, tab],
        out_specs=[ublk, bmat, bmat, cmat, cmat, qrow, qrow, urow],
        out_shape=[_sds((t, SSM_WIDTH), BF16),
                   _sds((N_SLAB, LANES, SLAB_STATE), F32), _sds((N_SLAB, LANES, SLAB_STATE), F32),
                   _sds((N_SLAB, SLAB_STATE, LANES), F32), _sds((N_SLAB, SLAB_STATE, LANES), F32),
                   _sds((1, nst), F32), _sds((1, nst), F32), _sds((1, SSM_WIDTH), F32)],
        scratch_shapes=[pltpu.VMEM((1, sps * SLAB_STATE), F32), pltpu.VMEM((1, sps * SLAB_STATE), F32),
                        pltpu.VMEM((SUBLANES, sps * SLAB_STATE), F32), pltpu.VMEM((SUBLANES, sps * SLAB_STATE), F32),
                        pltpu.VMEM((SUBLANES, sps * LANES), F32)],
        compiler_params=_params(("parallel", "arbitrary")),
    )(gy, z, p_re, p_im, dvec, bre, bim, cre, cimn, apr, api, air, aii)


def _sgu_mask():
    r = lax.broadcasted_iota(jnp.int32, (SGU_CHUNK, SGU_CHUNK), 0)
    c = lax.broadcasted_iota(jnp.int32, (SGU_CHUNK, SGU_CHUNK), 1)
    return r >= c


def _sgu_common(zu, zv, lng, lnb):
    us, v = _gelu(zu), _gelu(zv)
    mu = jnp.mean(v, axis=-1, keepdims=True)
    vc = v - mu
    rstd = lax.rsqrt(jnp.mean(vc * vc, axis=-1, keepdims=True) + EPS)
    xhat = vc * rstd
    return us, xhat, rstd, xhat * lng + lnb


def _sgu_fwd(z, lng, lnb, w, bexp):
    t = z.shape[0]
    hd = SGU_CHUNK

    def body(zu_ref, zv_ref, lng_ref, lnb_ref, w_ref, b_ref, y_ref):
        us, _, _, vn = _sgu_common(zu_ref[...], zv_ref[...], lng_ref[...], lnb_ref[...])
        vnb = vn.astype(BF16)
        mask = _sgu_mask()
        for h in range(SGU_HEADS):
            sl = slice(h * hd, (h + 1) * hd)
            wt = jnp.where(mask, w_ref[h], 0.0).astype(BF16)
            y_ref[:, sl] = us[:, sl] * (_dot(wt, vnb[:, sl]) + b_ref[h])

    row = lambda c: _bs((SGU_CHUNK, SSM_WIDTH), lambda i: (i, c))
    vec = _bs((1, SSM_WIDTH), lambda i: (0, 0))
    hmat = _bs((SGU_HEADS, hd, hd), lambda i: (0, 0, 0))
    return pl.pallas_call(
        body, name="sgu_fwd", grid=(t // SGU_CHUNK,),
        in_specs=[row(1), row(2), vec, vec, hmat, hmat],
        out_specs=row(0), out_shape=_sds((t, SSM_WIDTH), F32),
        compiler_params=_params(("parallel",)),
    )(z, z, lng, lnb, w, bexp)


def _sgu_bwd(dy, du_ssm, z, lng, lnb, w, bexp):
    t = z.shape[0]
    hd = SGU_CHUNK
    nc = t // SGU_CHUNK

    def body(dy_ref, dus_ref, zu_ref, zv_ref, lng_ref, lnb_ref, w_ref, b_ref,
             dz_ref, dlng_o, dlnb_o, dw_o, db_o):
        i = pl.program_id(0)

        @pl.when(i == 0)
        def _():
            for ref in (dlng_o, dlnb_o, dw_o, db_o):
                ref[...] = jnp.zeros_like(ref)

        zu, zv = zu_ref[...], zv_ref[...]
        lng = lng_ref[...]
        us, xhat, rstd, vn = _sgu_common(zu, zv, lng, lnb_ref[...])
        vnb = vn.astype(BF16)
        dyv = dy_ref[...]
        mask = _sgu_mask()
        dus_parts, dvn_parts = [], []
        for h in range(SGU_HEADS):
            sl = slice(h * hd, (h + 1) * hd)
            wt = jnp.where(mask, w_ref[h], 0.0).astype(BF16)
            mixed = _dot(wt, vnb[:, sl]) + b_ref[h]
            dus_parts.append(dyv[:, sl] * mixed)
            dmix = dyv[:, sl] * us[:, sl]
            dmb = dmix.astype(BF16)
            db_o[h] += dmix
            dw_o[h] += _dot(dmb, vnb[:, sl], "nt")
            dvn_parts.append(_dot(wt, dmb, "tn"))
        dus = jnp.concatenate(dus_parts, axis=1)
        dvn = jnp.concatenate(dvn_parts, axis=1)
        dlng_o[...] += jnp.sum(dvn * xhat, axis=0, keepdims=True)
        dlnb_o[...] += jnp.sum(dvn, axis=0, keepdims=True)
        dxh = dvn * lng
        dv = rstd * (dxh - jnp.mean(dxh, axis=-1, keepdims=True)
                     - xhat * jnp.mean(dxh * xhat, axis=-1, keepdims=True))
        dz_ref[:, 0:SSM_WIDTH] = dus_ref[...]
        dz_ref[:, SSM_WIDTH:2 * SSM_WIDTH] = (dus * _gelu_grad(zu)).astype(BF16)
        dz_ref[:, 2 * SSM_WIDTH:] = (dv * _gelu_grad(zv)).astype(BF16)

        @pl.when(i == nc - 1)
        def _():
            for h in range(SGU_HEADS):
                dw_o[h] = jnp.where(mask, dw_o[h], 0.0)
                db_o[h] = jnp.broadcast_to(jnp.sum(db_o[h], axis=1, keepdims=True), (hd, hd))

    row = lambda c: _bs((SGU_CHUNK, SSM_WIDTH), lambda i: (i, c))
    vec = _bs((1, SSM_WIDTH), lambda i: (0, 0))
    hmat = _bs((SGU_HEADS, hd, hd), lambda i: (0, 0, 0))
    return pl.pallas_call(
        body, name="sgu_bwd", grid=(nc,),
        in_specs=[row(0), row(0), row(1), row(2), vec, vec, hmat, hmat],
        out_specs=[_bs((SGU_CHUNK, 3 * SSM_WIDTH), lambda i: (i, 0)), vec, vec, hmat, hmat],
        out_shape=[_sds((t, 3 * SSM_WIDTH), BF16), _sds((1, SSM_WIDTH), F32), _sds((1, SSM_WIDTH), F32),
                   _sds((SGU_HEADS, hd, hd), F32), _sds((SGU_HEADS, hd, hd), F32)],
        compiler_params=_params(("arbitrary",)),
    )(dy, du_ssm, z, z, lng, lnb, w, bexp)


def _place():
    x, y, c = (lax.axis_index(a) for a in MESH_AXES)
    return x, y, c


def _index(p):
    return 4 * p[0] + 2 * p[1] + p[2]


def _any_spec():
    return pl.BlockSpec(memory_space=pl.ANY)


def _all_gather(name, shards):
    n = len(shards)

    def body(*refs):
        ins, outs = refs[:n], refs[n:2 * n]
        send, recv, loc = refs[2 * n:]
        x, y, c = _place()
        me, sib = (x, y, c), (x, y, 1 - c)
        chips = [(1 - x, y), (x, 1 - y), (1 - x, 1 - y)]

        def cp(w, k, block, to, src=None):
            dst = outs[w].at[_index(block)]
            return pltpu.make_async_remote_copy(
                src_ref=dst if src is None else src, dst_ref=dst,
                send_sem=send.at[w * 7 + k], recv_sem=recv.at[w * 7 + k],
                device_id=to, device_id_type=pl.DeviceIdType.MESH)

        mines, sends = [], []
        for w in range(n):
            m = pltpu.make_async_copy(ins[w], outs[w].at[_index(me)], loc.at[w])
            m.start()
            mines.append(m)
            first = [cp(w, 0, me, sib, src=ins[w])]
            first += [cp(w, 1 + j, me, (*chip, c), src=ins[w]) for j, chip in enumerate(chips)]
            for q in first:
                q.start()
            sends += first
        for j, chip in enumerate(chips):
            for w in range(n):
                cp(w, 1 + j, (*chip, c), me).wait_recv()
                q = cp(w, 4 + j, (*chip, c), sib)
                q.start()
                sends.append(q)
        for w in range(n):
            cp(w, 0, sib, me).wait_recv()
            for j, chip in enumerate(chips):
                cp(w, 4 + j, (*chip, 1 - c), me).wait_recv()
        for q in sends:
            q.wait_send()
        for m in mines:
            m.wait()

    return pl.pallas_call(
        body, name=name,
        in_specs=[_any_spec()] * n, out_specs=[_any_spec()] * n,
        out_shape=[_sds((N_DEV,) + s.shape, s.dtype) for s in shards],
        scratch_shapes=[pltpu.SemaphoreType.DMA((n * 7,)), pltpu.SemaphoreType.DMA((n * 7,)),
                        pltpu.SemaphoreType.DMA((n,))],
        compiler_params=pltpu.CompilerParams(has_side_effects=True),
    )(*shards)


def _peer(r, x, y, c):
    return ((1 - x) if r & 4 else x, (1 - y) if r & 2 else y, (1 - c) if r & 1 else c)


def _send_start(name, src, land, scatter, after=None):
    n_after = 0 if after is None else 1

    def body(*refs):
        src_ref, land_ref = refs[0], refs[1]
        send, recv, _, _, token = refs[2 + n_after:]
        x, y, c = _place()
        me = _index((x, y, c))
        for r in range(1, N_DEV):
            p = _peer(r, x, y, c)
            pltpu.make_async_remote_copy(
                src_ref=src_ref.at[_index(p)] if scatter else src_ref, dst_ref=land_ref.at[me],
                send_sem=send.at[r - 1], recv_sem=recv.at[r - 1],
                device_id=p, device_id_type=pl.DeviceIdType.MESH).start()
        token[...] = jnp.zeros_like(token)

    hbm, sem = pl.BlockSpec(memory_space=pltpu.HBM), pl.BlockSpec(memory_space=pltpu.SEMAPHORE)
    return pl.pallas_call(
        body, name=name,
        out_shape=(pltpu.SemaphoreType.DMA((N_DEV - 1,)), pltpu.SemaphoreType.DMA((N_DEV - 1,)),
                   pltpu.HBM(src.shape, src.dtype), pltpu.HBM(land.shape, land.dtype),
                   _sds((SUBLANES, LANES), F32)),
        in_specs=(hbm, hbm) + (_any_spec(),) * n_after,
        out_specs=(sem, sem, hbm, hbm, pl.BlockSpec(memory_space=pltpu.VMEM)),
        input_output_aliases={0: 2, 1: 3},
        compiler_params=pltpu.CompilerParams(has_side_effects=pltpu.SideEffectType.DATAFLOW_SIDE_EFFECTING),
    )(pltpu.with_memory_space_constraint(src, pltpu.HBM), pltpu.with_memory_space_constraint(land, pltpu.HBM),
      *([] if after is None else [after]))


def _send_wait(name, started, after, scatter):
    send, recv, src_thru, land_thru, _ = started

    def body(src_ref, land_ref, send_r, recv_r, after_ref, src_out, land_out):
        x, y, c = _place()
        for r in range(1, N_DEV):
            p = _peer(r, x, y, c)
            k = _index(p)
            cp = pltpu.make_async_remote_copy(
                src_ref=src_ref.at[k] if scatter else src_ref, dst_ref=land_ref.at[k],
                send_sem=send_r.at[r - 1], recv_sem=recv_r.at[r - 1],
                device_id=p, device_id_type=pl.DeviceIdType.MESH)
            cp.wait_send()
            cp.wait_recv()

    hbm, sem = pl.BlockSpec(memory_space=pltpu.HBM), pl.BlockSpec(memory_space=pltpu.SEMAPHORE)
    return pl.pallas_call(
        body, name=name,
        out_shape=(pltpu.HBM(src_thru.shape, src_thru.dtype), pltpu.HBM(land_thru.shape, land_thru.dtype)),
        in_specs=(hbm, hbm, sem, sem, _any_spec()), out_specs=(hbm, hbm),
        input_output_aliases={0: 0, 1: 1},
        compiler_params=pltpu.CompilerParams(has_side_effects=pltpu.SideEffectType.DATAFLOW_SIDE_EFFECTING),
    )(src_thru, land_thru, send, recv, after)


def _own_block(blocks, block, me):
    return lax.dynamic_update_index_in_dim(blocks, block, me, 0)


def _adamw(name, parts, w, m, v):
    rows, cols = w.shape
    br = min(ADAM_ROWS, rows)
    c1 = 1.0 / (1.0 - ADAM_B1 ** ADAM_STEP)
    c2 = 1.0 / (1.0 - ADAM_B2 ** ADAM_STEP)

    def body(p_ref, w_ref, m_ref, v_ref, g_o, d_o, m_o, v_o):
        g = p_ref[0].astype(F32)
        for k in range(1, N_DEV):
            g = g + p_ref[k].astype(F32)
        mn = ADAM_B1 * m_ref[...] + (1.0 - ADAM_B1) * g
        vn = ADAM_B2 * v_ref[...] + (1.0 - ADAM_B2) * (g * g)
        g_o[...] = g
        m_o[...] = mn
        v_o[...] = vn
        d_o[...] = -ADAM_LR * ((mn * c1) / (jnp.sqrt(vn * c2) + ADAM_EPS) + ADAM_WD * w_ref[...])

    blk = _bs((br, cols), lambda i: (i, 0))
    out = _sds((rows, cols), F32)
    return pl.pallas_call(
        body, name=name, grid=(rows // br,),
        in_specs=[_bs((N_DEV, br, cols), lambda i: (0, i, 0)), blk, blk, blk],
        out_specs=[blk] * 4, out_shape=[out] * 4,
        compiler_params=_params(("parallel",)),
    )(parts, w, m, v)


def _pack(arrs):
    tile = SUBLANES * LANES
    flat = []
    for a in arrs:
        f = a.reshape(-1).astype(F32)
        pad = (-f.shape[0]) % tile
        flat.append(jnp.pad(f, (0, pad)) if pad else f)
    total = sum(f.shape[0] for f in flat)
    tail = (-total) % (ADAM_ROWS * LANES)
    if tail:
        flat.append(jnp.zeros((tail,), F32))
    return jnp.concatenate(flat).reshape(-1, LANES)


def _unpack(buf, like):
    tile = SUBLANES * LANES
    flat = buf.reshape(-1)
    out, off = [], 0
    for a in like:
        n = math.prod(a.shape)
        out.append(flat[off:off + n].reshape(a.shape))
        off += n + ((-n) % tile)
    return out


def _to_block_b(b):
    gl = LANES // SSM_GROUP
    tb = b.reshape(N_SLAB, gl, SSM_STATE, SSM_GROUP).transpose(0, 1, 3, 2)
    eye = jnp.eye(gl, dtype=F32)
    return (tb[:, :, :, None, :] * eye[None, :, None, :, None]).reshape(N_SLAB, LANES, SLAB_STATE)


def _from_block_b(bx):
    gl = LANES // SSM_GROUP
    d = jnp.einsum("jghgp->jgph", bx.reshape(N_SLAB, gl, SSM_GROUP, gl, SSM_STATE))
    return d.reshape(N_SLAB * gl, SSM_STATE, SSM_GROUP)


def _to_block_c(cm):
    gl = LANES // SSM_GROUP
    tc = cm.reshape(N_SLAB, gl, SSM_GROUP, SSM_STATE).transpose(0, 1, 3, 2)
    eye = jnp.eye(gl, dtype=F32)
    return (tc[:, :, :, None, :] * eye[None, :, None, :, None]).reshape(N_SLAB, SLAB_STATE, LANES)


def _from_block_c(cx):
    gl = LANES // SSM_GROUP
    d = jnp.einsum("jgpgh->jghp", cx.reshape(N_SLAB, gl, SSM_STATE, gl, SSM_GROUP))
    return d.reshape(N_SLAB * gl, SSM_GROUP, SSM_STATE)


def kernel(x, norm_mix_g, w_in, ssm_a_re, ssm_a_im, ssm_b_re, ssm_b_im, ssm_c_re, ssm_c_im, ssm_d, ssm_log_dt, ssm_glu_w, ssm_glu_b, sgu_ln_g, sgu_ln_b, sgu_w, sgu_b, out_norm_ssm_g, out_norm_sgu_g, w_out, norm_mlp_g, w_up, w_down, norm_final_g, loss_target, m_norm_mix_g, m_w_in, m_ssm_a_re, m_ssm_a_im, m_ssm_b_re, m_ssm_b_im, m_ssm_c_re, m_ssm_c_im, m_ssm_d, m_ssm_log_dt, m_ssm_glu_w, m_ssm_glu_b, m_sgu_ln_g, m_sgu_ln_b, m_sgu_w, m_sgu_b, m_out_norm_ssm_g, m_out_norm_sgu_g, m_w_out, m_norm_mlp_g, m_w_up, m_w_down, m_norm_final_g, v_norm_mix_g, v_w_in, v_ssm_a_re, v_ssm_a_im, v_ssm_b_re, v_ssm_b_im, v_ssm_c_re, v_ssm_c_im, v_ssm_d, v_ssm_log_dt, v_ssm_glu_w, v_ssm_glu_b, v_sgu_ln_g, v_sgu_ln_b, v_sgu_w, v_sgu_b, v_out_norm_ssm_g, v_out_norm_sgu_g, v_w_out, v_norm_mlp_g, v_w_up, v_w_down, v_norm_final_g):
    given = dict(locals())
    names = ["norm_mix_g", "w_in", "ssm_a_re", "ssm_a_im", "ssm_b_re", "ssm_b_im", "ssm_c_re", "ssm_c_im",
             "ssm_d", "ssm_log_dt", "ssm_glu_w", "ssm_glu_b", "sgu_ln_g", "sgu_ln_b", "sgu_w", "sgu_b",
             "out_norm_ssm_g", "out_norm_sgu_g", "w_out", "norm_mlp_g", "w_up", "w_down", "norm_final_g"]
    big = ["w_in", "ssm_glu_w", "w_out", "w_up", "w_down"]
    small = [n for n in names if n not in big]

    d = D_MODEL
    t = x.shape[1]
    tb = min(1024, t)
    xs = x[0]
    target = loss_target[0]
    nsh_in = w_in.shape[2]
    nsh_up = w_up.shape[2]
    d_ff = nsh_up * N_DEV
    n_in = nsh_in * N_DEV

    me = _index(_place())
    (wg_in,) = _all_gather("gather_w_in", [w_in[0].astype(BF16)])
    gathers = {}

    def start_gather(n, after=None):
        shard = given[n][0].astype(BF16)
        gathers[n] = _send_start("gather_start_" + n, shard, lax.empty((N_DEV,) + shard.shape, BF16), False, after)
        return gathers[n][4]

    tokens = [start_gather("ssm_glu_w"), start_gather("w_out")]

    def gathered(n, after):
        shard, blocks = _send_wait("gather_wait_" + n, gathers[n], after, False)
        return _own_block(blocks, shard, me)

    nst = N_SLAB * SLAB_STATE
    are, aim = ssm_a_re.reshape(1, nst), ssm_a_im.reshape(1, nst)
    ldt = jnp.repeat(ssm_log_dt[0], SSM_STATE).reshape(1, nst)
    bxr, bxi = _to_block_b(ssm_b_re[0]), _to_block_b(ssm_b_im[0])
    cxr, cxi = _to_block_c(ssm_c_re[0]), _to_block_c(ssm_c_im[0])
    dvec = ssm_d.reshape(1, SSM_WIDTH)
    bre, bim, cre, cimn, apr, api, air, aii = _ssm_prep(are, aim, ldt, bxr, bxi, cxr, cxi)
    tabs = (bre, bim, cre, cimn, apr, api, air, aii)

    h1 = _rms_fwd("norm_mix", xs, norm_mix_g, deps=tokens)
    (z,) = _mm("in_proj", "nn", (t // tb, N_DEV, 1),
               (h1, _bs((tb, d), lambda i, j, k: (i, 0))),
               (wg_in, _bs((None, d, nsh_in), lambda i, j, k: (j, 0, 0))),
               [(_sds((t, n_in), F32), _bs((tb, nsh_in), lambda i, j, k: (i, j)))])
    tokens = [start_gather("w_up", z), start_gather("w_down", z)]
    y_pre, yg_b, p_re, p_im = _ssm_fwd(z, dvec, *tabs, deps=tokens)

    def glu_ep(acc, yp, b):
        gate = _sigmoid(acc + b)
        return _gelu(yp) * gate, gate

    hw = SSM_WIDTH // 2
    wg_glu = gathered("ssm_glu_w", yg_b).reshape(SSM_WIDTH, SSM_WIDTH)
    tile_g = _bs((tb, hw), lambda i, j, k: (i, j))
    y_ssm, gate = _mm("glu", "nn", (t // tb, 2, 1),
                      (yg_b, _bs((tb, SSM_WIDTH), lambda i, j, k: (i, 0))),
                      (wg_glu, _bs((SSM_WIDTH, hw), lambda i, j, k: (0, j))),
                      [(_sds((t, SSM_WIDTH), F32), tile_g), (_sds((t, SSM_WIDTH), F32), tile_g)],
                      extras=[(y_pre, tile_g), (ssm_glu_b, _bs((1, hw), lambda i, j, k: (0, j)))],
                      epilogue=glu_ep)

    sgu_bexp = jnp.broadcast_to(sgu_b[0][:, :, None], (SGU_HEADS, SGU_CHUNK, SGU_CHUNK))
    y_sgu = _sgu_fwd(z, sgu_ln_g, sgu_ln_b, sgu_w[0], sgu_bexp)
    mixed = _mix_norm(y_ssm, y_sgu, out_norm_ssm_g, out_norm_sgu_g)

    bn_o = 512
    wg_out = gathered("w_out", mixed).reshape(d, d)
    tile_o = _bs((tb, bn_o), lambda i, j, k: (i, j))
    (x2,) = _mm("out_proj", "nn", (t // tb, d // bn_o, 1),
                (mixed, _bs((tb, d), lambda i, j, k: (i, 0))),
                (wg_out, _bs((d, bn_o), lambda i, j, k: (0, j))),
                [(_sds((t, d), F32), tile_o)],
                extras=[(xs, tile_o)], epilogue=lambda acc, r: (acc + r,))
    h2 = _rms_fwd("norm_mlp", x2, norm_mlp_g)

    def up_ep(acc):
        r = jnp.maximum(acc, 0.0)
        return r * r, r

    tile_f = _bs((tb, nsh_up), lambda i, j, k: (i, j))
    wg_up = gathered("w_up", h2)
    f_act, r_act = _mm("mlp_up", "nn", (t // tb, N_DEV, 1),
                       (h2, _bs((tb, d), lambda i, j, k: (i, 0))),
                       (wg_up, _bs((None, d, nsh_up), lambda i, j, k: (j, 0, 0))),
                       [(_sds((t, d_ff), BF16), tile_f), (_sds((t, d_ff), BF16), tile_f)],
                       epilogue=up_ep)
    bk_d = 2048
    wg_down = gathered("w_down", f_act).reshape(d_ff, d)
    (x3,) = _mm("mlp_down", "nn", (t // tb, d // bn_o, d_ff // bk_d),
                (f_act, _bs((tb, bk_d), lambda i, j, k: (i, k))),
                (wg_down, _bs((bk_d, bn_o), lambda i, j, k: (k, j))),
                [(_sds((t, d), F32), tile_o)],
                extras=[(x2, tile_o)], epilogue=lambda acc, r: (acc + r,))

    dx3, dx3_b, g_final, err2 = _final_loss(x3, target, norm_final_g.reshape(1, d))
    loss = lax.psum(0.5 * jnp.sum(err2) / d, MESH_AXES)

    sends = {}

    def send_grad(n, g):
        sends[n] = _send_start("grad_start_" + n, g, lax.empty(g.shape, BF16), True)
        return [sends[n][4]]

    bn_a = 1024
    tile_a = _bs((tb, bn_a), lambda i, j, k: (i, j))
    (da,) = _mm("mlp_down_dx", "nt", (t // tb, d_ff // bn_a, 1),
                (dx3_b, _bs((tb, d), lambda i, j, k: (i, 0))),
                (wg_down, _bs((bn_a, d), lambda i, j, k: (j, 0))),
                [(_sds((t, d_ff), BF16), tile_a)],
                extras=[(r_act, tile_a)], epilogue=lambda acc, r: (acc * (2.0 * r.astype(F32)),))
    sq = 1024
    (gw_down,) = _mm("mlp_down_dw", "tn", (d_ff // sq, d // sq, t // tb),
                     (f_act, _bs((tb, sq), lambda i, j, k: (k, i))),
                     (dx3_b, _bs((tb, sq), lambda i, j, k: (k, j))),
                     [(_sds((d_ff, d), BF16), _bs((sq, sq), lambda i, j, k: (i, j)))])
    sent = send_grad("w_down", gw_down.reshape(N_DEV, -1, d))
    (gw_up,) = _mm("mlp_up_dw", "tn", (d // sq, d_ff // nsh_up, t // tb),
                   (h2, _bs((tb, sq), lambda i, j, k: (k, i))),
                   (da, _bs((tb, nsh_up), lambda i, j, k: (k, j))),
                   [(_sds((N_DEV, d, nsh_up), BF16), _bs((None, sq, nsh_up), lambda i, j, k: (j, i, 0)))],
                   deps=sent)
    sent = send_grad("w_up", gw_up)
    (dh2,) = _mm("mlp_up_dx", "nt", (t // tb, d // sq, N_DEV),
                 (da, _bs((tb, nsh_up), lambda i, j, k: (i, k))),
                 (wg_up, _bs((None, sq, nsh_up), lambda i, j, k: (k, j, 0))),
                 [(_sds((t, d), F32), _bs((tb, sq), lambda i, j, k: (i, j)))], deps=sent)
    dx2, dx2_b, g_norm_mlp = _rms_bwd("norm_mlp_bwd", dh2, 0, x2, norm_mlp_g, dres=dx3, want_bf16=True)

    (dmixed,) = _mm("out_proj_dx", "nt", (t // tb, d // sq, 1),
                    (dx2_b, _bs((tb, d), lambda i, j, k: (i, 0))),
                    (wg_out, _bs((sq, d), lambda i, j, k: (j, 0))),
                    [(_sds((t, d), F32), _bs((tb, sq), lambda i, j, k: (i, j)))])
    (gw_out,) = _mm("out_proj_dw", "tn", (d // sq, d // sq, t // tb),
                    (mixed, _bs((tb, sq), lambda i, j, k: (k, i))),
                    (dx2_b, _bs((tb, sq), lambda i, j, k: (k, j))),
                    [(_sds((d, d), BF16), _bs((sq, sq), lambda i, j, k: (i, j)))])
    sent = send_grad("w_out", gw_out.reshape(N_DEV, -1, d))
    dy_ssm, g_onorm_ssm = _rms_bwd("out_norm_ssm_bwd", dmixed, 0, y_ssm, out_norm_ssm_g, deps=sent)
    dy_sgu, g_onorm_sgu = _rms_bwd("out_norm_sgu_bwd", dmixed, 1, y_sgu, out_norm_sgu_g)

    dpre_b, g_glu_b = _glu_pre_bwd(dy_ssm, y_pre, gate)
    (gw_glu,) = _mm("glu_dw", "tn", (1, 1, t // tb),
                    (yg_b, _bs((tb, SSM_WIDTH), lambda i, j, k: (k, 0))),
                    (dpre_b, _bs((tb, SSM_WIDTH), lambda i, j, k: (k, 0))),
                    [(_sds((SSM_WIDTH, SSM_WIDTH), BF16), _bs((SSM_WIDTH, SSM_WIDTH), lambda i, j, k: (0, 0)))])
    sent = send_grad("ssm_glu_w", gw_glu.reshape(N_DEV, -1, SSM_WIDTH))
    (dy_pre,) = _mm("glu_dx", "nt", (t // tb, 2, 1),
                    (dpre_b, _bs((tb, SSM_WIDTH), lambda i, j, k: (i, 0))),
                    (wg_glu, _bs((hw, SSM_WIDTH), lambda i, j, k: (j, 0))),
                    [(_sds((t, SSM_WIDTH), F32), tile_g)],
                    extras=[(dy_ssm, tile_g), (gate, tile_g), (y_pre, tile_g)],
                    epilogue=lambda acc, dy, gt, yp: ((dy * gt + acc) * _gelu_grad(yp),), deps=sent)
    du_b, dbre, dbim, dcre, dcimn, q_re, q_im, dd = _ssm_bwd(dy_pre, z, p_re, p_im, dvec, *tabs)
    dare, daim, dldt, dbxr, dbxi = _ssm_prep_bwd(are, aim, ldt, bxr, bxi, dbre, dbim, q_re, q_im)

    dz_b, g_ln_g, g_ln_b, g_sgu_w, g_sgu_bx = _sgu_bwd(dy_sgu, du_b, z, sgu_ln_g, sgu_ln_b, sgu_w[0], sgu_bexp)
    (gw_in,) = _mm("in_proj_dw", "tn", (d // sq, N_DEV, t // tb),
                   (h1, _bs((tb, sq), lambda i, j, k: (k, i))),
                   (dz_b, _bs((tb, nsh_in), lambda i, j, k: (k, j))),
                   [(_sds((N_DEV, d, nsh_in), BF16), _bs((None, sq, nsh_in), lambda i, j, k: (j, i, 0)))])
    sent = send_grad("w_in", gw_in)
    (dh1,) = _mm("in_proj_dx", "nt", (t // tb, d // sq, N_DEV),
                 (dz_b, _bs((tb, nsh_in), lambda i, j, k: (i, k))),
                 (wg_in, _bs((None, sq, nsh_in), lambda i, j, k: (k, j, 0))),
                 [(_sds((t, d), F32), _bs((tb, sq), lambda i, j, k: (i, j)))], deps=sent)
    grad_x, g_norm_mix = _rms_bwd("norm_mix_bwd", dh1, 0, xs, norm_mix_g, dres=dx2)

    local_small = {
        "norm_mix_g": g_norm_mix, "ssm_a_re": dare, "ssm_a_im": daim,
        "ssm_b_re": _from_block_b(dbxr), "ssm_b_im": _from_block_b(dbxi),
        "ssm_c_re": _from_block_c(dcre), "ssm_c_im": -_from_block_c(dcimn),
        "ssm_d": dd, "ssm_log_dt": dldt.reshape(-1, SSM_STATE).sum(axis=-1),
        "ssm_glu_b": g_glu_b, "sgu_ln_g": g_ln_g, "sgu_ln_b": g_ln_b, "sgu_w": g_sgu_w,
        "sgu_b": g_sgu_bx[:, :, 0], "out_norm_ssm_g": g_onorm_ssm, "out_norm_sgu_g": g_onorm_sgu,
        "norm_mlp_g": g_norm_mlp, "norm_final_g": g_final,
    }
    (small_parts,) = _all_gather("gather_small_grads", [_pack([local_small[n] for n in small])])

    grads, deltas, new_m, new_v = {}, {}, {}, {}
    for n in big:
        w2 = given[n][0]
        sent_blocks, landed = _send_wait("grad_wait_" + n, sends[n], grad_x, True)
        parts = _own_block(landed, lax.dynamic_index_in_dim(sent_blocks, me, 0, keepdims=False), me)
        res = _adamw("adamw_" + n, parts, w2, given["m_" + n][0], given["v_" + n][0])
        grads[n], deltas[n], new_m[n], new_v[n] = [r.reshape(given[n].shape) for r in res]
    like = [given[n] for n in small]
    res = _adamw("adamw_small", small_parts, _pack(like), _pack([given["m_" + n] for n in small]),
                 _pack([given["v_" + n] for n in small]))
    for store, buf in zip((grads, deltas, new_m, new_v), res):
        for n, a in zip(small, _unpack(buf, like)):
            store[n] = a

    return (loss, grad_x.reshape(x.shape), *[grads[n] for n in names], *[deltas[n] for n in names],
            *[new_m[n] for n in names], *[new_v[n] for n in names])
```

```python
import functools
import math

import jax
import jax.numpy as jnp
from jax import lax
from jax.experimental import pallas as pl
from jax.experimental.pallas import tpu as pltpu

F32, BF16 = jnp.float32, jnp.bfloat16
EPS = 1e-6
N_DEV = 8
D_MODEL = 2048
SSM_WIDTH = 1024
SSM_GROUP = 16
SSM_STATE = 64
SGU_HEADS = 8
SGU_CHUNK = 128
LANES = 128
SUBLANES = 8
N_SLAB = SSM_WIDTH // LANES
SLAB_STATE = (LANES // SSM_GROUP) * SSM_STATE
SCAN_BLOCK = 128
SLABS_PER_STEP = 4
VMEM_LIMIT = 48 * 1024 * 1024
ROW_BLOCK = 256
ADAM_ROWS = 128
MESH_AXES = ("x", "y", "c")

ADAM_LR, ADAM_B1, ADAM_B2, ADAM_EPS, ADAM_WD, ADAM_STEP = 0.001, 0.9, 0.999, 1e-08, 0.01, 10

_GELU_C0 = math.sqrt(2.0 / math.pi)
_GELU_C1 = 0.044715


def _gelu(v):
    return 0.5 * v * (1.0 + jnp.tanh(_GELU_C0 * (v + _GELU_C1 * v * v * v)))


def _gelu_grad(v):
    th = jnp.tanh(_GELU_C0 * (v + _GELU_C1 * v * v * v))
    return 0.5 * (1.0 + th) + 0.5 * v * (1.0 - th * th) * _GELU_C0 * (1.0 + 3.0 * _GELU_C1 * v * v)


def _sigmoid(v):
    return 1.0 / (1.0 + jnp.exp(-v))


def _params(sem=None):
    return pltpu.CompilerParams(dimension_semantics=sem, vmem_limit_bytes=VMEM_LIMIT)


def _dot(a, b, mode="nn"):
    dims = {"nn": ((1,), (0,)), "nt": ((1,), (1,)), "tn": ((0,), (0,))}[mode]
    return lax.dot_general(a, b, (dims, ((), ())), preferred_element_type=F32)


def _mm(name, mode, grid, a, b, outs, extras=(), epilogue=None, deps=()):
    nk = grid[2]
    n_ex, n_out, n_dep = len(extras), len(outs), len(deps)
    acc_shape = tuple(d for d in outs[0][1].block_shape if d is not None)

    def body(*refs):
        a_ref, b_ref = refs[0], refs[1]
        ex = refs[2:2 + n_ex]
        out_refs = refs[2 + n_ex + n_dep:2 + n_ex + n_dep + n_out]
        acc = refs[-1]
        k = pl.program_id(2)

        @pl.when(k == 0)
        def _():
            acc[...] = jnp.zeros_like(acc)

        acc[...] += _dot(a_ref[...], b_ref[...], mode)

        @pl.when(k == nk - 1)
        def _():
            res = acc[...]
            res = (res,) if epilogue is None else epilogue(res, *[e[...] for e in ex])
            for o, r in zip(out_refs, res):
                o[...] = r.astype(o.dtype)

    res = pl.pallas_call(
        body, name=name, grid=grid,
        in_specs=[a[1], b[1]] + [e[1] for e in extras] + [_any_spec()] * n_dep,
        out_specs=[o[1] for o in outs],
        out_shape=[o[0] for o in outs],
        scratch_shapes=[pltpu.VMEM(acc_shape, F32)],
        compiler_params=_params(("parallel", "parallel", "arbitrary")),
    )(a[0], b[0], *[e[0] for e in extras], *deps)
    return res


def _sds(shape, dtype):
    return jax.ShapeDtypeStruct(shape, dtype)


def _bs(shape, fn):
    return pl.BlockSpec(shape, fn)


def _rms_fwd(name, x, g, deps=()):
    t, w = x.shape
    br = min(ROW_BLOCK, t)

    def body(*refs):
        x_ref, g_ref, o_ref = refs[0], refs[1], refs[-1]
        xv = x_ref[...]
        r = lax.rsqrt(jnp.mean(xv * xv, axis=-1, keepdims=True) + EPS)
        o_ref[...] = (xv * r * g_ref[...]).astype(BF16)

    return pl.pallas_call(
        body, name=name, grid=(t // br,),
        in_specs=[_bs((br, w), lambda i: (i, 0)), _bs((1, w), lambda i: (0, 0))] + [_any_spec()] * len(deps),
        out_specs=_bs((br, w), lambda i: (i, 0)),
        out_shape=_sds((t, w), BF16),
        compiler_params=_params(("parallel",)),
    )(x, g, *deps)


def _mix_norm(ya, yb, ga, gb):
    t, w = ya.shape
    br = min(ROW_BLOCK, t)

    def body(a_ref, b_ref, ga_ref, gb_ref, o_ref):
        for src, g_ref, col in ((a_ref, ga_ref, 0), (b_ref, gb_ref, w)):
            v = src[...]
            r = lax.rsqrt(jnp.mean(v * v, axis=-1, keepdims=True) + EPS)
            o_ref[:, col:col + w] = (v * r * g_ref[...]).astype(BF16)

    row = _bs((br, w), lambda i: (i, 0))
    vec = _bs((1, w), lambda i: (0, 0))
    return pl.pallas_call(
        body, name="mix_norm", grid=(t // br,),
        in_specs=[row, row, vec, vec],
        out_specs=_bs((br, 2 * w), lambda i: (i, 0)),
        out_shape=_sds((t, 2 * w), BF16),
        compiler_params=_params(("parallel",)),
    )(ya, yb, ga, gb)


def _rms_bwd(name, dh, dh_col, x, g, dres=None, want_bf16=False, deps=()):
    t, w = x.shape
    br = min(ROW_BLOCK, t)
    has_res = dres is not None

    def body(*refs):
        dh_ref, x_ref, g_ref = refs[0], refs[1], refs[2]
        pos = 3
        res_ref = refs[pos] if has_res else None
        pos += int(has_res) + len(deps)
        dx_ref = refs[pos]
        dxb_ref = refs[pos + 1] if want_bf16 else None
        dg_ref = refs[-1]
        xv, dy = x_ref[...], dh_ref[...]
        r = lax.rsqrt(jnp.mean(xv * xv, axis=-1, keepdims=True) + EPS)
        xhat = xv * r
        dxhat = dy * g_ref[...]
        dx = r * (dxhat - xhat * jnp.mean(dxhat * xhat, axis=-1, keepdims=True))
        if has_res:
            dx = dx + res_ref[...]
        dx_ref[...] = dx
        if want_bf16:
            dxb_ref[...] = dx.astype(BF16)

        @pl.when(pl.program_id(0) == 0)
        def _():
            dg_ref[...] = jnp.zeros_like(dg_ref)

        dg_ref[...] += jnp.sum(dy * xhat, axis=0, keepdims=True)

    row = _bs((br, w), lambda i: (i, 0))
    vec = _bs((1, w), lambda i: (0, 0))
    in_specs = ([_bs((br, w), lambda i: (i, dh_col)), row, vec] + ([row] if has_res else [])
                + [_any_spec()] * len(deps))
    out_specs = [row] + ([row] if want_bf16 else []) + [vec]
    out_shape = [_sds((t, w), F32)] + ([_sds((t, w), BF16)] if want_bf16 else []) + [_sds((1, w), F32)]
    args = [dh, x, g] + ([dres] if has_res else []) + list(deps)
    return pl.pallas_call(
        body, name=name, grid=(t // br,),
        in_specs=in_specs, out_specs=out_specs, out_shape=out_shape,
        compiler_params=_params(("arbitrary",)),
    )(*args)


def _final_loss(x3, target, g):
    t, w = x3.shape
    br = min(ROW_BLOCK, t)

    def body(x_ref, tg_ref, g_ref, dx_ref, dxb_ref, dg_ref, l_ref):
        xv = x_ref[...]
        r = lax.rsqrt(jnp.mean(xv * xv, axis=-1, keepdims=True) + EPS)
        xhat = xv * r
        err = xhat * g_ref[...] - tg_ref[...]
        dy = err * (1.0 / w)
        dxhat = dy * g_ref[...]
        dx = r * (dxhat - xhat * jnp.mean(dxhat * xhat, axis=-1, keepdims=True))
        dx_ref[...] = dx
        dxb_ref[...] = dx.astype(BF16)

        @pl.when(pl.program_id(0) == 0)
        def _():
            dg_ref[...] = jnp.zeros_like(dg_ref)
            l_ref[...] = jnp.zeros_like(l_ref)

        dg_ref[...] += jnp.sum(dy * xhat, axis=0, keepdims=True)
        l_ref[...] += jnp.sum(err * err, axis=0, keepdims=True)

    row = _bs((br, w), lambda i: (i, 0))
    vec = _bs((1, w), lambda i: (0, 0))
    return pl.pallas_call(
        body, name="final_loss", grid=(t // br,),
        in_specs=[row, row, vec], out_specs=[row, row, vec, vec],
        out_shape=[_sds((t, w), F32), _sds((t, w), BF16), _sds((1, w), F32), _sds((1, w), F32)],
        compiler_params=_params(("arbitrary",)),
    )(x3, target, g)


def _glu_pre_bwd(dy, ypre, gate):
    t, w = dy.shape
    br = min(ROW_BLOCK, t)

    def body(dy_ref, y_ref, gt_ref, dp_ref, db_ref):
        gt = gt_ref[...]
        dp = dy_ref[...] * _gelu(y_ref[...]) * gt * (1.0 - gt)
        dp_ref[...] = dp.astype(BF16)

        @pl.when(pl.program_id(0) == 0)
        def _():
            db_ref[...] = jnp.zeros_like(db_ref)

        db_ref[...] += jnp.sum(dp, axis=0, keepdims=True)

    row = _bs((br, w), lambda i: (i, 0))
    vec = _bs((1, w), lambda i: (0, 0))
    return pl.pallas_call(
        body, name="glu_pre_bwd", grid=(t // br,),
        in_specs=[row, row, row], out_specs=[row, vec],
        out_shape=[_sds((t, w), BF16), _sds((1, w), F32)],
        compiler_params=_params(("arbitrary",)),
    )(dy, ypre, gate)


def _prep_math(are, aim, ldt, bxr, bxi):
    dt = jnp.exp(ldt)
    er = jnp.exp(are * dt)
    th = aim * dt
    abr, abi = er * jnp.cos(th), er * jnp.sin(th)
    nr, ni = abr - 1.0, abi
    den = are * are + aim * aim
    cr = (nr * are + ni * aim) / den
    ci = (ni * are - nr * aim) / den
    bbr, bbi = [], []
    for j in range(N_SLAB):
        sl = slice(j * SLAB_STATE, (j + 1) * SLAB_STATE)
        bbr.append(cr[:, sl] * bxr[j] - ci[:, sl] * bxi[j])
        bbi.append(cr[:, sl] * bxi[j] + ci[:, sl] * bxr[j])
    return abr, abi, bbr, bbi


def _ssm_prep(are, aim, ldt, bxr, bxi, cxr, cxi):
    nst = N_SLAB * SLAB_STATE

    def body(are_r, aim_r, ldt_r, bxr_r, bxi_r, cxr_r, cxi_r,
             bre_o, bim_o, cre_o, cimn_o, apr_o, api_o, air_o, aii_o):
        abr, abi, bbr, bbi = _prep_math(are_r[...], aim_r[...], ldt_r[...],
                                        [bxr_r[j] for j in range(N_SLAB)], [bxi_r[j] for j in range(N_SLAB)])
        for j in range(N_SLAB):
            bre_o[j] = bbr[j].astype(BF16)
            bim_o[j] = bbi[j].astype(BF16)
        cre_o[...] = cxr_r[...].astype(BF16)
        cimn_o[...] = (-cxi_r[...]).astype(BF16)

        def step(k, cur):
            cr, ci = cur
            den = cr * cr + ci * ci
            apr_o[pl.ds(k, 1), :] = cr
            api_o[pl.ds(k, 1), :] = ci
            air_o[pl.ds(k, 1), :] = cr / den
            aii_o[pl.ds(k, 1), :] = -ci / den
            return cr * abr - ci * abi, cr * abi + ci * abr

        lax.fori_loop(0, SCAN_BLOCK, step, (jnp.ones((1, nst), F32), jnp.zeros((1, nst), F32)))

    tab = _sds((SCAN_BLOCK, nst), F32)
    return pl.pallas_call(
        body, name="ssm_prep",
        out_shape=[_sds(bxr.shape, BF16), _sds(bxr.shape, BF16), _sds(cxr.shape, BF16), _sds(cxr.shape, BF16),
                   tab, tab, tab, tab],
        compiler_params=_params(),
    )(are, aim, ldt, bxr, bxi, cxr, cxi)


def _ssm_prep_bwd(are, aim, ldt, bxr, bxi, dbre, dbim, qr, qi):
    def body(are_r, aim_r, ldt_r, bxr_r, bxi_r, dbre_r, dbim_r, qr_r, qi_r,
             dare_o, daim_o, dldt_o, dbxr_o, dbxi_o):
        prim = (are_r[...], aim_r[...], ldt_r[...],
                [bxr_r[j] for j in range(N_SLAB)], [bxi_r[j] for j in range(N_SLAB)])
        (abr, abi, _, _), vjp = jax.vjp(_prep_math, *prim)
        den = abr * abr + abi * abi
        q_r, q_i = qr_r[...], qi_r[...]
        gar = (q_r * abr - q_i * abi) / den
        gai = (q_r * abi + q_i * abr) / den
        ct = (gar, gai, [dbre_r[j] for j in range(N_SLAB)], [dbim_r[j] for j in range(N_SLAB)])
        dare, daim, dldt, dbxr, dbxi = vjp(ct)
        dare_o[...] = dare
        daim_o[...] = daim
        dldt_o[...] = dldt
        for j in range(N_SLAB):
            dbxr_o[j] = dbxr[j]
            dbxi_o[j] = dbxi[j]

    row = _sds(are.shape, F32)
    return pl.pallas_call(
        body, name="ssm_prep_bwd",
        out_shape=[row, row, row, _sds(bxr.shape, F32), _sds(bxr.shape, F32)],
        compiler_params=_params(),
    )(are, aim, ldt, bxr, bxi, dbre, dbim, qr, qi)


def _tri(lower):
    r = lax.broadcasted_iota(jnp.int32, (SCAN_BLOCK, SCAN_BLOCK), 0)
    c = lax.broadcasted_iota(jnp.int32, (SCAN_BLOCK, SCAN_BLOCK), 1)
    return jnp.where((r >= c) if lower else (r <= c), 1.0, 0.0).astype(BF16)


def _cumsum_mxu(tri, v):
    hi = v.astype(BF16)
    lo = (v - hi.astype(F32)).astype(BF16)
    return _dot(tri, hi) + _dot(tri, lo)


def _ssm_specs(t):
    nt = t // SCAN_BLOCK
    sps = SLABS_PER_STEP
    tab = _bs((SCAN_BLOCK, sps * SLAB_STATE), lambda j, i: (0, j))
    bmat = _bs((sps, LANES, SLAB_STATE), lambda j, i: (j, 0, 0))
    cmat = _bs((sps, SLAB_STATE, LANES), lambda j, i: (j, 0, 0))
    return nt, tab, bmat, cmat


def _slab_slices(s):
    return slice(s * LANES, (s + 1) * LANES), slice(s * SLAB_STATE, (s + 1) * SLAB_STATE)


def _ssm_fwd(z, dvec, bre, bim, cre, cimn, apr, api, air, aii, deps=()):
    t = z.shape[0]
    nt, tab, bmat, cmat = _ssm_specs(t)
    nst = N_SLAB * SLAB_STATE
    last = SCAN_BLOCK - 1

    sps = SLABS_PER_STEP

    def body(*refs):
        u_ref, d_ref, bre_r, bim_r, cre_r, cimn_r, apr_r, api_r, air_r, aii_r = refs[:10]
        y_ref, yg_ref, pr_ref, pi_ref, car_r, car_i = refs[10 + len(deps):]

        @pl.when(pl.program_id(1) == 0)
        def _():
            car_r[...] = jnp.zeros_like(car_r)
            car_i[...] = jnp.zeros_like(car_i)

        tri = _tri(True)
        for s in range(sps):
            ul, sl = _slab_slices(s)
            u = u_ref[:, ul]
            ub = u.astype(BF16)
            bur, bui = _dot(ub, bre_r[s]), _dot(ub, bim_r[s])
            ir, ii = air_r[:, sl], aii_r[:, sl]
            csr = _cumsum_mxu(tri, ir * bur - ii * bui)
            csi = _cumsum_mxu(tri, ir * bui + ii * bur)
            pr, pi = apr_r[:, sl], api_r[:, sl]
            a_r, a_i = apr_r[1:2, sl], api_r[1:2, sl]
            c_r, c_i = car_r[:, sl], car_i[:, sl]
            wr = csr + (a_r * c_r - a_i * c_i)
            wi = csi + (a_r * c_i + a_i * c_r)
            sr = pr * wr - pi * wi
            si = pr * wi + pi * wr
            car_r[:, sl] = sr[last:last + 1, :]
            car_i[:, sl] = si[last:last + 1, :]
            pr_ref[:, sl] = (sr - bur).astype(BF16)
            pi_ref[:, sl] = (si - bui).astype(BF16)
            y = _dot(sr.astype(BF16), cre_r[s]) + _dot(si.astype(BF16), cimn_r[s]) + d_ref[:, ul] * u
            y_ref[:, ul] = y
            yg_ref[:, ul] = _gelu(y).astype(BF16)

    ublk = _bs((SCAN_BLOCK, sps * LANES), lambda j, i: (i, j))
    sblk = _bs((SCAN_BLOCK, sps * SLAB_STATE), lambda j, i: (i, j))
    return pl.pallas_call(
        body, name="ssm_fwd", grid=(N_SLAB // sps, nt),
        in_specs=[ublk, _bs((1, sps * LANES), lambda j, i: (0, j)), bmat, bmat, cmat, cmat, tab, tab, tab, tab]
        + [_any_spec()] * len(deps),
        out_specs=[ublk, ublk, sblk, sblk],
        out_shape=[_sds((t, SSM_WIDTH), F32), _sds((t, SSM_WIDTH), BF16),
                   _sds((t, nst), BF16), _sds((t, nst), BF16)],
        scratch_shapes=[pltpu.VMEM((1, sps * SLAB_STATE), F32), pltpu.VMEM((1, sps * SLAB_STATE), F32)],
        compiler_params=_params(("parallel", "arbitrary")),
    )(z, dvec, bre, bim, cre, cimn, apr, api, air, aii, *deps)


def _ssm_bwd(gy, z, p_re, p_im, dvec, bre, bim, cre, cimn, apr, api, air, aii):
    t = z.shape[0]
    nt, tab, bmat, cmat = _ssm_specs(t)
    last = SCAN_BLOCK - 1

    def fold(v):
        return v.reshape(SCAN_BLOCK // SUBLANES, SUBLANES, v.shape[-1]).sum(axis=0)

    sps = SLABS_PER_STEP

    def body(g_ref, u_ref, pr_ref, pi_ref, d_ref, bre_r, bim_r, cre_r, cimn_r, apr_r, api_r, air_r, aii_r,
             du_ref, dbre_o, dbim_o, dcre_o, dcimn_o, qr_o, qi_o, dd_o, car_r, car_i, qacc_r, qacc_i, dacc):
        i = pl.program_id(1)

        @pl.when(i == 0)
        def _():
            for ref in (car_r, car_i, qacc_r, qacc_i, dacc, dbre_o, dbim_o, dcre_o, dcimn_o):
                ref[...] = jnp.zeros_like(ref)

        tri = _tri(False)
        for s in range(sps):
            ul, sl = _slab_slices(s)
            g = g_ref[:, ul]
            gb = g.astype(BF16)
            u = u_ref[:, ul]
            ub = u.astype(BF16)
            bur, bui = _dot(ub, bre_r[s]), _dot(ub, bim_r[s])
            p_r, p_i = pr_ref[:, sl].astype(F32), pi_ref[:, sl].astype(F32)
            srb, sib = (p_r + bur).astype(BF16), (p_i + bui).astype(BF16)
            dcre_o[s] += _dot(srb, gb, "tn")
            dcimn_o[s] += _dot(sib, gb, "tn")
            dsr, dsi = _dot(gb, cre_r[s], "nt"), _dot(gb, cimn_r[s], "nt")
            pr, pi = apr_r[:, sl], api_r[:, sl]
            csr = _cumsum_mxu(tri, pr * dsr + pi * dsi)
            csi = _cumsum_mxu(tri, pr * dsi - pi * dsr)
            al_r, al_i = apr_r[last:last + 1, sl], api_r[last:last + 1, sl]
            c_r, c_i = car_r[:, sl], car_i[:, sl]
            wr = csr + (al_r * c_r + al_i * c_i)
            wi = csi + (al_r * c_i - al_i * c_r)
            ir, ii = air_r[:, sl], aii_r[:, sl]
            lr = ir * wr + ii * wi
            li = ir * wi - ii * wr
            a_r, a_i = apr_r[1:2, sl], api_r[1:2, sl]
            car_r[:, sl] = a_r * lr[0:1, :] + a_i * li[0:1, :]
            car_i[:, sl] = a_r * li[0:1, :] - a_i * lr[0:1, :]
            lrb, lib = lr.astype(BF16), li.astype(BF16)
            dbre_o[s] += _dot(ub, lrb, "tn")
            dbim_o[s] += _dot(ub, lib, "tn")
            du = d_ref[:, ul] * g + _dot(lrb, bre_r[s], "nt") + _dot(lib, bim_r[s], "nt")
            du_ref[:, ul] = du.astype(BF16)
            qacc_r[:, sl] += fold(lr * p_r + li * p_i)
            qacc_i[:, sl] += fold(li * p_r - lr * p_i)
            dacc[:, ul] += fold(g * u)

        @pl.when(i == nt - 1)
        def _():
            qr_o[...] = jnp.sum(qacc_r[...], axis=0, keepdims=True)
            qi_o[...] = jnp.sum(qacc_i[...], axis=0, keepdims=True)
            dd_o[...] = jnp.sum(dacc[...], axis=0, keepdims=True)

    rev = lambda j, i: (nt - 1 - i, j)
    ublk = _bs((SCAN_BLOCK, sps * LANES), rev)
    sblk = _bs((SCAN_BLOCK, sps * SLAB_STATE), rev)
    qrow = _bs((1, sps * SLAB_STATE), lambda j, i: (0, j))
    urow = _bs((1, sps * LANES), lambda j, i: (0, j))
    nst = N_SLAB * SLAB_STATE
    return pl.pallas_call(
        body, name="ssm_bwd", grid=(N_SLAB // sps, nt),
        in_specs=[ublk, ublk, sblk, sblk, urow, bmat, bmat, cmat, cmat, tab, tab, tab, tab],
        out_specs=[ublk, bmat, bmat, cmat, cmat, qrow, qrow, urow],
        out_shape=[_sds((t, SSM_WIDTH), BF16),
                   _sds((N_SLAB, LANES, SLAB_STATE), F32), _sds((N_SLAB, LANES, SLAB_STATE), F32),
                   _sds((N_SLAB, SLAB_STATE, LANES), F32), _sds((N_SLAB, SLAB_STATE, LANES), F32),
                   _sds((1, nst), F32), _sds((1, nst), F32), _sds((1, SSM_WIDTH), F32)],
        scratch_shapes=[pltpu.VMEM((1, sps * SLAB_STATE), F32), pltpu.VMEM((1, sps * SLAB_STATE), F32),
                        pltpu.VMEM((SUBLANES, sps * SLAB_STATE), F32), pltpu.VMEM((SUBLANES, sps * SLAB_STATE), F32),
                        pltpu.VMEM((SUBLANES, sps * LANES), F32)],
        compiler_params=_params(("parallel", "arbitrary")),
    )(gy, z, p_re, p_im, dvec, bre, bim, cre, cimn, apr, api, air, aii)


def _sgu_mask():
    r = lax.broadcasted_iota(jnp.int32, (SGU_CHUNK, SGU_CHUNK), 0)
    c = lax.broadcasted_iota(jnp.int32, (SGU_CHUNK, SGU_CHUNK), 1)
    return r >= c


def _sgu_common(zu, zv, lng, lnb):
    us, v = _gelu(zu), _gelu(zv)
    mu = jnp.mean(v, axis=-1, keepdims=True)
    vc = v - mu
    rstd = lax.rsqrt(jnp.mean(vc * vc, axis=-1, keepdims=True) + EPS)
    xhat = vc * rstd
    return us, xhat, rstd, xhat * lng + lnb


def _sgu_fwd(z, lng, lnb, w, bexp):
    t = z.shape[0]
    hd = SGU_CHUNK

    def body(zu_ref, zv_ref, lng_ref, lnb_ref, w_ref, b_ref, y_ref):
        us, _, _, vn = _sgu_common(zu_ref[...], zv_ref[...], lng_ref[...], lnb_ref[...])
        vnb = vn.astype(BF16)
        mask = _sgu_mask()
        for h in range(SGU_HEADS):
            sl = slice(h * hd, (h + 1) * hd)
            wt = jnp.where(mask, w_ref[h], 0.0).astype(BF16)
            y_ref[:, sl] = us[:, sl] * (_dot(wt, vnb[:, sl]) + b_ref[h])

    row = lambda c: _bs((SGU_CHUNK, SSM_WIDTH), lambda i: (i, c))
    vec = _bs((1, SSM_WIDTH), lambda i: (0, 0))
    hmat = _bs((SGU_HEADS, hd, hd), lambda i: (0, 0, 0))
    return pl.pallas_call(
        body, name="sgu_fwd", grid=(t // SGU_CHUNK,),
        in_specs=[row(1), row(2), vec, vec, hmat, hmat],
        out_specs=row(0), out_shape=_sds((t, SSM_WIDTH), F32),
        compiler_params=_params(("parallel",)),
    )(z, z, lng, lnb, w, bexp)


def _sgu_bwd(dy, du_ssm, z, lng, lnb, w, bexp):
    t = z.shape[0]
    hd = SGU_CHUNK
    nc = t // SGU_CHUNK

    def body(dy_ref, dus_ref, zu_ref, zv_ref, lng_ref, lnb_ref, w_ref, b_ref,
             dz_ref, dlng_o, dlnb_o, dw_o, db_o):
        i = pl.program_id(0)

        @pl.when(i == 0)
        def _():
            for ref in (dlng_o, dlnb_o, dw_o, db_o):
                ref[...] = jnp.zeros_like(ref)

        zu, zv = zu_ref[...], zv_ref[...]
        lng = lng_ref[...]
        us, xhat, rstd, vn = _sgu_common(zu, zv, lng, lnb_ref[...])
        vnb = vn.astype(BF16)
        dyv = dy_ref[...]
        mask = _sgu_mask()
        dus_parts, dvn_parts = [], []
        for h in range(SGU_HEADS):
            sl = slice(h * hd, (h + 1) * hd)
            wt = jnp.where(mask, w_ref[h], 0.0).astype(BF16)
            mixed = _dot(wt, vnb[:, sl]) + b_ref[h]
            dus_parts.append(dyv[:, sl] * mixed)
            dmix = dyv[:, sl] * us[:, sl]
            dmb = dmix.astype(BF16)
            db_o[h] += dmix
            dw_o[h] += _dot(dmb, vnb[:, sl], "nt")
            dvn_parts.append(_dot(wt, dmb, "tn"))
        dus = jnp.concatenate(dus_parts, axis=1)
        dvn = jnp.concatenate(dvn_parts, axis=1)
        dlng_o[...] += jnp.sum(dvn * xhat, axis=0, keepdims=True)
        dlnb_o[...] += jnp.sum(dvn, axis=0, keepdims=True)
        dxh = dvn * lng
        dv = rstd * (dxh - jnp.mean(dxh, axis=-1, keepdims=True)
                     - xhat * jnp.mean(dxh * xhat, axis=-1, keepdims=True))
        dz_ref[:, 0:SSM_WIDTH] = dus_ref[...]
        dz_ref[:, SSM_WIDTH:2 * SSM_WIDTH] = (dus * _gelu_grad(zu)).astype(BF16)
        dz_ref[:, 2 * SSM_WIDTH:] = (dv * _gelu_grad(zv)).astype(BF16)

        @pl.when(i == nc - 1)
        def _():
            for h in range(SGU_HEADS):
                dw_o[h] = jnp.where(mask, dw_o[h], 0.0)
                db_o[h] = jnp.broadcast_to(jnp.sum(db_o[h], axis=1, keepdims=True), (hd, hd))

    row = lambda c: _bs((SGU_CHUNK, SSM_WIDTH), lambda i: (i, c))
    vec = _bs((1, SSM_WIDTH), lambda i: (0, 0))
    hmat = _bs((SGU_HEADS, hd, hd), lambda i: (0, 0, 0))
    return pl.pallas_call(
        body, name="sgu_bwd", grid=(nc,),
        in_specs=[row(0), row(0), row(1), row(2), vec, vec, hmat, hmat],
        out_specs=[_bs((SGU_CHUNK, 3 * SSM_WIDTH), lambda i: (i, 0)), vec, vec, hmat, hmat],
        out_shape=[_sds((t, 3 * SSM_WIDTH), BF16), _sds((1, SSM_WIDTH), F32), _sds((1, SSM_WIDTH), F32),
                   _sds((SGU_HEADS, hd, hd), F32), _sds((SGU_HEADS, hd, hd), F32)],
        compiler_params=_params(("arbitrary",)),
    )(dy, du_ssm, z, z, lng, lnb, w, bexp)


def _place():
    x, y, c = (lax.axis_index(a) for a in MESH_AXES)
    return x, y, c


def _index(p):
    return 4 * p[0] + 2 * p[1] + p[2]


def _any_spec():
    return pl.BlockSpec(memory_space=pl.ANY)


def _col_block(ref, k, width):
    return ref.at[:, pl.ds(pl.multiple_of(k * width, LANES), width)]


def _all_gather(name, shards, by_columns=False):
    n = len(shards)

    def body(*refs):
        ins, outs = refs[:n], refs[n:2 * n]
        send, recv, loc = refs[2 * n:]
        x, y, c = _place()
        me, sib = (x, y, c), (x, y, 1 - c)
        chips = [(1 - x, y), (x, 1 - y), (1 - x, 1 - y)]

        def blk(w, p):
            if by_columns:
                return _col_block(outs[w], _index(p), shards[w].shape[1])
            return outs[w].at[_index(p)]

        def cp(w, k, block, to, src=None):
            dst = blk(w, block)
            return pltpu.make_async_remote_copy(
                src_ref=dst if src is None else src, dst_ref=dst,
                send_sem=send.at[w * 7 + k], recv_sem=recv.at[w * 7 + k],
                device_id=to, device_id_type=pl.DeviceIdType.MESH)

        mines, sends = [], []
        for w in range(n):
            m = pltpu.make_async_copy(ins[w], blk(w, me), loc.at[w])
            m.start()
            mines.append(m)
            first = [cp(w, 0, me, sib, src=ins[w])]
            first += [cp(w, 1 + j, me, (*chip, c), src=ins[w]) for j, chip in enumerate(chips)]
            for q in first:
                q.start()
            sends += first
        for j, chip in enumerate(chips):
            for w in range(n):
                cp(w, 1 + j, (*chip, c), me).wait_recv()
                q = cp(w, 4 + j, (*chip, c), sib)
                q.start()
                sends.append(q)
        for w in range(n):
            cp(w, 0, sib, me).wait_recv()
            for j, chip in enumerate(chips):
                cp(w, 4 + j, (*chip, 1 - c), me).wait_recv()
        for q in sends:
            q.wait_send()
        for m in mines:
            m.wait()

    return pl.pallas_call(
        body, name=name,
        in_specs=[_any_spec()] * n, out_specs=[_any_spec()] * n,
        out_shape=[_sds((s.shape[0], N_DEV * s.shape[1]) if by_columns else (N_DEV,) + s.shape, s.dtype)
                   for s in shards],
        scratch_shapes=[pltpu.SemaphoreType.DMA((n * 7,)), pltpu.SemaphoreType.DMA((n * 7,)),
                        pltpu.SemaphoreType.DMA((n,))],
        compiler_params=pltpu.CompilerParams(has_side_effects=True),
    )(*shards)


def _peer(r, x, y, c):
    return ((1 - x) if r & 4 else x, (1 - y) if r & 2 else y, (1 - c) if r & 1 else c)


def _sent_block(src_ref, land_ref, k, scatter):
    if not scatter:
        return src_ref
    if len(src_ref.shape) == len(land_ref.shape):
        return src_ref.at[k]
    return _col_block(src_ref, k, land_ref.shape[2])


def _send_start(name, src, land, scatter, after=None):
    n_after = 0 if after is None else 1

    def body(*refs):
        src_ref, land_ref = refs[0], refs[1]
        send, recv, _, _, token = refs[2 + n_after:]
        x, y, c = _place()
        me = _index((x, y, c))
        for r in range(1, N_DEV):
            p = _peer(r, x, y, c)
            pltpu.make_async_remote_copy(
                src_ref=_sent_block(src_ref, land_ref, _index(p), scatter), dst_ref=land_ref.at[me],
                send_sem=send.at[r - 1], recv_sem=recv.at[r - 1],
                device_id=p, device_id_type=pl.DeviceIdType.MESH).start()
        token[...] = jnp.zeros_like(token)

    hbm, sem = pl.BlockSpec(memory_space=pltpu.HBM), pl.BlockSpec(memory_space=pltpu.SEMAPHORE)
    return pl.pallas_call(
        body, name=name,
        out_shape=(pltpu.SemaphoreType.DMA((N_DEV - 1,)), pltpu.SemaphoreType.DMA((N_DEV - 1,)),
                   pltpu.HBM(src.shape, src.dtype), pltpu.HBM(land.shape, land.dtype),
                   _sds((SUBLANES, LANES), F32)),
        in_specs=(hbm, hbm) + (_any_spec(),) * n_after,
        out_specs=(sem, sem, hbm, hbm, pl.BlockSpec(memory_space=pltpu.VMEM)),
        input_output_aliases={0: 2, 1: 3},
        compiler_params=pltpu.CompilerParams(has_side_effects=pltpu.SideEffectType.DATAFLOW_SIDE_EFFECTING),
    )(pltpu.with_memory_space_constraint(src, pltpu.HBM), pltpu.with_memory_space_constraint(land, pltpu.HBM),
      *([] if after is None else [after]))


def _send_wait(name, started, after, scatter):
    send, recv, src_thru, land_thru, _ = started

    def body(src_ref, land_ref, send_r, recv_r, after_ref, src_out, land_out):
        x, y, c = _place()
        for r in range(1, N_DEV):
            p = _peer(r, x, y, c)
            k = _index(p)
            cp = pltpu.make_async_remote_copy(
                src_ref=_sent_block(src_ref, land_ref, k, scatter), dst_ref=land_ref.at[k],
                send_sem=send_r.at[r - 1], recv_sem=recv_r.at[r - 1],
                device_id=p, device_id_type=pl.DeviceIdType.MESH)
            cp.wait_send()
            cp.wait_recv()

    hbm, sem = pl.BlockSpec(memory_space=pltpu.HBM), pl.BlockSpec(memory_space=pltpu.SEMAPHORE)
    return pl.pallas_call(
        body, name=name,
        out_shape=(pltpu.HBM(src_thru.shape, src_thru.dtype), pltpu.HBM(land_thru.shape, land_thru.dtype)),
        in_specs=(hbm, hbm, sem, sem, _any_spec()), out_specs=(hbm, hbm),
        input_output_aliases={0: 0, 1: 1},
        compiler_params=pltpu.CompilerParams(has_side_effects=pltpu.SideEffectType.DATAFLOW_SIDE_EFFECTING),
    )(src_thru, land_thru, send, recv, after)


def _own_block(blocks, block, me):
    return lax.dynamic_update_index_in_dim(blocks, block, me, 0)


def _adamw(name, parts, w, m, v):
    rows, cols = w.shape
    br = min(ADAM_ROWS, rows)
    c1 = 1.0 / (1.0 - ADAM_B1 ** ADAM_STEP)
    c2 = 1.0 / (1.0 - ADAM_B2 ** ADAM_STEP)

    def body(p_ref, w_ref, m_ref, v_ref, g_o, d_o, m_o, v_o):
        g = p_ref[0].astype(F32)
        for k in range(1, N_DEV):
            g = g + p_ref[k].astype(F32)
        mn = ADAM_B1 * m_ref[...] + (1.0 - ADAM_B1) * g
        vn = ADAM_B2 * v_ref[...] + (1.0 - ADAM_B2) * (g * g)
        g_o[...] = g
        m_o[...] = mn
        v_o[...] = vn
        d_o[...] = -ADAM_LR * ((mn * c1) / (jnp.sqrt(vn * c2) + ADAM_EPS) + ADAM_WD * w_ref[...])

    blk = _bs((br, cols), lambda i: (i, 0))
    out = _sds((rows, cols), F32)
    return pl.pallas_call(
        body, name=name, grid=(rows // br,),
        in_specs=[_bs((N_DEV, br, cols), lambda i: (0, i, 0)), blk, blk, blk],
        out_specs=[blk] * 4, out_shape=[out] * 4,
        compiler_params=_params(("parallel",)),
    )(parts, w, m, v)


def _pack(arrs):
    tile = SUBLANES * LANES
    flat = []
    for a in arrs:
        f = a.reshape(-1).astype(F32)
        pad = (-f.shape[0]) % tile
        flat.append(jnp.pad(f, (0, pad)) if pad else f)
    total = sum(f.shape[0] for f in flat)
    tail = (-total) % (ADAM_ROWS * LANES)
    if tail:
        flat.append(jnp.zeros((tail,), F32))
    return jnp.concatenate(flat).reshape(-1, LANES)


def _unpack(buf, like):
    tile = SUBLANES * LANES
    flat = buf.reshape(-1)
    out, off = [], 0
    for a in like:
        n = math.prod(a.shape)
        out.append(flat[off:off + n].reshape(a.shape))
        off += n + ((-n) % tile)
    return out


def _to_block_b(b):
    gl = LANES // SSM_GROUP
    tb = b.reshape(N_SLAB, gl, SSM_STATE, SSM_GROUP).transpose(0, 1, 3, 2)
    eye = jnp.eye(gl, dtype=F32)
    return (tb[:, :, :, None, :] * eye[None, :, None, :, None]).reshape(N_SLAB, LANES, SLAB_STATE)


def _from_block_b(bx):
    gl = LANES // SSM_GROUP
    d = jnp.einsum("jghgp->jgph", bx.reshape(N_SLAB, gl, SSM_GROUP, gl, SSM_STATE))
    return d.reshape(N_SLAB * gl, SSM_STATE, SSM_GROUP)


def _to_block_c(cm):
    gl = LANES // SSM_GROUP
    tc = cm.reshape(N_SLAB, gl, SSM_GROUP, SSM_STATE).transpose(0, 1, 3, 2)
    eye = jnp.eye(gl, dtype=F32)
    return (tc[:, :, :, None, :] * eye[None, :, None, :, None]).reshape(N_SLAB, SLAB_STATE, LANES)


def _from_block_c(cx):
    gl = LANES // SSM_GROUP
    d = jnp.einsum("jgpgh->jghp", cx.reshape(N_SLAB, gl, SSM_STATE, gl, SSM_GROUP))
    return d.reshape(N_SLAB * gl, SSM_GROUP, SSM_STATE)


def kernel(x, norm_mix_g, w_in, ssm_a_re, ssm_a_im, ssm_b_re, ssm_b_im, ssm_c_re, ssm_c_im, ssm_d, ssm_log_dt, ssm_glu_w, ssm_glu_b, sgu_ln_g, sgu_ln_b, sgu_w, sgu_b, out_norm_ssm_g, out_norm_sgu_g, w_out, norm_mlp_g, w_up, w_down, norm_final_g, loss_target, m_norm_mix_g, m_w_in, m_ssm_a_re, m_ssm_a_im, m_ssm_b_re, m_ssm_b_im, m_ssm_c_re, m_ssm_c_im, m_ssm_d, m_ssm_log_dt, m_ssm_glu_w, m_ssm_glu_b, m_sgu_ln_g, m_sgu_ln_b, m_sgu_w, m_sgu_b, m_out_norm_ssm_g, m_out_norm_sgu_g, m_w_out, m_norm_mlp_g, m_w_up, m_w_down, m_norm_final_g, v_norm_mix_g, v_w_in, v_ssm_a_re, v_ssm_a_im, v_ssm_b_re, v_ssm_b_im, v_ssm_c_re, v_ssm_c_im, v_ssm_d, v_ssm_log_dt, v_ssm_glu_w, v_ssm_glu_b, v_sgu_ln_g, v_sgu_ln_b, v_sgu_w, v_sgu_b, v_out_norm_ssm_g, v_out_norm_sgu_g, v_w_out, v_norm_mlp_g, v_w_up, v_w_down, v_norm_final_g):
    given = dict(locals())
    names = ["norm_mix_g", "w_in", "ssm_a_re", "ssm_a_im", "ssm_b_re", "ssm_b_im", "ssm_c_re", "ssm_c_im",
             "ssm_d", "ssm_log_dt", "ssm_glu_w", "ssm_glu_b", "sgu_ln_g", "sgu_ln_b", "sgu_w", "sgu_b",
             "out_norm_ssm_g", "out_norm_sgu_g", "w_out", "norm_mlp_g", "w_up", "w_down", "norm_final_g"]
    big = ["w_in", "ssm_glu_w", "w_out", "w_up", "w_down"]
    small = [n for n in names if n not in big]

    d = D_MODEL
    t = x.shape[1]
    tb = min(1024, t)
    xs = x[0]
    target = loss_target[0]
    nsh_in = w_in.shape[2]
    nsh_up = w_up.shape[2]
    d_ff = nsh_up * N_DEV
    n_in = nsh_in * N_DEV

    me = _index(_place())
    (wg_in,) = _all_gather("gather_w_in", [w_in[0].astype(BF16)], by_columns=True)
    gathers = {}

    def start_gather(n, after=None):
        shard = given[n][0].astype(BF16)
        gathers[n] = _send_start("gather_start_" + n, shard, lax.empty((N_DEV,) + shard.shape, BF16), False, after)
        return gathers[n][4]

    tokens = [start_gather("ssm_glu_w", wg_in), start_gather("w_out", wg_in)]

    def gathered(n, after):
        shard, blocks = _send_wait("gather_wait_" + n, gathers[n], after, False)
        return _own_block(blocks, shard, me)

    nst = N_SLAB * SLAB_STATE
    are, aim = ssm_a_re.reshape(1, nst), ssm_a_im.reshape(1, nst)
    ldt = jnp.repeat(ssm_log_dt[0], SSM_STATE).reshape(1, nst)
    bxr, bxi = _to_block_b(ssm_b_re[0]), _to_block_b(ssm_b_im[0])
    cxr, cxi = _to_block_c(ssm_c_re[0]), _to_block_c(ssm_c_im[0])
    dvec = ssm_d.reshape(1, SSM_WIDTH)
    bre, bim, cre, cimn, apr, api, air, aii = _ssm_prep(are, aim, ldt, bxr, bxi, cxr, cxi)
    tabs = (bre, bim, cre, cimn, apr, api, air, aii)

    h1 = _rms_fwd("norm_mix", xs, norm_mix_g, deps=tokens)
    bn_i = n_in // 2
    (z,) = _mm("in_proj", "nn", (t // tb, n_in // bn_i, 1),
               (h1, _bs((tb, d), lambda i, j, k: (i, 0))),
               (wg_in, _bs((d, bn_i), lambda i, j, k: (0, j))),
               [(_sds((t, n_in), F32), _bs((tb, bn_i), lambda i, j, k: (i, j)))])
    tokens = [start_gather("w_up", z), start_gather("w_down", z)]
    y_pre, yg_b, p_re, p_im = _ssm_fwd(z, dvec, *tabs, deps=tokens)

    def glu_ep(acc, yp, b):
        gate = _sigmoid(acc + b)
        return _gelu(yp) * gate, gate

    hw = SSM_WIDTH // 2
    wg_glu = gathered("ssm_glu_w", yg_b).reshape(SSM_WIDTH, SSM_WIDTH)
    tile_g = _bs((tb, hw), lambda i, j, k: (i, j))
    y_ssm, gate = _mm("glu", "nn", (t // tb, 2, 1),
                      (yg_b, _bs((tb, SSM_WIDTH), lambda i, j, k: (i, 0))),
                      (wg_glu, _bs((SSM_WIDTH, hw), lambda i, j, k: (0, j))),
                      [(_sds((t, SSM_WIDTH), F32), tile_g), (_sds((t, SSM_WIDTH), F32), tile_g)],
                      extras=[(y_pre, tile_g), (ssm_glu_b, _bs((1, hw), lambda i, j, k: (0, j)))],
                      epilogue=glu_ep)

    sgu_bexp = jnp.broadcast_to(sgu_b[0][:, :, None], (SGU_HEADS, SGU_CHUNK, SGU_CHUNK))
    y_sgu = _sgu_fwd(z, sgu_ln_g, sgu_ln_b, sgu_w[0], sgu_bexp)
    mixed = _mix_norm(y_ssm, y_sgu, out_norm_ssm_g, out_norm_sgu_g)

    bn_o = 1024
    wg_out = gathered("w_out", mixed).reshape(d, d)
    tile_o = _bs((tb, bn_o), lambda i, j, k: (i, j))
    (x2,) = _mm("out_proj", "nn", (t // tb, d // bn_o, 1),
                (mixed, _bs((tb, d), lambda i, j, k: (i, 0))),
                (wg_out, _bs((d, bn_o), lambda i, j, k: (0, j))),
                [(_sds((t, d), F32), tile_o)],
                extras=[(xs, tile_o)], epilogue=lambda acc, r: (acc + r,))
    h2 = _rms_fwd("norm_mlp", x2, norm_mlp_g)

    def up_ep(acc):
        r = jnp.maximum(acc, 0.0)
        return r * r, r

    tile_f = _bs((tb, nsh_up), lambda i, j, k: (i, j))
    wg_up = gathered("w_up", h2)
    f_act, r_act = _mm("mlp_up", "nn", (t // tb, N_DEV, 1),
                       (h2, _bs((tb, d), lambda i, j, k: (i, 0))),
                       (wg_up, _bs((None, d, nsh_up), lambda i, j, k: (j, 0, 0))),
                       [(_sds((t, d_ff), BF16), tile_f), (_sds((t, d_ff), BF16), tile_f)],
                       epilogue=up_ep)
    bk_d = 2048
    wg_down = gathered("w_down", f_act).reshape(d_ff, d)
    (x3,) = _mm("mlp_down", "nn", (t // tb, d // bn_o, d_ff // bk_d),
                (f_act, _bs((tb, bk_d), lambda i, j, k: (i, k))),
                (wg_down, _bs((bk_d, bn_o), lambda i, j, k: (k, j))),
                [(_sds((t, d), F32), tile_o)],
                extras=[(x2, tile_o)], epilogue=lambda acc, r: (acc + r,))

    dx3, dx3_b, g_final, err2 = _final_loss(x3, target, norm_final_g.reshape(1, d))
    loss = lax.psum(0.5 * jnp.sum(err2) / d, MESH_AXES)

    sends = {}

    def send_grad(n, g, land_shape=None):
        sends[n] = _send_start("grad_start_" + n, g, lax.empty(land_shape or g.shape, BF16), True)
        return [sends[n][4]]

    bn_a = 1024
    tile_a = _bs((tb, bn_a), lambda i, j, k: (i, j))
    (da,) = _mm("mlp_down_dx", "nt", (t // tb, d_ff // bn_a, 1),
                (dx3_b, _bs((tb, d), lambda i, j, k: (i, 0))),
                (wg_down, _bs((bn_a, d), lambda i, j, k: (j, 0))),
                [(_sds((t, d_ff), BF16), tile_a)],
                extras=[(r_act, tile_a)], epilogue=lambda acc, r: (acc * (2.0 * r.astype(F32)),))
    sq = 1024
    (gw_down,) = _mm("mlp_down_dw", "tn", (d_ff // sq, 1, t // tb),
                     (f_act, _bs((tb, sq), lambda i, j, k: (k, i))),
                     (dx3_b, _bs((tb, d), lambda i, j, k: (k, 0))),
                     [(_sds((d_ff, d), BF16), _bs((sq, d), lambda i, j, k: (i, 0)))])
    sent = send_grad("w_down", gw_down.reshape(N_DEV, -1, d))
    (gw_up,) = _mm("mlp_up_dw", "tn", (1, N_DEV, t // tb),
                   (h2, _bs((tb, d), lambda i, j, k: (k, 0))),
                   (da, _bs((tb, nsh_up), lambda i, j, k: (k, j))),
                   [(_sds((N_DEV, d, nsh_up), BF16), _bs((None, d, nsh_up), lambda i, j, k: (j, 0, 0)))],
                   deps=sent)
    sent = send_grad("w_up", gw_up)
    (dh2,) = _mm("mlp_up_dx", "nt", (t // tb, 1, N_DEV),
                 (da, _bs((tb, nsh_up), lambda i, j, k: (i, k))),
                 (wg_up, _bs((None, d, nsh_up), lambda i, j, k: (k, 0, 0))),
                 [(_sds((t, d), F32), _bs((tb, d), lambda i, j, k: (i, 0)))], deps=sent)
    dx2, dx2_b, g_norm_mlp = _rms_bwd("norm_mlp_bwd", dh2, 0, x2, norm_mlp_g, dres=dx3, want_bf16=True)

    (dmixed,) = _mm("out_proj_dx", "nt", (t // tb, d // sq, 1),
                    (dx2_b, _bs((tb, d), lambda i, j, k: (i, 0))),
                    (wg_out, _bs((sq, d), lambda i, j, k: (j, 0))),
                    [(_sds((t, d), F32), _bs((tb, sq), lambda i, j, k: (i, j)))])
    tk = min(2048, t)
    (gw_out,) = _mm("out_proj_dw", "tn", (d // sq, d // sq, t // tk),
                    (mixed, _bs((tk, sq), lambda i, j, k: (k, i))),
                    (dx2_b, _bs((tk, sq), lambda i, j, k: (k, j))),
                    [(_sds((d, d), BF16), _bs((sq, sq), lambda i, j, k: (i, j)))])
    sent = send_grad("w_out", gw_out.reshape(N_DEV, -1, d))
    dy_ssm, g_onorm_ssm = _rms_bwd("out_norm_ssm_bwd", dmixed, 0, y_ssm, out_norm_ssm_g, deps=sent)
    dy_sgu, g_onorm_sgu = _rms_bwd("out_norm_sgu_bwd", dmixed, 1, y_sgu, out_norm_sgu_g)

    dpre_b, g_glu_b = _glu_pre_bwd(dy_ssm, y_pre, gate)
    (gw_glu,) = _mm("glu_dw", "tn", (1, 1, t // tb),
                    (yg_b, _bs((tb, SSM_WIDTH), lambda i, j, k: (k, 0))),
                    (dpre_b, _bs((tb, SSM_WIDTH), lambda i, j, k: (k, 0))),
                    [(_sds((SSM_WIDTH, SSM_WIDTH), BF16), _bs((SSM_WIDTH, SSM_WIDTH), lambda i, j, k: (0, 0)))])
    sent = send_grad("ssm_glu_w", gw_glu.reshape(N_DEV, -1, SSM_WIDTH))
    (dy_pre,) = _mm("glu_dx", "nt", (t // tb, 2, 1),
                    (dpre_b, _bs((tb, SSM_WIDTH), lambda i, j, k: (i, 0))),
                    (wg_glu, _bs((hw, SSM_WIDTH), lambda i, j, k: (j, 0))),
                    [(_sds((t, SSM_WIDTH), F32), tile_g)],
                    extras=[(dy_ssm, tile_g), (gate, tile_g), (y_pre, tile_g)],
                    epilogue=lambda acc, dy, gt, yp: ((dy * gt + acc) * _gelu_grad(yp),), deps=sent)
    du_b, dbre, dbim, dcre, dcimn, q_re, q_im, dd = _ssm_bwd(dy_pre, z, p_re, p_im, dvec, *tabs)
    dare, daim, dldt, dbxr, dbxi = _ssm_prep_bwd(are, aim, ldt, bxr, bxi, dbre, dbim, q_re, q_im)

    dz_b, g_ln_g, g_ln_b, g_sgu_w, g_sgu_bx = _sgu_bwd(dy_sgu, du_b, z, sgu_ln_g, sgu_ln_b, sgu_w[0], sgu_bexp)
    local_small = {
        "ssm_a_re": dare, "ssm_a_im": daim,
        "ssm_b_re": _from_block_b(dbxr), "ssm_b_im": _from_block_b(dbxi),
        "ssm_c_re": _from_block_c(dcre), "ssm_c_im": -_from_block_c(dcimn),
        "ssm_d": dd, "ssm_log_dt": dldt.reshape(-1, SSM_STATE).sum(axis=-1),
        "ssm_glu_b": g_glu_b, "sgu_ln_g": g_ln_g, "sgu_ln_b": g_ln_b, "sgu_w": g_sgu_w,
        "sgu_b": g_sgu_bx[:, :, 0], "out_norm_ssm_g": g_onorm_ssm, "out_norm_sgu_g": g_onorm_sgu,
        "norm_mlp_g": g_norm_mlp, "norm_final_g": g_final,
    }
    small_early = [n for n in small if n in local_small]
    small_late = [n for n in small if n not in local_small]
    packed = _pack([local_small[n] for n in small_early])
    small_send = _send_start("small_start", packed, lax.empty((N_DEV,) + packed.shape, F32), False)
    (gw_in,) = _mm("in_proj_dw", "tn", (d // sq, n_in // bn_i, t // tb),
                   (h1, _bs((tb, sq), lambda i, j, k: (k, i))),
                   (dz_b, _bs((tb, bn_i), lambda i, j, k: (k, j))),
                   [(_sds((d, n_in), BF16), _bs((sq, bn_i), lambda i, j, k: (i, j)))],
                   deps=[small_send[4]])
    sent = send_grad("w_in", gw_in, (N_DEV, d, nsh_in))
    (dh1,) = _mm("in_proj_dx", "nt", (t // tb, d // sq, 1),
                 (dz_b, _bs((tb, n_in), lambda i, j, k: (i, 0))),
                 (wg_in, _bs((sq, n_in), lambda i, j, k: (j, 0))),
                 [(_sds((t, d), F32), _bs((tb, sq), lambda i, j, k: (i, j)))], deps=sent)
    grad_x, g_norm_mix = _rms_bwd("norm_mix_bwd", dh1, 0, xs, norm_mix_g, dres=dx2)

    (late_parts,) = _all_gather("gather_late_grads", [_pack([g_norm_mix])])
    packed, early_parts = _send_wait("small_wait", small_send, grad_x, False)
    early_parts = _own_block(early_parts, packed, me)

    grads, deltas, new_m, new_v = {}, {}, {}, {}
    for n in big:
        w2 = given[n][0]
        sent_blocks, landed = _send_wait("grad_wait_" + n, sends[n], grad_x, True)
        if sent_blocks.ndim == landed.ndim:
            own = lax.dynamic_index_in_dim(sent_blocks, me, 0, keepdims=False)
        else:
            own = lax.dynamic_slice_in_dim(sent_blocks, me * landed.shape[2], landed.shape[2], axis=1)
        parts = _own_block(landed, own, me)
        res = _adamw("adamw_" + n, parts, w2, given["m_" + n][0], given["v_" + n][0])
        grads[n], deltas[n], new_m[n], new_v[n] = [r.reshape(given[n].shape) for r in res]
    for tag, group, parts in (("early", small_early, early_parts), ("late", small_late, late_parts)):
        like = [given[n] for n in group]
        res = _adamw("adamw_small_" + tag, parts, _pack(like), _pack([given["m_" + n] for n in group]),
                     _pack([given["v_" + n] for n in group]))
        for store, buf in zip((grads, deltas, new_m, new_v), res):
            for n, a in zip(group, _unpack(buf, like)):
                store[n] = a

    return (loss, grad_x.reshape(x.shape), *[grads[n] for n in names], *[deltas[n] for n in names],
            *[new_m[n] for n in names], *[new_v[n] for n in names])
```

```python
import functools
import math

import jax
import jax.numpy as jnp
from jax import lax
from jax.experimental import pallas as pl
from jax.experimental.pallas import tpu as pltpu

F32, BF16 = jnp.float32, jnp.bfloat16
EPS = 1e-6
N_DEV = 8
D_MODEL = 2048
SSM_WIDTH = 1024
SSM_GROUP = 16
SSM_STATE = 64
SGU_HEADS = 8
SGU_CHUNK = 128
LANES = 128
SUBLANES = 8
N_SLAB = SSM_WIDTH // LANES
SLAB_STATE = (LANES // SSM_GROUP) * SSM_STATE
SCAN_BLOCK = 128
SLABS_PER_STEP = 8
VMEM_LIMIT = 48 * 1024 * 1024
ROW_BLOCK = 256
EP_ROWS = 128
ADAM_ROWS = 128
MESH_AXES = ("x", "y", "c")

ADAM_LR, ADAM_B1, ADAM_B2, ADAM_EPS, ADAM_WD, ADAM_STEP = 0.001, 0.9, 0.999, 1e-08, 0.01, 10

_GELU_C0 = math.sqrt(2.0 / math.pi)
_GELU_C1 = 0.044715


def _gelu(v):
    return 0.5 * v * (1.0 + jnp.tanh(_GELU_C0 * (v + _GELU_C1 * v * v * v)))


def _gelu_grad(v):
    th = jnp.tanh(_GELU_C0 * (v + _GELU_C1 * v * v * v))
    return 0.5 * (1.0 + th) + 0.5 * v * (1.0 - th * th) * _GELU_C0 * (1.0 + 3.0 * _GELU_C1 * v * v)


def _sigmoid(v):
    return 1.0 / (1.0 + jnp.exp(-v))


def _params(sem=None):
    return pltpu.CompilerParams(dimension_semantics=sem, vmem_limit_bytes=VMEM_LIMIT)


def _dot(a, b, mode="nn"):
    dims = {"nn": ((1,), (0,)), "nt": ((1,), (1,)), "tn": ((0,), (0,))}[mode]
    return lax.dot_general(a, b, (dims, ((), ())), preferred_element_type=F32)


def _mm(name, mode, grid, a, b, outs, extras=(), epilogue=None, deps=(), sums=(), acc_shape=None,
        ep_rows=None):
    nk = grid[2]
    n_ex, n_out, n_dep, n_sum = len(extras), len(outs), len(deps), len(sums)
    assert not sums or grid[1] == 1
    if acc_shape is None:
        acc_shape = tuple(d for d in outs[0][1].block_shape if d is not None)

    def body(*refs):
        a_ref, b_ref = refs[0], refs[1]
        ex = refs[2:2 + n_ex]
        out_refs = refs[2 + n_ex + n_dep:2 + n_ex + n_dep + n_out]
        sum_refs = refs[2 + n_ex + n_dep + n_out:2 + n_ex + n_dep + n_out + n_sum]
        acc = refs[-1]
        k = pl.program_id(2)

        @pl.when(k == 0)
        def _():
            acc[...] = jnp.zeros_like(acc)

        acc[...] += _dot(a_ref[...], b_ref[...], mode)

        def finish(rows):
            args = [e[rows, :] if e.shape[0] == acc_shape[0] else e[...] for e in ex]
            res = acc[rows, :]
            res = (res,) if epilogue is None else epilogue(res, *args)
            for o, r in zip(out_refs, res[:n_out]):
                o[rows, :] = r.astype(o.dtype)
            return tuple(res[n_out:])

        @pl.when(k == nk - 1)
        def _():
            if ep_rows is None:
                terms = finish(slice(None))
            else:
                def chunk(c, tot):
                    rows = pl.ds(pl.multiple_of(c * ep_rows, ep_rows), ep_rows)
                    return tuple(s + r for s, r in zip(tot, finish(rows)))

                zero = tuple(jnp.zeros(o.shape, F32) for o in sum_refs)
                terms = lax.fori_loop(0, acc_shape[0] // ep_rows, chunk, zero)
            for o, r in zip(sum_refs, terms):
                _add_up(o, r, pl.program_id(0) == 0)

    sem = ("arbitrary",) * 3 if sums else ("parallel", "parallel", "arbitrary")
    res = pl.pallas_call(
        body, name=name, grid=grid,
        in_specs=[a[1], b[1]] + [e[1] for e in extras] + [_any_spec()] * n_dep,
        out_specs=[o[1] for o in outs] + [o[1] for o in sums],
        out_shape=[o[0] for o in outs] + [o[0] for o in sums],
        scratch_shapes=[pltpu.VMEM(acc_shape, F32)],
        compiler_params=_params(sem),
    )(a[0], b[0], *[e[0] for e in extras], *deps)
    return res


def _add_up(ref, term, first):
    @pl.when(first)
    def _():
        ref[...] = term

    @pl.when(jnp.logical_not(first))
    def _():
        ref[...] += term


def _rms_math(xv, g):
    return xv * lax.rsqrt(jnp.mean(xv * xv, axis=-1, keepdims=True) + EPS) * g


def _rms_bwd_math(dy, xv, g):
    r = lax.rsqrt(jnp.mean(xv * xv, axis=-1, keepdims=True) + EPS)
    xhat = xv * r
    dxhat = dy * g
    dx = r * (dxhat - xhat * jnp.mean(dxhat * xhat, axis=-1, keepdims=True))
    return dx, jnp.sum(dy * xhat, axis=0, keepdims=True)


def _sds(shape, dtype):
    return jax.ShapeDtypeStruct(shape, dtype)


def _bs(shape, fn):
    return pl.BlockSpec(shape, fn)


def _rms_fwd(name, x, g, deps=()):
    t, w = x.shape
    br = min(ROW_BLOCK, t)

    def body(*refs):
        x_ref, g_ref, o_ref = refs[0], refs[1], refs[-1]
        xv = x_ref[...]
        r = lax.rsqrt(jnp.mean(xv * xv, axis=-1, keepdims=True) + EPS)
        o_ref[...] = (xv * r * g_ref[...]).astype(BF16)

    return pl.pallas_call(
        body, name=name, grid=(t // br,),
        in_specs=[_bs((br, w), lambda i: (i, 0)), _bs((1, w), lambda i: (0, 0))] + [_any_spec()] * len(deps),
        out_specs=_bs((br, w), lambda i: (i, 0)),
        out_shape=_sds((t, w), BF16),
        compiler_params=_params(("parallel",)),
    )(x, g, *deps)


def _mix_norm(ya, yb, ga, gb):
    t, w = ya.shape
    br = min(ROW_BLOCK, t)

    def body(a_ref, b_ref, ga_ref, gb_ref, o_ref):
        for src, g_ref, col in ((a_ref, ga_ref, 0), (b_ref, gb_ref, w)):
            v = src[...]
            r = lax.rsqrt(jnp.mean(v * v, axis=-1, keepdims=True) + EPS)
            o_ref[:, col:col + w] = (v * r * g_ref[...]).astype(BF16)

    row = _bs((br, w), lambda i: (i, 0))
    vec = _bs((1, w), lambda i: (0, 0))
    return pl.pallas_call(
        body, name="mix_norm", grid=(t // br,),
        in_specs=[row, row, vec, vec],
        out_specs=_bs((br, 2 * w), lambda i: (i, 0)),
        out_shape=_sds((t, 2 * w), BF16),
        compiler_params=_params(("parallel",)),
    )(ya, yb, ga, gb)


def _loss_math(xv, target, g):
    r = lax.rsqrt(jnp.mean(xv * xv, axis=-1, keepdims=True) + EPS)
    xhat = xv * r
    err = xhat * g - target
    dy = err * (1.0 / xv.shape[-1])
    dxhat = dy * g
    dx = r * (dxhat - xhat * jnp.mean(dxhat * xhat, axis=-1, keepdims=True))
    return dx, jnp.sum(dy * xhat, axis=0, keepdims=True), jnp.sum(err * err, axis=0, keepdims=True)


def _glu_pre_bwd(dy, ypre, gate):
    t, w = dy.shape
    br = min(ROW_BLOCK, t)

    def body(dy_ref, y_ref, gt_ref, dp_ref, db_ref):
        gt = gt_ref[...]
        dp = dy_ref[...] * _gelu(y_ref[...]) * gt * (1.0 - gt)
        dp_ref[...] = dp.astype(BF16)

        @pl.when(pl.program_id(0) == 0)
        def _():
            db_ref[...] = jnp.zeros_like(db_ref)

        db_ref[...] += jnp.sum(dp, axis=0, keepdims=True)

    row = _bs((br, w), lambda i: (i, 0))
    vec = _bs((1, w), lambda i: (0, 0))
    return pl.pallas_call(
        body, name="glu_pre_bwd", grid=(t // br,),
        in_specs=[row, row, row], out_specs=[row, vec],
        out_shape=[_sds((t, w), BF16), _sds((1, w), F32)],
        compiler_params=_params(("arbitrary",)),
    )(dy, ypre, gate)


def _prep_math(are, aim, ldt, bxr, bxi):
    dt = jnp.exp(ldt)
    er = jnp.exp(are * dt)
    th = aim * dt
    abr, abi = er * jnp.cos(th), er * jnp.sin(th)
    nr, ni = abr - 1.0, abi
    den = are * are + aim * aim
    cr = (nr * are + ni * aim) / den
    ci = (ni * are - nr * aim) / den
    bbr, bbi = [], []
    for j in range(N_SLAB):
        sl = slice(j * SLAB_STATE, (j + 1) * SLAB_STATE)
        bbr.append(cr[:, sl] * bxr[j] - ci[:, sl] * bxi[j])
        bbi.append(cr[:, sl] * bxi[j] + ci[:, sl] * bxr[j])
    return abr, abi, bbr, bbi


def _ssm_prep(are, aim, ldt, bxr, bxi, cxr, cxi):
    nst = N_SLAB * SLAB_STATE

    def body(are_r, aim_r, ldt_r, bxr_r, bxi_r, cxr_r, cxi_r,
             bre_o, bim_o, cre_o, cimn_o, apr_o, api_o, air_o, aii_o):
        abr, abi, bbr, bbi = _prep_math(are_r[...], aim_r[...], ldt_r[...],
                                        [bxr_r[j] for j in range(N_SLAB)], [bxi_r[j] for j in range(N_SLAB)])
        for j in range(N_SLAB):
            bre_o[j] = bbr[j].astype(BF16)
            bim_o[j] = bbi[j].astype(BF16)
        cre_o[...] = cxr_r[...].astype(BF16)
        cimn_o[...] = (-cxi_r[...]).astype(BF16)

        def step(k, cur):
            cr, ci = cur
            den = cr * cr + ci * ci
            apr_o[pl.ds(k, 1), :] = cr
            api_o[pl.ds(k, 1), :] = ci
            air_o[pl.ds(k, 1), :] = cr / den
            aii_o[pl.ds(k, 1), :] = -ci / den
            return cr * abr - ci * abi, cr * abi + ci * abr

        lax.fori_loop(0, SCAN_BLOCK, step, (jnp.ones((1, nst), F32), jnp.zeros((1, nst), F32)))

    tab = _sds((SCAN_BLOCK, nst), F32)
    return pl.pallas_call(
        body, name="ssm_prep",
        out_shape=[_sds(bxr.shape, BF16), _sds(bxr.shape, BF16), _sds(cxr.shape, BF16), _sds(cxr.shape, BF16),
                   tab, tab, tab, tab],
        compiler_params=_params(),
    )(are, aim, ldt, bxr, bxi, cxr, cxi)


def _ssm_prep_bwd(are, aim, ldt, bxr, bxi, dbre, dbim, qr, qi):
    def body(are_r, aim_r, ldt_r, bxr_r, bxi_r, dbre_r, dbim_r, qr_r, qi_r,
             dare_o, daim_o, dldt_o, dbxr_o, dbxi_o):
        prim = (are_r[...], aim_r[...], ldt_r[...],
                [bxr_r[j] for j in range(N_SLAB)], [bxi_r[j] for j in range(N_SLAB)])
        (abr, abi, _, _), vjp = jax.vjp(_prep_math, *prim)
        den = abr * abr + abi * abi
        q_r, q_i = qr_r[...], qi_r[...]
        gar = (q_r * abr - q_i * abi) / den
        gai = (q_r * abi + q_i * abr) / den
        ct = (gar, gai, [dbre_r[j] for j in range(N_SLAB)], [dbim_r[j] for j in range(N_SLAB)])
        dare, daim, dldt, dbxr, dbxi = vjp(ct)
        dare_o[...] = dare
        daim_o[...] = daim
        dldt_o[...] = dldt
        for j in range(N_SLAB):
            dbxr_o[j] = dbxr[j]
            dbxi_o[j] = dbxi[j]

    row = _sds(are.shape, F32)
    return pl.pallas_call(
        body, name="ssm_prep_bwd",
        out_shape=[row, row, row, _sds(bxr.shape, F32), _sds(bxr.shape, F32)],
        compiler_params=_params(),
    )(are, aim, ldt, bxr, bxi, dbre, dbim, qr, qi)


def _tri(lower):
    r = lax.broadcasted_iota(jnp.int32, (SCAN_BLOCK, SCAN_BLOCK), 0)
    c = lax.broadcasted_iota(jnp.int32, (SCAN_BLOCK, SCAN_BLOCK), 1)
    return jnp.where((r >= c) if lower else (r <= c), 1.0, 0.0).astype(BF16)


def _cumsum_mxu(tri, v):
    return _dot(tri, v.astype(BF16))


def _ssm_specs(t):
    nt = t // SCAN_BLOCK
    sps = SLABS_PER_STEP
    tab = _bs((SCAN_BLOCK, sps * SLAB_STATE), lambda j, i: (0, j))
    bmat = _bs((sps, LANES, SLAB_STATE), lambda j, i: (j, 0, 0))
    cmat = _bs((sps, SLAB_STATE, LANES), lambda j, i: (j, 0, 0))
    return nt, tab, bmat, cmat


def _slab_slices(s):
    return slice(s * LANES, (s + 1) * LANES), slice(s * SLAB_STATE, (s + 1) * SLAB_STATE)


def _ssm_fwd(z, dvec, bre, bim, cre, cimn, apr, api, air, aii, deps=()):
    t = z.shape[0]
    nt, tab, bmat, cmat = _ssm_specs(t)
    nst = N_SLAB * SLAB_STATE
    last = SCAN_BLOCK - 1

    sps = SLABS_PER_STEP

    def body(*refs):
        u_ref, d_ref, bre_r, bim_r, cre_r, cimn_r, apr_r, api_r, air_r, aii_r = refs[:10]
        y_ref, yg_ref, pr_ref, pi_ref, car_r, car_i = refs[10 + len(deps):]

        @pl.when(pl.program_id(1) == 0)
        def _():
            car_r[...] = jnp.zeros_like(car_r)
            car_i[...] = jnp.zeros_like(car_i)

        tri = _tri(True)
        for s in range(sps):
            ul, sl = _slab_slices(s)
            u = u_ref[:, ul]
            ub = u.astype(BF16)
            bur, bui = _dot(ub, bre_r[s]), _dot(ub, bim_r[s])
            ir, ii = air_r[:, sl], aii_r[:, sl]
            csr = _cumsum_mxu(tri, ir * bur - ii * bui)
            csi = _cumsum_mxu(tri, ir * bui + ii * bur)
            pr, pi = apr_r[:, sl], api_r[:, sl]
            a_r, a_i = apr_r[1:2, sl], api_r[1:2, sl]
            c_r, c_i = car_r[:, sl], car_i[:, sl]
            wr = csr + (a_r * c_r - a_i * c_i)
            wi = csi + (a_r * c_i + a_i * c_r)
            sr = pr * wr - pi * wi
            si = pr * wi + pi * wr
            car_r[:, sl] = sr[last:last + 1, :]
            car_i[:, sl] = si[last:last + 1, :]
            pr_ref[:, sl] = (sr - bur).astype(BF16)
            pi_ref[:, sl] = (si - bui).astype(BF16)
            y = _dot(sr.astype(BF16), cre_r[s]) + _dot(si.astype(BF16), cimn_r[s]) + d_ref[:, ul] * u
            y_ref[:, ul] = y
            yg_ref[:, ul] = _gelu(y).astype(BF16)

    ublk = _bs((SCAN_BLOCK, sps * LANES), lambda j, i: (i, j))
    sblk = _bs((SCAN_BLOCK, sps * SLAB_STATE), lambda j, i: (i, j))
    return pl.pallas_call(
        body, name="ssm_fwd", grid=(N_SLAB // sps, nt),
        in_specs=[ublk, _bs((1, sps * LANES), lambda j, i: (0, j)), bmat, bmat, cmat, cmat, tab, tab, tab, tab]
        + [_any_spec()] * len(deps),
        out_specs=[ublk, ublk, sblk, sblk],
        out_shape=[_sds((t, SSM_WIDTH), F32), _sds((t, SSM_WIDTH), BF16),
                   _sds((t, nst), BF16), _sds((t, nst), BF16)],
        scratch_shapes=[pltpu.VMEM((1, sps * SLAB_STATE), F32), pltpu.VMEM((1, sps * SLAB_STATE), F32)],
        compiler_params=_params(("parallel", "arbitrary")),
    )(z, dvec, bre, bim, cre, cimn, apr, api, air, aii, *deps)


def _ssm_bwd(gy, z, p_re, p_im, dvec, bre, bim, cre, cimn, apr, api, air, aii):
    t = z.shape[0]
    nt, tab, bmat, cmat = _ssm_specs(t)
    last = SCAN_BLOCK - 1

    def fold(v):
        return v.reshape(SCAN_BLOCK // SUBLANES, SUBLANES, v.shape[-1]).sum(axis=0)

    sps = SLABS_PER_STEP

    def body(g_ref, u_ref, pr_ref, pi_ref, d_ref, bre_r, bim_r, cre_r, cimn_r, apr_r, api_r, air_r, aii_r,
             du_ref, dbre_o, dbim_o, dcre_o, dcimn_o, qr_o, qi_o, dd_o, car_r, car_i, qacc_r, qacc_i, dacc):
        i = pl.program_id(1)

        @pl.when(i == 0)
        def _():
            for ref in (car_r, car_i, qacc_r, qacc_i, dacc, dbre_o, dbim_o, dcre_o, dcimn_o):
                ref[...] = jnp.zeros_like(ref)

        tri = _tri(False)
        for s in range(sps):
            ul, sl = _slab_slices(s)
            g = g_ref[:, ul]
            gb = g.astype(BF16)
            u = u_ref[:, ul]
            ub = u.astype(BF16)
            bur, bui = _dot(ub, bre_r[s]), _dot(ub, bim_r[s])
            p_r, p_i = pr_ref[:, sl].astype(F32), pi_ref[:, sl].astype(F32)
            srb, sib = (p_r + bur).astype(BF16), (p_i + bui).astype(BF16)
            dcre_o[s] += _dot(srb, gb, "tn")
            dcimn_o[s] += _dot(sib, gb, "tn")
            dsr, dsi = _dot(gb, cre_r[s], "nt"), _dot(gb, cimn_r[s], "nt")
            pr, pi = apr_r[:, sl], api_r[:, sl]
            csr = _cumsum_mxu(tri, pr * dsr + pi * dsi)
            csi = _cumsum_mxu(tri, pr * dsi - pi * dsr)
            al_r, al_i = apr_r[last:last + 1, sl], api_r[last:last + 1, sl]
            c_r, c_i = car_r[:, sl], car_i[:, sl]
            wr = csr + (al_r * c_r + al_i * c_i)
            wi = csi + (al_r * c_i - al_i * c_r)
            ir, ii = air_r[:, sl], aii_r[:, sl]
            lr = ir * wr + ii * wi
            li = ir * wi - ii * wr
            a_r, a_i = apr_r[1:2, sl], api_r[1:2, sl]
            car_r[:, sl] = a_r * lr[0:1, :] + a_i * li[0:1, :]
            car_i[:, sl] = a_r * li[0:1, :] - a_i * lr[0:1, :]
            lrb, lib = lr.astype(BF16), li.astype(BF16)
            dbre_o[s] += _dot(ub, lrb, "tn")
            dbim_o[s] += _dot(ub, lib, "tn")
            du = d_ref[:, ul] * g + _dot(lrb, bre_r[s], "nt") + _dot(lib, bim_r[s], "nt")
            du_ref[:, ul] = du.astype(BF16)
            qacc_r[:, sl] += fold(lr * p_r + li * p_i)
            qacc_i[:, sl] += fold(li * p_r - lr * p_i)
            dacc[:, ul] += fold(g * u)

        @pl.when(i == nt - 1)
        def _():
            qr_o[...] = jnp.sum(qacc_r[...], axis=0, keepdims=True)
            qi_o[...] = jnp.sum(qacc_i[...], axis=0, keepdims=True)
            dd_o[...] = jnp.sum(dacc[...], axis=0, keepdims=True)

    rev = lambda j, i: (nt - 1 - i, j)
    ublk = _bs((SCAN_BLOCK, sps * LANES), rev)
    sblk = _bs((SCAN_BLOCK, sps * SLAB_STATE), rev)
    qrow = _bs((1, sps * SLAB_STATE), lambda j, i: (0, j))
    urow = _bs((1, sps * LANES), lambda j, i: (0, j))
    nst = N_SLAB * SLAB_STATE
    return pl.pallas_call(
        body, name="ssm_bwd", grid=(N_SLAB // sps, nt),
        in_specs=[ublk, ublk, sblk, sblk, urow, bmat, bmat, cmat, cmat, tab, tab, tab, tab],
        out_specs=[ublk, bmat, bmat, cmat, cmat, qrow, qrow, urow],
        out_shape=[_sds((t, SSM_WIDTH), BF16),
                   _sds((N_SLAB, LANES, SLAB_STATE), F32), _sds((N_SLAB, LANES, SLAB_STATE), F32),
                   _sds((N_SLAB, SLAB_STATE, LANES), F32), _sds((N_SLAB, SLAB_STATE, LANES), F32),
                   _sds((1, nst), F32), _sds((1, nst), F32), _sds((1, SSM_WIDTH), F32)],
        scratch_shapes=[pltpu.VMEM((1, sps * SLAB_STATE), F32), pltpu.VMEM((1, sps * SLAB_STATE), F32),
                        pltpu.VMEM((SUBLANES, sps * SLAB_STATE), F32), pltpu.VMEM((SUBLANES, sps * SLAB_STATE), F32),
                        pltpu.VMEM((SUBLANES, sps * LANES), F32)],
        compiler_params=_params(("parallel", "arbitrary")),
    )(gy, z, p_re, p_im, dvec, bre, bim, cre, cimn, apr, api, air, aii)


def _sgu_mask():
    r = lax.broadcasted_iota(jnp.int32, (SGU_CHUNK, SGU_CHUNK), 0)
    c = lax.broadcasted_iota(jnp.int32, (SGU_CHUNK, SGU_CHUNK), 1)
    return r >= c


def _sgu_common(zu, zv, lng, lnb):
    us, v = _gelu(zu), _gelu(zv)
    mu = jnp.mean(v, axis=-1, keepdims=True)
    vc = v - mu
    rstd = lax.rsqrt(jnp.mean(vc * vc, axis=-1, keepdims=True) + EPS)
    xhat = vc * rstd
    return us, xhat, rstd, xhat * lng + lnb


def _sgu_fwd(z, lng, lnb, w, bexp):
    t = z.shape[0]
    hd = SGU_CHUNK

    def body(zu_ref, zv_ref, lng_ref, lnb_ref, w_ref, b_ref, y_ref):
        us, _, _, vn = _sgu_common(zu_ref[...], zv_ref[...], lng_ref[...], lnb_ref[...])
        vnb = vn.astype(BF16)
        mask = _sgu_mask()
        for h in range(SGU_HEADS):
            sl = slice(h * hd, (h + 1) * hd)
            wt = jnp.where(mask, w_ref[h], 0.0).astype(BF16)
            y_ref[:, sl] = us[:, sl] * (_dot(wt, vnb[:, sl]) + b_ref[h])

    row = lambda c: _bs((SGU_CHUNK, SSM_WIDTH), lambda i: (i, c))
    vec = _bs((1, SSM_WIDTH), lambda i: (0, 0))
    hmat = _bs((SGU_HEADS, hd, hd), lambda i: (0, 0, 0))
    return pl.pallas_call(
        body, name="sgu_fwd", grid=(t // SGU_CHUNK,),
        in_specs=[row(1), row(2), vec, vec, hmat, hmat],
        out_specs=row(0), out_shape=_sds((t, SSM_WIDTH), F32),
        compiler_params=_params(("parallel",)),
    )(z, z, lng, lnb, w, bexp)


def _sgu_bwd(dy, du_ssm, z, lng, lnb, w, bexp):
    t = z.shape[0]
    hd = SGU_CHUNK
    nc = t // SGU_CHUNK

    def body(dy_ref, dus_ref, zu_ref, zv_ref, lng_ref, lnb_ref, w_ref, b_ref,
             dz_ref, dlng_o, dlnb_o, dw_o, db_o):
        i = pl.program_id(0)

        @pl.when(i == 0)
        def _():
            for ref in (dlng_o, dlnb_o, dw_o, db_o):
                ref[...] = jnp.zeros_like(ref)

        zu, zv = zu_ref[...], zv_ref[...]
        lng = lng_ref[...]
        us, xhat, rstd, vn = _sgu_common(zu, zv, lng, lnb_ref[...])
        vnb = vn.astype(BF16)
        dyv = dy_ref[...]
        mask = _sgu_mask()
        dus_parts, dvn_parts = [], []
        for h in range(SGU_HEADS):
            sl = slice(h * hd, (h + 1) * hd)
            wt = jnp.where(mask, w_ref[h], 0.0).astype(BF16)
            mixed = _dot(wt, vnb[:, sl]) + b_ref[h]
            dus_parts.append(dyv[:, sl] * mixed)
            dmix = dyv[:, sl] * us[:, sl]
            dmb = dmix.astype(BF16)
            db_o[h] += dmix
            dw_o[h] += _dot(dmb, vnb[:, sl], "nt")
            dvn_parts.append(_dot(wt, dmb, "tn"))
        dus = jnp.concatenate(dus_parts, axis=1)
        dvn = jnp.concatenate(dvn_parts, axis=1)
        dlng_o[...] += jnp.sum(dvn * xhat, axis=0, keepdims=True)
        dlnb_o[...] += jnp.sum(dvn, axis=0, keepdims=True)
        dxh = dvn * lng
        dv = rstd * (dxh - jnp.mean(dxh, axis=-1, keepdims=True)
                     - xhat * jnp.mean(dxh * xhat, axis=-1, keepdims=True))
        dz_ref[:, 0:SSM_WIDTH] = dus_ref[...]
        dz_ref[:, SSM_WIDTH:2 * SSM_WIDTH] = (dus * _gelu_grad(zu)).astype(BF16)
        dz_ref[:, 2 * SSM_WIDTH:] = (dv * _gelu_grad(zv)).astype(BF16)

        @pl.when(i == nc - 1)
        def _():
            for h in range(SGU_HEADS):
                dw_o[h] = jnp.where(mask, dw_o[h], 0.0)
                db_o[h] = jnp.broadcast_to(jnp.sum(db_o[h], axis=1, keepdims=True), (hd, hd))

    row = lambda c: _bs((SGU_CHUNK, SSM_WIDTH), lambda i: (i, c))
    vec = _bs((1, SSM_WIDTH), lambda i: (0, 0))
    hmat = _bs((SGU_HEADS, hd, hd), lambda i: (0, 0, 0))
    return pl.pallas_call(
        body, name="sgu_bwd", grid=(nc,),
        in_specs=[row(0), row(0), row(1), row(2), vec, vec, hmat, hmat],
        out_specs=[_bs((SGU_CHUNK, 3 * SSM_WIDTH), lambda i: (i, 0)), vec, vec, hmat, hmat],
        out_shape=[_sds((t, 3 * SSM_WIDTH), BF16), _sds((1, SSM_WIDTH), F32), _sds((1, SSM_WIDTH), F32),
                   _sds((SGU_HEADS, hd, hd), F32), _sds((SGU_HEADS, hd, hd), F32)],
        compiler_params=_params(("arbitrary",)),
    )(dy, du_ssm, z, z, lng, lnb, w, bexp)


def _place():
    x, y, c = (lax.axis_index(a) for a in MESH_AXES)
    return x, y, c


def _index(p):
    return 4 * p[0] + 2 * p[1] + p[2]


def _any_spec():
    return pl.BlockSpec(memory_space=pl.ANY)


def _col_block(ref, k, width):
    return ref.at[:, pl.ds(pl.multiple_of(k * width, LANES), width)]


def _all_gather(name, shards, by_columns=False):
    n = len(shards)

    def body(*refs):
        ins, outs = refs[:n], refs[n:2 * n]
        send, recv, loc = refs[2 * n:]
        x, y, c = _place()
        me, sib = (x, y, c), (x, y, 1 - c)
        chips = [(1 - x, y), (x, 1 - y), (1 - x, 1 - y)]

        def blk(w, p):
            if by_columns:
                return _col_block(outs[w], _index(p), shards[w].shape[1])
            return outs[w].at[_index(p)]

        def cp(w, k, block, to, src=None):
            dst = blk(w, block)
            return pltpu.make_async_remote_copy(
                src_ref=dst if src is None else src, dst_ref=dst,
                send_sem=send.at[w * 7 + k], recv_sem=recv.at[w * 7 + k],
                device_id=to, device_id_type=pl.DeviceIdType.MESH)

        mines, sends = [], []
        for w in range(n):
            m = pltpu.make_async_copy(ins[w], blk(w, me), loc.at[w])
            m.start()
            mines.append(m)
            first = [cp(w, 0, me, sib, src=ins[w])]
            first += [cp(w, 1 + j, me, (*chip, c), src=ins[w]) for j, chip in enumerate(chips)]
            for q in first:
                q.start()
            sends += first
        for j, chip in enumerate(chips):
            for w in range(n):
                cp(w, 1 + j, (*chip, c), me).wait_recv()
                q = cp(w, 4 + j, (*chip, c), sib)
                q.start()
                sends.append(q)
        for w in range(n):
            cp(w, 0, sib, me).wait_recv()
            for j, chip in enumerate(chips):
                cp(w, 4 + j, (*chip, 1 - c), me).wait_recv()
        for q in sends:
            q.wait_send()
        for m in mines:
            m.wait()

    return pl.pallas_call(
        body, name=name,
        in_specs=[_any_spec()] * n, out_specs=[_any_spec()] * n,
        out_shape=[_sds((s.shape[0], N_DEV * s.shape[1]) if by_columns else (N_DEV,) + s.shape, s.dtype)
                   for s in shards],
        scratch_shapes=[pltpu.SemaphoreType.DMA((n * 7,)), pltpu.SemaphoreType.DMA((n * 7,)),
                        pltpu.SemaphoreType.DMA((n,))],
        compiler_params=pltpu.CompilerParams(has_side_effects=True),
    )(*shards)


def _peer(r, x, y, c):
    return ((1 - x) if r & 4 else x, (1 - y) if r & 2 else y, (1 - c) if r & 1 else c)


def _sent_block(src_ref, land_ref, k, scatter):
    if not scatter:
        return src_ref
    if len(src_ref.shape) == len(land_ref.shape):
        return src_ref.at[k]
    return _col_block(src_ref, k, land_ref.shape[2])


def _send_start(name, src, land, scatter, after=None):
    n_after = 0 if after is None else 1

    def body(*refs):
        src_ref, land_ref = refs[0], refs[1]
        send, recv, _, _, token = refs[2 + n_after:]
        x, y, c = _place()
        me = _index((x, y, c))
        for r in range(1, N_DEV):
            p = _peer(r, x, y, c)
            pltpu.make_async_remote_copy(
                src_ref=_sent_block(src_ref, land_ref, _index(p), scatter), dst_ref=land_ref.at[me],
                send_sem=send.at[r - 1], recv_sem=recv.at[r - 1],
                device_id=p, device_id_type=pl.DeviceIdType.MESH).start()
        token[...] = jnp.zeros_like(token)

    hbm, sem = pl.BlockSpec(memory_space=pltpu.HBM), pl.BlockSpec(memory_space=pltpu.SEMAPHORE)
    return pl.pallas_call(
        body, name=name,
        out_shape=(pltpu.SemaphoreType.DMA((N_DEV - 1,)), pltpu.SemaphoreType.DMA((N_DEV - 1,)),
                   pltpu.HBM(src.shape, src.dtype), pltpu.HBM(land.shape, land.dtype),
                   _sds((SUBLANES, LANES), F32)),
        in_specs=(hbm, hbm) + (_any_spec(),) * n_after,
        out_specs=(sem, sem, hbm, hbm, pl.BlockSpec(memory_space=pltpu.VMEM)),
        input_output_aliases={0: 2, 1: 3},
        compiler_params=pltpu.CompilerParams(has_side_effects=pltpu.SideEffectType.DATAFLOW_SIDE_EFFECTING),
    )(pltpu.with_memory_space_constraint(src, pltpu.HBM), pltpu.with_memory_space_constraint(land, pltpu.HBM),
      *([] if after is None else [after]))


def _send_wait(name, started, after, scatter):
    send, recv, src_thru, land_thru, _ = started

    def body(src_ref, land_ref, send_r, recv_r, after_ref, src_out, land_out):
        x, y, c = _place()
        for r in range(1, N_DEV):
            p = _peer(r, x, y, c)
            k = _index(p)
            cp = pltpu.make_async_remote_copy(
                src_ref=_sent_block(src_ref, land_ref, k, scatter), dst_ref=land_ref.at[k],
                send_sem=send_r.at[r - 1], recv_sem=recv_r.at[r - 1],
                device_id=p, device_id_type=pl.DeviceIdType.MESH)
            cp.wait_send()
            cp.wait_recv()

    hbm, sem = pl.BlockSpec(memory_space=pltpu.HBM), pl.BlockSpec(memory_space=pltpu.SEMAPHORE)
    return pl.pallas_call(
        body, name=name,
        out_shape=(pltpu.HBM(src_thru.shape, src_thru.dtype), pltpu.HBM(land_thru.shape, land_thru.dtype)),
        in_specs=(hbm, hbm, sem, sem, _any_spec()), out_specs=(hbm, hbm),
        input_output_aliases={0: 0, 1: 1},
        compiler_params=pltpu.CompilerParams(has_side_effects=pltpu.SideEffectType.DATAFLOW_SIDE_EFFECTING),
    )(src_thru, land_thru, send, recv, after)


def _own_block(blocks, block, me):
    return lax.dynamic_update_index_in_dim(blocks, block, me, 0)


def _adamw(name, parts, w, m, v):
    rows, cols = w.shape
    br = min(ADAM_ROWS, rows)
    c1 = 1.0 / (1.0 - ADAM_B1 ** ADAM_STEP)
    c2 = 1.0 / (1.0 - ADAM_B2 ** ADAM_STEP)

    def body(p_ref, w_ref, m_ref, v_ref, g_o, d_o, m_o, v_o):
        g = p_ref[0].astype(F32)
        for k in range(1, N_DEV):
            g = g + p_ref[k].astype(F32)
        mn = ADAM_B1 * m_ref[...] + (1.0 - ADAM_B1) * g
        vn = ADAM_B2 * v_ref[...] + (1.0 - ADAM_B2) * (g * g)
        g_o[...] = g
        m_o[...] = mn
        v_o[...] = vn
        d_o[...] = -ADAM_LR * ((mn * c1) / (jnp.sqrt(vn * c2) + ADAM_EPS) + ADAM_WD * w_ref[...])

    blk = _bs((br, cols), lambda i: (i, 0))
    out = _sds((rows, cols), F32)
    return pl.pallas_call(
        body, name=name, grid=(rows // br,),
        in_specs=[_bs((N_DEV, br, cols), lambda i: (0, i, 0)), blk, blk, blk],
        out_specs=[blk] * 4, out_shape=[out] * 4,
        compiler_params=_params(("parallel",)),
    )(parts, w, m, v)


def _pack(arrs):
    tile = SUBLANES * LANES
    flat = []
    for a in arrs:
        f = a.reshape(-1).astype(F32)
        pad = (-f.shape[0]) % tile
        flat.append(jnp.pad(f, (0, pad)) if pad else f)
    total = sum(f.shape[0] for f in flat)
    tail = (-total) % (ADAM_ROWS * LANES)
    if tail:
        flat.append(jnp.zeros((tail,), F32))
    return jnp.concatenate(flat).reshape(-1, LANES)


def _unpack(buf, like):
    tile = SUBLANES * LANES
    flat = buf.reshape(-1)
    out, off = [], 0
    for a in like:
        n = math.prod(a.shape)
        out.append(flat[off:off + n].reshape(a.shape))
        off += n + ((-n) % tile)
    return out


def _to_block_b(b):
    gl = LANES // SSM_GROUP
    tb = b.reshape(N_SLAB, gl, SSM_STATE, SSM_GROUP).transpose(0, 1, 3, 2)
    eye = jnp.eye(gl, dtype=F32)
    return (tb[:, :, :, None, :] * eye[None, :, None, :, None]).reshape(N_SLAB, LANES, SLAB_STATE)


def _from_block_b(bx):
    gl = LANES // SSM_GROUP
    d = jnp.einsum("jghgp->jgph", bx.reshape(N_SLAB, gl, SSM_GROUP, gl, SSM_STATE))
    return d.reshape(N_SLAB * gl, SSM_STATE, SSM_GROUP)


def _to_block_c(cm):
    gl = LANES // SSM_GROUP
    tc = cm.reshape(N_SLAB, gl, SSM_GROUP, SSM_STATE).transpose(0, 1, 3, 2)
    eye = jnp.eye(gl, dtype=F32)
    return (tc[:, :, :, None, :] * eye[None, :, None, :, None]).reshape(N_SLAB, SLAB_STATE, LANES)


def _from_block_c(cx):
    gl = LANES // SSM_GROUP
    d = jnp.einsum("jgpgh->jghp", cx.reshape(N_SLAB, gl, SSM_STATE, gl, SSM_GROUP))
    return d.reshape(N_SLAB * gl, SSM_GROUP, SSM_STATE)


def kernel(x, norm_mix_g, w_in, ssm_a_re, ssm_a_im, ssm_b_re, ssm_b_im, ssm_c_re, ssm_c_im, ssm_d, ssm_log_dt, ssm_glu_w, ssm_glu_b, sgu_ln_g, sgu_ln_b, sgu_w, sgu_b, out_norm_ssm_g, out_norm_sgu_g, w_out, norm_mlp_g, w_up, w_down, norm_final_g, loss_target, m_norm_mix_g, m_w_in, m_ssm_a_re, m_ssm_a_im, m_ssm_b_re, m_ssm_b_im, m_ssm_c_re, m_ssm_c_im, m_ssm_d, m_ssm_log_dt, m_ssm_glu_w, m_ssm_glu_b, m_sgu_ln_g, m_sgu_ln_b, m_sgu_w, m_sgu_b, m_out_norm_ssm_g, m_out_norm_sgu_g, m_w_out, m_norm_mlp_g, m_w_up, m_w_down, m_norm_final_g, v_norm_mix_g, v_w_in, v_ssm_a_re, v_ssm_a_im, v_ssm_b_re, v_ssm_b_im, v_ssm_c_re, v_ssm_c_im, v_ssm_d, v_ssm_log_dt, v_ssm_glu_w, v_ssm_glu_b, v_sgu_ln_g, v_sgu_ln_b, v_sgu_w, v_sgu_b, v_out_norm_ssm_g, v_out_norm_sgu_g, v_w_out, v_norm_mlp_g, v_w_up, v_w_down, v_norm_final_g):
    given = dict(locals())
    names = ["norm_mix_g", "w_in", "ssm_a_re", "ssm_a_im", "ssm_b_re", "ssm_b_im", "ssm_c_re", "ssm_c_im",
             "ssm_d", "ssm_log_dt", "ssm_glu_w", "ssm_glu_b", "sgu_ln_g", "sgu_ln_b", "sgu_w", "sgu_b",
             "out_norm_ssm_g", "out_norm_sgu_g", "w_out", "norm_mlp_g", "w_up", "w_down", "norm_final_g"]
    big = ["w_in", "ssm_glu_w", "w_out", "w_up", "w_down"]
    small = [n for n in names if n not in big]

    d = D_MODEL
    t = x.shape[1]
    tb = min(1024, t)
    xs = x[0]
    target = loss_target[0]
    nsh_in = w_in.shape[2]
    nsh_up = w_up.shape[2]
    d_ff = nsh_up * N_DEV
    n_in = nsh_in * N_DEV

    me = _index(_place())
    (wg_in,) = _all_gather("gather_w_in", [w_in[0].astype(BF16)], by_columns=True)
    gathers = {}

    def start_gather(n, after=None):
        shard = given[n][0].astype(BF16)
        gathers[n] = _send_start("gather_start_" + n, shard, lax.empty((N_DEV,) + shard.shape, BF16), False, after)
        return gathers[n][4]

    tokens = [start_gather("ssm_glu_w", wg_in), start_gather("w_out", wg_in)]

    def gathered(n, after):
        shard, blocks = _send_wait("gather_wait_" + n, gathers[n], after, False)
        return _own_block(blocks, shard, me)

    nst = N_SLAB * SLAB_STATE
    are, aim = ssm_a_re.reshape(1, nst), ssm_a_im.reshape(1, nst)
    ldt = jnp.repeat(ssm_log_dt[0], SSM_STATE).reshape(1, nst)
    bxr, bxi = _to_block_b(ssm_b_re[0]), _to_block_b(ssm_b_im[0])
    cxr, cxi = _to_block_c(ssm_c_re[0]), _to_block_c(ssm_c_im[0])
    dvec = ssm_d.reshape(1, SSM_WIDTH)
    bre, bim, cre, cimn, apr, api, air, aii = _ssm_prep(are, aim, ldt, bxr, bxi, cxr, cxi)
    tabs = (bre, bim, cre, cimn, apr, api, air, aii)

    h1 = _rms_fwd("norm_mix", xs, norm_mix_g, deps=tokens)
    bn_i = n_in // 2
    (z,) = _mm("in_proj", "nn", (t // tb, n_in // bn_i, 1),
               (h1, _bs((tb, d), lambda i, j, k: (i, 0))),
               (wg_in, _bs((d, bn_i), lambda i, j, k: (0, j))),
               [(_sds((t, n_in), F32), _bs((tb, bn_i), lambda i, j, k: (i, j)))])
    tokens = [start_gather("w_up", z), start_gather("w_down", z)]
    y_pre, yg_b, p_re, p_im = _ssm_fwd(z, dvec, *tabs, deps=tokens)

    def glu_ep(acc, yp, b):
        gate = _sigmoid(acc + b)
        return _gelu(yp) * gate, gate

    hw = SSM_WIDTH // 2
    wg_glu = gathered("ssm_glu_w", yg_b).reshape(SSM_WIDTH, SSM_WIDTH)
    tile_g = _bs((tb, hw), lambda i, j, k: (i, j))
    y_ssm, gate = _mm("glu", "nn", (t // tb, 2, 1),
                      (yg_b, _bs((tb, SSM_WIDTH), lambda i, j, k: (i, 0))),
                      (wg_glu, _bs((SSM_WIDTH, hw), lambda i, j, k: (0, j))),
                      [(_sds((t, SSM_WIDTH), F32), tile_g), (_sds((t, SSM_WIDTH), F32), tile_g)],
                      extras=[(y_pre, tile_g), (ssm_glu_b, _bs((1, hw), lambda i, j, k: (0, j)))],
                      epilogue=glu_ep)

    sgu_bexp = jnp.broadcast_to(sgu_b[0][:, :, None], (SGU_HEADS, SGU_CHUNK, SGU_CHUNK))
    y_sgu = _sgu_fwd(z, sgu_ln_g, sgu_ln_b, sgu_w[0], sgu_bexp)
    mixed = _mix_norm(y_ssm, y_sgu, out_norm_ssm_g, out_norm_sgu_g)

    tb2 = min(512, t)
    row2 = _bs((tb2, d), lambda i, j, k: (i, 0))
    vec2 = _bs((1, d), lambda i, j, k: (0, 0))
    vec_sum = (_sds((1, d), F32), vec2)
    wg_out = gathered("w_out", mixed).reshape(d, d)

    def out_ep(acc, r, g):
        x2v = acc + r
        return x2v, _rms_math(x2v, g)

    x2, h2 = _mm("out_proj", "nn", (t // tb2, 1, 1),
                 (mixed, row2), (wg_out, _bs((d, d), lambda i, j, k: (0, 0))),
                 [(_sds((t, d), F32), row2), (_sds((t, d), BF16), row2)],
                 extras=[(xs, row2), (norm_mlp_g, vec2)], epilogue=out_ep, ep_rows=EP_ROWS)

    def up_ep(acc):
        r = jnp.maximum(acc, 0.0)
        return r * r, r

    tile_f = _bs((tb, nsh_up), lambda i, j, k: (i, j))
    wg_up = gathered("w_up", h2)
    f_act, r_act = _mm("mlp_up", "nn", (t // tb, N_DEV, 1),
                       (h2, _bs((tb, d), lambda i, j, k: (i, 0))),
                       (wg_up, _bs((None, d, nsh_up), lambda i, j, k: (j, 0, 0))),
                       [(_sds((t, d_ff), BF16), tile_f), (_sds((t, d_ff), BF16), tile_f)],
                       epilogue=up_ep)
    bk_d = 1024
    wg_down = gathered("w_down", f_act).reshape(d_ff, d)

    def down_ep(acc, r, tg, g):
        dx, dg, e2 = _loss_math(acc + r, tg, g)
        return dx, dx, dg, e2

    dx3, dx3_b, g_final, err2 = _mm("mlp_down", "nn", (t // tb2, 1, d_ff // bk_d),
                                    (f_act, _bs((tb2, bk_d), lambda i, j, k: (i, k))),
                                    (wg_down, _bs((bk_d, d), lambda i, j, k: (k, 0))),
                                    [(_sds((t, d), F32), row2), (_sds((t, d), BF16), row2)],
                                    extras=[(x2, row2), (target, row2), (norm_final_g.reshape(1, d), vec2)],
                                    epilogue=down_ep, sums=[vec_sum, vec_sum], ep_rows=EP_ROWS)
    loss = lax.psum(0.5 * jnp.sum(err2) / d, MESH_AXES)

    sends = {}

    def send_grad(n, g, land_shape=None):
        sends[n] = _send_start("grad_start_" + n, g, lax.empty(land_shape or g.shape, BF16), True)
        return [sends[n][4]]

    bn_a = 1024
    tile_a = _bs((tb, bn_a), lambda i, j, k: (i, j))
    (da,) = _mm("mlp_down_dx", "nt", (t // tb, d_ff // bn_a, 1),
                (dx3_b, _bs((tb, d), lambda i, j, k: (i, 0))),
                (wg_down, _bs((bn_a, d), lambda i, j, k: (j, 0))),
                [(_sds((t, d_ff), BF16), tile_a)],
                extras=[(r_act, tile_a)], epilogue=lambda acc, r: (acc * (2.0 * r.astype(F32)),))
    sq = 1024
    (gw_down,) = _mm("mlp_down_dw", "tn", (d_ff // sq, 1, t // tb),
                     (f_act, _bs((tb, sq), lambda i, j, k: (k, i))),
                     (dx3_b, _bs((tb, d), lambda i, j, k: (k, 0))),
                     [(_sds((d_ff, d), BF16), _bs((sq, d), lambda i, j, k: (i, 0)))])
    sent = send_grad("w_down", gw_down.reshape(N_DEV, -1, d))
    (gw_up,) = _mm("mlp_up_dw", "tn", (1, N_DEV, t // tb),
                   (h2, _bs((tb, d), lambda i, j, k: (k, 0))),
                   (da, _bs((tb, nsh_up), lambda i, j, k: (k, j))),
                   [(_sds((N_DEV, d, nsh_up), BF16), _bs((None, d, nsh_up), lambda i, j, k: (j, 0, 0)))],
                   deps=sent)
    sent = send_grad("w_up", gw_up)
    def norm_bwd_ep(acc, xv, dres, g):
        dx, dg = _rms_bwd_math(acc, xv, g)
        dx = dx + dres
        return dx, dx, dg

    dx2, dx2_b, g_norm_mlp = _mm("mlp_up_dx", "nt", (t // tb2, 1, N_DEV),
                                 (da, _bs((tb2, nsh_up), lambda i, j, k: (i, k))),
                                 (wg_up, _bs((None, d, nsh_up), lambda i, j, k: (k, 0, 0))),
                                 [(_sds((t, d), F32), row2), (_sds((t, d), BF16), row2)],
                                 extras=[(x2, row2), (dx3, row2), (norm_mlp_g, vec2)],
                                 epilogue=norm_bwd_ep, sums=[vec_sum], deps=sent, ep_rows=EP_ROWS)

    tk = min(2048, t)
    (gw_out,) = _mm("out_proj_dw", "tn", (d // sq, d // sq, t // tk),
                    (mixed, _bs((tk, sq), lambda i, j, k: (k, i))),
                    (dx2_b, _bs((tk, sq), lambda i, j, k: (k, j))),
                    [(_sds((d, d), BF16), _bs((sq, sq), lambda i, j, k: (i, j)))])
    sent = send_grad("w_out", gw_out.reshape(N_DEV, -1, d))
    half2 = _bs((tb2, SSM_WIDTH), lambda i, j, k: (i, 0))
    vech = _bs((1, SSM_WIDTH), lambda i, j, k: (0, 0))
    half_sum = (_sds((1, SSM_WIDTH), F32), vech)

    def out_dx_ep(acc, ya, yb, ga, gb):
        dya, dga = _rms_bwd_math(acc[:, :SSM_WIDTH], ya, ga)
        dyb, dgb = _rms_bwd_math(acc[:, SSM_WIDTH:], yb, gb)
        return dya, dyb, dga, dgb

    dy_ssm, dy_sgu, g_onorm_ssm, g_onorm_sgu = _mm(
        "out_proj_dx", "nt", (t // tb2, 1, 1),
        (dx2_b, row2), (wg_out, _bs((d, d), lambda i, j, k: (0, 0))),
        [(_sds((t, SSM_WIDTH), F32), half2), (_sds((t, SSM_WIDTH), F32), half2)],
        extras=[(y_ssm, half2), (y_sgu, half2), (out_norm_ssm_g, vech), (out_norm_sgu_g, vech)],
        epilogue=out_dx_ep, sums=[half_sum, half_sum], deps=sent, acc_shape=(tb2, d), ep_rows=EP_ROWS)

    dpre_b, g_glu_b = _glu_pre_bwd(dy_ssm, y_pre, gate)
    (gw_glu,) = _mm("glu_dw", "tn", (1, 1, t // tb),
                    (yg_b, _bs((tb, SSM_WIDTH), lambda i, j, k: (k, 0))),
                    (dpre_b, _bs((tb, SSM_WIDTH), lambda i, j, k: (k, 0))),
                    [(_sds((SSM_WIDTH, SSM_WIDTH), BF16), _bs((SSM_WIDTH, SSM_WIDTH), lambda i, j, k: (0, 0)))])
    sent = send_grad("ssm_glu_w", gw_glu.reshape(N_DEV, -1, SSM_WIDTH))
    (dy_pre,) = _mm("glu_dx", "nt", (t // tb, 2, 1),
                    (dpre_b, _bs((tb, SSM_WIDTH), lambda i, j, k: (i, 0))),
                    (wg_glu, _bs((hw, SSM_WIDTH), lambda i, j, k: (j, 0))),
                    [(_sds((t, SSM_WIDTH), F32), tile_g)],
                    extras=[(dy_ssm, tile_g), (gate, tile_g), (y_pre, tile_g)],
                    epilogue=lambda acc, dy, gt, yp: ((dy * gt + acc) * _gelu_grad(yp),), deps=sent)
    du_b, dbre, dbim, dcre, dcimn, q_re, q_im, dd = _ssm_bwd(dy_pre, z, p_re, p_im, dvec, *tabs)
    dare, daim, dldt, dbxr, dbxi = _ssm_prep_bwd(are, aim, ldt, bxr, bxi, dbre, dbim, q_re, q_im)

    dz_b, g_ln_g, g_ln_b, g_sgu_w, g_sgu_bx = _sgu_bwd(dy_sgu, du_b, z, sgu_ln_g, sgu_ln_b, sgu_w[0], sgu_bexp)
    local_small = {
        "ssm_a_re": dare, "ssm_a_im": daim,
        "ssm_b_re": _from_block_b(dbxr), "ssm_b_im": _from_block_b(dbxi),
        "ssm_c_re": _from_block_c(dcre), "ssm_c_im": -_from_block_c(dcimn),
        "ssm_d": dd, "ssm_log_dt": dldt.reshape(-1, SSM_STATE).sum(axis=-1),
        "ssm_glu_b": g_glu_b, "sgu_ln_g": g_ln_g, "sgu_ln_b": g_ln_b, "sgu_w": g_sgu_w,
        "sgu_b": g_sgu_bx[:, :, 0], "out_norm_ssm_g": g_onorm_ssm, "out_norm_sgu_g": g_onorm_sgu,
        "norm_mlp_g": g_norm_mlp, "norm_final_g": g_final,
    }
    small_early = [n for n in small if n in local_small]
    small_late = [n for n in small if n not in local_small]
    packed = _pack([local_small[n] for n in small_early])
    small_send = _send_start("small_start", packed, lax.empty((N_DEV,) + packed.shape, F32), False)
    (gw_in,) = _mm("in_proj_dw", "tn", (d // sq, n_in // bn_i, t // tb),
                   (h1, _bs((tb, sq), lambda i, j, k: (k, i))),
                   (dz_b, _bs((tb, bn_i), lambda i, j, k: (k, j))),
                   [(_sds((d, n_in), BF16), _bs((sq, bn_i), lambda i, j, k: (i, j)))],
                   deps=[small_send[4]])
    sent = send_grad("w_in", gw_in, (N_DEV, d, nsh_in))
    def in_dx_ep(acc, xv, dres, g):
        dx, dg = _rms_bwd_math(acc, xv, g)
        return dx + dres, dg

    grad_x, g_norm_mix = _mm("in_proj_dx", "nt", (t // tb2, 1, n_in // bn_i),
                             (dz_b, _bs((tb2, bn_i), lambda i, j, k: (i, k))),
                             (wg_in, _bs((d, bn_i), lambda i, j, k: (0, k))),
                             [(_sds((t, d), F32), row2)],
                             extras=[(xs, row2), (dx2, row2), (norm_mix_g, vec2)],
                             epilogue=in_dx_ep, sums=[vec_sum], deps=sent, ep_rows=EP_ROWS)

    (late_parts,) = _all_gather("gather_late_grads", [_pack([g_norm_mix])])
    packed, early_parts = _send_wait("small_wait", small_send, grad_x, False)
    early_parts = _own_block(early_parts, packed, me)

    grads, deltas, new_m, new_v = {}, {}, {}, {}
    for n in big:
        w2 = given[n][0]
        sent_blocks, landed = _send_wait("grad_wait_" + n, sends[n], grad_x, True)
        if sent_blocks.ndim == landed.ndim:
            own = lax.dynamic_index_in_dim(sent_blocks, me, 0, keepdims=False)
        else:
            own = lax.dynamic_slice_in_dim(sent_blocks, me * landed.shape[2], landed.shape[2], axis=1)
        parts = _own_block(landed, own, me)
        res = _adamw("adamw_" + n, parts, w2, given["m_" + n][0], given["v_" + n][0])
        grads[n], deltas[n], new_m[n], new_v[n] = [r.reshape(given[n].shape) for r in res]
    for tag, group, parts in (("early", small_early, early_parts), ("late", small_late, late_parts)):
        like = [given[n] for n in group]
        res = _adamw("adamw_small_" + tag, parts, _pack(like), _pack([given["m_" + n] for n in group]),
                     _pack([given["v_" + n] for n in group]))
        for store, buf in zip((grads, deltas, new_m, new_v), res):
            for n, a in zip(group, _unpack(buf, like)):
                store[n] = a

    return (loss, grad_x.reshape(x.shape), *[grads[n] for n in names], *[deltas[n] for n in names],
            *[new_m[n] for n in names], *[new_v[n] for n in names])
```

```python
import functools
import math

import jax
import jax.numpy as jnp
from jax import lax
from jax.experimental import pallas as pl
from jax.experimental.pallas import tpu as pltpu

F32, BF16 = jnp.float32, jnp.bfloat16
EPS = 1e-6
N_DEV = 8
D_MODEL = 2048
SSM_WIDTH = 1024
SSM_GROUP = 16
SSM_STATE = 64
SGU_HEADS = 8
SGU_CHUNK = 128
LANES = 128
SUBLANES = 8
N_SLAB = SSM_WIDTH // LANES
SLAB_STATE = (LANES // SSM_GROUP) * SSM_STATE
SCAN_BLOCK = 128
SLABS_PER_STEP = 8
VMEM_LIMIT = 56 * 1024 * 1024
ROW_BLOCK = 256
EP_ROWS = 128
ADAM_ROWS = 128
MESH_AXES = ("x", "y", "c")

ADAM_LR, ADAM_B1, ADAM_B2, ADAM_EPS, ADAM_WD, ADAM_STEP = 0.001, 0.9, 0.999, 1e-08, 0.01, 10

_GELU_C0 = math.sqrt(2.0 / math.pi)
_GELU_C1 = 0.044715


def _gelu(v):
    return 0.5 * v * (1.0 + jnp.tanh(_GELU_C0 * (v + _GELU_C1 * v * v * v)))


def _gelu_grad(v):
    th = jnp.tanh(_GELU_C0 * (v + _GELU_C1 * v * v * v))
    return 0.5 * (1.0 + th) + 0.5 * v * (1.0 - th * th) * _GELU_C0 * (1.0 + 3.0 * _GELU_C1 * v * v)


def _sigmoid(v):
    return 1.0 / (1.0 + jnp.exp(-v))


def _params(sem=None):
    return pltpu.CompilerParams(dimension_semantics=sem, vmem_limit_bytes=VMEM_LIMIT)


def _dot(a, b, mode="nn"):
    dims = {"nn": ((1,), (0,)), "nt": ((1,), (1,)), "tn": ((0,), (0,))}[mode]
    return lax.dot_general(a, b, (dims, ((), ())), preferred_element_type=F32)


def _mm(name, mode, grid, a, b, outs, extras=(), epilogue=None, deps=(), sums=(), acc_shape=None,
        ep_rows=None, side=None):
    nk = grid[2]
    n_ex, n_out, n_dep, n_sum = len(extras), len(outs), len(deps), len(sums)
    assert not sums or grid[1] == 1
    if acc_shape is None:
        acc_shape = tuple(d for d in outs[0][1].block_shape if d is not None)
    side_fn, side_ins, side_outs, side_sums, side_rows = side or (None, (), (), (), 0)
    s_in, s_out, s_sum = len(side_ins), len(side_outs), len(side_sums)

    def body(*refs):
        a_ref, b_ref = refs[0], refs[1]
        ex = refs[2:2 + n_ex]
        pos = 2 + n_ex
        side_in = refs[pos:pos + s_in]
        pos += s_in + n_dep
        out_refs = refs[pos:pos + n_out]
        sum_refs = refs[pos + n_out:pos + n_out + n_sum]
        pos += n_out + n_sum
        side_out = refs[pos:pos + s_out]
        side_sum = refs[pos + s_out:pos + s_out + s_sum]
        acc = refs[-1]
        k = pl.program_id(2)

        @pl.when(k == 0)
        def _():
            acc[...] = jnp.zeros_like(acc)

        if side_sum:
            @pl.when((pl.program_id(0) == 0) & (pl.program_id(1) == 0) & (k == 0))
            def _():
                for o in side_sum:
                    o[...] = jnp.zeros_like(o)

        acc[...] += _dot(a_ref[...], b_ref[...], mode)
        if side_fn is not None:
            res = side_fn(*[r[...] for r in side_in])
            for o, r in zip(side_out, res[:s_out]):
                o[...] = r.astype(o.dtype)
            for o, r in zip(side_sum, res[s_out:]):
                o[...] += r

        def finish(rows):
            args = [e[rows, :] if e.shape[0] == acc_shape[0] else e[...] for e in ex]
            res = acc[rows, :]
            res = (res,) if epilogue is None else epilogue(res, *args)
            for o, r in zip(out_refs, res[:n_out]):
                o[rows, :] = r.astype(o.dtype)
            return tuple(res[n_out:])

        @pl.when(k == nk - 1)
        def _():
            if ep_rows is None:
                terms = finish(slice(None))
            else:
                def chunk(c, tot):
                    rows = pl.ds(pl.multiple_of(c * ep_rows, ep_rows), ep_rows)
                    return tuple(s + r for s, r in zip(tot, finish(rows)))

                zero = tuple(jnp.zeros(o.shape, F32) for o in sum_refs)
                terms = lax.fori_loop(0, acc_shape[0] // ep_rows, chunk, zero)
            for o, r in zip(sum_refs, terms):
                _add_up(o, r, pl.program_id(0) == 0)

    def side_rows_spec(width):
        return _bs((side_rows, width), lambda i, j, k: ((i * grid[1] + j) * grid[2] + k, 0))

    def side_vec_spec(width):
        return _bs((1, width), lambda i, j, k: (0, 0))

    sem = ("arbitrary",) * 3 if sums or side else ("parallel", "parallel", "arbitrary")
    res = pl.pallas_call(
        body, name=name, grid=grid,
        in_specs=[a[1], b[1]] + [e[1] for e in extras]
        + [side_rows_spec(w) if x.shape[0] > 1 else side_vec_spec(w) for x, w in side_ins]
        + [_any_spec()] * n_dep,
        out_specs=[o[1] for o in outs] + [o[1] for o in sums]
        + [side_rows_spec(w) for _, w in side_outs] + [side_vec_spec(w) for _, w in side_sums],
        out_shape=[o[0] for o in outs] + [o[0] for o in sums]
        + [o[0] for o in side_outs] + [o[0] for o in side_sums],
        scratch_shapes=[pltpu.VMEM(acc_shape, F32)],
        compiler_params=_params(sem),
    )(a[0], b[0], *[e[0] for e in extras], *[x for x, _ in side_ins], *deps)
    return res


def _add_up(ref, term, first):
    @pl.when(first)
    def _():
        ref[...] = term

    @pl.when(jnp.logical_not(first))
    def _():
        ref[...] += term


def _rms_math(xv, g):
    return xv * lax.rsqrt(jnp.mean(xv * xv, axis=-1, keepdims=True) + EPS) * g


def _rms_bwd_math(dy, xv, g):
    r = lax.rsqrt(jnp.mean(xv * xv, axis=-1, keepdims=True) + EPS)
    xhat = xv * r
    dxhat = dy * g
    dx = r * (dxhat - xhat * jnp.mean(dxhat * xhat, axis=-1, keepdims=True))
    return dx, jnp.sum(dy * xhat, axis=0, keepdims=True)


def _sds(shape, dtype):
    return jax.ShapeDtypeStruct(shape, dtype)


def _bs(shape, fn):
    return pl.BlockSpec(shape, fn)


def _rms_fwd(name, x, g, deps=()):
    t, w = x.shape
    br = min(ROW_BLOCK, t)

    def body(*refs):
        x_ref, g_ref, o_ref = refs[0], refs[1], refs[-1]
        xv = x_ref[...]
        r = lax.rsqrt(jnp.mean(xv * xv, axis=-1, keepdims=True) + EPS)
        o_ref[...] = (xv * r * g_ref[...]).astype(BF16)

    return pl.pallas_call(
        body, name=name, grid=(t // br,),
        in_specs=[_bs((br, w), lambda i: (i, 0)), _bs((1, w), lambda i: (0, 0))] + [_any_spec()] * len(deps),
        out_specs=_bs((br, w), lambda i: (i, 0)),
        out_shape=_sds((t, w), BF16),
        compiler_params=_params(("parallel",)),
    )(x, g, *deps)


def _mix_norm(ya, yb, ga, gb):
    t, w = ya.shape
    br = min(ROW_BLOCK, t)

    def body(a_ref, b_ref, ga_ref, gb_ref, o_ref):
        for src, g_ref, col in ((a_ref, ga_ref, 0), (b_ref, gb_ref, w)):
            v = src[...]
            r = lax.rsqrt(jnp.mean(v * v, axis=-1, keepdims=True) + EPS)
            o_ref[:, col:col + w] = (v * r * g_ref[...]).astype(BF16)

    row = _bs((br, w), lambda i: (i, 0))
    vec = _bs((1, w), lambda i: (0, 0))
    return pl.pallas_call(
        body, name="mix_norm", grid=(t // br,),
        in_specs=[row, row, vec, vec],
        out_specs=_bs((br, 2 * w), lambda i: (i, 0)),
        out_shape=_sds((t, 2 * w), BF16),
        compiler_params=_params(("parallel",)),
    )(ya, yb, ga, gb)


def _loss_math(xv, target, g):
    r = lax.rsqrt(jnp.mean(xv * xv, axis=-1, keepdims=True) + EPS)
    xhat = xv * r
    err = xhat * g - target
    dy = err * (1.0 / xv.shape[-1])
    dxhat = dy * g
    dx = r * (dxhat - xhat * jnp.mean(dxhat * xhat, axis=-1, keepdims=True))
    return dx, jnp.sum(dy * xhat, axis=0, keepdims=True), jnp.sum(err * err, axis=0, keepdims=True)


def _final_loss(x3, target, g):
    t, w = x3.shape
    br = min(ROW_BLOCK, t)

    def body(x_ref, tg_ref, g_ref, dx_ref, dxb_ref, dg_ref, l_ref):
        dx, dg, e2 = _loss_math(x_ref[...], tg_ref[...], g_ref[...])
        dx_ref[...] = dx
        dxb_ref[...] = dx.astype(BF16)
        _add_up(dg_ref, dg, pl.program_id(0) == 0)
        _add_up(l_ref, e2, pl.program_id(0) == 0)

    row = _bs((br, w), lambda i: (i, 0))
    vec = _bs((1, w), lambda i: (0, 0))
    return pl.pallas_call(
        body, name="final_loss", grid=(t // br,),
        in_specs=[row, row, vec], out_specs=[row, row, vec, vec],
        out_shape=[_sds((t, w), F32), _sds((t, w), BF16), _sds((1, w), F32), _sds((1, w), F32)],
        compiler_params=_params(("arbitrary",)),
    )(x3, target, g)


def _prep_math(are, aim, ldt, bxr, bxi):
    dt = jnp.exp(ldt)
    er = jnp.exp(are * dt)
    th = aim * dt
    abr, abi = er * jnp.cos(th), er * jnp.sin(th)
    nr, ni = abr - 1.0, abi
    den = are * are + aim * aim
    cr = (nr * are + ni * aim) / den
    ci = (ni * are - nr * aim) / den
    bbr, bbi = [], []
    for j in range(N_SLAB):
        sl = slice(j * SLAB_STATE, (j + 1) * SLAB_STATE)
        bbr.append(cr[:, sl] * bxr[j] - ci[:, sl] * bxi[j])
        bbi.append(cr[:, sl] * bxi[j] + ci[:, sl] * bxr[j])
    return abr, abi, bbr, bbi


def _ssm_prep(are, aim, ldt, bxr, bxi, cxr, cxi):
    nst = N_SLAB * SLAB_STATE

    def body(are_r, aim_r, ldt_r, bxr_r, bxi_r, cxr_r, cxi_r,
             bre_o, bim_o, cre_o, cimn_o, apr_o, api_o, air_o, aii_o):
        abr, abi, bbr, bbi = _prep_math(are_r[...], aim_r[...], ldt_r[...],
                                        [bxr_r[j] for j in range(N_SLAB)], [bxi_r[j] for j in range(N_SLAB)])
        for j in range(N_SLAB):
            bre_o[j] = bbr[j].astype(BF16)
            bim_o[j] = bbi[j].astype(BF16)
        cre_o[...] = cxr_r[...].astype(BF16)
        cimn_o[...] = (-cxi_r[...]).astype(BF16)

        def step(k, cur):
            cr, ci = cur
            den = cr * cr + ci * ci
            apr_o[pl.ds(k, 1), :] = cr
            api_o[pl.ds(k, 1), :] = ci
            air_o[pl.ds(k, 1), :] = cr / den
            aii_o[pl.ds(k, 1), :] = -ci / den
            return cr * abr - ci * abi, cr * abi + ci * abr

        lax.fori_loop(0, SCAN_BLOCK, step, (jnp.ones((1, nst), F32), jnp.zeros((1, nst), F32)))

    tab = _sds((SCAN_BLOCK, nst), F32)
    return pl.pallas_call(
        body, name="ssm_prep",
        out_shape=[_sds(bxr.shape, BF16), _sds(bxr.shape, BF16), _sds(cxr.shape, BF16), _sds(cxr.shape, BF16),
                   tab, tab, tab, tab],
        compiler_params=_params(),
    )(are, aim, ldt, bxr, bxi, cxr, cxi)


def _ssm_prep_bwd(are, aim, ldt, bxr, bxi, dbre, dbim, qr, qi):
    def body(are_r, aim_r, ldt_r, bxr_r, bxi_r, dbre_r, dbim_r, qr_r, qi_r,
             dare_o, daim_o, dldt_o, dbxr_o, dbxi_o):
        prim = (are_r[...], aim_r[...], ldt_r[...],
                [bxr_r[j] for j in range(N_SLAB)], [bxi_r[j] for j in range(N_SLAB)])
        (abr, abi, _, _), vjp = jax.vjp(_prep_math, *prim)
        den = abr * abr + abi * abi
        q_r, q_i = qr_r[...], qi_r[...]
        gar = (q_r * abr - q_i * abi) / den
        gai = (q_r * abi + q_i * abr) / den
        ct = (gar, gai, [dbre_r[j] for j in range(N_SLAB)], [dbim_r[j] for j in range(N_SLAB)])
        dare, daim, dldt, dbxr, dbxi = vjp(ct)
        dare_o[...] = dare
        daim_o[...] = daim
        dldt_o[...] = dldt
        for j in range(N_SLAB):
            dbxr_o[j] = dbxr[j]
            dbxi_o[j] = dbxi[j]

    row = _sds(are.shape, F32)
    return pl.pallas_call(
        body, name="ssm_prep_bwd",
        out_shape=[row, row, row, _sds(bxr.shape, F32), _sds(bxr.shape, F32)],
        compiler_params=_params(),
    )(are, aim, ldt, bxr, bxi, dbre, dbim, qr, qi)


def _tri(lower):
    r = lax.broadcasted_iota(jnp.int32, (SCAN_BLOCK, SCAN_BLOCK), 0)
    c = lax.broadcasted_iota(jnp.int32, (SCAN_BLOCK, SCAN_BLOCK), 1)
    return jnp.where((r >= c) if lower else (r <= c), 1.0, 0.0).astype(BF16)


def _cumsum_mxu(tri, v):
    return _dot(tri, v.astype(BF16))


def _ssm_specs(t):
    nt = t // SCAN_BLOCK
    sps = SLABS_PER_STEP
    tab = _bs((SCAN_BLOCK, sps * SLAB_STATE), lambda j, i: (0, j))
    bmat = _bs((sps, LANES, SLAB_STATE), lambda j, i: (j, 0, 0))
    cmat = _bs((sps, SLAB_STATE, LANES), lambda j, i: (j, 0, 0))
    return nt, tab, bmat, cmat


def _slab_slices(s):
    return slice(s * LANES, (s + 1) * LANES), slice(s * SLAB_STATE, (s + 1) * SLAB_STATE)


def _ssm_fwd(z, dvec, bre, bim, cre, cimn, apr, api, air, aii, deps=()):
    t = z.shape[0]
    nt, tab, bmat, cmat = _ssm_specs(t)
    nst = N_SLAB * SLAB_STATE
    last = SCAN_BLOCK - 1

    sps = SLABS_PER_STEP

    def body(*refs):
        u_ref, d_ref, bre_r, bim_r, cre_r, cimn_r, apr_r, api_r, air_r, aii_r = refs[:10]
        y_ref, yg_ref, pr_ref, pi_ref, car_r, car_i = refs[10 + len(deps):]

        @pl.when(pl.program_id(1) == 0)
        def _():
            car_r[...] = jnp.zeros_like(car_r)
            car_i[...] = jnp.zeros_like(car_i)

        tri = _tri(True)
        for s in range(sps):
            ul, sl = _slab_slices(s)
            u = u_ref[:, ul]
            ub = u.astype(BF16)
            bur, bui = _dot(ub, bre_r[s]), _dot(ub, bim_r[s])
            ir, ii = air_r[:, sl], aii_r[:, sl]
            csr = _cumsum_mxu(tri, ir * bur - ii * bui)
            csi = _cumsum_mxu(tri, ir * bui + ii * bur)
            pr, pi = apr_r[:, sl], api_r[:, sl]
            a_r, a_i = apr_r[1:2, sl], api_r[1:2, sl]
            c_r, c_i = car_r[:, sl], car_i[:, sl]
            wr = csr + (a_r * c_r - a_i * c_i)
            wi = csi + (a_r * c_i + a_i * c_r)
            sr = pr * wr - pi * wi
            si = pr * wi + pi * wr
            car_r[:, sl] = sr[last:last + 1, :]
            car_i[:, sl] = si[last:last + 1, :]
            pr_ref[:, sl] = (sr - bur).astype(BF16)
            pi_ref[:, sl] = (si - bui).astype(BF16)
            y = _dot(sr.astype(BF16), cre_r[s]) + _dot(si.astype(BF16), cimn_r[s]) + d_ref[:, ul] * u
            y_ref[:, ul] = y
            yg_ref[:, ul] = _gelu(y).astype(BF16)

    ublk = _bs((SCAN_BLOCK, sps * LANES), lambda j, i: (i, j))
    sblk = _bs((SCAN_BLOCK, sps * SLAB_STATE), lambda j, i: (i, j))
    return pl.pallas_call(
        body, name="ssm_fwd", grid=(N_SLAB // sps, nt),
        in_specs=[ublk, _bs((1, sps * LANES), lambda j, i: (0, j)), bmat, bmat, cmat, cmat, tab, tab, tab, tab]
        + [_any_spec()] * len(deps),
        out_specs=[ublk, ublk, sblk, sblk],
        out_shape=[_sds((t, SSM_WIDTH), F32), _sds((t, SSM_WIDTH), BF16),
                   _sds((t, nst), BF16), _sds((t, nst), BF16)],
        scratch_shapes=[pltpu.VMEM((1, sps * SLAB_STATE), F32), pltpu.VMEM((1, sps * SLAB_STATE), F32)],
        compiler_params=_params(("parallel", "arbitrary")),
    )(z, dvec, bre, bim, cre, cimn, apr, api, air, aii, *deps)


def _ssm_bwd(gy, z, p_re, p_im, dvec, bre, bim, cre, cimn, apr, api, air, aii):
    t = z.shape[0]
    nt, tab, bmat, cmat = _ssm_specs(t)
    last = SCAN_BLOCK - 1

    def fold(v):
        return v.reshape(SCAN_BLOCK // SUBLANES, SUBLANES, v.shape[-1]).sum(axis=0)

    sps = SLABS_PER_STEP

    def body(g_ref, u_ref, pr_ref, pi_ref, d_ref, bre_r, bim_r, cre_r, cimn_r, apr_r, api_r, air_r, aii_r,
             du_ref, dbre_o, dbim_o, dcre_o, dcimn_o, qr_o, qi_o, dd_o, car_r, car_i, qacc_r, qacc_i, dacc):
        i = pl.program_id(1)

        @pl.when(i == 0)
        def _():
            for ref in (car_r, car_i, qacc_r, qacc_i, dacc, dbre_o, dbim_o, dcre_o, dcimn_o):
                ref[...] = jnp.zeros_like(ref)

        tri = _tri(False)
        for s in range(sps):
            ul, sl = _slab_slices(s)
            g = g_ref[:, ul]
            gb = g.astype(BF16)
            u = u_ref[:, ul]
            ub = u.astype(BF16)
            bur, bui = _dot(ub, bre_r[s]), _dot(ub, bim_r[s])
            p_r, p_i = pr_ref[:, sl].astype(F32), pi_ref[:, sl].astype(F32)
            srb, sib = (p_r + bur).astype(BF16), (p_i + bui).astype(BF16)
            dcre_o[s] += _dot(srb, gb, "tn")
            dcimn_o[s] += _dot(sib, gb, "tn")
            dsr, dsi = _dot(gb, cre_r[s], "nt"), _dot(gb, cimn_r[s], "nt")
            pr, pi = apr_r[:, sl], api_r[:, sl]
            csr = _cumsum_mxu(tri, pr * dsr + pi * dsi)
            csi = _cumsum_mxu(tri, pr * dsi - pi * dsr)
            al_r, al_i = apr_r[last:last + 1, sl], api_r[last:last + 1, sl]
            c_r, c_i = car_r[:, sl], car_i[:, sl]
            wr = csr + (al_r * c_r + al_i * c_i)
            wi = csi + (al_r * c_i - al_i * c_r)
            ir, ii = air_r[:, sl], aii_r[:, sl]
            lr = ir * wr + ii * wi
            li = ir * wi - ii * wr
            a_r, a_i = apr_r[1:2, sl], api_r[1:2, sl]
            car_r[:, sl] = a_r * lr[0:1, :] + a_i * li[0:1, :]
            car_i[:, sl] = a_r * li[0:1, :] - a_i * lr[0:1, :]
            lrb, lib = lr.astype(BF16), li.astype(BF16)
            dbre_o[s] += _dot(ub, lrb, "tn")
            dbim_o[s] += _dot(ub, lib, "tn")
            du = d_ref[:, ul] * g + _dot(lrb, bre_r[s], "nt") + _dot(lib, bim_r[s], "nt")
            du_ref[:, ul] = du.astype(BF16)
            qacc_r[:, sl] += fold(lr * p_r + li * p_i)
            qacc_i[:, sl] += fold(li * p_r - lr * p_i)
            dacc[:, ul] += fold(g * u)

        @pl.when(i == nt - 1)
        def _():
            qr_o[...] = jnp.sum(qacc_r[...], axis=0, keepdims=True)
            qi_o[...] = jnp.sum(qacc_i[...], axis=0, keepdims=True)
            dd_o[...] = jnp.sum(dacc[...], axis=0, keepdims=True)

    rev = lambda j, i: (nt - 1 - i, j)
    ublk = _bs((SCAN_BLOCK, sps * LANES), rev)
    sblk = _bs((SCAN_BLOCK, sps * SLAB_STATE), rev)
    qrow = _bs((1, sps * SLAB_STATE), lambda j, i: (0, j))
    urow = _bs((1, sps * LANES), lambda j, i: (0, j))
    nst = N_SLAB * SLAB_STATE
    return pl.pallas_call(
        body, name="ssm_bwd", grid=(N_SLAB // sps, nt),
        in_specs=[ublk, ublk, sblk, sblk, urow, bmat, bmat, cmat, cmat, tab, tab, tab, tab],
        out_specs=[ublk, bmat, bmat, cmat, cmat, qrow, qrow, urow],
        out_shape=[_sds((t, SSM_WIDTH), BF16),
                   _sds((N_SLAB, LANES, SLAB_STATE), F32), _sds((N_SLAB, LANES, SLAB_STATE), F32),
                   _sds((N_SLAB, SLAB_STATE, LANES), F32), _sds((N_SLAB, SLAB_STATE, LANES), F32),
                   _sds((1, nst), F32), _sds((1, nst), F32), _sds((1, SSM_WIDTH), F32)],
        scratch_shapes=[pltpu.VMEM((1, sps * SLAB_STATE), F32), pltpu.VMEM((1, sps * SLAB_STATE), F32),
                        pltpu.VMEM((SUBLANES, sps * SLAB_STATE), F32), pltpu.VMEM((SUBLANES, sps * SLAB_STATE), F32),
                        pltpu.VMEM((SUBLANES, sps * LANES), F32)],
        compiler_params=_params(("parallel", "arbitrary")),
    )(gy, z, p_re, p_im, dvec, bre, bim, cre, cimn, apr, api, air, aii)


def _sgu_mask():
    r = lax.broadcasted_iota(jnp.int32, (SGU_CHUNK, SGU_CHUNK), 0)
    c = lax.broadcasted_iota(jnp.int32, (SGU_CHUNK, SGU_CHUNK), 1)
    return r >= c


def _sgu_common(zu, zv, lng, lnb):
    us, v = _gelu(zu), _gelu(zv)
    mu = jnp.mean(v, axis=-1, keepdims=True)
    vc = v - mu
    rstd = lax.rsqrt(jnp.mean(vc * vc, axis=-1, keepdims=True) + EPS)
    xhat = vc * rstd
    return us, xhat, rstd, xhat * lng + lnb


def _sgu_fwd(z, lng, lnb, w, bexp):
    t = z.shape[0]
    hd = SGU_CHUNK

    def body(zu_ref, zv_ref, lng_ref, lnb_ref, w_ref, b_ref, y_ref):
        us, _, _, vn = _sgu_common(zu_ref[...], zv_ref[...], lng_ref[...], lnb_ref[...])
        vnb = vn.astype(BF16)
        mask = _sgu_mask()
        for h in range(SGU_HEADS):
            sl = slice(h * hd, (h + 1) * hd)
            wt = jnp.where(mask, w_ref[h], 0.0).astype(BF16)
            y_ref[:, sl] = us[:, sl] * (_dot(wt, vnb[:, sl]) + b_ref[h])

    row = lambda c: _bs((SGU_CHUNK, SSM_WIDTH), lambda i: (i, c))
    vec = _bs((1, SSM_WIDTH), lambda i: (0, 0))
    hmat = _bs((SGU_HEADS, hd, hd), lambda i: (0, 0, 0))
    return pl.pallas_call(
        body, name="sgu_fwd", grid=(t // SGU_CHUNK,),
        in_specs=[row(1), row(2), vec, vec, hmat, hmat],
        out_specs=row(0), out_shape=_sds((t, SSM_WIDTH), F32),
        compiler_params=_params(("parallel",)),
    )(z, z, lng, lnb, w, bexp)


def _sgu_bwd(dy, du_ssm, z, lng, lnb, w, bexp):
    t = z.shape[0]
    hd = SGU_CHUNK
    nc = t // SGU_CHUNK

    def body(dy_ref, dus_ref, zu_ref, zv_ref, lng_ref, lnb_ref, w_ref, b_ref,
             dz_ref, dlng_o, dlnb_o, dw_o, db_o):
        i = pl.program_id(0)

        @pl.when(i == 0)
        def _():
            for ref in (dlng_o, dlnb_o, dw_o, db_o):
                ref[...] = jnp.zeros_like(ref)

        zu, zv = zu_ref[...], zv_ref[...]
        lng = lng_ref[...]
        us, xhat, rstd, vn = _sgu_common(zu, zv, lng, lnb_ref[...])
        vnb = vn.astype(BF16)
        dyv = dy_ref[...]
        mask = _sgu_mask()
        dus_parts, dvn_parts = [], []
        for h in range(SGU_HEADS):
            sl = slice(h * hd, (h + 1) * hd)
            wt = jnp.where(mask, w_ref[h], 0.0).astype(BF16)
            mixed = _dot(wt, vnb[:, sl]) + b_ref[h]
            dus_parts.append(dyv[:, sl] * mixed)
            dmix = dyv[:, sl] * us[:, sl]
            dmb = dmix.astype(BF16)
            db_o[h] += dmix
            dw_o[h] += _dot(dmb, vnb[:, sl], "nt")
            dvn_parts.append(_dot(wt, dmb, "tn"))
        dus = jnp.concatenate(dus_parts, axis=1)
        dvn = jnp.concatenate(dvn_parts, axis=1)
        dlng_o[...] += jnp.sum(dvn * xhat, axis=0, keepdims=True)
        dlnb_o[...] += jnp.sum(dvn, axis=0, keepdims=True)
        dxh = dvn * lng
        dv = rstd * (dxh - jnp.mean(dxh, axis=-1, keepdims=True)
                     - xhat * jnp.mean(dxh * xhat, axis=-1, keepdims=True))
        dz_ref[:, 0:SSM_WIDTH] = dus_ref[...]
        dz_ref[:, SSM_WIDTH:2 * SSM_WIDTH] = (dus * _gelu_grad(zu)).astype(BF16)
        dz_ref[:, 2 * SSM_WIDTH:] = (dv * _gelu_grad(zv)).astype(BF16)

        @pl.when(i == nc - 1)
        def _():
            for h in range(SGU_HEADS):
                dw_o[h] = jnp.where(mask, dw_o[h], 0.0)
                db_o[h] = jnp.broadcast_to(jnp.sum(db_o[h], axis=1, keepdims=True), (hd, hd))

    row = lambda c: _bs((SGU_CHUNK, SSM_WIDTH), lambda i: (i, c))
    vec = _bs((1, SSM_WIDTH), lambda i: (0, 0))
    hmat = _bs((SGU_HEADS, hd, hd), lambda i: (0, 0, 0))
    return pl.pallas_call(
        body, name="sgu_bwd", grid=(nc,),
        in_specs=[row(0), row(0), row(1), row(2), vec, vec, hmat, hmat],
        out_specs=[_bs((SGU_CHUNK, 3 * SSM_WIDTH), lambda i: (i, 0)), vec, vec, hmat, hmat],
        out_shape=[_sds((t, 3 * SSM_WIDTH), BF16), _sds((1, SSM_WIDTH), F32), _sds((1, SSM_WIDTH), F32),
                   _sds((SGU_HEADS, hd, hd), F32), _sds((SGU_HEADS, hd, hd), F32)],
        compiler_params=_params(("arbitrary",)),
    )(dy, du_ssm, z, z, lng, lnb, w, bexp)


def _place():
    x, y, c = (lax.axis_index(a) for a in MESH_AXES)
    return x, y, c


def _index(p):
    return 4 * p[0] + 2 * p[1] + p[2]


def _any_spec():
    return pl.BlockSpec(memory_space=pl.ANY)


def _col_block(ref, k, width):
    return ref.at[:, pl.ds(pl.multiple_of(k * width, LANES), width)]


def _all_gather(name, shards, by_columns=False):
    n = len(shards)

    def body(*refs):
        ins, outs = refs[:n], refs[n:2 * n]
        send, recv, loc = refs[2 * n:]
        x, y, c = _place()
        me, sib = (x, y, c), (x, y, 1 - c)
        chips = [(1 - x, y), (x, 1 - y), (1 - x, 1 - y)]

        def blk(w, p):
            if by_columns:
                return _col_block(outs[w], _index(p), shards[w].shape[1])
            return outs[w].at[_index(p)]

        def cp(w, k, block, to, src=None):
            dst = blk(w, block)
            return pltpu.make_async_remote_copy(
                src_ref=dst if src is None else src, dst_ref=dst,
                send_sem=send.at[w * 7 + k], recv_sem=recv.at[w * 7 + k],
                device_id=to, device_id_type=pl.DeviceIdType.MESH)

        mines, sends = [], []
        for w in range(n):
            m = pltpu.make_async_copy(ins[w], blk(w, me), loc.at[w])
            m.start()
            mines.append(m)
            first = [cp(w, 0, me, sib, src=ins[w])]
            first += [cp(w, 1 + j, me, (*chip, c), src=ins[w]) for j, chip in enumerate(chips)]
            for q in first:
                q.start()
            sends += first
        for j, chip in enumerate(chips):
            for w in range(n):
                cp(w, 1 + j, (*chip, c), me).wait_recv()
                q = cp(w, 4 + j, (*chip, c), sib)
                q.start()
                sends.append(q)
        for w in range(n):
            cp(w, 0, sib, me).wait_recv()
            for j, chip in enumerate(chips):
                cp(w, 4 + j, (*chip, 1 - c), me).wait_recv()
        for q in sends:
            q.wait_send()
        for m in mines:
            m.wait()

    return pl.pallas_call(
        body, name=name,
        in_specs=[_any_spec()] * n, out_specs=[_any_spec()] * n,
        out_shape=[_sds((s.shape[0], N_DEV * s.shape[1]) if by_columns else (N_DEV,) + s.shape, s.dtype)
                   for s in shards],
        scratch_shapes=[pltpu.SemaphoreType.DMA((n * 7,)), pltpu.SemaphoreType.DMA((n * 7,)),
                        pltpu.SemaphoreType.DMA((n,))],
        compiler_params=pltpu.CompilerParams(has_side_effects=True),
    )(*shards)


def _peer(r, x, y, c):
    return ((1 - x) if r & 4 else x, (1 - y) if r & 2 else y, (1 - c) if r & 1 else c)


def _sent_block(src_ref, land_ref, k, scatter):
    if not scatter:
        return src_ref
    if len(src_ref.shape) == len(land_ref.shape):
        return src_ref.at[k]
    return _col_block(src_ref, k, land_ref.shape[2])


def _send_start(name, src, land, scatter, after=None):
    n_after = 0 if after is None else 1

    def body(*refs):
        src_ref, land_ref = refs[0], refs[1]
        send, recv, _, _, token = refs[2 + n_after:]
        x, y, c = _place()
        me = _index((x, y, c))
        for r in range(1, N_DEV):
            p = _peer(r, x, y, c)
            pltpu.make_async_remote_copy(
                src_ref=_sent_block(src_ref, land_ref, _index(p), scatter), dst_ref=land_ref.at[me],
                send_sem=send.at[r - 1], recv_sem=recv.at[r - 1],
                device_id=p, device_id_type=pl.DeviceIdType.MESH).start()
        token[...] = jnp.zeros_like(token)

    hbm, sem = pl.BlockSpec(memory_space=pltpu.HBM), pl.BlockSpec(memory_space=pltpu.SEMAPHORE)
    return pl.pallas_call(
        body, name=name,
        out_shape=(pltpu.SemaphoreType.DMA((N_DEV - 1,)), pltpu.SemaphoreType.DMA((N_DEV - 1,)),
                   pltpu.HBM(src.shape, src.dtype), pltpu.HBM(land.shape, land.dtype),
                   _sds((SUBLANES, LANES), F32)),
        in_specs=(hbm, hbm) + (_any_spec(),) * n_after,
        out_specs=(sem, sem, hbm, hbm, pl.BlockSpec(memory_space=pltpu.VMEM)),
        input_output_aliases={0: 2, 1: 3},
        compiler_params=pltpu.CompilerParams(has_side_effects=pltpu.SideEffectType.DATAFLOW_SIDE_EFFECTING),
    )(pltpu.with_memory_space_constraint(src, pltpu.HBM), pltpu.with_memory_space_constraint(land, pltpu.HBM),
      *([] if after is None else [after]))


def _send_wait(name, started, after, scatter):
    send, recv, src_thru, land_thru, _ = started

    def body(src_ref, land_ref, send_r, recv_r, after_ref, src_out, land_out):
        x, y, c = _place()
        for r in range(1, N_DEV):
            p = _peer(r, x, y, c)
            k = _index(p)
            cp = pltpu.make_async_remote_copy(
                src_ref=_sent_block(src_ref, land_ref, k, scatter), dst_ref=land_ref.at[k],
                send_sem=send_r.at[r - 1], recv_sem=recv_r.at[r - 1],
                device_id=p, device_id_type=pl.DeviceIdType.MESH)
            cp.wait_send()
            cp.wait_recv()

    hbm, sem = pl.BlockSpec(memory_space=pltpu.HBM), pl.BlockSpec(memory_space=pltpu.SEMAPHORE)
    return pl.pallas_call(
        body, name=name,
        out_shape=(pltpu.HBM(src_thru.shape, src_thru.dtype), pltpu.HBM(land_thru.shape, land_thru.dtype)),
        in_specs=(hbm, hbm, sem, sem, _any_spec()), out_specs=(hbm, hbm),
        input_output_aliases={0: 0, 1: 1},
        compiler_params=pltpu.CompilerParams(has_side_effects=pltpu.SideEffectType.DATAFLOW_SIDE_EFFECTING),
    )(src_thru, land_thru, send, recv, after)


def _own_block(blocks, block, me):
    return lax.dynamic_update_index_in_dim(blocks, block, me, 0)


def _adamw(name, parts, w, m, v):
    rows, cols = w.shape
    br = min(ADAM_ROWS, rows)
    c1 = 1.0 / (1.0 - ADAM_B1 ** ADAM_STEP)
    c2 = 1.0 / (1.0 - ADAM_B2 ** ADAM_STEP)

    def body(p_ref, w_ref, m_ref, v_ref, g_o, d_o, m_o, v_o):
        g = p_ref[0].astype(F32)
        for k in range(1, N_DEV):
            g = g + p_ref[k].astype(F32)
        mn = ADAM_B1 * m_ref[...] + (1.0 - ADAM_B1) * g
        vn = ADAM_B2 * v_ref[...] + (1.0 - ADAM_B2) * (g * g)
        g_o[...] = g
        m_o[...] = mn
        v_o[...] = vn
        d_o[...] = -ADAM_LR * ((mn * c1) / (jnp.sqrt(vn * c2) + ADAM_EPS) + ADAM_WD * w_ref[...])

    blk = _bs((br, cols), lambda i: (i, 0))
    out = _sds((rows, cols), F32)
    return pl.pallas_call(
        body, name=name, grid=(rows // br,),
        in_specs=[_bs((N_DEV, br, cols), lambda i: (0, i, 0)), blk, blk, blk],
        out_specs=[blk] * 4, out_shape=[out] * 4,
        compiler_params=_params(("parallel",)),
    )(parts, w, m, v)


def _pack(arrs):
    tile = SUBLANES * LANES
    flat = []
    for a in arrs:
        f = a.reshape(-1).astype(F32)
        pad = (-f.shape[0]) % tile
        flat.append(jnp.pad(f, (0, pad)) if pad else f)
    total = sum(f.shape[0] for f in flat)
    tail = (-total) % (ADAM_ROWS * LANES)
    if tail:
        flat.append(jnp.zeros((tail,), F32))
    return jnp.concatenate(flat).reshape(-1, LANES)


def _unpack(buf, like):
    tile = SUBLANES * LANES
    flat = buf.reshape(-1)
    out, off = [], 0
    for a in like:
        n = math.prod(a.shape)
        out.append(flat[off:off + n].reshape(a.shape))
        off += n + ((-n) % tile)
    return out


def _to_block_b(b):
    gl = LANES // SSM_GROUP
    tb = b.reshape(N_SLAB, gl, SSM_STATE, SSM_GROUP).transpose(0, 1, 3, 2)
    eye = jnp.eye(gl, dtype=F32)
    return (tb[:, :, :, None, :] * eye[None, :, None, :, None]).reshape(N_SLAB, LANES, SLAB_STATE)


def _from_block_b(bx):
    gl = LANES // SSM_GROUP
    d = jnp.einsum("jghgp->jgph", bx.reshape(N_SLAB, gl, SSM_GROUP, gl, SSM_STATE))
    return d.reshape(N_SLAB * gl, SSM_STATE, SSM_GROUP)


def _to_block_c(cm):
    gl = LANES // SSM_GROUP
    tc = cm.reshape(N_SLAB, gl, SSM_GROUP, SSM_STATE).transpose(0, 1, 3, 2)
    eye = jnp.eye(gl, dtype=F32)
    return (tc[:, :, :, None, :] * eye[None, :, None, :, None]).reshape(N_SLAB, SLAB_STATE, LANES)


def _from_block_c(cx):
    gl = LANES // SSM_GROUP
    d = jnp.einsum("jgpgh->jghp", cx.reshape(N_SLAB, gl, SSM_STATE, gl, SSM_GROUP))
    return d.reshape(N_SLAB * gl, SSM_GROUP, SSM_STATE)


def kernel(x, norm_mix_g, w_in, ssm_a_re, ssm_a_im, ssm_b_re, ssm_b_im, ssm_c_re, ssm_c_im, ssm_d, ssm_log_dt, ssm_glu_w, ssm_glu_b, sgu_ln_g, sgu_ln_b, sgu_w, sgu_b, out_norm_ssm_g, out_norm_sgu_g, w_out, norm_mlp_g, w_up, w_down, norm_final_g, loss_target, m_norm_mix_g, m_w_in, m_ssm_a_re, m_ssm_a_im, m_ssm_b_re, m_ssm_b_im, m_ssm_c_re, m_ssm_c_im, m_ssm_d, m_ssm_log_dt, m_ssm_glu_w, m_ssm_glu_b, m_sgu_ln_g, m_sgu_ln_b, m_sgu_w, m_sgu_b, m_out_norm_ssm_g, m_out_norm_sgu_g, m_w_out, m_norm_mlp_g, m_w_up, m_w_down, m_norm_final_g, v_norm_mix_g, v_w_in, v_ssm_a_re, v_ssm_a_im, v_ssm_b_re, v_ssm_b_im, v_ssm_c_re, v_ssm_c_im, v_ssm_d, v_ssm_log_dt, v_ssm_glu_w, v_ssm_glu_b, v_sgu_ln_g, v_sgu_ln_b, v_sgu_w, v_sgu_b, v_out_norm_ssm_g, v_out_norm_sgu_g, v_w_out, v_norm_mlp_g, v_w_up, v_w_down, v_norm_final_g):
    given = dict(locals())
    names = ["norm_mix_g", "w_in", "ssm_a_re", "ssm_a_im", "ssm_b_re", "ssm_b_im", "ssm_c_re", "ssm_c_im",
             "ssm_d", "ssm_log_dt", "ssm_glu_w", "ssm_glu_b", "sgu_ln_g", "sgu_ln_b", "sgu_w", "sgu_b",
             "out_norm_ssm_g", "out_norm_sgu_g", "w_out", "norm_mlp_g", "w_up", "w_down", "norm_final_g"]
    big = ["w_in", "ssm_glu_w", "w_out", "w_up", "w_down"]
    small = [n for n in names if n not in big]

    d = D_MODEL
    t = x.shape[1]
    tb = min(1024, t)
    xs = x[0]
    target = loss_target[0]
    nsh_in = w_in.shape[2]
    nsh_up = w_up.shape[2]
    d_ff = nsh_up * N_DEV
    n_in = nsh_in * N_DEV

    me = _index(_place())
    (wg_in,) = _all_gather("gather_w_in", [w_in[0].astype(BF16)], by_columns=True)
    gathers = {}

    def start_gather(n, after=None):
        shard = given[n][0].astype(BF16)
        gathers[n] = _send_start("gather_start_" + n, shard, lax.empty((N_DEV,) + shard.shape, BF16), False, after)
        return gathers[n][4]

    tokens = [start_gather("ssm_glu_w", wg_in), start_gather("w_out", wg_in)]

    def gathered(n, after):
        shard, blocks = _send_wait("gather_wait_" + n, gathers[n], after, False)
        return _own_block(blocks, shard, me)

    nst = N_SLAB * SLAB_STATE
    are, aim = ssm_a_re.reshape(1, nst), ssm_a_im.reshape(1, nst)
    ldt = jnp.repeat(ssm_log_dt[0], SSM_STATE).reshape(1, nst)
    bxr, bxi = _to_block_b(ssm_b_re[0]), _to_block_b(ssm_b_im[0])
    cxr, cxi = _to_block_c(ssm_c_re[0]), _to_block_c(ssm_c_im[0])
    dvec = ssm_d.reshape(1, SSM_WIDTH)
    bre, bim, cre, cimn, apr, api, air, aii = _ssm_prep(are, aim, ldt, bxr, bxi, cxr, cxi)
    tabs = (bre, bim, cre, cimn, apr, api, air, aii)

    h1 = _rms_fwd("norm_mix", xs, norm_mix_g, deps=tokens)
    bn_i = n_in // 2
    (z,) = _mm("in_proj", "nn", (t // tb, n_in // bn_i, 1),
               (h1, _bs((tb, d), lambda i, j, k: (i, 0))),
               (wg_in, _bs((d, bn_i), lambda i, j, k: (0, j))),
               [(_sds((t, n_in), F32), _bs((tb, bn_i), lambda i, j, k: (i, j)))])
    tokens = [start_gather("w_up", z), start_gather("w_down", z)]
    y_pre, yg_b, p_re, p_im = _ssm_fwd(z, dvec, *tabs, deps=tokens)

    def glu_ep(acc, yp, b):
        gate = _sigmoid(acc + b)
        return _gelu(yp) * gate, gate

    hw = SSM_WIDTH // 2
    wg_glu = gathered("ssm_glu_w", yg_b).reshape(SSM_WIDTH, SSM_WIDTH)
    tile_g = _bs((tb, hw), lambda i, j, k: (i, j))
    y_ssm, gate = _mm("glu", "nn", (t // tb, 2, 1),
                      (yg_b, _bs((tb, SSM_WIDTH), lambda i, j, k: (i, 0))),
                      (wg_glu, _bs((SSM_WIDTH, hw), lambda i, j, k: (0, j))),
                      [(_sds((t, SSM_WIDTH), F32), tile_g), (_sds((t, SSM_WIDTH), F32), tile_g)],
                      extras=[(y_pre, tile_g), (ssm_glu_b, _bs((1, hw), lambda i, j, k: (0, j)))],
                      epilogue=glu_ep)

    sgu_bexp = jnp.broadcast_to(sgu_b[0][:, :, None], (SGU_HEADS, SGU_CHUNK, SGU_CHUNK))
    y_sgu = _sgu_fwd(z, sgu_ln_g, sgu_ln_b, sgu_w[0], sgu_bexp)
    mixed = _mix_norm(y_ssm, y_sgu, out_norm_ssm_g, out_norm_sgu_g)

    tb2 = min(512, t)
    row2 = _bs((tb2, d), lambda i, j, k: (i, 0))
    vec2 = _bs((1, d), lambda i, j, k: (0, 0))
    vec_sum = (_sds((1, d), F32), vec2)
    wg_out = gathered("w_out", mixed).reshape(d, d)

    def out_ep(acc, r, g):
        x2v = acc + r
        return x2v, _rms_math(x2v, g)

    x2, h2 = _mm("out_proj", "nn", (t // tb2, 1, 1),
                 (mixed, row2), (wg_out, _bs((d, d), lambda i, j, k: (0, 0))),
                 [(_sds((t, d), F32), row2), (_sds((t, d), BF16), row2)],
                 extras=[(xs, row2), (norm_mlp_g, vec2)], epilogue=out_ep, ep_rows=EP_ROWS)

    def up_ep(acc):
        r = jnp.maximum(acc, 0.0)
        return r * r, r

    tile_f = _bs((tb, nsh_up), lambda i, j, k: (i, j))
    wg_up = gathered("w_up", h2)
    f_act, r_act = _mm("mlp_up", "nn", (t // tb, N_DEV, 1),
                       (h2, _bs((tb, d), lambda i, j, k: (i, 0))),
                       (wg_up, _bs((None, d, nsh_up), lambda i, j, k: (j, 0, 0))),
                       [(_sds((t, d_ff), BF16), tile_f), (_sds((t, d_ff), BF16), tile_f)],
                       epilogue=up_ep)
    bk_d, bn_o = 2048, 1024
    tile_o = _bs((tb, bn_o), lambda i, j, k: (i, j))
    wg_down = gathered("w_down", f_act).reshape(d_ff, d)
    (x3,) = _mm("mlp_down", "nn", (t // tb, d // bn_o, d_ff // bk_d),
                (f_act, _bs((tb, bk_d), lambda i, j, k: (i, k))),
                (wg_down, _bs((bk_d, bn_o), lambda i, j, k: (k, j))),
                [(_sds((t, d), F32), tile_o)],
                extras=[(x2, tile_o)], epilogue=lambda acc, r: (acc + r,))
    dx3, dx3_b, g_final, err2 = _final_loss(x3, target, norm_final_g.reshape(1, d))
    loss = lax.psum(0.5 * jnp.sum(err2) / d, MESH_AXES)

    sends = {}

    def send_grad(n, g, land_shape=None):
        sends[n] = _send_start("grad_start_" + n, g, lax.empty(land_shape or g.shape, BF16), True)
        return [sends[n][4]]

    bn_a = 1024
    tile_a = _bs((tb, bn_a), lambda i, j, k: (i, j))
    (da,) = _mm("mlp_down_dx", "nt", (t // tb, d_ff // bn_a, 1),
                (dx3_b, _bs((tb, d), lambda i, j, k: (i, 0))),
                (wg_down, _bs((bn_a, d), lambda i, j, k: (j, 0))),
                [(_sds((t, d_ff), BF16), tile_a)],
                extras=[(r_act, tile_a)], epilogue=lambda acc, r: (acc * (2.0 * r.astype(F32)),))
    sq = 1024
    (gw_down,) = _mm("mlp_down_dw", "tn", (d_ff // sq, 1, t // tb),
                     (f_act, _bs((tb, sq), lambda i, j, k: (k, i))),
                     (dx3_b, _bs((tb, d), lambda i, j, k: (k, 0))),
                     [(_sds((d_ff, d), BF16), _bs((sq, d), lambda i, j, k: (i, 0)))])
    sent = send_grad("w_down", gw_down.reshape(N_DEV, -1, d))
    (dh2,) = _mm("mlp_up_dx", "nt", (t // tb, 1, N_DEV),
                 (da, _bs((tb, nsh_up), lambda i, j, k: (i, k))),
                 (wg_up, _bs((None, d, nsh_up), lambda i, j, k: (k, 0, 0))),
                 [(_sds((t, d), F32), _bs((tb, d), lambda i, j, k: (i, 0)))], deps=sent)

    def norm_bwd_side(dh, xv, dres, g):
        dx, dg = _rms_bwd_math(dh, xv, g)
        dx = dx + dres
        return dx, dx, dg

    up_dw_grid = (1, N_DEV, t // tb)
    gw_up, dx2, dx2_b, g_norm_mlp = _mm(
        "mlp_up_dw", "tn", up_dw_grid,
        (h2, _bs((tb, d), lambda i, j, k: (k, 0))),
        (da, _bs((tb, nsh_up), lambda i, j, k: (k, j))),
        [(_sds((N_DEV, d, nsh_up), BF16), _bs((None, d, nsh_up), lambda i, j, k: (j, 0, 0)))],
        side=(norm_bwd_side, [(dh2, d), (x2, d), (dx3, d), (norm_mlp_g, d)],
              [(_sds((t, d), F32), d), (_sds((t, d), BF16), d)], [(_sds((1, d), F32), d)],
              t // math.prod(up_dw_grid)))
    sent = send_grad("w_up", gw_up)

    tk = min(2048, t)
    (gw_out,) = _mm("out_proj_dw", "tn", (d // sq, d // sq, t // tk),
                    (mixed, _bs((tk, sq), lambda i, j, k: (k, i))),
                    (dx2_b, _bs((tk, sq), lambda i, j, k: (k, j))),
                    [(_sds((d, d), BF16), _bs((sq, sq), lambda i, j, k: (i, j)))], deps=sent)
    sent = send_grad("w_out", gw_out.reshape(N_DEV, -1, d))
    half2 = _bs((tb2, SSM_WIDTH), lambda i, j, k: (i, 0))
    vech = _bs((1, SSM_WIDTH), lambda i, j, k: (0, 0))
    half_sum = (_sds((1, SSM_WIDTH), F32), vech)

    def out_dx_ep(acc, ya, yb, ga, gb, yp, gt):
        dya, dga = _rms_bwd_math(acc[:, :SSM_WIDTH], ya, ga)
        dyb, dgb = _rms_bwd_math(acc[:, SSM_WIDTH:], yb, gb)
        dpre = dya * _gelu(yp) * gt * (1.0 - gt)
        return dya, dyb, dpre, dga, dgb, jnp.sum(dpre, axis=0, keepdims=True)

    dy_ssm, dy_sgu, dpre_b, g_onorm_ssm, g_onorm_sgu, g_glu_b = _mm(
        "out_proj_dx", "nt", (t // tb2, 1, 1),
        (dx2_b, row2), (wg_out, _bs((d, d), lambda i, j, k: (0, 0))),
        [(_sds((t, SSM_WIDTH), F32), half2), (_sds((t, SSM_WIDTH), F32), half2),
         (_sds((t, SSM_WIDTH), BF16), half2)],
        extras=[(y_ssm, half2), (y_sgu, half2), (out_norm_ssm_g, vech), (out_norm_sgu_g, vech),
                (y_pre, half2), (gate, half2)],
        epilogue=out_dx_ep, sums=[half_sum, half_sum, half_sum], deps=sent, acc_shape=(tb2, d),
        ep_rows=EP_ROWS)

    (gw_glu,) = _mm("glu_dw", "tn", (1, 1, t // tb),
                    (yg_b, _bs((tb, SSM_WIDTH), lambda i, j, k: (k, 0))),
                    (dpre_b, _bs((tb, SSM_WIDTH), lambda i, j, k: (k, 0))),
                    [(_sds((SSM_WIDTH, SSM_WIDTH), BF16), _bs((SSM_WIDTH, SSM_WIDTH), lambda i, j, k: (0, 0)))])
    sent = send_grad("ssm_glu_w", gw_glu.reshape(N_DEV, -1, SSM_WIDTH))
    (dy_pre,) = _mm("glu_dx", "nt", (t // tb, 2, 1),
                    (dpre_b, _bs((tb, SSM_WIDTH), lambda i, j, k: (i, 0))),
                    (wg_glu, _bs((hw, SSM_WIDTH), lambda i, j, k: (j, 0))),
                    [(_sds((t, SSM_WIDTH), F32), tile_g)],
                    extras=[(dy_ssm, tile_g), (gate, tile_g), (y_pre, tile_g)],
                    epilogue=lambda acc, dy, gt, yp: ((dy * gt + acc) * _gelu_grad(yp),), deps=sent)
    du_b, dbre, dbim, dcre, dcimn, q_re, q_im, dd = _ssm_bwd(dy_pre, z, p_re, p_im, dvec, *tabs)
    dare, daim, dldt, dbxr, dbxi = _ssm_prep_bwd(are, aim, ldt, bxr, bxi, dbre, dbim, q_re, q_im)

    dz_b, g_ln_g, g_ln_b, g_sgu_w, g_sgu_bx = _sgu_bwd(dy_sgu, du_b, z, sgu_ln_g, sgu_ln_b, sgu_w[0], sgu_bexp)
    local_small = {
        "ssm_a_re": dare, "ssm_a_im": daim,
        "ssm_b_re": _from_block_b(dbxr), "ssm_b_im": _from_block_b(dbxi),
        "ssm_c_re": _from_block_c(dcre), "ssm_c_im": -_from_block_c(dcimn),
        "ssm_d": dd, "ssm_log_dt": dldt.reshape(-1, SSM_STATE).sum(axis=-1),
        "ssm_glu_b": g_glu_b, "sgu_ln_g": g_ln_g, "sgu_ln_b": g_ln_b, "sgu_w": g_sgu_w,
        "sgu_b": g_sgu_bx[:, :, 0], "out_norm_ssm_g": g_onorm_ssm, "out_norm_sgu_g": g_onorm_sgu,
        "norm_mlp_g": g_norm_mlp, "norm_final_g": g_final,
    }
    small_early = [n for n in small if n in local_small]
    small_late = [n for n in small if n not in local_small]
    packed = _pack([local_small[n] for n in small_early])
    small_send = _send_start("small_start", packed, lax.empty((N_DEV,) + packed.shape, F32), False)
    (gw_in,) = _mm("in_proj_dw", "tn", (d // sq, n_in // bn_i, t // tb),
                   (h1, _bs((tb, sq), lambda i, j, k: (k, i))),
                   (dz_b, _bs((tb, bn_i), lambda i, j, k: (k, j))),
                   [(_sds((d, n_in), BF16), _bs((sq, bn_i), lambda i, j, k: (i, j)))],
                   deps=[small_send[4]])
    sent = send_grad("w_in", gw_in, (N_DEV, d, nsh_in))

    def in_dx_ep(acc, xv, dres, g):
        dx, dg = _rms_bwd_math(acc, xv, g)
        return dx + dres, dg

    grad_x, g_norm_mix = _mm("in_proj_dx", "nt", (t // tb2, 1, n_in // bn_i),
                             (dz_b, _bs((tb2, bn_i), lambda i, j, k: (i, k))),
                             (wg_in, _bs((d, bn_i), lambda i, j, k: (0, k))),
                             [(_sds((t, d), F32), row2)],
                             extras=[(xs, row2), (dx2, row2), (norm_mix_g, vec2)],
                             epilogue=in_dx_ep, sums=[vec_sum], deps=sent, ep_rows=EP_ROWS)

    (late_parts,) = _all_gather("gather_late_grads", [_pack([g_norm_mix])])
    packed, early_parts = _send_wait("small_wait", small_send, grad_x, False)
    early_parts = _own_block(early_parts, packed, me)

    grads, deltas, new_m, new_v = {}, {}, {}, {}
    for n in big:
        w2 = given[n][0]
        sent_blocks, landed = _send_wait("grad_wait_" + n, sends[n], grad_x, True)
        if sent_blocks.ndim == landed.ndim:
            own = lax.dynamic_index_in_dim(sent_blocks, me, 0, keepdims=False)
        else:
            own = lax.dynamic_slice_in_dim(sent_blocks, me * landed.shape[2], landed.shape[2], axis=1)
        parts = _own_block(landed, own, me)
        res = _adamw("adamw_" + n, parts, w2, given["m_" + n][0], given["v_" + n][0])
        grads[n], deltas[n], new_m[n], new_v[n] = [r.reshape(given[n].shape) for r in res]
    for tag, group, parts in (("early", small_early, early_parts), ("late", small_late, late_parts)):
        like = [given[n] for n in group]
        res = _adamw("adamw_small_" + tag, parts, _pack(like), _pack([given["m_" + n] for n in group]),
                     _pack([given["v_" + n] for n in group]))
        for store, buf in zip((grads, deltas, new_m, new_v), res):
            for n, a in zip(group, _unpack(buf, like)):
                store[n] = a

    return (loss, grad_x.reshape(x.shape), *[grads[n] for n in names], *[deltas[n] for n in names],
            *[new_m[n] for n in names], *[new_v[n] for n in names])
```

```python
import functools
import math

import jax
import jax.numpy as jnp
from jax import lax
from jax.experimental import pallas as pl
from jax.experimental.pallas import tpu as pltpu

F32, BF16 = jnp.float32, jnp.bfloat16
EPS = 1e-6
N_DEV = 8
D_MODEL = 2048
SSM_WIDTH = 1024
SSM_GROUP = 16
SSM_STATE = 64
SGU_HEADS = 8
SGU_CHUNK = 128
LANES = 128
SUBLANES = 8
N_SLAB = SSM_WIDTH // LANES
SLAB_STATE = (LANES // SSM_GROUP) * SSM_STATE
SCAN_BLOCK = 256
SLABS_PER_STEP = 8
VMEM_LIMIT = 56 * 1024 * 1024
ROW_BLOCK = 256
EP_ROWS = 128
ADAM_ROWS = 128
MESH_AXES = ("x", "y", "c")

ADAM_LR, ADAM_B1, ADAM_B2, ADAM_EPS, ADAM_WD, ADAM_STEP = 0.001, 0.9, 0.999, 1e-08, 0.01, 10

_GELU_C0 = math.sqrt(2.0 / math.pi)
_GELU_C1 = 0.044715


def _gelu(v):
    return 0.5 * v * (1.0 + jnp.tanh(_GELU_C0 * (v + _GELU_C1 * v * v * v)))


def _gelu_grad(v):
    th = jnp.tanh(_GELU_C0 * (v + _GELU_C1 * v * v * v))
    return 0.5 * (1.0 + th) + 0.5 * v * (1.0 - th * th) * _GELU_C0 * (1.0 + 3.0 * _GELU_C1 * v * v)


def _sigmoid(v):
    return 1.0 / (1.0 + jnp.exp(-v))


def _params(sem=None):
    return pltpu.CompilerParams(dimension_semantics=sem, vmem_limit_bytes=VMEM_LIMIT)


def _dot(a, b, mode="nn"):
    dims = {"nn": ((1,), (0,)), "nt": ((1,), (1,)), "tn": ((0,), (0,))}[mode]
    return lax.dot_general(a, b, (dims, ((), ())), preferred_element_type=F32)


def _mm(name, mode, grid, a, b, outs, extras=(), epilogue=None, deps=(), sums=(), acc_shape=None,
        ep_rows=None, side=None):
    nk = grid[2]
    n_ex, n_out, n_dep, n_sum = len(extras), len(outs), len(deps), len(sums)
    assert not sums or grid[1] == 1
    if acc_shape is None:
        acc_shape = tuple(d for d in outs[0][1].block_shape if d is not None)
    side_fn, side_ins, side_outs, side_sums, side_rows = side or (None, (), (), (), 0)
    s_in, s_out, s_sum = len(side_ins), len(side_outs), len(side_sums)

    def body(*refs):
        a_ref, b_ref = refs[0], refs[1]
        ex = refs[2:2 + n_ex]
        pos = 2 + n_ex
        side_in = refs[pos:pos + s_in]
        pos += s_in + n_dep
        out_refs = refs[pos:pos + n_out]
        sum_refs = refs[pos + n_out:pos + n_out + n_sum]
        pos += n_out + n_sum
        side_out = refs[pos:pos + s_out]
        side_sum = refs[pos + s_out:pos + s_out + s_sum]
        acc = refs[-1]
        k = pl.program_id(2)

        @pl.when(k == 0)
        def _():
            acc[...] = jnp.zeros_like(acc)

        if side_sum:
            @pl.when((pl.program_id(0) == 0) & (pl.program_id(1) == 0) & (k == 0))
            def _():
                for o in side_sum:
                    o[...] = jnp.zeros_like(o)

        acc[...] += _dot(a_ref[...], b_ref[...], mode)
        if side_fn is not None:
            res = side_fn(*[r[...] for r in side_in])
            for o, r in zip(side_out, res[:s_out]):
                o[...] = r.astype(o.dtype)
            for o, r in zip(side_sum, res[s_out:]):
                o[...] += r

        def finish(rows):
            args = [e[rows, :] if e.shape[0] == acc_shape[0] else e[...] for e in ex]
            res = acc[rows, :]
            res = (res,) if epilogue is None else epilogue(res, *args)
            for o, r in zip(out_refs, res[:n_out]):
                o[rows, :] = r.astype(o.dtype)
            return tuple(res[n_out:])

        @pl.when(k == nk - 1)
        def _():
            if ep_rows is None:
                terms = finish(slice(None))
            else:
                def chunk(c, tot):
                    rows = pl.ds(pl.multiple_of(c * ep_rows, ep_rows), ep_rows)
                    return tuple(s + r for s, r in zip(tot, finish(rows)))

                zero = tuple(jnp.zeros(o.shape, F32) for o in sum_refs)
                terms = lax.fori_loop(0, acc_shape[0] // ep_rows, chunk, zero)
            for o, r in zip(sum_refs, terms):
                _add_up(o, r, pl.program_id(0) == 0)

    def side_rows_spec(width):
        return _bs((side_rows, width), lambda i, j, k: ((i * grid[1] + j) * grid[2] + k, 0))

    def side_vec_spec(width):
        return _bs((1, width), lambda i, j, k: (0, 0))

    sem = ("arbitrary",) * 3 if sums or side else ("parallel", "parallel", "arbitrary")
    res = pl.pallas_call(
        body, name=name, grid=grid,
        in_specs=[a[1], b[1]] + [e[1] for e in extras]
        + [side_rows_spec(w) if x.shape[0] > 1 else side_vec_spec(w) for x, w in side_ins]
        + [_any_spec()] * n_dep,
        out_specs=[o[1] for o in outs] + [o[1] for o in sums]
        + [side_rows_spec(w) for _, w in side_outs] + [side_vec_spec(w) for _, w in side_sums],
        out_shape=[o[0] for o in outs] + [o[0] for o in sums]
        + [o[0] for o in side_outs] + [o[0] for o in side_sums],
        scratch_shapes=[pltpu.VMEM(acc_shape, F32)],
        compiler_params=_params(sem),
    )(a[0], b[0], *[e[0] for e in extras], *[x for x, _ in side_ins], *deps)
    return res


def _add_up(ref, term, first):
    @pl.when(first)
    def _():
        ref[...] = term

    @pl.when(jnp.logical_not(first))
    def _():
        ref[...] += term


def _rms_math(xv, g):
    return xv * lax.rsqrt(jnp.mean(xv * xv, axis=-1, keepdims=True) + EPS) * g


def _rms_bwd_math(dy, xv, g):
    r = lax.rsqrt(jnp.mean(xv * xv, axis=-1, keepdims=True) + EPS)
    xhat = xv * r
    dxhat = dy * g
    dx = r * (dxhat - xhat * jnp.mean(dxhat * xhat, axis=-1, keepdims=True))
    return dx, jnp.sum(dy * xhat, axis=0, keepdims=True)


def _sds(shape, dtype):
    return jax.ShapeDtypeStruct(shape, dtype)


def _bs(shape, fn):
    return pl.BlockSpec(shape, fn)


def _rms_fwd(name, x, g, deps=()):
    t, w = x.shape
    br = min(ROW_BLOCK, t)

    def body(*refs):
        x_ref, g_ref, o_ref = refs[0], refs[1], refs[-1]
        xv = x_ref[...]
        r = lax.rsqrt(jnp.mean(xv * xv, axis=-1, keepdims=True) + EPS)
        o_ref[...] = (xv * r * g_ref[...]).astype(BF16)

    return pl.pallas_call(
        body, name=name, grid=(t // br,),
        in_specs=[_bs((br, w), lambda i: (i, 0)), _bs((1, w), lambda i: (0, 0))] + [_any_spec()] * len(deps),
        out_specs=_bs((br, w), lambda i: (i, 0)),
        out_shape=_sds((t, w), BF16),
        compiler_params=_params(("parallel",)),
    )(x, g, *deps)


def _mix_norm(ya, yb, ga, gb):
    t, w = ya.shape
    br = min(ROW_BLOCK, t)

    def body(a_ref, b_ref, ga_ref, gb_ref, o_ref):
        for src, g_ref, col in ((a_ref, ga_ref, 0), (b_ref, gb_ref, w)):
            v = src[...]
            r = lax.rsqrt(jnp.mean(v * v, axis=-1, keepdims=True) + EPS)
            o_ref[:, col:col + w] = (v * r * g_ref[...]).astype(BF16)

    row = _bs((br, w), lambda i: (i, 0))
    vec = _bs((1, w), lambda i: (0, 0))
    return pl.pallas_call(
        body, name="mix_norm", grid=(t // br,),
        in_specs=[row, row, vec, vec],
        out_specs=_bs((br, 2 * w), lambda i: (i, 0)),
        out_shape=_sds((t, 2 * w), BF16),
        compiler_params=_params(("parallel",)),
    )(ya, yb, ga, gb)


def _loss_math(xv, target, g):
    r = lax.rsqrt(jnp.mean(xv * xv, axis=-1, keepdims=True) + EPS)
    xhat = xv * r
    err = xhat * g - target
    dy = err * (1.0 / xv.shape[-1])
    dxhat = dy * g
    dx = r * (dxhat - xhat * jnp.mean(dxhat * xhat, axis=-1, keepdims=True))
    return dx, jnp.sum(dy * xhat, axis=0, keepdims=True), jnp.sum(err * err, axis=0, keepdims=True)


def _final_loss(x3, target, g):
    t, w = x3.shape
    br = min(ROW_BLOCK, t)

    def body(x_ref, tg_ref, g_ref, dx_ref, dxb_ref, dg_ref, l_ref):
        dx, dg, e2 = _loss_math(x_ref[...], tg_ref[...], g_ref[...])
        dx_ref[...] = dx
        dxb_ref[...] = dx.astype(BF16)
        _add_up(dg_ref, dg, pl.program_id(0) == 0)
        _add_up(l_ref, e2, pl.program_id(0) == 0)

    row = _bs((br, w), lambda i: (i, 0))
    vec = _bs((1, w), lambda i: (0, 0))
    return pl.pallas_call(
        body, name="final_loss", grid=(t // br,),
        in_specs=[row, row, vec], out_specs=[row, row, vec, vec],
        out_shape=[_sds((t, w), F32), _sds((t, w), BF16), _sds((1, w), F32), _sds((1, w), F32)],
        compiler_params=_params(("arbitrary",)),
    )(x3, target, g)


def _prep_math(are, aim, ldt, bxr, bxi):
    dt = jnp.exp(ldt)
    er = jnp.exp(are * dt)
    th = aim * dt
    abr, abi = er * jnp.cos(th), er * jnp.sin(th)
    nr, ni = abr - 1.0, abi
    den = are * are + aim * aim
    cr = (nr * are + ni * aim) / den
    ci = (ni * are - nr * aim) / den
    bbr, bbi = [], []
    for j in range(N_SLAB):
        sl = slice(j * SLAB_STATE, (j + 1) * SLAB_STATE)
        bbr.append(cr[:, sl] * bxr[j] - ci[:, sl] * bxi[j])
        bbi.append(cr[:, sl] * bxi[j] + ci[:, sl] * bxr[j])
    return abr, abi, bbr, bbi


def _ssm_prep(are, aim, ldt, bxr, bxi, cxr, cxi):
    nst = N_SLAB * SLAB_STATE

    def body(are_r, aim_r, ldt_r, bxr_r, bxi_r, cxr_r, cxi_r,
             bre_o, bim_o, cre_o, cimn_o, apr_o, api_o, air_o, aii_o):
        abr, abi, bbr, bbi = _prep_math(are_r[...], aim_r[...], ldt_r[...],
                                        [bxr_r[j] for j in range(N_SLAB)], [bxi_r[j] for j in range(N_SLAB)])
        for j in range(N_SLAB):
            bre_o[j] = bbr[j].astype(BF16)
            bim_o[j] = bbi[j].astype(BF16)
        cre_o[...] = cxr_r[...].astype(BF16)
        cimn_o[...] = (-cxi_r[...]).astype(BF16)

        def step(k, cur):
            cr, ci = cur
            den = cr * cr + ci * ci
            apr_o[pl.ds(k, 1), :] = cr
            api_o[pl.ds(k, 1), :] = ci
            air_o[pl.ds(k, 1), :] = cr / den
            aii_o[pl.ds(k, 1), :] = -ci / den
            return cr * abr - ci * abi, cr * abi + ci * abr

        lax.fori_loop(0, SCAN_BLOCK, step, (jnp.ones((1, nst), F32), jnp.zeros((1, nst), F32)))

    tab = _sds((SCAN_BLOCK, nst), F32)
    return pl.pallas_call(
        body, name="ssm_prep",
        out_shape=[_sds(bxr.shape, BF16), _sds(bxr.shape, BF16), _sds(cxr.shape, BF16), _sds(cxr.shape, BF16),
                   tab, tab, tab, tab],
        compiler_params=_params(),
    )(are, aim, ldt, bxr, bxi, cxr, cxi)


def _ssm_prep_bwd(are, aim, ldt, bxr, bxi, dbre, dbim, qr, qi):
    def body(are_r, aim_r, ldt_r, bxr_r, bxi_r, dbre_r, dbim_r, qr_r, qi_r,
             dare_o, daim_o, dldt_o, dbxr_o, dbxi_o):
        prim = (are_r[...], aim_r[...], ldt_r[...],
                [bxr_r[j] for j in range(N_SLAB)], [bxi_r[j] for j in range(N_SLAB)])
        (abr, abi, _, _), vjp = jax.vjp(_prep_math, *prim)
        den = abr * abr + abi * abi
        q_r, q_i = qr_r[...], qi_r[...]
        gar = (q_r * abr - q_i * abi) / den
        gai = (q_r * abi + q_i * abr) / den
        ct = (gar, gai, [dbre_r[j] for j in range(N_SLAB)], [dbim_r[j] for j in range(N_SLAB)])
        dare, daim, dldt, dbxr, dbxi = vjp(ct)
        dare_o[...] = dare
        daim_o[...] = daim
        dldt_o[...] = dldt
        for j in range(N_SLAB):
            dbxr_o[j] = dbxr[j]
            dbxi_o[j] = dbxi[j]

    row = _sds(are.shape, F32)
    return pl.pallas_call(
        body, name="ssm_prep_bwd",
        out_shape=[row, row, row, _sds(bxr.shape, F32), _sds(bxr.shape, F32)],
        compiler_params=_params(),
    )(are, aim, ldt, bxr, bxi, dbre, dbim, qr, qi)


def _tri(lower):
    r = lax.broadcasted_iota(jnp.int32, (SCAN_BLOCK, SCAN_BLOCK), 0)
    c = lax.broadcasted_iota(jnp.int32, (SCAN_BLOCK, SCAN_BLOCK), 1)
    return jnp.where((r >= c) if lower else (r <= c), 1.0, 0.0).astype(BF16)


def _cumsum_mxu(tri, v):
    return _dot(tri, v.astype(BF16))


def _ssm_specs(t, sps):
    nt = t // SCAN_BLOCK
    tab = _bs((SCAN_BLOCK, sps * SLAB_STATE), lambda j, i: (0, j))
    bmat = _bs((sps, LANES, SLAB_STATE), lambda j, i: (j, 0, 0))
    cmat = _bs((sps, SLAB_STATE, LANES), lambda j, i: (j, 0, 0))
    return nt, tab, bmat, cmat


def _slab_slices(s):
    return slice(s * LANES, (s + 1) * LANES), slice(s * SLAB_STATE, (s + 1) * SLAB_STATE)


def _ssm_fwd(z, dvec, bre, bim, cre, cimn, apr, api, air, aii, deps=()):
    t = z.shape[0]
    sps = SLABS_PER_STEP
    nt, tab, bmat, cmat = _ssm_specs(t, sps)
    nst = N_SLAB * SLAB_STATE
    last = SCAN_BLOCK - 1

    def body(*refs):
        u_ref, d_ref, bre_r, bim_r, cre_r, cimn_r, apr_r, api_r, air_r, aii_r = refs[:10]
        y_ref, yg_ref, pr_ref, pi_ref, car_r, car_i = refs[10 + len(deps):]

        @pl.when(pl.program_id(1) == 0)
        def _():
            car_r[...] = jnp.zeros_like(car_r)
            car_i[...] = jnp.zeros_like(car_i)

        tri = _tri(True)
        for s in range(sps):
            ul, sl = _slab_slices(s)
            u = u_ref[:, ul]
            ub = u.astype(BF16)
            bur, bui = _dot(ub, bre_r[s]), _dot(ub, bim_r[s])
            ir, ii = air_r[:, sl], aii_r[:, sl]
            csr = _cumsum_mxu(tri, ir * bur - ii * bui)
            csi = _cumsum_mxu(tri, ir * bui + ii * bur)
            pr, pi = apr_r[:, sl], api_r[:, sl]
            a_r, a_i = apr_r[1:2, sl], api_r[1:2, sl]
            c_r, c_i = car_r[:, sl], car_i[:, sl]
            wr = csr + (a_r * c_r - a_i * c_i)
            wi = csi + (a_r * c_i + a_i * c_r)
            sr = pr * wr - pi * wi
            si = pr * wi + pi * wr
            car_r[:, sl] = sr[last:last + 1, :]
            car_i[:, sl] = si[last:last + 1, :]
            pr_ref[:, sl] = (sr - bur).astype(BF16)
            pi_ref[:, sl] = (si - bui).astype(BF16)
            y = _dot(sr.astype(BF16), cre_r[s]) + _dot(si.astype(BF16), cimn_r[s]) + d_ref[:, ul] * u
            y_ref[:, ul] = y
            yg_ref[:, ul] = _gelu(y).astype(BF16)

    ublk = _bs((SCAN_BLOCK, sps * LANES), lambda j, i: (i, j))
    sblk = _bs((SCAN_BLOCK, sps * SLAB_STATE), lambda j, i: (i, j))
    return pl.pallas_call(
        body, name="ssm_fwd", grid=(N_SLAB // sps, nt),
        in_specs=[ublk, _bs((1, sps * LANES), lambda j, i: (0, j)), bmat, bmat, cmat, cmat, tab, tab, tab, tab]
        + [_any_spec()] * len(deps),
        out_specs=[ublk, ublk, sblk, sblk],
        out_shape=[_sds((t, SSM_WIDTH), F32), _sds((t, SSM_WIDTH), BF16),
                   _sds((t, nst), BF16), _sds((t, nst), BF16)],
        scratch_shapes=[pltpu.VMEM((1, sps * SLAB_STATE), F32), pltpu.VMEM((1, sps * SLAB_STATE), F32)],
        compiler_params=_params(("parallel", "arbitrary")),
    )(z, dvec, bre, bim, cre, cimn, apr, api, air, aii, *deps)


def _ssm_bwd(gy, z, p_re, p_im, dvec, bre, bim, cre, cimn, apr, api, air, aii):
    t = z.shape[0]
    sps = SLABS_PER_STEP
    nt, tab, bmat, cmat = _ssm_specs(t, sps)
    last = SCAN_BLOCK - 1

    def fold(v):
        return v.reshape(SCAN_BLOCK // SUBLANES, SUBLANES, v.shape[-1]).sum(axis=0)

    def body(g_ref, u_ref, pr_ref, pi_ref, d_ref, bre_r, bim_r, cre_r, cimn_r, apr_r, api_r, air_r, aii_r,
             du_ref, dbre_o, dbim_o, dcre_o, dcimn_o, qr_o, qi_o, dd_o, car_r, car_i, qacc_r, qacc_i, dacc):
        i = pl.program_id(1)

        @pl.when(i == 0)
        def _():
            for ref in (car_r, car_i, qacc_r, qacc_i, dacc, dbre_o, dbim_o, dcre_o, dcimn_o):
                ref[...] = jnp.zeros_like(ref)

        tri = _tri(False)
        for s in range(sps):
            ul, sl = _slab_slices(s)
            g = g_ref[:, ul]
            gb = g.astype(BF16)
            u = u_ref[:, ul]
            ub = u.astype(BF16)
            bur, bui = _dot(ub, bre_r[s]), _dot(ub, bim_r[s])
            p_r, p_i = pr_ref[:, sl].astype(F32), pi_ref[:, sl].astype(F32)
            srb, sib = (p_r + bur).astype(BF16), (p_i + bui).astype(BF16)
            dcre_o[s] += _dot(srb, gb, "tn")
            dcimn_o[s] += _dot(sib, gb, "tn")
            dsr, dsi = _dot(gb, cre_r[s], "nt"), _dot(gb, cimn_r[s], "nt")
            pr, pi = apr_r[:, sl], api_r[:, sl]
            csr = _cumsum_mxu(tri, pr * dsr + pi * dsi)
            csi = _cumsum_mxu(tri, pr * dsi - pi * dsr)
            al_r, al_i = apr_r[last:last + 1, sl], api_r[last:last + 1, sl]
            c_r, c_i = car_r[:, sl], car_i[:, sl]
            wr = csr + (al_r * c_r + al_i * c_i)
            wi = csi + (al_r * c_i - al_i * c_r)
            ir, ii = air_r[:, sl], aii_r[:, sl]
            lr = ir * wr + ii * wi
            li = ir * wi - ii * wr
            a_r, a_i = apr_r[1:2, sl], api_r[1:2, sl]
            car_r[:, sl] = a_r * lr[0:1, :] + a_i * li[0:1, :]
            car_i[:, sl] = a_r * li[0:1, :] - a_i * lr[0:1, :]
            lrb, lib = lr.astype(BF16), li.astype(BF16)
            dbre_o[s] += _dot(ub, lrb, "tn")
            dbim_o[s] += _dot(ub, lib, "tn")
            du = d_ref[:, ul] * g + _dot(lrb, bre_r[s], "nt") + _dot(lib, bim_r[s], "nt")
            du_ref[:, ul] = du.astype(BF16)
            qacc_r[:, sl] += fold(lr * p_r + li * p_i)
            qacc_i[:, sl] += fold(li * p_r - lr * p_i)
            dacc[:, ul] += fold(g * u)

        @pl.when(i == nt - 1)
        def _():
            qr_o[...] = jnp.sum(qacc_r[...], axis=0, keepdims=True)
            qi_o[...] = jnp.sum(qacc_i[...], axis=0, keepdims=True)
            dd_o[...] = jnp.sum(dacc[...], axis=0, keepdims=True)

    rev = lambda j, i: (nt - 1 - i, j)
    ublk = _bs((SCAN_BLOCK, sps * LANES), rev)
    sblk = _bs((SCAN_BLOCK, sps * SLAB_STATE), rev)
    qrow = _bs((1, sps * SLAB_STATE), lambda j, i: (0, j))
    urow = _bs((1, sps * LANES), lambda j, i: (0, j))
    nst = N_SLAB * SLAB_STATE
    return pl.pallas_call(
        body, name="ssm_bwd", grid=(N_SLAB // sps, nt),
        in_specs=[ublk, ublk, sblk, sblk, urow, bmat, bmat, cmat, cmat, tab, tab, tab, tab],
        out_specs=[ublk, bmat, bmat, cmat, cmat, qrow, qrow, urow],
        out_shape=[_sds((t, SSM_WIDTH), BF16),
                   _sds((N_SLAB, LANES, SLAB_STATE), F32), _sds((N_SLAB, LANES, SLAB_STATE), F32),
                   _sds((N_SLAB, SLAB_STATE, LANES), F32), _sds((N_SLAB, SLAB_STATE, LANES), F32),
                   _sds((1, nst), F32), _sds((1, nst), F32), _sds((1, SSM_WIDTH), F32)],
        scratch_shapes=[pltpu.VMEM((1, sps * SLAB_STATE), F32), pltpu.VMEM((1, sps * SLAB_STATE), F32),
                        pltpu.VMEM((SUBLANES, sps * SLAB_STATE), F32), pltpu.VMEM((SUBLANES, sps * SLAB_STATE), F32),
                        pltpu.VMEM((SUBLANES, sps * LANES), F32)],
        compiler_params=_params(("parallel", "arbitrary")),
    )(gy, z, p_re, p_im, dvec, bre, bim, cre, cimn, apr, api, air, aii)


def _sgu_mask():
    r = lax.broadcasted_iota(jnp.int32, (SGU_CHUNK, SGU_CHUNK), 0)
    c = lax.broadcasted_iota(jnp.int32, (SGU_CHUNK, SGU_CHUNK), 1)
    return r >= c


def _sgu_common(zu, zv, lng, lnb):
    us, v = _gelu(zu), _gelu(zv)
    mu = jnp.mean(v, axis=-1, keepdims=True)
    vc = v - mu
    rstd = lax.rsqrt(jnp.mean(vc * vc, axis=-1, keepdims=True) + EPS)
    xhat = vc * rstd
    return us, xhat, rstd, xhat * lng + lnb


def _sgu_fwd(z, lng, lnb, w, bexp):
    t = z.shape[0]
    hd = SGU_CHUNK

    def body(zu_ref, zv_ref, lng_ref, lnb_ref, w_ref, b_ref, y_ref):
        us, _, _, vn = _sgu_common(zu_ref[...], zv_ref[...], lng_ref[...], lnb_ref[...])
        vnb = vn.astype(BF16)
        mask = _sgu_mask()
        for h in range(SGU_HEADS):
            sl = slice(h * hd, (h + 1) * hd)
            wt = jnp.where(mask, w_ref[h], 0.0).astype(BF16)
            y_ref[:, sl] = us[:, sl] * (_dot(wt, vnb[:, sl]) + b_ref[h])

    row = lambda c: _bs((SGU_CHUNK, SSM_WIDTH), lambda i: (i, c))
    vec = _bs((1, SSM_WIDTH), lambda i: (0, 0))
    hmat = _bs((SGU_HEADS, hd, hd), lambda i: (0, 0, 0))
    return pl.pallas_call(
        body, name="sgu_fwd", grid=(t // SGU_CHUNK,),
        in_specs=[row(1), row(2), vec, vec, hmat, hmat],
        out_specs=row(0), out_shape=_sds((t, SSM_WIDTH), F32),
        compiler_params=_params(("parallel",)),
    )(z, z, lng, lnb, w, bexp)


def _sgu_bwd(dy, du_ssm, z, lng, lnb, w, bexp, deps=()):
    t = z.shape[0]
    hd = SGU_CHUNK
    nc = t // SGU_CHUNK

    def body(*refs):
        dy_ref, dus_ref, zu_ref, zv_ref, lng_ref, lnb_ref, w_ref, b_ref = refs[:8]
        dz_ref, dlng_o, dlnb_o, dw_o, db_o = refs[8 + len(deps):]
        i = pl.program_id(0)

        @pl.when(i == 0)
        def _():
            for ref in (dlng_o, dlnb_o, dw_o, db_o):
                ref[...] = jnp.zeros_like(ref)

        zu, zv = zu_ref[...], zv_ref[...]
        lng = lng_ref[...]
        us, xhat, rstd, vn = _sgu_common(zu, zv, lng, lnb_ref[...])
        vnb = vn.astype(BF16)
        dyv = dy_ref[...]
        mask = _sgu_mask()
        dus_parts, dvn_parts = [], []
        for h in range(SGU_HEADS):
            sl = slice(h * hd, (h + 1) * hd)
            wt = jnp.where(mask, w_ref[h], 0.0).astype(BF16)
            mixed = _dot(wt, vnb[:, sl]) + b_ref[h]
            dus_parts.append(dyv[:, sl] * mixed)
            dmix = dyv[:, sl] * us[:, sl]
            dmb = dmix.astype(BF16)
            db_o[h] += dmix
            dw_o[h] += _dot(dmb, vnb[:, sl], "nt")
            dvn_parts.append(_dot(wt, dmb, "tn"))
        dus = jnp.concatenate(dus_parts, axis=1)
        dvn = jnp.concatenate(dvn_parts, axis=1)
        dlng_o[...] += jnp.sum(dvn * xhat, axis=0, keepdims=True)
        dlnb_o[...] += jnp.sum(dvn, axis=0, keepdims=True)
        dxh = dvn * lng
        dv = rstd * (dxh - jnp.mean(dxh, axis=-1, keepdims=True)
                     - xhat * jnp.mean(dxh * xhat, axis=-1, keepdims=True))
        dz_ref[:, 0:SSM_WIDTH] = dus_ref[...]
        dz_ref[:, SSM_WIDTH:2 * SSM_WIDTH] = (dus * _gelu_grad(zu)).astype(BF16)
        dz_ref[:, 2 * SSM_WIDTH:] = (dv * _gelu_grad(zv)).astype(BF16)

        @pl.when(i == nc - 1)
        def _():
            for h in range(SGU_HEADS):
                dw_o[h] = jnp.where(mask, dw_o[h], 0.0)
                db_o[h] = jnp.broadcast_to(jnp.sum(db_o[h], axis=1, keepdims=True), (hd, hd))

    row = lambda c: _bs((SGU_CHUNK, SSM_WIDTH), lambda i: (i, c))
    vec = _bs((1, SSM_WIDTH), lambda i: (0, 0))
    hmat = _bs((SGU_HEADS, hd, hd), lambda i: (0, 0, 0))
    return pl.pallas_call(
        body, name="sgu_bwd", grid=(nc,),
        in_specs=[row(0), row(0), row(1), row(2), vec, vec, hmat, hmat] + [_any_spec()] * len(deps),
        out_specs=[_bs((SGU_CHUNK, 3 * SSM_WIDTH), lambda i: (i, 0)), vec, vec, hmat, hmat],
        out_shape=[_sds((t, 3 * SSM_WIDTH), BF16), _sds((1, SSM_WIDTH), F32), _sds((1, SSM_WIDTH), F32),
                   _sds((SGU_HEADS, hd, hd), F32), _sds((SGU_HEADS, hd, hd), F32)],
        compiler_params=_params(("arbitrary",)),
    )(dy, du_ssm, z, z, lng, lnb, w, bexp, *deps)


def _place():
    x, y, c = (lax.axis_index(a) for a in MESH_AXES)
    return x, y, c


def _index(p):
    return 4 * p[0] + 2 * p[1] + p[2]


def _any_spec():
    return pl.BlockSpec(memory_space=pl.ANY)


def _col_block(ref, k, width):
    return ref.at[:, pl.ds(pl.multiple_of(k * width, LANES), width)]


def _all_gather(name, shards, by_columns=False):
    n = len(shards)

    def body(*refs):
        ins, outs = refs[:n], refs[n:2 * n]
        send, recv, loc = refs[2 * n:]
        x, y, c = _place()
        me, sib = (x, y, c), (x, y, 1 - c)
        chips = [(1 - x, y), (x, 1 - y), (1 - x, 1 - y)]

        def blk(w, p):
            if by_columns:
                return _col_block(outs[w], _index(p), shards[w].shape[1])
            return outs[w].at[_index(p)]

        def cp(w, k, block, to, src=None):
            dst = blk(w, block)
            return pltpu.make_async_remote_copy(
                src_ref=dst if src is None else src, dst_ref=dst,
                send_sem=send.at[w * 7 + k], recv_sem=recv.at[w * 7 + k],
                device_id=to, device_id_type=pl.DeviceIdType.MESH)

        mines, sends = [], []
        for w in range(n):
            m = pltpu.make_async_copy(ins[w], blk(w, me), loc.at[w])
            m.start()
            mines.append(m)
            first = [cp(w, 0, me, sib, src=ins[w])]
            first += [cp(w, 1 + j, me, (*chip, c), src=ins[w]) for j, chip in enumerate(chips)]
            for q in first:
                q.start()
            sends += first
        for j, chip in enumerate(chips):
            for w in range(n):
                cp(w, 1 + j, (*chip, c), me).wait_recv()
                q = cp(w, 4 + j, (*chip, c), sib)
                q.start()
                sends.append(q)
        for w in range(n):
            cp(w, 0, sib, me).wait_recv()
            for j, chip in enumerate(chips):
                cp(w, 4 + j, (*chip, 1 - c), me).wait_recv()
        for q in sends:
            q.wait_send()
        for m in mines:
            m.wait()

    return pl.pallas_call(
        body, name=name,
        in_specs=[_any_spec()] * n, out_specs=[_any_spec()] * n,
        out_shape=[_sds((s.shape[0], N_DEV * s.shape[1]) if by_columns else (N_DEV,) + s.shape, s.dtype)
                   for s in shards],
        scratch_shapes=[pltpu.SemaphoreType.DMA((n * 7,)), pltpu.SemaphoreType.DMA((n * 7,)),
                        pltpu.SemaphoreType.DMA((n,))],
        compiler_params=pltpu.CompilerParams(has_side_effects=True),
    )(*shards)


def _peer(r, x, y, c):
    return ((1 - x) if r & 4 else x, (1 - y) if r & 2 else y, (1 - c) if r & 1 else c)


def _sent_block(src_ref, land_ref, k, scatter):
    if not scatter:
        return src_ref
    if len(src_ref.shape) == len(land_ref.shape):
        return src_ref.at[k]
    return _col_block(src_ref, k, land_ref.shape[2])


def _send_start(name, src, land, scatter, after=None):
    n_after = 0 if after is None else 1

    def body(*refs):
        src_ref, land_ref = refs[0], refs[1]
        send, recv, _, _, token = refs[2 + n_after:]
        x, y, c = _place()
        me = _index((x, y, c))
        for r in range(1, N_DEV):
            p = _peer(r, x, y, c)
            pltpu.make_async_remote_copy(
                src_ref=_sent_block(src_ref, land_ref, _index(p), scatter), dst_ref=land_ref.at[me],
                send_sem=send.at[r - 1], recv_sem=recv.at[r - 1],
                device_id=p, device_id_type=pl.DeviceIdType.MESH).start()
        token[...] = jnp.zeros_like(token)

    hbm, sem = pl.BlockSpec(memory_space=pltpu.HBM), pl.BlockSpec(memory_space=pltpu.SEMAPHORE)
    return pl.pallas_call(
        body, name=name,
        out_shape=(pltpu.SemaphoreType.DMA((N_DEV - 1,)), pltpu.SemaphoreType.DMA((N_DEV - 1,)),
                   pltpu.HBM(src.shape, src.dtype), pltpu.HBM(land.shape, land.dtype),
                   _sds((SUBLANES, LANES), F32)),
        in_specs=(hbm, hbm) + (_any_spec(),) * n_after,
        out_specs=(sem, sem, hbm, hbm, pl.BlockSpec(memory_space=pltpu.VMEM)),
        input_output_aliases={0: 2, 1: 3},
        compiler_params=pltpu.CompilerParams(has_side_effects=pltpu.SideEffectType.DATAFLOW_SIDE_EFFECTING),
    )(pltpu.with_memory_space_constraint(src, pltpu.HBM), pltpu.with_memory_space_constraint(land, pltpu.HBM),
      *([] if after is None else [after]))


def _send_wait(name, started, after, scatter):
    send, recv, src_thru, land_thru, _ = started

    def body(src_ref, land_ref, send_r, recv_r, after_ref, src_out, land_out):
        x, y, c = _place()
        for r in range(1, N_DEV):
            p = _peer(r, x, y, c)
            k = _index(p)
            cp = pltpu.make_async_remote_copy(
                src_ref=_sent_block(src_ref, land_ref, k, scatter), dst_ref=land_ref.at[k],
                send_sem=send_r.at[r - 1], recv_sem=recv_r.at[r - 1],
                device_id=p, device_id_type=pl.DeviceIdType.MESH)
            cp.wait_send()
            cp.wait_recv()

    hbm, sem = pl.BlockSpec(memory_space=pltpu.HBM), pl.BlockSpec(memory_space=pltpu.SEMAPHORE)
    return pl.pallas_call(
        body, name=name,
        out_shape=(pltpu.HBM(src_thru.shape, src_thru.dtype), pltpu.HBM(land_thru.shape, land_thru.dtype)),
        in_specs=(hbm, hbm, sem, sem, _any_spec()), out_specs=(hbm, hbm),
        input_output_aliases={0: 0, 1: 1},
        compiler_params=pltpu.CompilerParams(has_side_effects=pltpu.SideEffectType.DATAFLOW_SIDE_EFFECTING),
    )(src_thru, land_thru, send, recv, after)


def _own_block(blocks, block, me):
    return lax.dynamic_update_index_in_dim(blocks, block, me, 0)


def _adamw(name, parts, w, m, v):
    rows, cols = w.shape
    br = min(ADAM_ROWS, rows)
    c1 = 1.0 / (1.0 - ADAM_B1 ** ADAM_STEP)
    c2 = 1.0 / (1.0 - ADAM_B2 ** ADAM_STEP)

    def body(p_ref, w_ref, m_ref, v_ref, g_o, d_o, m_o, v_o):
        g = p_ref[0].astype(F32)
        for k in range(1, N_DEV):
            g = g + p_ref[k].astype(F32)
        mn = ADAM_B1 * m_ref[...] + (1.0 - ADAM_B1) * g
        vn = ADAM_B2 * v_ref[...] + (1.0 - ADAM_B2) * (g * g)
        g_o[...] = g
        m_o[...] = mn
        v_o[...] = vn
        d_o[...] = -ADAM_LR * ((mn * c1) / (jnp.sqrt(vn * c2) + ADAM_EPS) + ADAM_WD * w_ref[...])

    blk = _bs((br, cols), lambda i: (i, 0))
    out = _sds((rows, cols), F32)
    return pl.pallas_call(
        body, name=name, grid=(rows // br,),
        in_specs=[_bs((N_DEV, br, cols), lambda i: (0, i, 0)), blk, blk, blk],
        out_specs=[blk] * 4, out_shape=[out] * 4,
        compiler_params=_params(("parallel",)),
    )(parts, w, m, v)


def _pack(arrs):
    tile = SUBLANES * LANES
    flat = []
    for a in arrs:
        f = a.reshape(-1).astype(F32)
        pad = (-f.shape[0]) % tile
        flat.append(jnp.pad(f, (0, pad)) if pad else f)
    total = sum(f.shape[0] for f in flat)
    tail = (-total) % (ADAM_ROWS * LANES)
    if tail:
        flat.append(jnp.zeros((tail,), F32))
    return jnp.concatenate(flat).reshape(-1, LANES)


def _unpack(buf, like):
    tile = SUBLANES * LANES
    flat = buf.reshape(-1)
    out, off = [], 0
    for a in like:
        n = math.prod(a.shape)
        out.append(flat[off:off + n].reshape(a.shape))
        off += n + ((-n) % tile)
    return out


def _to_block_b(b):
    gl = LANES // SSM_GROUP
    tb = b.reshape(N_SLAB, gl, SSM_STATE, SSM_GROUP).transpose(0, 1, 3, 2)
    eye = jnp.eye(gl, dtype=F32)
    return (tb[:, :, :, None, :] * eye[None, :, None, :, None]).reshape(N_SLAB, LANES, SLAB_STATE)


def _from_block_b(bx):
    gl = LANES // SSM_GROUP
    d = jnp.einsum("jghgp->jgph", bx.reshape(N_SLAB, gl, SSM_GROUP, gl, SSM_STATE))
    return d.reshape(N_SLAB * gl, SSM_STATE, SSM_GROUP)


def _to_block_c(cm):
    gl = LANES // SSM_GROUP
    tc = cm.reshape(N_SLAB, gl, SSM_GROUP, SSM_STATE).transpose(0, 1, 3, 2)
    eye = jnp.eye(gl, dtype=F32)
    return (tc[:, :, :, None, :] * eye[None, :, None, :, None]).reshape(N_SLAB, SLAB_STATE, LANES)


def _from_block_c(cx):
    gl = LANES // SSM_GROUP
    d = jnp.einsum("jgpgh->jghp", cx.reshape(N_SLAB, gl, SSM_STATE, gl, SSM_GROUP))
    return d.reshape(N_SLAB * gl, SSM_GROUP, SSM_STATE)


def kernel(x, norm_mix_g, w_in, ssm_a_re, ssm_a_im, ssm_b_re, ssm_b_im, ssm_c_re, ssm_c_im, ssm_d, ssm_log_dt, ssm_glu_w, ssm_glu_b, sgu_ln_g, sgu_ln_b, sgu_w, sgu_b, out_norm_ssm_g, out_norm_sgu_g, w_out, norm_mlp_g, w_up, w_down, norm_final_g, loss_target, m_norm_mix_g, m_w_in, m_ssm_a_re, m_ssm_a_im, m_ssm_b_re, m_ssm_b_im, m_ssm_c_re, m_ssm_c_im, m_ssm_d, m_ssm_log_dt, m_ssm_glu_w, m_ssm_glu_b, m_sgu_ln_g, m_sgu_ln_b, m_sgu_w, m_sgu_b, m_out_norm_ssm_g, m_out_norm_sgu_g, m_w_out, m_norm_mlp_g, m_w_up, m_w_down, m_norm_final_g, v_norm_mix_g, v_w_in, v_ssm_a_re, v_ssm_a_im, v_ssm_b_re, v_ssm_b_im, v_ssm_c_re, v_ssm_c_im, v_ssm_d, v_ssm_log_dt, v_ssm_glu_w, v_ssm_glu_b, v_sgu_ln_g, v_sgu_ln_b, v_sgu_w, v_sgu_b, v_out_norm_ssm_g, v_out_norm_sgu_g, v_w_out, v_norm_mlp_g, v_w_up, v_w_down, v_norm_final_g):
    given = dict(locals())
    names = ["norm_mix_g", "w_in", "ssm_a_re", "ssm_a_im", "ssm_b_re", "ssm_b_im", "ssm_c_re", "ssm_c_im",
             "ssm_d", "ssm_log_dt", "ssm_glu_w", "ssm_glu_b", "sgu_ln_g", "sgu_ln_b", "sgu_w", "sgu_b",
             "out_norm_ssm_g", "out_norm_sgu_g", "w_out", "norm_mlp_g", "w_up", "w_down", "norm_final_g"]
    big = ["w_in", "ssm_glu_w", "w_out", "w_up", "w_down"]
    small = [n for n in names if n not in big]

    d = D_MODEL
    t = x.shape[1]
    tb = min(1024, t)
    xs = x[0]
    target = loss_target[0]
    nsh_in = w_in.shape[2]
    nsh_up = w_up.shape[2]
    d_ff = nsh_up * N_DEV
    n_in = nsh_in * N_DEV

    me = _index(_place())
    (wg_in,) = _all_gather("gather_w_in", [w_in[0].astype(BF16)], by_columns=True)
    gathers = {}

    def start_gather(n, after=None):
        shard = given[n][0].astype(BF16)
        gathers[n] = _send_start("gather_start_" + n, shard, lax.empty((N_DEV,) + shard.shape, BF16), False, after)
        return gathers[n][4]

    tokens = [start_gather("ssm_glu_w", wg_in), start_gather("w_out", wg_in)]

    def gathered(n, after):
        shard, blocks = _send_wait("gather_wait_" + n, gathers[n], after, False)
        return _own_block(blocks, shard, me)

    nst = N_SLAB * SLAB_STATE
    are, aim = ssm_a_re.reshape(1, nst), ssm_a_im.reshape(1, nst)
    ldt = jnp.repeat(ssm_log_dt[0], SSM_STATE).reshape(1, nst)
    bxr, bxi = _to_block_b(ssm_b_re[0]), _to_block_b(ssm_b_im[0])
    cxr, cxi = _to_block_c(ssm_c_re[0]), _to_block_c(ssm_c_im[0])
    dvec = ssm_d.reshape(1, SSM_WIDTH)
    bre, bim, cre, cimn, apr, api, air, aii = _ssm_prep(are, aim, ldt, bxr, bxi, cxr, cxi)
    tabs = (bre, bim, cre, cimn, apr, api, air, aii)

    h1 = _rms_fwd("norm_mix", xs, norm_mix_g, deps=tokens)
    bn_i = n_in // 2
    (z,) = _mm("in_proj", "nn", (t // tb, n_in // bn_i, 1),
               (h1, _bs((tb, d), lambda i, j, k: (i, 0))),
               (wg_in, _bs((d, bn_i), lambda i, j, k: (0, j))),
               [(_sds((t, n_in), F32), _bs((tb, bn_i), lambda i, j, k: (i, j)))])
    tokens = [start_gather("w_up", z), start_gather("w_down", z)]
    y_pre, yg_b, p_re, p_im = _ssm_fwd(z, dvec, *tabs, deps=tokens)

    def glu_ep(acc, yp, b):
        gate = _sigmoid(acc + b)
        return _gelu(yp) * gate, gate

    hw = SSM_WIDTH // 2
    wg_glu = gathered("ssm_glu_w", yg_b).reshape(SSM_WIDTH, SSM_WIDTH)
    tile_g = _bs((tb, hw), lambda i, j, k: (i, j))
    y_ssm, gate = _mm("glu", "nn", (t // tb, 2, 1),
                      (yg_b, _bs((tb, SSM_WIDTH), lambda i, j, k: (i, 0))),
                      (wg_glu, _bs((SSM_WIDTH, hw), lambda i, j, k: (0, j))),
                      [(_sds((t, SSM_WIDTH), F32), tile_g), (_sds((t, SSM_WIDTH), F32), tile_g)],
                      extras=[(y_pre, tile_g), (ssm_glu_b, _bs((1, hw), lambda i, j, k: (0, j)))],
                      epilogue=glu_ep)

    sgu_bexp = jnp.broadcast_to(sgu_b[0][:, :, None], (SGU_HEADS, SGU_CHUNK, SGU_CHUNK))
    y_sgu = _sgu_fwd(z, sgu_ln_g, sgu_ln_b, sgu_w[0], sgu_bexp)
    mixed = _mix_norm(y_ssm, y_sgu, out_norm_ssm_g, out_norm_sgu_g)

    tb2 = min(512, t)
    row2 = _bs((tb2, d), lambda i, j, k: (i, 0))
    vec2 = _bs((1, d), lambda i, j, k: (0, 0))
    vec_sum = (_sds((1, d), F32), vec2)
    wg_out = gathered("w_out", mixed).reshape(d, d)

    def out_ep(acc, r, g):
        x2v = acc + r
        return x2v, _rms_math(x2v, g)

    x2, h2 = _mm("out_proj", "nn", (t // tb2, 1, 1),
                 (mixed, row2), (wg_out, _bs((d, d), lambda i, j, k: (0, 0))),
                 [(_sds((t, d), F32), row2), (_sds((t, d), BF16), row2)],
                 extras=[(xs, row2), (norm_mlp_g, vec2)], epilogue=out_ep, ep_rows=EP_ROWS)

    def up_ep(acc):
        r = jnp.maximum(acc, 0.0)
        return r * r, r

    tile_f = _bs((tb, nsh_up), lambda i, j, k: (i, j))
    wg_up = gathered("w_up", h2)
    f_act, r_act = _mm("mlp_up", "nn", (t // tb, N_DEV, 1),
                       (h2, _bs((tb, d), lambda i, j, k: (i, 0))),
                       (wg_up, _bs((None, d, nsh_up), lambda i, j, k: (j, 0, 0))),
                       [(_sds((t, d_ff), BF16), tile_f), (_sds((t, d_ff), BF16), tile_f)],
                       epilogue=up_ep)
    bk_d, bn_o = 2048, 1024
    tile_o = _bs((tb, bn_o), lambda i, j, k: (i, j))
    wg_down = gathered("w_down", f_act).reshape(d_ff, d)
    (x3,) = _mm("mlp_down", "nn", (t // tb, d // bn_o, d_ff // bk_d),
                (f_act, _bs((tb, bk_d), lambda i, j, k: (i, k))),
                (wg_down, _bs((bk_d, bn_o), lambda i, j, k: (k, j))),
                [(_sds((t, d), F32), tile_o)],
                extras=[(x2, tile_o)], epilogue=lambda acc, r: (acc + r,))
    dx3, dx3_b, g_final, err2 = _final_loss(x3, target, norm_final_g.reshape(1, d))
    loss = lax.psum(0.5 * jnp.sum(err2) / d, MESH_AXES)

    sends = {}

    def send_grad(n, g, land_shape=None):
        sends[n] = _send_start("grad_start_" + n, g, lax.empty(land_shape or g.shape, BF16), True)
        return [sends[n][4]]

    bn_a = 1024
    tile_a = _bs((tb, bn_a), lambda i, j, k: (i, j))
    (da,) = _mm("mlp_down_dx", "nt", (t // tb, d_ff // bn_a, 1),
                (dx3_b, _bs((tb, d), lambda i, j, k: (i, 0))),
                (wg_down, _bs((bn_a, d), lambda i, j, k: (j, 0))),
                [(_sds((t, d_ff), BF16), tile_a)],
                extras=[(r_act, tile_a)], epilogue=lambda acc, r: (acc * (2.0 * r.astype(F32)),))
    sq = 1024
    (gw_down,) = _mm("mlp_down_dw", "tn", (d_ff // sq, 1, t // tb),
                     (f_act, _bs((tb, sq), lambda i, j, k: (k, i))),
                     (dx3_b, _bs((tb, d), lambda i, j, k: (k, 0))),
                     [(_sds((d_ff, d), BF16), _bs((sq, d), lambda i, j, k: (i, 0)))])
    sent = send_grad("w_down", gw_down.reshape(N_DEV, -1, d))
    (dh2,) = _mm("mlp_up_dx", "nt", (t // tb, 1, N_DEV),
                 (da, _bs((tb, nsh_up), lambda i, j, k: (i, k))),
                 (wg_up, _bs((None, d, nsh_up), lambda i, j, k: (k, 0, 0))),
                 [(_sds((t, d), F32), _bs((tb, d), lambda i, j, k: (i, 0)))], deps=sent)

    def norm_bwd_side(dh, xv, dres, g):
        dx, dg = _rms_bwd_math(dh, xv, g)
        dx = dx + dres
        return dx, dx, dg

    up_dw_grid = (1, N_DEV, t // tb)
    gw_up, dx2, dx2_b, g_norm_mlp = _mm(
        "mlp_up_dw", "tn", up_dw_grid,
        (h2, _bs((tb, d), lambda i, j, k: (k, 0))),
        (da, _bs((tb, nsh_up), lambda i, j, k: (k, j))),
        [(_sds((N_DEV, d, nsh_up), BF16), _bs((None, d, nsh_up), lambda i, j, k: (j, 0, 0)))],
        side=(norm_bwd_side, [(dh2, d), (x2, d), (dx3, d), (norm_mlp_g, d)],
              [(_sds((t, d), F32), d), (_sds((t, d), BF16), d)], [(_sds((1, d), F32), d)],
              t // math.prod(up_dw_grid)))
    sent = send_grad("w_up", gw_up)

    tk = min(2048, t)
    (gw_out,) = _mm("out_proj_dw", "tn", (d // sq, d // sq, t // tk),
                    (mixed, _bs((tk, sq), lambda i, j, k: (k, i))),
                    (dx2_b, _bs((tk, sq), lambda i, j, k: (k, j))),
                    [(_sds((d, d), BF16), _bs((sq, sq), lambda i, j, k: (i, j)))], deps=sent)
    sent = send_grad("w_out", gw_out.reshape(N_DEV, -1, d))
    half2 = _bs((tb2, SSM_WIDTH), lambda i, j, k: (i, 0))
    vech = _bs((1, SSM_WIDTH), lambda i, j, k: (0, 0))
    half_sum = (_sds((1, SSM_WIDTH), F32), vech)

    def out_dx_ep(acc, ya, yb, ga, gb, yp, gt):
        dya, dga = _rms_bwd_math(acc[:, :SSM_WIDTH], ya, ga)
        dyb, dgb = _rms_bwd_math(acc[:, SSM_WIDTH:], yb, gb)
        dpre = dya * _gelu(yp) * gt * (1.0 - gt)
        return dya, dyb, dpre, dga, dgb, jnp.sum(dpre, axis=0, keepdims=True)

    dy_ssm, dy_sgu, dpre_b, g_onorm_ssm, g_onorm_sgu, g_glu_b = _mm(
        "out_proj_dx", "nt", (t // tb2, 1, 1),
        (dx2_b, row2), (wg_out, _bs((d, d), lambda i, j, k: (0, 0))),
        [(_sds((t, SSM_WIDTH), F32), half2), (_sds((t, SSM_WIDTH), F32), half2),
         (_sds((t, SSM_WIDTH), BF16), half2)],
        extras=[(y_ssm, half2), (y_sgu, half2), (out_norm_ssm_g, vech), (out_norm_sgu_g, vech),
                (y_pre, half2), (gate, half2)],
        epilogue=out_dx_ep, sums=[half_sum, half_sum, half_sum], deps=sent, acc_shape=(tb2, d),
        ep_rows=EP_ROWS)

    (gw_glu,) = _mm("glu_dw", "tn", (1, 1, t // tb),
                    (yg_b, _bs((tb, SSM_WIDTH), lambda i, j, k: (k, 0))),
                    (dpre_b, _bs((tb, SSM_WIDTH), lambda i, j, k: (k, 0))),
                    [(_sds((SSM_WIDTH, SSM_WIDTH), BF16), _bs((SSM_WIDTH, SSM_WIDTH), lambda i, j, k: (0, 0)))])
    sent = send_grad("ssm_glu_w", gw_glu.reshape(N_DEV, -1, SSM_WIDTH))
    (dy_pre,) = _mm("glu_dx", "nt", (t // tb, 2, 1),
                    (dpre_b, _bs((tb, SSM_WIDTH), lambda i, j, k: (i, 0))),
                    (wg_glu, _bs((hw, SSM_WIDTH), lambda i, j, k: (j, 0))),
                    [(_sds((t, SSM_WIDTH), F32), tile_g)],
                    extras=[(dy_ssm, tile_g), (gate, tile_g), (y_pre, tile_g)],
                    epilogue=lambda acc, dy, gt, yp: ((dy * gt + acc) * _gelu_grad(yp),), deps=sent)
    du_b, dbre, dbim, dcre, dcimn, q_re, q_im, dd = _ssm_bwd(dy_pre, z, p_re, p_im, dvec, *tabs)
    dare, daim, dldt, dbxr, dbxi = _ssm_prep_bwd(are, aim, ldt, bxr, bxi, dbre, dbim, q_re, q_im)

    dz_b, g_ln_g, g_ln_b, g_sgu_w, g_sgu_bx = _sgu_bwd(dy_sgu, du_b, z, sgu_ln_g, sgu_ln_b, sgu_w[0], sgu_bexp)
    local_small = {
        "ssm_a_re": dare, "ssm_a_im": daim,
        "ssm_b_re": _from_block_b(dbxr), "ssm_b_im": _from_block_b(dbxi),
        "ssm_c_re": _from_block_c(dcre), "ssm_c_im": -_from_block_c(dcimn),
        "ssm_d": dd, "ssm_log_dt": dldt.reshape(-1, SSM_STATE).sum(axis=-1),
        "ssm_glu_b": g_glu_b, "sgu_ln_g": g_ln_g, "sgu_ln_b": g_ln_b, "sgu_w": g_sgu_w,
        "sgu_b": g_sgu_bx[:, :, 0], "out_norm_ssm_g": g_onorm_ssm, "out_norm_sgu_g": g_onorm_sgu,
        "norm_mlp_g": g_norm_mlp, "norm_final_g": g_final,
    }
    small_early = [n for n in small if n in local_small]
    small_late = [n for n in small if n not in local_small]
    packed = _pack([local_small[n] for n in small_early])
    small_send = _send_start("small_start", packed, lax.empty((N_DEV,) + packed.shape, F32), False)
    (gw_in,) = _mm("in_proj_dw", "tn", (d // sq, n_in // bn_i, t // tb),
                   (h1, _bs((tb, sq), lambda i, j, k: (k, i))),
                   (dz_b, _bs((tb, bn_i), lambda i, j, k: (k, j))),
                   [(_sds((d, n_in), BF16), _bs((sq, bn_i), lambda i, j, k: (i, j)))],
                   deps=[small_send[4]])
    sent = send_grad("w_in", gw_in, (N_DEV, d, nsh_in))

    def in_dx_ep(acc, xv, dres, g):
        dx, dg = _rms_bwd_math(acc, xv, g)
        return dx + dres, dg

    grad_x, g_norm_mix = _mm("in_proj_dx", "nt", (t // tb2, 1, n_in // bn_i),
                             (dz_b, _bs((tb2, bn_i), lambda i, j, k: (i, k))),
                             (wg_in, _bs((d, bn_i), lambda i, j, k: (0, k))),
                             [(_sds((t, d), F32), row2)],
                             extras=[(xs, row2), (dx2, row2), (norm_mix_g, vec2)],
                             epilogue=in_dx_ep, sums=[vec_sum], deps=sent, ep_rows=EP_ROWS)

    (late_parts,) = _all_gather("gather_late_grads", [_pack([g_norm_mix])])
    packed, early_parts = _send_wait("small_wait", small_send, grad_x, False)
    early_parts = _own_block(early_parts, packed, me)

    grads, deltas, new_m, new_v = {}, {}, {}, {}
    for n in big:
        w2 = given[n][0]
        sent_blocks, landed = _send_wait("grad_wait_" + n, sends[n], grad_x, True)
        if sent_blocks.ndim == landed.ndim:
            own = lax.dynamic_index_in_dim(sent_blocks, me, 0, keepdims=False)
        else:
            own = lax.dynamic_slice_in_dim(sent_blocks, me * landed.shape[2], landed.shape[2], axis=1)
        parts = _own_block(landed, own, me)
        res = _adamw("adamw_" + n, parts, w2, given["m_" + n][0], given["v_" + n][0])
        grads[n], deltas[n], new_m[n], new_v[n] = [r.reshape(given[n].shape) for r in res]
    for tag, group, parts in (("early", small_early, early_parts), ("late", small_late, late_parts)):
        like = [given[n] for n in group]
        res = _adamw("adamw_small_" + tag, parts, _pack(like), _pack([given["m_" + n] for n in group]),
                     _pack([given["v_" + n] for n in group]))
        for store, buf in zip((grads, deltas, new_m, new_v), res):
            for n, a in zip(group, _unpack(buf, like)):
                store[n] = a

    return (loss, grad_x.reshape(x.shape), *[grads[n] for n in names], *[deltas[n] for n in names],
            *[new_m[n] for n in names], *[new_v[n] for n in names])
```

```python
import functools
import math

import jax
import jax.numpy as jnp
from jax import lax
from jax.experimental import pallas as pl
from jax.experimental.pallas import tpu as pltpu

F32, BF16 = jnp.float32, jnp.bfloat16
EPS = 1e-6
N_DEV = 8
D_MODEL = 2048
SSM_WIDTH = 1024
SSM_GROUP = 16
SSM_STATE = 64
SGU_HEADS = 8
SGU_CHUNK = 128
LANES = 128
SUBLANES = 8
N_SLAB = SSM_WIDTH // LANES
SLAB_STATE = (LANES // SSM_GROUP) * SSM_STATE
SCAN_BLOCK = 256
SLABS_PER_STEP = 8
VMEM_LIMIT = 56 * 1024 * 1024
ROW_BLOCK = 256
EP_ROWS = 128
ADAM_ROWS = 128
MESH_AXES = ("x", "y", "c")

ADAM_LR, ADAM_B1, ADAM_B2, ADAM_EPS, ADAM_WD, ADAM_STEP = 0.001, 0.9, 0.999, 1e-08, 0.01, 10

_GELU_C0 = math.sqrt(2.0 / math.pi)
_GELU_C1 = 0.044715


def _gelu(v):
    return 0.5 * v * (1.0 + jnp.tanh(_GELU_C0 * (v + _GELU_C1 * v * v * v)))


def _gelu_grad(v):
    th = jnp.tanh(_GELU_C0 * (v + _GELU_C1 * v * v * v))
    return 0.5 * (1.0 + th) + 0.5 * v * (1.0 - th * th) * _GELU_C0 * (1.0 + 3.0 * _GELU_C1 * v * v)


def _sigmoid(v):
    return 1.0 / (1.0 + jnp.exp(-v))


def _params(sem=None):
    return pltpu.CompilerParams(dimension_semantics=sem, vmem_limit_bytes=VMEM_LIMIT)


def _dot(a, b, mode="nn"):
    dims = {"nn": ((1,), (0,)), "nt": ((1,), (1,)), "tn": ((0,), (0,))}[mode]
    return lax.dot_general(a, b, (dims, ((), ())), preferred_element_type=F32)


def _mm(name, mode, grid, a, b, outs, extras=(), epilogue=None, deps=(), sums=(), acc_shape=None,
        ep_rows=None, side=None):
    nk = grid[2]
    n_ex, n_out, n_dep, n_sum = len(extras), len(outs), len(deps), len(sums)
    assert not sums or grid[1] == 1
    if acc_shape is None:
        acc_shape = tuple(d for d in outs[0][1].block_shape if d is not None)
    side_fn, side_ins, side_outs, side_sums = side or (None, (), (), ())
    s_in, s_out, s_sum = len(side_ins), len(side_outs), len(side_sums)

    def body(*refs):
        a_ref, b_ref = refs[0], refs[1]
        ex = refs[2:2 + n_ex]
        pos = 2 + n_ex
        side_in = refs[pos:pos + s_in]
        pos += s_in + n_dep
        out_refs = refs[pos:pos + n_out]
        sum_refs = refs[pos + n_out:pos + n_out + n_sum]
        pos += n_out + n_sum
        side_out = refs[pos:pos + s_out]
        side_sum = refs[pos + s_out:pos + s_out + s_sum]
        acc = refs[-1]
        k = pl.program_id(2)

        @pl.when(k == 0)
        def _():
            acc[...] = jnp.zeros_like(acc)

        if side_sum:
            @pl.when((pl.program_id(0) == 0) & (pl.program_id(1) == 0) & (k == 0))
            def _():
                for o in side_sum:
                    o[...] = jnp.zeros_like(o)

        acc[...] += _dot(a_ref[...], b_ref[...], mode)
        if side_fn is not None:
            res = side_fn(*[r[...] for r in side_in])
            for o, r in zip(side_out, res[:s_out]):
                o[...] = r.astype(o.dtype)
            for o, r in zip(side_sum, res[s_out:]):
                o[...] += r

        def finish(rows):
            args = [e[rows, :] if e.shape[0] == acc_shape[0] else e[...] for e in ex]
            res = acc[rows, :]
            res = (res,) if epilogue is None else epilogue(res, *args)
            for o, r in zip(out_refs, res[:n_out]):
                o[rows, :] = r.astype(o.dtype)
            return tuple(res[n_out:])

        @pl.when(k == nk - 1)
        def _():
            if ep_rows is None:
                terms = finish(slice(None))
            else:
                def chunk(c, tot):
                    rows = pl.ds(pl.multiple_of(c * ep_rows, ep_rows), ep_rows)
                    return tuple(s + r for s, r in zip(tot, finish(rows)))

                zero = tuple(jnp.zeros(o.shape, F32) for o in sum_refs)
                terms = lax.fori_loop(0, acc_shape[0] // ep_rows, chunk, zero)
            for o, r in zip(sum_refs, terms):
                _add_up(o, r, pl.program_id(0) == 0)

    def side_spec(block):
        nd = len(block)
        if nd == 2 and block[0] == 1:
            return _bs(block, lambda i, j, k: (0, 0))
        return _bs(block, lambda i, j, k: (0,) * (nd - 2) + ((i * grid[1] + j) * grid[2] + k, 0))

    sem = ("arbitrary",) * 3 if sums or side else ("parallel", "parallel", "arbitrary")
    res = pl.pallas_call(
        body, name=name, grid=grid,
        in_specs=[a[1], b[1]] + [e[1] for e in extras] + [side_spec(blk) for _, blk in side_ins]
        + [_any_spec()] * n_dep,
        out_specs=[o[1] for o in outs] + [o[1] for o in sums]
        + [side_spec(blk) for _, blk in side_outs] + [side_spec(blk) for _, blk in side_sums],
        out_shape=[o[0] for o in outs] + [o[0] for o in sums]
        + [o[0] for o in side_outs] + [o[0] for o in side_sums],
        scratch_shapes=[pltpu.VMEM(acc_shape, F32)],
        compiler_params=_params(sem),
    )(a[0], b[0], *[e[0] for e in extras], *[x for x, _ in side_ins], *deps)
    return res


def _add_up(ref, term, first):
    @pl.when(first)
    def _():
        ref[...] = term

    @pl.when(jnp.logical_not(first))
    def _():
        ref[...] += term


def _rms_math(xv, g):
    return xv * lax.rsqrt(jnp.mean(xv * xv, axis=-1, keepdims=True) + EPS) * g


def _rms_bwd_math(dy, xv, g):
    r = lax.rsqrt(jnp.mean(xv * xv, axis=-1, keepdims=True) + EPS)
    xhat = xv * r
    dxhat = dy * g
    dx = r * (dxhat - xhat * jnp.mean(dxhat * xhat, axis=-1, keepdims=True))
    return dx, jnp.sum(dy * xhat, axis=0, keepdims=True)


def _sds(shape, dtype):
    return jax.ShapeDtypeStruct(shape, dtype)


def _bs(shape, fn):
    return pl.BlockSpec(shape, fn)


def _rms_fwd(name, x, g, deps=()):
    t, w = x.shape
    br = min(ROW_BLOCK, t)

    def body(*refs):
        x_ref, g_ref, o_ref = refs[0], refs[1], refs[-1]
        xv = x_ref[...]
        r = lax.rsqrt(jnp.mean(xv * xv, axis=-1, keepdims=True) + EPS)
        o_ref[...] = (xv * r * g_ref[...]).astype(BF16)

    return pl.pallas_call(
        body, name=name, grid=(t // br,),
        in_specs=[_bs((br, w), lambda i: (i, 0)), _bs((1, w), lambda i: (0, 0))] + [_any_spec()] * len(deps),
        out_specs=_bs((br, w), lambda i: (i, 0)),
        out_shape=_sds((t, w), BF16),
        compiler_params=_params(("parallel",)),
    )(x, g, *deps)


def _mix_norm(ya, yb, ga, gb):
    t, w = ya.shape
    br = min(ROW_BLOCK, t)

    def body(a_ref, b_ref, ga_ref, gb_ref, o_ref):
        for src, g_ref, col in ((a_ref, ga_ref, 0), (b_ref, gb_ref, w)):
            v = src[...]
            r = lax.rsqrt(jnp.mean(v * v, axis=-1, keepdims=True) + EPS)
            o_ref[:, col:col + w] = (v * r * g_ref[...]).astype(BF16)

    row = _bs((br, w), lambda i: (i, 0))
    vec = _bs((1, w), lambda i: (0, 0))
    return pl.pallas_call(
        body, name="mix_norm", grid=(t // br,),
        in_specs=[row, row, vec, vec],
        out_specs=_bs((br, 2 * w), lambda i: (i, 0)),
        out_shape=_sds((t, 2 * w), BF16),
        compiler_params=_params(("parallel",)),
    )(ya, yb, ga, gb)


def _loss_math(xv, target, g):
    r = lax.rsqrt(jnp.mean(xv * xv, axis=-1, keepdims=True) + EPS)
    xhat = xv * r
    err = xhat * g - target
    dy = err * (1.0 / xv.shape[-1])
    dxhat = dy * g
    dx = r * (dxhat - xhat * jnp.mean(dxhat * xhat, axis=-1, keepdims=True))
    return dx, jnp.sum(dy * xhat, axis=0, keepdims=True), jnp.sum(err * err, axis=0, keepdims=True)


def _final_loss(x3, target, g):
    t, w = x3.shape
    br = min(ROW_BLOCK, t)

    def body(x_ref, tg_ref, g_ref, dx_ref, dxb_ref, dg_ref, l_ref):
        dx, dg, e2 = _loss_math(x_ref[...], tg_ref[...], g_ref[...])
        dx_ref[...] = dx
        dxb_ref[...] = dx.astype(BF16)
        _add_up(dg_ref, dg, pl.program_id(0) == 0)
        _add_up(l_ref, e2, pl.program_id(0) == 0)

    row = _bs((br, w), lambda i: (i, 0))
    vec = _bs((1, w), lambda i: (0, 0))
    return pl.pallas_call(
        body, name="final_loss", grid=(t // br,),
        in_specs=[row, row, vec], out_specs=[row, row, vec, vec],
        out_shape=[_sds((t, w), F32), _sds((t, w), BF16), _sds((1, w), F32), _sds((1, w), F32)],
        compiler_params=_params(("arbitrary",)),
    )(x3, target, g)


def _prep_math(are, aim, ldt, bxr, bxi):
    dt = jnp.exp(ldt)
    er = jnp.exp(are * dt)
    th = aim * dt
    abr, abi = er * jnp.cos(th), er * jnp.sin(th)
    nr, ni = abr - 1.0, abi
    den = are * are + aim * aim
    cr = (nr * are + ni * aim) / den
    ci = (ni * are - nr * aim) / den
    bbr, bbi = [], []
    for j in range(N_SLAB):
        sl = slice(j * SLAB_STATE, (j + 1) * SLAB_STATE)
        bbr.append(cr[:, sl] * bxr[j] - ci[:, sl] * bxi[j])
        bbi.append(cr[:, sl] * bxi[j] + ci[:, sl] * bxr[j])
    return abr, abi, bbr, bbi


def _ssm_prep(are, aim, ldt, bxr, bxi, cxr, cxi):
    nst = N_SLAB * SLAB_STATE

    def body(are_r, aim_r, ldt_r, bxr_r, bxi_r, cxr_r, cxi_r,
             bre_o, bim_o, cre_o, cimn_o, apr_o, api_o, air_o, aii_o):
        abr, abi, bbr, bbi = _prep_math(are_r[...], aim_r[...], ldt_r[...],
                                        [bxr_r[j] for j in range(N_SLAB)], [bxi_r[j] for j in range(N_SLAB)])
        for j in range(N_SLAB):
            bre_o[j] = bbr[j].astype(BF16)
            bim_o[j] = bbi[j].astype(BF16)
        cre_o[...] = cxr_r[...].astype(BF16)
        cimn_o[...] = (-cxi_r[...]).astype(BF16)

        def step(k, cur):
            cr, ci = cur
            den = cr * cr + ci * ci
            apr_o[pl.ds(k, 1), :] = cr
            api_o[pl.ds(k, 1), :] = ci
            air_o[pl.ds(k, 1), :] = cr / den
            aii_o[pl.ds(k, 1), :] = -ci / den
            return cr * abr - ci * abi, cr * abi + ci * abr

        lax.fori_loop(0, SCAN_BLOCK, step, (jnp.ones((1, nst), F32), jnp.zeros((1, nst), F32)))

    tab = _sds((SCAN_BLOCK, nst), F32)
    return pl.pallas_call(
        body, name="ssm_prep",
        out_shape=[_sds(bxr.shape, BF16), _sds(bxr.shape, BF16), _sds(cxr.shape, BF16), _sds(cxr.shape, BF16),
                   tab, tab, tab, tab],
        compiler_params=_params(),
    )(are, aim, ldt, bxr, bxi, cxr, cxi)


def _ssm_prep_bwd(are, aim, ldt, bxr, bxi, dbre, dbim, qr, qi):
    def body(are_r, aim_r, ldt_r, bxr_r, bxi_r, dbre_r, dbim_r, qr_r, qi_r,
             dare_o, daim_o, dldt_o, dbxr_o, dbxi_o):
        prim = (are_r[...], aim_r[...], ldt_r[...],
                [bxr_r[j] for j in range(N_SLAB)], [bxi_r[j] for j in range(N_SLAB)])
        (abr, abi, _, _), vjp = jax.vjp(_prep_math, *prim)
        den = abr * abr + abi * abi
        q_r, q_i = qr_r[...], qi_r[...]
        gar = (q_r * abr - q_i * abi) / den
        gai = (q_r * abi + q_i * abr) / den
        ct = (gar, gai, [dbre_r[j] for j in range(N_SLAB)], [dbim_r[j] for j in range(N_SLAB)])
        dare, daim, dldt, dbxr, dbxi = vjp(ct)
        dare_o[...] = dare
        daim_o[...] = daim
        dldt_o[...] = dldt
        for j in range(N_SLAB):
            dbxr_o[j] = dbxr[j]
            dbxi_o[j] = dbxi[j]

    row = _sds(are.shape, F32)
    return pl.pallas_call(
        body, name="ssm_prep_bwd",
        out_shape=[row, row, row, _sds(bxr.shape, F32), _sds(bxr.shape, F32)],
        compiler_params=_params(),
    )(are, aim, ldt, bxr, bxi, dbre, dbim, qr, qi)


def _tri(lower):
    r = lax.broadcasted_iota(jnp.int32, (SCAN_BLOCK, SCAN_BLOCK), 0)
    c = lax.broadcasted_iota(jnp.int32, (SCAN_BLOCK, SCAN_BLOCK), 1)
    return jnp.where((r >= c) if lower else (r <= c), 1.0, 0.0).astype(BF16)


def _cumsum_mxu(tri, v):
    return _dot(tri, v.astype(BF16))


def _ssm_specs(t, sps):
    nt = t // SCAN_BLOCK
    tab = _bs((SCAN_BLOCK, sps * SLAB_STATE), lambda j, i: (0, j))
    bmat = _bs((sps, LANES, SLAB_STATE), lambda j, i: (j, 0, 0))
    cmat = _bs((sps, SLAB_STATE, LANES), lambda j, i: (j, 0, 0))
    return nt, tab, bmat, cmat


def _slab_slices(s):
    return slice(s * LANES, (s + 1) * LANES), slice(s * SLAB_STATE, (s + 1) * SLAB_STATE)


def _ssm_fwd(z, dvec, bre, bim, cre, cimn, apr, api, air, aii, deps=()):
    t = z.shape[0]
    sps = SLABS_PER_STEP
    nt, tab, bmat, cmat = _ssm_specs(t, sps)
    nst = N_SLAB * SLAB_STATE
    last = SCAN_BLOCK - 1

    def body(*refs):
        u_ref, d_ref, bre_r, bim_r, cre_r, cimn_r, apr_r, api_r, air_r, aii_r = refs[:10]
        y_ref, yg_ref, pr_ref, pi_ref, car_r, car_i = refs[10 + len(deps):]

        @pl.when(pl.program_id(1) == 0)
        def _():
            car_r[...] = jnp.zeros_like(car_r)
            car_i[...] = jnp.zeros_like(car_i)

        tri = _tri(True)
        for s in range(sps):
            ul, sl = _slab_slices(s)
            u = u_ref[:, ul]
            ub = u.astype(BF16)
            bur, bui = _dot(ub, bre_r[s]), _dot(ub, bim_r[s])
            ir, ii = air_r[:, sl], aii_r[:, sl]
            csr = _cumsum_mxu(tri, ir * bur - ii * bui)
            csi = _cumsum_mxu(tri, ir * bui + ii * bur)
            pr, pi = apr_r[:, sl], api_r[:, sl]
            a_r, a_i = apr_r[1:2, sl], api_r[1:2, sl]
            c_r, c_i = car_r[:, sl], car_i[:, sl]
            wr = csr + (a_r * c_r - a_i * c_i)
            wi = csi + (a_r * c_i + a_i * c_r)
            sr = pr * wr - pi * wi
            si = pr * wi + pi * wr
            car_r[:, sl] = sr[last:last + 1, :]
            car_i[:, sl] = si[last:last + 1, :]
            pr_ref[:, sl] = (sr - bur).astype(BF16)
            pi_ref[:, sl] = (si - bui).astype(BF16)
            y = _dot(sr.astype(BF16), cre_r[s]) + _dot(si.astype(BF16), cimn_r[s]) + d_ref[:, ul] * u
            y_ref[:, ul] = y
            yg_ref[:, ul] = _gelu(y).astype(BF16)

    ublk = _bs((SCAN_BLOCK, sps * LANES), lambda j, i: (i, j))
    sblk = _bs((SCAN_BLOCK, sps * SLAB_STATE), lambda j, i: (i, j))
    return pl.pallas_call(
        body, name="ssm_fwd", grid=(N_SLAB // sps, nt),
        in_specs=[ublk, _bs((1, sps * LANES), lambda j, i: (0, j)), bmat, bmat, cmat, cmat, tab, tab, tab, tab]
        + [_any_spec()] * len(deps),
        out_specs=[ublk, ublk, sblk, sblk],
        out_shape=[_sds((t, SSM_WIDTH), F32), _sds((t, SSM_WIDTH), BF16),
                   _sds((t, nst), BF16), _sds((t, nst), BF16)],
        scratch_shapes=[pltpu.VMEM((1, sps * SLAB_STATE), F32), pltpu.VMEM((1, sps * SLAB_STATE), F32)],
        compiler_params=_params(("parallel", "arbitrary")),
    )(z, dvec, bre, bim, cre, cimn, apr, api, air, aii, *deps)


def _ssm_bwd(gy, z, p_re, p_im, dvec, bre, bim, cre, cimn, apr, api, air, aii):
    t = z.shape[0]
    sps = SLABS_PER_STEP
    nt, tab, bmat, cmat = _ssm_specs(t, sps)
    last = SCAN_BLOCK - 1

    def fold(v):
        return v.reshape(SCAN_BLOCK // SUBLANES, SUBLANES, v.shape[-1]).sum(axis=0)

    def body(g_ref, u_ref, pr_ref, pi_ref, d_ref, bre_r, bim_r, cre_r, cimn_r, apr_r, api_r, air_r, aii_r,
             du_ref, dbre_o, dbim_o, dcre_o, dcimn_o, qr_o, qi_o, dd_o, car_r, car_i, qacc_r, qacc_i, dacc):
        i = pl.program_id(1)

        @pl.when(i == 0)
        def _():
            for ref in (car_r, car_i, qacc_r, qacc_i, dacc, dbre_o, dbim_o, dcre_o, dcimn_o):
                ref[...] = jnp.zeros_like(ref)

        tri = _tri(False)
        for s in range(sps):
            ul, sl = _slab_slices(s)
            g = g_ref[:, ul]
            gb = g.astype(BF16)
            u = u_ref[:, ul]
            ub = u.astype(BF16)
            bur, bui = _dot(ub, bre_r[s]), _dot(ub, bim_r[s])
            p_r, p_i = pr_ref[:, sl].astype(F32), pi_ref[:, sl].astype(F32)
            srb, sib = (p_r + bur).astype(BF16), (p_i + bui).astype(BF16)
            dcre_o[s] += _dot(srb, gb, "tn")
            dcimn_o[s] += _dot(sib, gb, "tn")
            dsr, dsi = _dot(gb, cre_r[s], "nt"), _dot(gb, cimn_r[s], "nt")
            pr, pi = apr_r[:, sl], api_r[:, sl]
            csr = _cumsum_mxu(tri, pr * dsr + pi * dsi)
            csi = _cumsum_mxu(tri, pr * dsi - pi * dsr)
            al_r, al_i = apr_r[last:last + 1, sl], api_r[last:last + 1, sl]
            c_r, c_i = car_r[:, sl], car_i[:, sl]
            wr = csr + (al_r * c_r + al_i * c_i)
            wi = csi + (al_r * c_i - al_i * c_r)
            ir, ii = air_r[:, sl], aii_r[:, sl]
            lr = ir * wr + ii * wi
            li = ir * wi - ii * wr
            a_r, a_i = apr_r[1:2, sl], api_r[1:2, sl]
            car_r[:, sl] = a_r * lr[0:1, :] + a_i * li[0:1, :]
            car_i[:, sl] = a_r * li[0:1, :] - a_i * lr[0:1, :]
            lrb, lib = lr.astype(BF16), li.astype(BF16)
            dbre_o[s] += _dot(ub, lrb, "tn")
            dbim_o[s] += _dot(ub, lib, "tn")
            du = d_ref[:, ul] * g + _dot(lrb, bre_r[s], "nt") + _dot(lib, bim_r[s], "nt")
            du_ref[:, ul] = du.astype(BF16)
            qacc_r[:, sl] += fold(lr * p_r + li * p_i)
            qacc_i[:, sl] += fold(li * p_r - lr * p_i)
            dacc[:, ul] += fold(g * u)

        @pl.when(i == nt - 1)
        def _():
            qr_o[...] = jnp.sum(qacc_r[...], axis=0, keepdims=True)
            qi_o[...] = jnp.sum(qacc_i[...], axis=0, keepdims=True)
            dd_o[...] = jnp.sum(dacc[...], axis=0, keepdims=True)

    rev = lambda j, i: (nt - 1 - i, j)
    ublk = _bs((SCAN_BLOCK, sps * LANES), rev)
    sblk = _bs((SCAN_BLOCK, sps * SLAB_STATE), rev)
    qrow = _bs((1, sps * SLAB_STATE), lambda j, i: (0, j))
    urow = _bs((1, sps * LANES), lambda j, i: (0, j))
    nst = N_SLAB * SLAB_STATE
    return pl.pallas_call(
        body, name="ssm_bwd", grid=(N_SLAB // sps, nt),
        in_specs=[ublk, ublk, sblk, sblk, urow, bmat, bmat, cmat, cmat, tab, tab, tab, tab],
        out_specs=[ublk, bmat, bmat, cmat, cmat, qrow, qrow, urow],
        out_shape=[_sds((t, SSM_WIDTH), BF16),
                   _sds((N_SLAB, LANES, SLAB_STATE), F32), _sds((N_SLAB, LANES, SLAB_STATE), F32),
                   _sds((N_SLAB, SLAB_STATE, LANES), F32), _sds((N_SLAB, SLAB_STATE, LANES), F32),
                   _sds((1, nst), F32), _sds((1, nst), F32), _sds((1, SSM_WIDTH), F32)],
        scratch_shapes=[pltpu.VMEM((1, sps * SLAB_STATE), F32), pltpu.VMEM((1, sps * SLAB_STATE), F32),
                        pltpu.VMEM((SUBLANES, sps * SLAB_STATE), F32), pltpu.VMEM((SUBLANES, sps * SLAB_STATE), F32),
                        pltpu.VMEM((SUBLANES, sps * LANES), F32)],
        compiler_params=_params(("parallel", "arbitrary")),
    )(gy, z, p_re, p_im, dvec, bre, bim, cre, cimn, apr, api, air, aii)


def _sgu_mask():
    r = lax.broadcasted_iota(jnp.int32, (SGU_CHUNK, SGU_CHUNK), 0)
    c = lax.broadcasted_iota(jnp.int32, (SGU_CHUNK, SGU_CHUNK), 1)
    return r >= c


def _sgu_common(zu, zv, lng, lnb):
    us, v = _gelu(zu), _gelu(zv)
    mu = jnp.mean(v, axis=-1, keepdims=True)
    vc = v - mu
    rstd = lax.rsqrt(jnp.mean(vc * vc, axis=-1, keepdims=True) + EPS)
    xhat = vc * rstd
    return us, xhat, rstd, xhat * lng + lnb


def _sgu_fwd(z, lng, lnb, w, bexp):
    t = z.shape[0]
    hd = SGU_CHUNK

    def body(zu_ref, zv_ref, lng_ref, lnb_ref, w_ref, b_ref, y_ref):
        us, _, _, vn = _sgu_common(zu_ref[...], zv_ref[...], lng_ref[...], lnb_ref[...])
        vnb = vn.astype(BF16)
        mask = _sgu_mask()
        for h in range(SGU_HEADS):
            sl = slice(h * hd, (h + 1) * hd)
            wt = jnp.where(mask, w_ref[h], 0.0).astype(BF16)
            y_ref[:, sl] = us[:, sl] * (_dot(wt, vnb[:, sl]) + b_ref[h])

    row = lambda c: _bs((SGU_CHUNK, SSM_WIDTH), lambda i: (i, c))
    vec = _bs((1, SSM_WIDTH), lambda i: (0, 0))
    hmat = _bs((SGU_HEADS, hd, hd), lambda i: (0, 0, 0))
    return pl.pallas_call(
        body, name="sgu_fwd", grid=(t // SGU_CHUNK,),
        in_specs=[row(1), row(2), vec, vec, hmat, hmat],
        out_specs=row(0), out_shape=_sds((t, SSM_WIDTH), F32),
        compiler_params=_params(("parallel",)),
    )(z, z, lng, lnb, w, bexp)


def _sgu_bwd(dy, du_ssm, z, lng, lnb, w, bexp, deps=()):
    t = z.shape[0]
    hd = SGU_CHUNK
    nc = t // SGU_CHUNK

    def body(*refs):
        dy_ref, dus_ref, zu_ref, zv_ref, lng_ref, lnb_ref, w_ref, b_ref = refs[:8]
        dz_ref, dlng_o, dlnb_o, dw_o, db_o = refs[8 + len(deps):]
        i = pl.program_id(0)

        @pl.when(i == 0)
        def _():
            for ref in (dlng_o, dlnb_o, dw_o, db_o):
                ref[...] = jnp.zeros_like(ref)

        zu, zv = zu_ref[...], zv_ref[...]
        lng = lng_ref[...]
        us, xhat, rstd, vn = _sgu_common(zu, zv, lng, lnb_ref[...])
        vnb = vn.astype(BF16)
        dyv = dy_ref[...]
        mask = _sgu_mask()
        dus_parts, dvn_parts = [], []
        for h in range(SGU_HEADS):
            sl = slice(h * hd, (h + 1) * hd)
            wt = jnp.where(mask, w_ref[h], 0.0).astype(BF16)
            mixed = _dot(wt, vnb[:, sl]) + b_ref[h]
            dus_parts.append(dyv[:, sl] * mixed)
            dmix = dyv[:, sl] * us[:, sl]
            dmb = dmix.astype(BF16)
            db_o[h] += dmix
            dw_o[h] += _dot(dmb, vnb[:, sl], "nt")
            dvn_parts.append(_dot(wt, dmb, "tn"))
        dus = jnp.concatenate(dus_parts, axis=1)
        dvn = jnp.concatenate(dvn_parts, axis=1)
        dlng_o[...] += jnp.sum(dvn * xhat, axis=0, keepdims=True)
        dlnb_o[...] += jnp.sum(dvn, axis=0, keepdims=True)
        dxh = dvn * lng
        dv = rstd * (dxh - jnp.mean(dxh, axis=-1, keepdims=True)
                     - xhat * jnp.mean(dxh * xhat, axis=-1, keepdims=True))
        dz_ref[:, 0:SSM_WIDTH] = dus_ref[...]
        dz_ref[:, SSM_WIDTH:2 * SSM_WIDTH] = (dus * _gelu_grad(zu)).astype(BF16)
        dz_ref[:, 2 * SSM_WIDTH:] = (dv * _gelu_grad(zv)).astype(BF16)

        @pl.when(i == nc - 1)
        def _():
            for h in range(SGU_HEADS):
                dw_o[h] = jnp.where(mask, dw_o[h], 0.0)
                db_o[h] = jnp.broadcast_to(jnp.sum(db_o[h], axis=1, keepdims=True), (hd, hd))

    row = lambda c: _bs((SGU_CHUNK, SSM_WIDTH), lambda i: (i, c))
    vec = _bs((1, SSM_WIDTH), lambda i: (0, 0))
    hmat = _bs((SGU_HEADS, hd, hd), lambda i: (0, 0, 0))
    return pl.pallas_call(
        body, name="sgu_bwd", grid=(nc,),
        in_specs=[row(0), row(0), row(1), row(2), vec, vec, hmat, hmat] + [_any_spec()] * len(deps),
        out_specs=[_bs((SGU_CHUNK, 3 * SSM_WIDTH), lambda i: (i, 0)), vec, vec, hmat, hmat],
        out_shape=[_sds((t, 3 * SSM_WIDTH), BF16), _sds((1, SSM_WIDTH), F32), _sds((1, SSM_WIDTH), F32),
                   _sds((SGU_HEADS, hd, hd), F32), _sds((SGU_HEADS, hd, hd), F32)],
        compiler_params=_params(("arbitrary",)),
    )(dy, du_ssm, z, z, lng, lnb, w, bexp, *deps)


def _place():
    x, y, c = (lax.axis_index(a) for a in MESH_AXES)
    return x, y, c


def _index(p):
    return 4 * p[0] + 2 * p[1] + p[2]


def _any_spec():
    return pl.BlockSpec(memory_space=pl.ANY)


def _col_block(ref, k, width):
    return ref.at[:, pl.ds(pl.multiple_of(k * width, LANES), width)]


def _all_gather(name, shards, by_columns=False):
    n = len(shards)

    def body(*refs):
        ins, outs = refs[:n], refs[n:2 * n]
        send, recv, loc = refs[2 * n:]
        x, y, c = _place()
        me, sib = (x, y, c), (x, y, 1 - c)
        chips = [(1 - x, y), (x, 1 - y), (1 - x, 1 - y)]

        def blk(w, p):
            if by_columns:
                return _col_block(outs[w], _index(p), shards[w].shape[1])
            return outs[w].at[_index(p)]

        def cp(w, k, block, to, src=None):
            dst = blk(w, block)
            return pltpu.make_async_remote_copy(
                src_ref=dst if src is None else src, dst_ref=dst,
                send_sem=send.at[w * 7 + k], recv_sem=recv.at[w * 7 + k],
                device_id=to, device_id_type=pl.DeviceIdType.MESH)

        mines, sends = [], []
        for w in range(n):
            m = pltpu.make_async_copy(ins[w], blk(w, me), loc.at[w])
            m.start()
            mines.append(m)
            first = [cp(w, 0, me, sib, src=ins[w])]
            first += [cp(w, 1 + j, me, (*chip, c), src=ins[w]) for j, chip in enumerate(chips)]
            for q in first:
                q.start()
            sends += first
        for j, chip in enumerate(chips):
            for w in range(n):
                cp(w, 1 + j, (*chip, c), me).wait_recv()
                q = cp(w, 4 + j, (*chip, c), sib)
                q.start()
                sends.append(q)
        for w in range(n):
            cp(w, 0, sib, me).wait_recv()
            for j, chip in enumerate(chips):
                cp(w, 4 + j, (*chip, 1 - c), me).wait_recv()
        for q in sends:
            q.wait_send()
        for m in mines:
            m.wait()

    return pl.pallas_call(
        body, name=name,
        in_specs=[_any_spec()] * n, out_specs=[_any_spec()] * n,
        out_shape=[_sds((s.shape[0], N_DEV * s.shape[1]) if by_columns else (N_DEV,) + s.shape, s.dtype)
                   for s in shards],
        scratch_shapes=[pltpu.SemaphoreType.DMA((n * 7,)), pltpu.SemaphoreType.DMA((n * 7,)),
                        pltpu.SemaphoreType.DMA((n,))],
        compiler_params=pltpu.CompilerParams(has_side_effects=True),
    )(*shards)


def _peer(r, x, y, c):
    return ((1 - x) if r & 4 else x, (1 - y) if r & 2 else y, (1 - c) if r & 1 else c)


def _sent_block(src_ref, land_ref, k, scatter):
    if not scatter:
        return src_ref
    if len(src_ref.shape) == len(land_ref.shape):
        return src_ref.at[k]
    return _col_block(src_ref, k, land_ref.shape[2])


def _send_start(name, src, land, scatter, after=None):
    n_after = 0 if after is None else 1

    def body(*refs):
        src_ref, land_ref = refs[0], refs[1]
        send, recv, _, _, token = refs[2 + n_after:]
        x, y, c = _place()
        me = _index((x, y, c))
        for r in range(1, N_DEV):
            p = _peer(r, x, y, c)
            pltpu.make_async_remote_copy(
                src_ref=_sent_block(src_ref, land_ref, _index(p), scatter), dst_ref=land_ref.at[me],
                send_sem=send.at[r - 1], recv_sem=recv.at[r - 1],
                device_id=p, device_id_type=pl.DeviceIdType.MESH).start()
        token[...] = jnp.zeros_like(token)

    hbm, sem = pl.BlockSpec(memory_space=pltpu.HBM), pl.BlockSpec(memory_space=pltpu.SEMAPHORE)
    return pl.pallas_call(
        body, name=name,
        out_shape=(pltpu.SemaphoreType.DMA((N_DEV - 1,)), pltpu.SemaphoreType.DMA((N_DEV - 1,)),
                   pltpu.HBM(src.shape, src.dtype), pltpu.HBM(land.shape, land.dtype),
                   _sds((SUBLANES, LANES), F32)),
        in_specs=(hbm, hbm) + (_any_spec(),) * n_after,
        out_specs=(sem, sem, hbm, hbm, pl.BlockSpec(memory_space=pltpu.VMEM)),
        input_output_aliases={0: 2, 1: 3},
        compiler_params=pltpu.CompilerParams(has_side_effects=pltpu.SideEffectType.DATAFLOW_SIDE_EFFECTING),
    )(pltpu.with_memory_space_constraint(src, pltpu.HBM), pltpu.with_memory_space_constraint(land, pltpu.HBM),
      *([] if after is None else [after]))


def _send_wait(name, started, after, scatter):
    send, recv, src_thru, land_thru, _ = started

    def body(src_ref, land_ref, send_r, recv_r, after_ref, src_out, land_out):
        x, y, c = _place()
        for r in range(1, N_DEV):
            p = _peer(r, x, y, c)
            k = _index(p)
            cp = pltpu.make_async_remote_copy(
                src_ref=_sent_block(src_ref, land_ref, k, scatter), dst_ref=land_ref.at[k],
                send_sem=send_r.at[r - 1], recv_sem=recv_r.at[r - 1],
                device_id=p, device_id_type=pl.DeviceIdType.MESH)
            cp.wait_send()
            cp.wait_recv()

    hbm, sem = pl.BlockSpec(memory_space=pltpu.HBM), pl.BlockSpec(memory_space=pltpu.SEMAPHORE)
    return pl.pallas_call(
        body, name=name,
        out_shape=(pltpu.HBM(src_thru.shape, src_thru.dtype), pltpu.HBM(land_thru.shape, land_thru.dtype)),
        in_specs=(hbm, hbm, sem, sem, _any_spec()), out_specs=(hbm, hbm),
        input_output_aliases={0: 0, 1: 1},
        compiler_params=pltpu.CompilerParams(has_side_effects=pltpu.SideEffectType.DATAFLOW_SIDE_EFFECTING),
    )(src_thru, land_thru, send, recv, after)


def _own_block(blocks, block, me):
    return lax.dynamic_update_index_in_dim(blocks, block, me, 0)


def _adam_math(parts, w, m, v):
    c1 = 1.0 / (1.0 - ADAM_B1 ** ADAM_STEP)
    c2 = 1.0 / (1.0 - ADAM_B2 ** ADAM_STEP)
    g = parts[0].astype(F32)
    for k in range(1, N_DEV):
        g = g + parts[k].astype(F32)
    mn = ADAM_B1 * m + (1.0 - ADAM_B1) * g
    vn = ADAM_B2 * v + (1.0 - ADAM_B2) * (g * g)
    return g, -ADAM_LR * ((mn * c1) / (jnp.sqrt(vn * c2) + ADAM_EPS) + ADAM_WD * w), mn, vn


def _adamw(name, parts, w, m, v):
    rows, cols = w.shape
    br = min(ADAM_ROWS, rows)

    def body(p_ref, w_ref, m_ref, v_ref, g_o, d_o, m_o, v_o):
        g_o[...], d_o[...], m_o[...], v_o[...] = _adam_math(p_ref[...], w_ref[...], m_ref[...], v_ref[...])

    blk = _bs((br, cols), lambda i: (i, 0))
    out = _sds((rows, cols), F32)
    return pl.pallas_call(
        body, name=name, grid=(rows // br,),
        in_specs=[_bs((N_DEV, br, cols), lambda i: (0, i, 0)), blk, blk, blk],
        out_specs=[blk] * 4, out_shape=[out] * 4,
        compiler_params=_params(("parallel",)),
    )(parts, w, m, v)


def _pack(arrs):
    tile = SUBLANES * LANES
    flat = []
    for a in arrs:
        f = a.reshape(-1).astype(F32)
        pad = (-f.shape[0]) % tile
        flat.append(jnp.pad(f, (0, pad)) if pad else f)
    total = sum(f.shape[0] for f in flat)
    tail = (-total) % (ADAM_ROWS * LANES)
    if tail:
        flat.append(jnp.zeros((tail,), F32))
    return jnp.concatenate(flat).reshape(-1, LANES)


def _unpack(buf, like):
    tile = SUBLANES * LANES
    flat = buf.reshape(-1)
    out, off = [], 0
    for a in like:
        n = math.prod(a.shape)
        out.append(flat[off:off + n].reshape(a.shape))
        off += n + ((-n) % tile)
    return out


def _to_block_b(b):
    gl = LANES // SSM_GROUP
    tb = b.reshape(N_SLAB, gl, SSM_STATE, SSM_GROUP).transpose(0, 1, 3, 2)
    eye = jnp.eye(gl, dtype=F32)
    return (tb[:, :, :, None, :] * eye[None, :, None, :, None]).reshape(N_SLAB, LANES, SLAB_STATE)


def _from_block_b(bx):
    gl = LANES // SSM_GROUP
    d = jnp.einsum("jghgp->jgph", bx.reshape(N_SLAB, gl, SSM_GROUP, gl, SSM_STATE))
    return d.reshape(N_SLAB * gl, SSM_STATE, SSM_GROUP)


def _to_block_c(cm):
    gl = LANES // SSM_GROUP
    tc = cm.reshape(N_SLAB, gl, SSM_GROUP, SSM_STATE).transpose(0, 1, 3, 2)
    eye = jnp.eye(gl, dtype=F32)
    return (tc[:, :, :, None, :] * eye[None, :, None, :, None]).reshape(N_SLAB, SLAB_STATE, LANES)


def _from_block_c(cx):
    gl = LANES // SSM_GROUP
    d = jnp.einsum("jgpgh->jghp", cx.reshape(N_SLAB, gl, SSM_STATE, gl, SSM_GROUP))
    return d.reshape(N_SLAB * gl, SSM_GROUP, SSM_STATE)


def kernel(x, norm_mix_g, w_in, ssm_a_re, ssm_a_im, ssm_b_re, ssm_b_im, ssm_c_re, ssm_c_im, ssm_d, ssm_log_dt, ssm_glu_w, ssm_glu_b, sgu_ln_g, sgu_ln_b, sgu_w, sgu_b, out_norm_ssm_g, out_norm_sgu_g, w_out, norm_mlp_g, w_up, w_down, norm_final_g, loss_target, m_norm_mix_g, m_w_in, m_ssm_a_re, m_ssm_a_im, m_ssm_b_re, m_ssm_b_im, m_ssm_c_re, m_ssm_c_im, m_ssm_d, m_ssm_log_dt, m_ssm_glu_w, m_ssm_glu_b, m_sgu_ln_g, m_sgu_ln_b, m_sgu_w, m_sgu_b, m_out_norm_ssm_g, m_out_norm_sgu_g, m_w_out, m_norm_mlp_g, m_w_up, m_w_down, m_norm_final_g, v_norm_mix_g, v_w_in, v_ssm_a_re, v_ssm_a_im, v_ssm_b_re, v_ssm_b_im, v_ssm_c_re, v_ssm_c_im, v_ssm_d, v_ssm_log_dt, v_ssm_glu_w, v_ssm_glu_b, v_sgu_ln_g, v_sgu_ln_b, v_sgu_w, v_sgu_b, v_out_norm_ssm_g, v_out_norm_sgu_g, v_w_out, v_norm_mlp_g, v_w_up, v_w_down, v_norm_final_g):
    given = dict(locals())
    names = ["norm_mix_g", "w_in", "ssm_a_re", "ssm_a_im", "ssm_b_re", "ssm_b_im", "ssm_c_re", "ssm_c_im",
             "ssm_d", "ssm_log_dt", "ssm_glu_w", "ssm_glu_b", "sgu_ln_g", "sgu_ln_b", "sgu_w", "sgu_b",
             "out_norm_ssm_g", "out_norm_sgu_g", "w_out", "norm_mlp_g", "w_up", "w_down", "norm_final_g"]
    big = ["w_in", "ssm_glu_w", "w_out", "w_up", "w_down"]
    small = [n for n in names if n not in big]

    d = D_MODEL
    t = x.shape[1]
    tb = min(1024, t)
    xs = x[0]
    target = loss_target[0]
    nsh_in = w_in.shape[2]
    nsh_up = w_up.shape[2]
    d_ff = nsh_up * N_DEV
    n_in = nsh_in * N_DEV

    me = _index(_place())
    (wg_in,) = _all_gather("gather_w_in", [w_in[0].astype(BF16)], by_columns=True)
    gathers = {}

    def start_gather(n, after=None):
        shard = given[n][0].astype(BF16)
        gathers[n] = _send_start("gather_start_" + n, shard, lax.empty((N_DEV,) + shard.shape, BF16), False, after)
        return gathers[n][4]

    tokens = [start_gather("ssm_glu_w", wg_in), start_gather("w_out", wg_in)]

    def gathered(n, after):
        shard, blocks = _send_wait("gather_wait_" + n, gathers[n], after, False)
        return _own_block(blocks, shard, me)

    nst = N_SLAB * SLAB_STATE
    are, aim = ssm_a_re.reshape(1, nst), ssm_a_im.reshape(1, nst)
    ldt = jnp.repeat(ssm_log_dt[0], SSM_STATE).reshape(1, nst)
    bxr, bxi = _to_block_b(ssm_b_re[0]), _to_block_b(ssm_b_im[0])
    cxr, cxi = _to_block_c(ssm_c_re[0]), _to_block_c(ssm_c_im[0])
    dvec = ssm_d.reshape(1, SSM_WIDTH)
    bre, bim, cre, cimn, apr, api, air, aii = _ssm_prep(are, aim, ldt, bxr, bxi, cxr, cxi)
    tabs = (bre, bim, cre, cimn, apr, api, air, aii)

    h1 = _rms_fwd("norm_mix", xs, norm_mix_g, deps=tokens)
    bn_i = n_in // 2
    (z,) = _mm("in_proj", "nn", (t // tb, n_in // bn_i, 1),
               (h1, _bs((tb, d), lambda i, j, k: (i, 0))),
               (wg_in, _bs((d, bn_i), lambda i, j, k: (0, j))),
               [(_sds((t, n_in), F32), _bs((tb, bn_i), lambda i, j, k: (i, j)))])
    tokens = [start_gather("w_up", z), start_gather("w_down", z)]
    y_pre, yg_b, p_re, p_im = _ssm_fwd(z, dvec, *tabs, deps=tokens)

    def glu_ep(acc, yp, b):
        gate = _sigmoid(acc + b)
        return _gelu(yp) * gate, gate

    hw = SSM_WIDTH // 2
    wg_glu = gathered("ssm_glu_w", yg_b).reshape(SSM_WIDTH, SSM_WIDTH)
    tile_g = _bs((tb, hw), lambda i, j, k: (i, j))
    y_ssm, gate = _mm("glu", "nn", (t // tb, 2, 1),
                      (yg_b, _bs((tb, SSM_WIDTH), lambda i, j, k: (i, 0))),
                      (wg_glu, _bs((SSM_WIDTH, hw), lambda i, j, k: (0, j))),
                      [(_sds((t, SSM_WIDTH), F32), tile_g), (_sds((t, SSM_WIDTH), F32), tile_g)],
                      extras=[(y_pre, tile_g), (ssm_glu_b, _bs((1, hw), lambda i, j, k: (0, j)))],
                      epilogue=glu_ep)

    sgu_bexp = jnp.broadcast_to(sgu_b[0][:, :, None], (SGU_HEADS, SGU_CHUNK, SGU_CHUNK))
    y_sgu = _sgu_fwd(z, sgu_ln_g, sgu_ln_b, sgu_w[0], sgu_bexp)
    mixed = _mix_norm(y_ssm, y_sgu, out_norm_ssm_g, out_norm_sgu_g)

    tb2 = min(512, t)
    row2 = _bs((tb2, d), lambda i, j, k: (i, 0))
    vec2 = _bs((1, d), lambda i, j, k: (0, 0))
    vec_sum = (_sds((1, d), F32), vec2)
    wg_out = gathered("w_out", mixed).reshape(d, d)

    def out_ep(acc, r, g):
        x2v = acc + r
        return x2v, _rms_math(x2v, g)

    x2, h2 = _mm("out_proj", "nn", (t // tb2, 1, 1),
                 (mixed, row2), (wg_out, _bs((d, d), lambda i, j, k: (0, 0))),
                 [(_sds((t, d), F32), row2), (_sds((t, d), BF16), row2)],
                 extras=[(xs, row2), (norm_mlp_g, vec2)], epilogue=out_ep, ep_rows=EP_ROWS)

    def up_ep(acc):
        r = jnp.maximum(acc, 0.0)
        return r * r, r

    tile_f = _bs((tb, nsh_up), lambda i, j, k: (i, j))
    wg_up = gathered("w_up", h2)
    f_act, r_act = _mm("mlp_up", "nn", (t // tb, N_DEV, 1),
                       (h2, _bs((tb, d), lambda i, j, k: (i, 0))),
                       (wg_up, _bs((None, d, nsh_up), lambda i, j, k: (j, 0, 0))),
                       [(_sds((t, d_ff), BF16), tile_f), (_sds((t, d_ff), BF16), tile_f)],
                       epilogue=up_ep)
    bk_d, bn_o = 2048, 1024
    tile_o = _bs((tb, bn_o), lambda i, j, k: (i, j))
    wg_down = gathered("w_down", f_act).reshape(d_ff, d)
    (x3,) = _mm("mlp_down", "nn", (t // tb, d // bn_o, d_ff // bk_d),
                (f_act, _bs((tb, bk_d), lambda i, j, k: (i, k))),
                (wg_down, _bs((bk_d, bn_o), lambda i, j, k: (k, j))),
                [(_sds((t, d), F32), tile_o)],
                extras=[(x2, tile_o)], epilogue=lambda acc, r: (acc + r,))
    dx3, dx3_b, g_final, err2 = _final_loss(x3, target, norm_final_g.reshape(1, d))
    loss = lax.psum(0.5 * jnp.sum(err2) / d, MESH_AXES)

    sends = {}

    def send_grad(n, g, land_shape=None):
        sends[n] = _send_start("grad_start_" + n, g, lax.empty(land_shape or g.shape, BF16), True)
        return [sends[n][4]]

    bn_a = 1024
    tile_a = _bs((tb, bn_a), lambda i, j, k: (i, j))
    (da,) = _mm("mlp_down_dx", "nt", (t // tb, d_ff // bn_a, 1),
                (dx3_b, _bs((tb, d), lambda i, j, k: (i, 0))),
                (wg_down, _bs((bn_a, d), lambda i, j, k: (j, 0))),
                [(_sds((t, d_ff), BF16), tile_a)],
                extras=[(r_act, tile_a)], epilogue=lambda acc, r: (acc * (2.0 * r.astype(F32)),))
    sq = 1024
    (gw_down,) = _mm("mlp_down_dw", "tn", (d_ff // sq, 1, t // tb),
                     (f_act, _bs((tb, sq), lambda i, j, k: (k, i))),
                     (dx3_b, _bs((tb, d), lambda i, j, k: (k, 0))),
                     [(_sds((d_ff, d), BF16), _bs((sq, d), lambda i, j, k: (i, 0)))])
    sent = send_grad("w_down", gw_down.reshape(N_DEV, -1, d))
    (dh2,) = _mm("mlp_up_dx", "nt", (t // tb, 1, N_DEV),
                 (da, _bs((tb, nsh_up), lambda i, j, k: (i, k))),
                 (wg_up, _bs((None, d, nsh_up), lambda i, j, k: (k, 0, 0))),
                 [(_sds((t, d), F32), _bs((tb, d), lambda i, j, k: (i, 0)))], deps=sent)

    def norm_bwd_side(dh, xv, dres, g):
        dx, dg = _rms_bwd_math(dh, xv, g)
        dx = dx + dres
        return dx, dx, dg

    up_dw_grid = (1, N_DEV, t // tb)
    side_rows = (t // math.prod(up_dw_grid), d)
    gw_up, dx2, dx2_b, g_norm_mlp = _mm(
        "mlp_up_dw", "tn", up_dw_grid,
        (h2, _bs((tb, d), lambda i, j, k: (k, 0))),
        (da, _bs((tb, nsh_up), lambda i, j, k: (k, j))),
        [(_sds((N_DEV, d, nsh_up), BF16), _bs((None, d, nsh_up), lambda i, j, k: (j, 0, 0)))],
        side=(norm_bwd_side, [(dh2, side_rows), (x2, side_rows), (dx3, side_rows), (norm_mlp_g, (1, d))],
              [(_sds((t, d), F32), side_rows), (_sds((t, d), BF16), side_rows)], [(_sds((1, d), F32), (1, d))]))
    sent = send_grad("w_up", gw_up)

    tk = min(2048, t)
    (gw_out,) = _mm("out_proj_dw", "tn", (d // sq, d // sq, t // tk),
                    (mixed, _bs((tk, sq), lambda i, j, k: (k, i))),
                    (dx2_b, _bs((tk, sq), lambda i, j, k: (k, j))),
                    [(_sds((d, d), BF16), _bs((sq, sq), lambda i, j, k: (i, j)))], deps=sent)
    sent = send_grad("w_out", gw_out.reshape(N_DEV, -1, d))
    half2 = _bs((tb2, SSM_WIDTH), lambda i, j, k: (i, 0))
    vech = _bs((1, SSM_WIDTH), lambda i, j, k: (0, 0))
    half_sum = (_sds((1, SSM_WIDTH), F32), vech)

    def out_dx_ep(acc, ya, yb, ga, gb, yp, gt):
        dya, dga = _rms_bwd_math(acc[:, :SSM_WIDTH], ya, ga)
        dyb, dgb = _rms_bwd_math(acc[:, SSM_WIDTH:], yb, gb)
        dpre = dya * _gelu(yp) * gt * (1.0 - gt)
        return dya, dyb, dpre, dga, dgb, jnp.sum(dpre, axis=0, keepdims=True)

    dy_ssm, dy_sgu, dpre_b, g_onorm_ssm, g_onorm_sgu, g_glu_b = _mm(
        "out_proj_dx", "nt", (t // tb2, 1, 1),
        (dx2_b, row2), (wg_out, _bs((d, d), lambda i, j, k: (0, 0))),
        [(_sds((t, SSM_WIDTH), F32), half2), (_sds((t, SSM_WIDTH), F32), half2),
         (_sds((t, SSM_WIDTH), BF16), half2)],
        extras=[(y_ssm, half2), (y_sgu, half2), (out_norm_ssm_g, vech), (out_norm_sgu_g, vech),
                (y_pre, half2), (gate, half2)],
        epilogue=out_dx_ep, sums=[half_sum, half_sum, half_sum], deps=sent, acc_shape=(tb2, d),
        ep_rows=EP_ROWS)

    (gw_glu,) = _mm("glu_dw", "tn", (1, 1, t // tb),
                    (yg_b, _bs((tb, SSM_WIDTH), lambda i, j, k: (k, 0))),
                    (dpre_b, _bs((tb, SSM_WIDTH), lambda i, j, k: (k, 0))),
                    [(_sds((SSM_WIDTH, SSM_WIDTH), BF16), _bs((SSM_WIDTH, SSM_WIDTH), lambda i, j, k: (0, 0)))])
    sent = send_grad("ssm_glu_w", gw_glu.reshape(N_DEV, -1, SSM_WIDTH))
    (dy_pre,) = _mm("glu_dx", "nt", (t // tb, 2, 1),
                    (dpre_b, _bs((tb, SSM_WIDTH), lambda i, j, k: (i, 0))),
                    (wg_glu, _bs((hw, SSM_WIDTH), lambda i, j, k: (j, 0))),
                    [(_sds((t, SSM_WIDTH), F32), tile_g)],
                    extras=[(dy_ssm, tile_g), (gate, tile_g), (y_pre, tile_g)],
                    epilogue=lambda acc, dy, gt, yp: ((dy * gt + acc) * _gelu_grad(yp),), deps=sent)
    du_b, dbre, dbim, dcre, dcimn, q_re, q_im, dd = _ssm_bwd(dy_pre, z, p_re, p_im, dvec, *tabs)
    dare, daim, dldt, dbxr, dbxi = _ssm_prep_bwd(are, aim, ldt, bxr, bxi, dbre, dbim, q_re, q_im)

    dz_b, g_ln_g, g_ln_b, g_sgu_w, g_sgu_bx = _sgu_bwd(dy_sgu, du_b, z, sgu_ln_g, sgu_ln_b, sgu_w[0], sgu_bexp)
    local_small = {
        "ssm_a_re": dare, "ssm_a_im": daim,
        "ssm_b_re": _from_block_b(dbxr), "ssm_b_im": _from_block_b(dbxi),
        "ssm_c_re": _from_block_c(dcre), "ssm_c_im": -_from_block_c(dcimn),
        "ssm_d": dd, "ssm_log_dt": dldt.reshape(-1, SSM_STATE).sum(axis=-1),
        "ssm_glu_b": g_glu_b, "sgu_ln_g": g_ln_g, "sgu_ln_b": g_ln_b, "sgu_w": g_sgu_w,
        "sgu_b": g_sgu_bx[:, :, 0], "out_norm_ssm_g": g_onorm_ssm, "out_norm_sgu_g": g_onorm_sgu,
        "norm_mlp_g": g_norm_mlp, "norm_final_g": g_final,
    }
    small_early = [n for n in small if n in local_small]
    small_late = [n for n in small if n not in local_small]
    packed = _pack([local_small[n] for n in small_early])
    small_send = _send_start("small_start", packed, lax.empty((N_DEV,) + packed.shape, F32), False)

    def landed_parts(n, after):
        sent_blocks, landed = _send_wait("grad_wait_" + n, sends[n], after, True)
        if sent_blocks.ndim == landed.ndim:
            own = lax.dynamic_index_in_dim(sent_blocks, me, 0, keepdims=False)
        else:
            own = lax.dynamic_slice_in_dim(sent_blocks, me * landed.shape[2], landed.shape[2], axis=1)
        return _own_block(landed, own, me)

    in_dw_grid = (d // sq, n_in // bn_i, t // tb)
    riders = ["w_down", "w_up"]
    side_ins, side_outs = [], []
    for n in riders:
        rows, cols = given[n].shape[1:]
        blk = (rows // math.prod(in_dw_grid), cols)
        side_ins += [(landed_parts(n, dz_b), (N_DEV,) + blk), (given[n][0], blk),
                     (given["m_" + n][0], blk), (given["v_" + n][0], blk)]
        side_outs += [(_sds((rows, cols), F32), blk)] * 4

    def adam_side(*tiles):
        return sum((_adam_math(*tiles[4 * r:4 * r + 4]) for r in range(len(riders))), ())

    gw_in, *rider_res = _mm("in_proj_dw", "tn", in_dw_grid,
                            (h1, _bs((tb, sq), lambda i, j, k: (k, i))),
                            (dz_b, _bs((tb, bn_i), lambda i, j, k: (k, j))),
                            [(_sds((d, n_in), BF16), _bs((sq, bn_i), lambda i, j, k: (i, j)))],
                            deps=[small_send[4]], side=(adam_side, side_ins, side_outs, []))
    sent = send_grad("w_in", gw_in, (N_DEV, d, nsh_in))

    def in_dx_ep(acc, xv, dres, g):
        dx, dg = _rms_bwd_math(acc, xv, g)
        return dx + dres, dg

    grad_x, g_norm_mix = _mm("in_proj_dx", "nt", (t // tb2, 1, n_in // bn_i),
                             (dz_b, _bs((tb2, bn_i), lambda i, j, k: (i, k))),
                             (wg_in, _bs((d, bn_i), lambda i, j, k: (0, k))),
                             [(_sds((t, d), F32), row2)],
                             extras=[(xs, row2), (dx2, row2), (norm_mix_g, vec2)],
                             epilogue=in_dx_ep, sums=[vec_sum], deps=sent, ep_rows=EP_ROWS)

    (late_parts,) = _all_gather("gather_late_grads", [_pack([g_norm_mix])])
    packed, early_parts = _send_wait("small_wait", small_send, grad_x, False)
    early_parts = _own_block(early_parts, packed, me)

    grads, deltas, new_m, new_v = {}, {}, {}, {}
    for r, n in enumerate(riders):
        grads[n], deltas[n], new_m[n], new_v[n] = [a.reshape(given[n].shape) for a in rider_res[4 * r:4 * r + 4]]
    for n in big:
        if n in riders:
            continue
        res = _adamw("adamw_" + n, landed_parts(n, grad_x), given[n][0], given["m_" + n][0], given["v_" + n][0])
        grads[n], deltas[n], new_m[n], new_v[n] = [r.reshape(given[n].shape) for r in res]
    for tag, group, parts in (("early", small_early, early_parts), ("late", small_late, late_parts)):
        like = [given[n] for n in group]
        res = _adamw("adamw_small_" + tag, parts, _pack(like), _pack([given["m_" + n] for n in group]),
                     _pack([given["v_" + n] for n in group]))
        for store, buf in zip((grads, deltas, new_m, new_v), res):
            for n, a in zip(group, _unpack(buf, like)):
                store[n] = a

    return (loss, grad_x.reshape(x.shape), *[grads[n] for n in names], *[deltas[n] for n in names],
            *[new_m[n] for n in names], *[new_v[n] for n in names])
```

```python
import functools
import math

import jax
import jax.numpy as jnp
from jax import lax
from jax.experimental import pallas as pl
from jax.experimental.pallas import tpu as pltpu

F32, BF16 = jnp.float32, jnp.bfloat16
EPS = 1e-6
N_DEV = 8
D_MODEL = 2048
SSM_WIDTH = 1024
SSM_GROUP = 16
SSM_STATE = 64
SGU_HEADS = 8
SGU_CHUNK = 128
LANES = 128
SUBLANES = 8
N_SLAB = SSM_WIDTH // LANES
SLAB_STATE = (LANES // SSM_GROUP) * SSM_STATE
SCAN_BLOCK = 256
SLABS_PER_STEP = 8
VMEM_LIMIT = 56 * 1024 * 1024
ROW_BLOCK = 256
EP_ROWS = 128
ADAM_ROWS = 128
MESH_AXES = ("x", "y", "c")

ADAM_LR, ADAM_B1, ADAM_B2, ADAM_EPS, ADAM_WD, ADAM_STEP = 0.001, 0.9, 0.999, 1e-08, 0.01, 10

_GELU_C0 = math.sqrt(2.0 / math.pi)
_GELU_C1 = 0.044715


def _gelu(v):
    return 0.5 * v * (1.0 + jnp.tanh(_GELU_C0 * (v + _GELU_C1 * v * v * v)))


def _gelu_grad(v):
    th = jnp.tanh(_GELU_C0 * (v + _GELU_C1 * v * v * v))
    return 0.5 * (1.0 + th) + 0.5 * v * (1.0 - th * th) * _GELU_C0 * (1.0 + 3.0 * _GELU_C1 * v * v)


def _sigmoid(v):
    return 1.0 / (1.0 + jnp.exp(-v))


def _params(sem=None):
    return pltpu.CompilerParams(dimension_semantics=sem, vmem_limit_bytes=VMEM_LIMIT)


def _dot(a, b, mode="nn"):
    dims = {"nn": ((1,), (0,)), "nt": ((1,), (1,)), "tn": ((0,), (0,))}[mode]
    return lax.dot_general(a, b, (dims, ((), ())), preferred_element_type=F32)


def _mm(name, mode, grid, a, b, outs, extras=(), epilogue=None, deps=(), sums=(), acc_shape=None,
        ep_rows=None, side=None):
    nk = grid[2]
    n_ex, n_out, n_dep, n_sum = len(extras), len(outs), len(deps), len(sums)
    assert not sums or grid[1] == 1
    if acc_shape is None:
        acc_shape = tuple(d for d in outs[0][1].block_shape if d is not None)
    side_fn, side_ins, side_outs, side_sums = side or (None, (), (), ())
    s_in, s_out, s_sum = len(side_ins), len(side_outs), len(side_sums)

    def body(*refs):
        a_ref, b_ref = refs[0], refs[1]
        ex = refs[2:2 + n_ex]
        pos = 2 + n_ex
        side_in = refs[pos:pos + s_in]
        pos += s_in + n_dep
        out_refs = refs[pos:pos + n_out]
        sum_refs = refs[pos + n_out:pos + n_out + n_sum]
        pos += n_out + n_sum
        side_out = refs[pos:pos + s_out]
        side_sum = refs[pos + s_out:pos + s_out + s_sum]
        acc = refs[-1]
        k = pl.program_id(2)

        @pl.when(k == 0)
        def _():
            acc[...] = jnp.zeros_like(acc)

        if side_sum:
            @pl.when((pl.program_id(0) == 0) & (pl.program_id(1) == 0) & (k == 0))
            def _():
                for o in side_sum:
                    o[...] = jnp.zeros_like(o)

        acc[...] += _dot(a_ref[...], b_ref[...], mode)
        if side_fn is not None:
            res = side_fn(*[r[...] for r in side_in])
            for o, r in zip(side_out, res[:s_out]):
                o[...] = r.astype(o.dtype)
            for o, r in zip(side_sum, res[s_out:]):
                o[...] += r

        def finish(rows):
            args = [e[rows, :] if e.shape[0] == acc_shape[0] else e[...] for e in ex]
            res = acc[rows, :]
            res = (res,) if epilogue is None else epilogue(res, *args)
            for o, r in zip(out_refs, res[:n_out]):
                o[rows, :] = r.astype(o.dtype)
            return tuple(res[n_out:])

        @pl.when(k == nk - 1)
        def _():
            if ep_rows is None:
                terms = finish(slice(None))
            else:
                def chunk(c, tot):
                    rows = pl.ds(pl.multiple_of(c * ep_rows, ep_rows), ep_rows)
                    return tuple(s + r for s, r in zip(tot, finish(rows)))

                zero = tuple(jnp.zeros(o.shape, F32) for o in sum_refs)
                terms = lax.fori_loop(0, acc_shape[0] // ep_rows, chunk, zero)
            for o, r in zip(sum_refs, terms):
                _add_up(o, r, pl.program_id(0) == 0)

    def side_spec(block):
        nd = len(block)
        if nd == 2 and block[0] == 1:
            return _bs(block, lambda i, j, k: (0, 0))
        return _bs(block, lambda i, j, k: (0,) * (nd - 2) + ((i * grid[1] + j) * grid[2] + k, 0))

    sem = ("arbitrary",) * 3 if sums or side else ("parallel", "parallel", "arbitrary")
    res = pl.pallas_call(
        body, name=name, grid=grid,
        in_specs=[a[1], b[1]] + [e[1] for e in extras] + [side_spec(blk) for _, blk in side_ins]
        + [_any_spec()] * n_dep,
        out_specs=[o[1] for o in outs] + [o[1] for o in sums]
        + [side_spec(blk) for _, blk in side_outs] + [side_spec(blk) for _, blk in side_sums],
        out_shape=[o[0] for o in outs] + [o[0] for o in sums]
        + [o[0] for o in side_outs] + [o[0] for o in side_sums],
        scratch_shapes=[pltpu.VMEM(acc_shape, F32)],
        compiler_params=_params(sem),
    )(a[0], b[0], *[e[0] for e in extras], *[x for x, _ in side_ins], *deps)
    return res


def _add_up(ref, term, first):
    @pl.when(first)
    def _():
        ref[...] = term

    @pl.when(jnp.logical_not(first))
    def _():
        ref[...] += term


def _rms_math(xv, g):
    return xv * lax.rsqrt(jnp.mean(xv * xv, axis=-1, keepdims=True) + EPS) * g


def _rms_bwd_math(dy, xv, g):
    r = lax.rsqrt(jnp.mean(xv * xv, axis=-1, keepdims=True) + EPS)
    xhat = xv * r
    dxhat = dy * g
    dx = r * (dxhat - xhat * jnp.mean(dxhat * xhat, axis=-1, keepdims=True))
    return dx, jnp.sum(dy * xhat, axis=0, keepdims=True)


def _sds(shape, dtype):
    return jax.ShapeDtypeStruct(shape, dtype)


def _bs(shape, fn):
    return pl.BlockSpec(shape, fn)


def _rms_fwd(name, x, g, deps=()):
    t, w = x.shape
    br = min(ROW_BLOCK, t)

    def body(*refs):
        x_ref, g_ref, o_ref = refs[0], refs[1], refs[-1]
        xv = x_ref[...]
        r = lax.rsqrt(jnp.mean(xv * xv, axis=-1, keepdims=True) + EPS)
        o_ref[...] = (xv * r * g_ref[...]).astype(BF16)

    return pl.pallas_call(
        body, name=name, grid=(t // br,),
        in_specs=[_bs((br, w), lambda i: (i, 0)), _bs((1, w), lambda i: (0, 0))] + [_any_spec()] * len(deps),
        out_specs=_bs((br, w), lambda i: (i, 0)),
        out_shape=_sds((t, w), BF16),
        compiler_params=_params(("parallel",)),
    )(x, g, *deps)


def _mix_norm(ya, yb, ga, gb):
    t, w = ya.shape
    br = min(ROW_BLOCK, t)

    def body(a_ref, b_ref, ga_ref, gb_ref, o_ref):
        for src, g_ref, col in ((a_ref, ga_ref, 0), (b_ref, gb_ref, w)):
            v = src[...]
            r = lax.rsqrt(jnp.mean(v * v, axis=-1, keepdims=True) + EPS)
            o_ref[:, col:col + w] = (v * r * g_ref[...]).astype(BF16)

    row = _bs((br, w), lambda i: (i, 0))
    vec = _bs((1, w), lambda i: (0, 0))
    return pl.pallas_call(
        body, name="mix_norm", grid=(t // br,),
        in_specs=[row, row, vec, vec],
        out_specs=_bs((br, 2 * w), lambda i: (i, 0)),
        out_shape=_sds((t, 2 * w), BF16),
        compiler_params=_params(("parallel",)),
    )(ya, yb, ga, gb)


def _loss_math(xv, target, g):
    r = lax.rsqrt(jnp.mean(xv * xv, axis=-1, keepdims=True) + EPS)
    xhat = xv * r
    err = xhat * g - target
    dy = err * (1.0 / xv.shape[-1])
    dxhat = dy * g
    dx = r * (dxhat - xhat * jnp.mean(dxhat * xhat, axis=-1, keepdims=True))
    return dx, jnp.sum(dy * xhat, axis=0, keepdims=True), jnp.sum(err * err, axis=0, keepdims=True)


def _final_loss(x3, target, g):
    t, w = x3.shape
    br = min(ROW_BLOCK, t)

    def body(x_ref, tg_ref, g_ref, dx_ref, dxb_ref, dg_ref, l_ref):
        dx, dg, e2 = _loss_math(x_ref[...], tg_ref[...], g_ref[...])
        dx_ref[...] = dx
        dxb_ref[...] = dx.astype(BF16)
        _add_up(dg_ref, dg, pl.program_id(0) == 0)
        _add_up(l_ref, e2, pl.program_id(0) == 0)

    row = _bs((br, w), lambda i: (i, 0))
    vec = _bs((1, w), lambda i: (0, 0))
    return pl.pallas_call(
        body, name="final_loss", grid=(t // br,),
        in_specs=[row, row, vec], out_specs=[row, row, vec, vec],
        out_shape=[_sds((t, w), F32), _sds((t, w), BF16), _sds((1, w), F32), _sds((1, w), F32)],
        compiler_params=_params(("arbitrary",)),
    )(x3, target, g)


def _prep_math(are, aim, ldt, bxr, bxi):
    dt = jnp.exp(ldt)
    er = jnp.exp(are * dt)
    th = aim * dt
    abr, abi = er * jnp.cos(th), er * jnp.sin(th)
    nr, ni = abr - 1.0, abi
    den = are * are + aim * aim
    cr = (nr * are + ni * aim) / den
    ci = (ni * are - nr * aim) / den
    bbr, bbi = [], []
    for j in range(N_SLAB):
        sl = slice(j * SLAB_STATE, (j + 1) * SLAB_STATE)
        bbr.append(cr[:, sl] * bxr[j] - ci[:, sl] * bxi[j])
        bbi.append(cr[:, sl] * bxi[j] + ci[:, sl] * bxr[j])
    return abr, abi, bbr, bbi


def _ssm_prep(are, aim, ldt, bxr, bxi, cxr, cxi):
    nst = N_SLAB * SLAB_STATE

    def body(are_r, aim_r, ldt_r, bxr_r, bxi_r, cxr_r, cxi_r,
             bre_o, bim_o, cre_o, cimn_o, apr_o, api_o, air_o, aii_o):
        abr, abi, bbr, bbi = _prep_math(are_r[...], aim_r[...], ldt_r[...],
                                        [bxr_r[j] for j in range(N_SLAB)], [bxi_r[j] for j in range(N_SLAB)])
        for j in range(N_SLAB):
            bre_o[j] = bbr[j].astype(BF16)
            bim_o[j] = bbi[j].astype(BF16)
        cre_o[...] = cxr_r[...].astype(BF16)
        cimn_o[...] = (-cxi_r[...]).astype(BF16)

        def step(k, cur):
            cr, ci = cur
            den = cr * cr + ci * ci
            apr_o[pl.ds(k, 1), :] = cr
            api_o[pl.ds(k, 1), :] = ci
            air_o[pl.ds(k, 1), :] = cr / den
            aii_o[pl.ds(k, 1), :] = -ci / den
            return cr * abr - ci * abi, cr * abi + ci * abr

        lax.fori_loop(0, SCAN_BLOCK, step, (jnp.ones((1, nst), F32), jnp.zeros((1, nst), F32)))

    tab = _sds((SCAN_BLOCK, nst), F32)
    return pl.pallas_call(
        body, name="ssm_prep",
        out_shape=[_sds(bxr.shape, BF16), _sds(bxr.shape, BF16), _sds(cxr.shape, BF16), _sds(cxr.shape, BF16),
                   tab, tab, tab, tab],
        compiler_params=_params(),
    )(are, aim, ldt, bxr, bxi, cxr, cxi)


def _ssm_prep_bwd(are, aim, ldt, bxr, bxi, dbre, dbim, qr, qi):
    def body(are_r, aim_r, ldt_r, bxr_r, bxi_r, dbre_r, dbim_r, qr_r, qi_r,
             dare_o, daim_o, dldt_o, dbxr_o, dbxi_o):
        prim = (are_r[...], aim_r[...], ldt_r[...],
                [bxr_r[j] for j in range(N_SLAB)], [bxi_r[j] for j in range(N_SLAB)])
        (abr, abi, _, _), vjp = jax.vjp(_prep_math, *prim)
        den = abr * abr + abi * abi
        q_r, q_i = qr_r[...], qi_r[...]
        gar = (q_r * abr - q_i * abi) / den
        gai = (q_r * abi + q_i * abr) / den
        ct = (gar, gai, [dbre_r[j] for j in range(N_SLAB)], [dbim_r[j] for j in range(N_SLAB)])
        dare, daim, dldt, dbxr, dbxi = vjp(ct)
        dare_o[...] = dare
        daim_o[...] = daim
        dldt_o[...] = dldt
        for j in range(N_SLAB):
            dbxr_o[j] = dbxr[j]
            dbxi_o[j] = dbxi[j]

    row = _sds(are.shape, F32)
    return pl.pallas_call(
        body, name="ssm_prep_bwd",
        out_shape=[row, row, row, _sds(bxr.shape, F32), _sds(bxr.shape, F32)],
        compiler_params=_params(),
    )(are, aim, ldt, bxr, bxi, dbre, dbim, qr, qi)


def _tri(lower):
    r = lax.broadcasted_iota(jnp.int32, (SCAN_BLOCK, SCAN_BLOCK), 0)
    c = lax.broadcasted_iota(jnp.int32, (SCAN_BLOCK, SCAN_BLOCK), 1)
    return jnp.where((r >= c) if lower else (r <= c), 1.0, 0.0).astype(BF16)


def _cumsum_mxu(tri, v):
    return _dot(tri, v.astype(BF16))


def _ssm_specs(t, sps):
    nt = t // SCAN_BLOCK
    tab = _bs((SCAN_BLOCK, sps * SLAB_STATE), lambda j, i: (0, j))
    bmat = _bs((sps, LANES, SLAB_STATE), lambda j, i: (j, 0, 0))
    cmat = _bs((sps, SLAB_STATE, LANES), lambda j, i: (j, 0, 0))
    return nt, tab, bmat, cmat


def _slab_slices(s):
    return slice(s * LANES, (s + 1) * LANES), slice(s * SLAB_STATE, (s + 1) * SLAB_STATE)


def _ssm_fwd(z, dvec, bre, bim, cre, cimn, apr, api, air, aii, deps=()):
    t = z.shape[0]
    sps = SLABS_PER_STEP
    nt, tab, bmat, cmat = _ssm_specs(t, sps)
    nst = N_SLAB * SLAB_STATE
    last = SCAN_BLOCK - 1

    def body(*refs):
        u_ref, d_ref, bre_r, bim_r, cre_r, cimn_r, apr_r, api_r, air_r, aii_r = refs[:10]
        y_ref, yg_ref, pr_ref, pi_ref, car_r, car_i = refs[10 + len(deps):]

        @pl.when(pl.program_id(1) == 0)
        def _():
            car_r[...] = jnp.zeros_like(car_r)
            car_i[...] = jnp.zeros_like(car_i)

        tri = _tri(True)
        for s in range(sps):
            ul, sl = _slab_slices(s)
            u = u_ref[:, ul]
            ub = u.astype(BF16)
            bur, bui = _dot(ub, bre_r[s]), _dot(ub, bim_r[s])
            ir, ii = air_r[:, sl], aii_r[:, sl]
            csr = _cumsum_mxu(tri, ir * bur - ii * bui)
            csi = _cumsum_mxu(tri, ir * bui + ii * bur)
            pr, pi = apr_r[:, sl], api_r[:, sl]
            a_r, a_i = apr_r[1:2, sl], api_r[1:2, sl]
            c_r, c_i = car_r[:, sl], car_i[:, sl]
            wr = csr + (a_r * c_r - a_i * c_i)
            wi = csi + (a_r * c_i + a_i * c_r)
            sr = pr * wr - pi * wi
            si = pr * wi + pi * wr
            car_r[:, sl] = sr[last:last + 1, :]
            car_i[:, sl] = si[last:last + 1, :]
            pr_ref[:, sl] = (sr - bur).astype(BF16)
            pi_ref[:, sl] = (si - bui).astype(BF16)
            y = _dot(sr.astype(BF16), cre_r[s]) + _dot(si.astype(BF16), cimn_r[s]) + d_ref[:, ul] * u
            y_ref[:, ul] = y
            yg_ref[:, ul] = _gelu(y).astype(BF16)

    ublk = _bs((SCAN_BLOCK, sps * LANES), lambda j, i: (i, j))
    sblk = _bs((SCAN_BLOCK, sps * SLAB_STATE), lambda j, i: (i, j))
    return pl.pallas_call(
        body, name="ssm_fwd", grid=(N_SLAB // sps, nt),
        in_specs=[ublk, _bs((1, sps * LANES), lambda j, i: (0, j)), bmat, bmat, cmat, cmat, tab, tab, tab, tab]
        + [_any_spec()] * len(deps),
        out_specs=[ublk, ublk, sblk, sblk],
        out_shape=[_sds((t, SSM_WIDTH), F32), _sds((t, SSM_WIDTH), BF16),
                   _sds((t, nst), BF16), _sds((t, nst), BF16)],
        scratch_shapes=[pltpu.VMEM((1, sps * SLAB_STATE), F32), pltpu.VMEM((1, sps * SLAB_STATE), F32)],
        compiler_params=_params(("parallel", "arbitrary")),
    )(z, dvec, bre, bim, cre, cimn, apr, api, air, aii, *deps)


def _ssm_bwd(gy, z, p_re, p_im, dvec, bre, bim, cre, cimn, apr, api, air, aii):
    t = z.shape[0]
    sps = SLABS_PER_STEP
    nt, tab, bmat, cmat = _ssm_specs(t, sps)
    last = SCAN_BLOCK - 1

    def fold(v):
        return v.reshape(SCAN_BLOCK // SUBLANES, SUBLANES, v.shape[-1]).sum(axis=0)

    def body(g_ref, u_ref, pr_ref, pi_ref, d_ref, bre_r, bim_r, cre_r, cimn_r, apr_r, api_r, air_r, aii_r,
             du_ref, dbre_o, dbim_o, dcre_o, dcimn_o, qr_o, qi_o, dd_o, car_r, car_i, qacc_r, qacc_i, dacc):
        i = pl.program_id(1)

        @pl.when(i == 0)
        def _():
            for ref in (car_r, car_i, qacc_r, qacc_i, dacc, dbre_o, dbim_o, dcre_o, dcimn_o):
                ref[...] = jnp.zeros_like(ref)

        tri = _tri(False)
        for s in range(sps):
            ul, sl = _slab_slices(s)
            g = g_ref[:, ul]
            gb = g.astype(BF16)
            u = u_ref[:, ul]
            ub = u.astype(BF16)
            bur, bui = _dot(ub, bre_r[s]), _dot(ub, bim_r[s])
            p_r, p_i = pr_ref[:, sl].astype(F32), pi_ref[:, sl].astype(F32)
            srb, sib = (p_r + bur).astype(BF16), (p_i + bui).astype(BF16)
            dcre_o[s] += _dot(srb, gb, "tn")
            dcimn_o[s] += _dot(sib, gb, "tn")
            dsr, dsi = _dot(gb, cre_r[s], "nt"), _dot(gb, cimn_r[s], "nt")
            pr, pi = apr_r[:, sl], api_r[:, sl]
            csr = _cumsum_mxu(tri, pr * dsr + pi * dsi)
            csi = _cumsum_mxu(tri, pr * dsi - pi * dsr)
            al_r, al_i = apr_r[last:last + 1, sl], api_r[last:last + 1, sl]
            c_r, c_i = car_r[:, sl], car_i[:, sl]
            wr = csr + (al_r * c_r + al_i * c_i)
            wi = csi + (al_r * c_i - al_i * c_r)
            ir, ii = air_r[:, sl], aii_r[:, sl]
            lr = ir * wr + ii * wi
            li = ir * wi - ii * wr
            a_r, a_i = apr_r[1:2, sl], api_r[1:2, sl]
            car_r[:, sl] = a_r * lr[0:1, :] + a_i * li[0:1, :]
            car_i[:, sl] = a_r * li[0:1, :] - a_i * lr[0:1, :]
            lrb, lib = lr.astype(BF16), li.astype(BF16)
            dbre_o[s] += _dot(ub, lrb, "tn")
            dbim_o[s] += _dot(ub, lib, "tn")
            du = d_ref[:, ul] * g + _dot(lrb, bre_r[s], "nt") + _dot(lib, bim_r[s], "nt")
            du_ref[:, ul] = du.astype(BF16)
            qacc_r[:, sl] += fold(lr * p_r + li * p_i)
            qacc_i[:, sl] += fold(li * p_r - lr * p_i)
            dacc[:, ul] += fold(g * u)

        @pl.when(i == nt - 1)
        def _():
            qr_o[...] = jnp.sum(qacc_r[...], axis=0, keepdims=True)
            qi_o[...] = jnp.sum(qacc_i[...], axis=0, keepdims=True)
            dd_o[...] = jnp.sum(dacc[...], axis=0, keepdims=True)

    rev = lambda j, i: (nt - 1 - i, j)
    ublk = _bs((SCAN_BLOCK, sps * LANES), rev)
    sblk = _bs((SCAN_BLOCK, sps * SLAB_STATE), rev)
    qrow = _bs((1, sps * SLAB_STATE), lambda j, i: (0, j))
    urow = _bs((1, sps * LANES), lambda j, i: (0, j))
    nst = N_SLAB * SLAB_STATE
    return pl.pallas_call(
        body, name="ssm_bwd", grid=(N_SLAB // sps, nt),
        in_specs=[ublk, ublk, sblk, sblk, urow, bmat, bmat, cmat, cmat, tab, tab, tab, tab],
        out_specs=[ublk, bmat, bmat, cmat, cmat, qrow, qrow, urow],
        out_shape=[_sds((t, SSM_WIDTH), BF16),
                   _sds((N_SLAB, LANES, SLAB_STATE), F32), _sds((N_SLAB, LANES, SLAB_STATE), F32),
                   _sds((N_SLAB, SLAB_STATE, LANES), F32), _sds((N_SLAB, SLAB_STATE, LANES), F32),
                   _sds((1, nst), F32), _sds((1, nst), F32), _sds((1, SSM_WIDTH), F32)],
        scratch_shapes=[pltpu.VMEM((1, sps * SLAB_STATE), F32), pltpu.VMEM((1, sps * SLAB_STATE), F32),
                        pltpu.VMEM((SUBLANES, sps * SLAB_STATE), F32), pltpu.VMEM((SUBLANES, sps * SLAB_STATE), F32),
                        pltpu.VMEM((SUBLANES, sps * LANES), F32)],
        compiler_params=_params(("parallel", "arbitrary")),
    )(gy, z, p_re, p_im, dvec, bre, bim, cre, cimn, apr, api, air, aii)


def _sgu_mask():
    r = lax.broadcasted_iota(jnp.int32, (SGU_CHUNK, SGU_CHUNK), 0)
    c = lax.broadcasted_iota(jnp.int32, (SGU_CHUNK, SGU_CHUNK), 1)
    return r >= c


def _sgu_common(zu, zv, lng, lnb):
    us, v = _gelu(zu), _gelu(zv)
    mu = jnp.mean(v, axis=-1, keepdims=True)
    vc = v - mu
    rstd = lax.rsqrt(jnp.mean(vc * vc, axis=-1, keepdims=True) + EPS)
    xhat = vc * rstd
    return us, xhat, rstd, xhat * lng + lnb


def _sgu_fwd(z, lng, lnb, w, bexp):
    t = z.shape[0]
    hd = SGU_CHUNK

    def body(zu_ref, zv_ref, lng_ref, lnb_ref, w_ref, b_ref, y_ref):
        us, _, _, vn = _sgu_common(zu_ref[...], zv_ref[...], lng_ref[...], lnb_ref[...])
        vnb = vn.astype(BF16)
        mask = _sgu_mask()
        for h in range(SGU_HEADS):
            sl = slice(h * hd, (h + 1) * hd)
            wt = jnp.where(mask, w_ref[h], 0.0).astype(BF16)
            y_ref[:, sl] = us[:, sl] * (_dot(wt, vnb[:, sl]) + b_ref[h])

    row = lambda c: _bs((SGU_CHUNK, SSM_WIDTH), lambda i: (i, c))
    vec = _bs((1, SSM_WIDTH), lambda i: (0, 0))
    hmat = _bs((SGU_HEADS, hd, hd), lambda i: (0, 0, 0))
    return pl.pallas_call(
        body, name="sgu_fwd", grid=(t // SGU_CHUNK,),
        in_specs=[row(1), row(2), vec, vec, hmat, hmat],
        out_specs=row(0), out_shape=_sds((t, SSM_WIDTH), F32),
        compiler_params=_params(("parallel",)),
    )(z, z, lng, lnb, w, bexp)


def _sgu_bwd(dy, du_ssm, z, lng, lnb, w, bexp, deps=()):
    t = z.shape[0]
    hd = SGU_CHUNK
    nc = t // SGU_CHUNK

    def body(*refs):
        dy_ref, dus_ref, zu_ref, zv_ref, lng_ref, lnb_ref, w_ref, b_ref = refs[:8]
        dz_ref, dlng_o, dlnb_o, dw_o, db_o = refs[8 + len(deps):]
        i = pl.program_id(0)

        @pl.when(i == 0)
        def _():
            for ref in (dlng_o, dlnb_o, dw_o, db_o):
                ref[...] = jnp.zeros_like(ref)

        zu, zv = zu_ref[...], zv_ref[...]
        lng = lng_ref[...]
        us, xhat, rstd, vn = _sgu_common(zu, zv, lng, lnb_ref[...])
        vnb = vn.astype(BF16)
        dyv = dy_ref[...]
        mask = _sgu_mask()
        dus_parts, dvn_parts = [], []
        for h in range(SGU_HEADS):
            sl = slice(h * hd, (h + 1) * hd)
            wt = jnp.where(mask, w_ref[h], 0.0).astype(BF16)
            mixed = _dot(wt, vnb[:, sl]) + b_ref[h]
            dus_parts.append(dyv[:, sl] * mixed)
            dmix = dyv[:, sl] * us[:, sl]
            dmb = dmix.astype(BF16)
            db_o[h] += dmix
            dw_o[h] += _dot(dmb, vnb[:, sl], "nt")
            dvn_parts.append(_dot(wt, dmb, "tn"))
        dus = jnp.concatenate(dus_parts, axis=1)
        dvn = jnp.concatenate(dvn_parts, axis=1)
        dlng_o[...] += jnp.sum(dvn * xhat, axis=0, keepdims=True)
        dlnb_o[...] += jnp.sum(dvn, axis=0, keepdims=True)
        dxh = dvn * lng
        dv = rstd * (dxh - jnp.mean(dxh, axis=-1, keepdims=True)
                     - xhat * jnp.mean(dxh * xhat, axis=-1, keepdims=True))
        dz_ref[:, 0:SSM_WIDTH] = dus_ref[...]
        dz_ref[:, SSM_WIDTH:2 * SSM_WIDTH] = (dus * _gelu_grad(zu)).astype(BF16)
        dz_ref[:, 2 * SSM_WIDTH:] = (dv * _gelu_grad(zv)).astype(BF16)

        @pl.when(i == nc - 1)
        def _():
            for h in range(SGU_HEADS):
                dw_o[h] = jnp.where(mask, dw_o[h], 0.0)
                db_o[h] = jnp.broadcast_to(jnp.sum(db_o[h], axis=1, keepdims=True), (hd, hd))

    row = lambda c: _bs((SGU_CHUNK, SSM_WIDTH), lambda i: (i, c))
    vec = _bs((1, SSM_WIDTH), lambda i: (0, 0))
    hmat = _bs((SGU_HEADS, hd, hd), lambda i: (0, 0, 0))
    return pl.pallas_call(
        body, name="sgu_bwd", grid=(nc,),
        in_specs=[row(0), row(0), row(1), row(2), vec, vec, hmat, hmat] + [_any_spec()] * len(deps),
        out_specs=[_bs((SGU_CHUNK, 3 * SSM_WIDTH), lambda i: (i, 0)), vec, vec, hmat, hmat],
        out_shape=[_sds((t, 3 * SSM_WIDTH), BF16), _sds((1, SSM_WIDTH), F32), _sds((1, SSM_WIDTH), F32),
                   _sds((SGU_HEADS, hd, hd), F32), _sds((SGU_HEADS, hd, hd), F32)],
        compiler_params=_params(("arbitrary",)),
    )(dy, du_ssm, z, z, lng, lnb, w, bexp, *deps)


def _place():
    x, y, c = (lax.axis_index(a) for a in MESH_AXES)
    return x, y, c


def _index(p):
    return 4 * p[0] + 2 * p[1] + p[2]


def _any_spec():
    return pl.BlockSpec(memory_space=pl.ANY)


def _col_block(ref, k, width):
    return ref.at[:, pl.ds(pl.multiple_of(k * width, LANES), width)]


def _all_gather(name, shards, by_columns=False):
    n = len(shards)

    def body(*refs):
        ins, outs = refs[:n], refs[n:2 * n]
        send, recv, loc = refs[2 * n:]
        x, y, c = _place()
        me, sib = (x, y, c), (x, y, 1 - c)
        chips = [(1 - x, y), (x, 1 - y), (1 - x, 1 - y)]

        def blk(w, p):
            if by_columns:
                return _col_block(outs[w], _index(p), shards[w].shape[1])
            return outs[w].at[_index(p)]

        def cp(w, k, block, to, src=None):
            dst = blk(w, block)
            return pltpu.make_async_remote_copy(
                src_ref=dst if src is None else src, dst_ref=dst,
                send_sem=send.at[w * 7 + k], recv_sem=recv.at[w * 7 + k],
                device_id=to, device_id_type=pl.DeviceIdType.MESH)

        mines, sends = [], []
        for w in range(n):
            m = pltpu.make_async_copy(ins[w], blk(w, me), loc.at[w])
            m.start()
            mines.append(m)
            first = [cp(w, 0, me, sib, src=ins[w])]
            first += [cp(w, 1 + j, me, (*chip, c), src=ins[w]) for j, chip in enumerate(chips)]
            for q in first:
                q.start()
            sends += first
        for j, chip in enumerate(chips):
            for w in range(n):
                cp(w, 1 + j, (*chip, c), me).wait_recv()
                q = cp(w, 4 + j, (*chip, c), sib)
                q.start()
                sends.append(q)
        for w in range(n):
            cp(w, 0, sib, me).wait_recv()
            for j, chip in enumerate(chips):
                cp(w, 4 + j, (*chip, 1 - c), me).wait_recv()
        for q in sends:
            q.wait_send()
        for m in mines:
            m.wait()

    return pl.pallas_call(
        body, name=name,
        in_specs=[_any_spec()] * n, out_specs=[_any_spec()] * n,
        out_shape=[_sds((s.shape[0], N_DEV * s.shape[1]) if by_columns else (N_DEV,) + s.shape, s.dtype)
                   for s in shards],
        scratch_shapes=[pltpu.SemaphoreType.DMA((n * 7,)), pltpu.SemaphoreType.DMA((n * 7,)),
                        pltpu.SemaphoreType.DMA((n,))],
        compiler_params=pltpu.CompilerParams(has_side_effects=True),
    )(*shards)


def _peer(r, x, y, c):
    return ((1 - x) if r & 4 else x, (1 - y) if r & 2 else y, (1 - c) if r & 1 else c)


def _sent_block(src_ref, land_ref, k, scatter):
    if not scatter:
        return src_ref
    if len(src_ref.shape) == len(land_ref.shape):
        return src_ref.at[k]
    return _col_block(src_ref, k, land_ref.shape[2])


def _send_start(name, src, land, scatter, after=None):
    n_after = 0 if after is None else 1

    def body(*refs):
        src_ref, land_ref = refs[0], refs[1]
        send, recv, _, _, token = refs[2 + n_after:]
        x, y, c = _place()
        me = _index((x, y, c))
        for r in range(1, N_DEV):
            p = _peer(r, x, y, c)
            pltpu.make_async_remote_copy(
                src_ref=_sent_block(src_ref, land_ref, _index(p), scatter), dst_ref=land_ref.at[me],
                send_sem=send.at[r - 1], recv_sem=recv.at[r - 1],
                device_id=p, device_id_type=pl.DeviceIdType.MESH).start()
        token[...] = jnp.zeros_like(token)

    hbm, sem = pl.BlockSpec(memory_space=pltpu.HBM), pl.BlockSpec(memory_space=pltpu.SEMAPHORE)
    return pl.pallas_call(
        body, name=name,
        out_shape=(pltpu.SemaphoreType.DMA((N_DEV - 1,)), pltpu.SemaphoreType.DMA((N_DEV - 1,)),
                   pltpu.HBM(src.shape, src.dtype), pltpu.HBM(land.shape, land.dtype),
                   _sds((SUBLANES, LANES), F32)),
        in_specs=(hbm, hbm) + (_any_spec(),) * n_after,
        out_specs=(sem, sem, hbm, hbm, pl.BlockSpec(memory_space=pltpu.VMEM)),
        input_output_aliases={0: 2, 1: 3},
        compiler_params=pltpu.CompilerParams(has_side_effects=pltpu.SideEffectType.DATAFLOW_SIDE_EFFECTING),
    )(pltpu.with_memory_space_constraint(src, pltpu.HBM), pltpu.with_memory_space_constraint(land, pltpu.HBM),
      *([] if after is None else [after]))


def _send_wait(name, started, after, scatter):
    send, recv, src_thru, land_thru, _ = started

    def body(src_ref, land_ref, send_r, recv_r, after_ref, src_out, land_out):
        x, y, c = _place()
        for r in range(1, N_DEV):
            p = _peer(r, x, y, c)
            k = _index(p)
            cp = pltpu.make_async_remote_copy(
                src_ref=_sent_block(src_ref, land_ref, k, scatter), dst_ref=land_ref.at[k],
                send_sem=send_r.at[r - 1], recv_sem=recv_r.at[r - 1],
                device_id=p, device_id_type=pl.DeviceIdType.MESH)
            cp.wait_send()
            cp.wait_recv()

    hbm, sem = pl.BlockSpec(memory_space=pltpu.HBM), pl.BlockSpec(memory_space=pltpu.SEMAPHORE)
    return pl.pallas_call(
        body, name=name,
        out_shape=(pltpu.HBM(src_thru.shape, src_thru.dtype), pltpu.HBM(land_thru.shape, land_thru.dtype)),
        in_specs=(hbm, hbm, sem, sem, _any_spec()), out_specs=(hbm, hbm),
        input_output_aliases={0: 0, 1: 1},
        compiler_params=pltpu.CompilerParams(has_side_effects=pltpu.SideEffectType.DATAFLOW_SIDE_EFFECTING),
    )(src_thru, land_thru, send, recv, after)


SPREAD, PASS_ON = (1, 2, 4, 6), (2, 4, 6)


def _cols_start(name, land, width, pass_on, after):
    peers = PASS_ON if pass_on else SPREAD

    def body(land_ref, after_ref, send, recv, land_thru, token):
        x, y, c = _place()
        for n, r in enumerate(peers):
            block = _col_block(land_ref, _index(_peer(r, x, y, c) if pass_on else (x, y, c)), width)
            pltpu.make_async_remote_copy(
                src_ref=block, dst_ref=block, send_sem=send.at[n], recv_sem=recv.at[n],
                device_id=_peer(1, x, y, c) if pass_on else _peer(r, x, y, c),
                device_id_type=pl.DeviceIdType.MESH).start()
        token[...] = jnp.zeros_like(token)

    hbm, sem = pl.BlockSpec(memory_space=pltpu.HBM), pl.BlockSpec(memory_space=pltpu.SEMAPHORE)
    return pl.pallas_call(
        body, name=name,
        out_shape=(pltpu.SemaphoreType.DMA((len(peers),)), pltpu.SemaphoreType.DMA((len(peers),)),
                   pltpu.HBM(land.shape, land.dtype), _sds((SUBLANES, LANES), F32)),
        in_specs=(hbm, _any_spec()), out_specs=(sem, sem, hbm, pl.BlockSpec(memory_space=pltpu.VMEM)),
        input_output_aliases={0: 2},
        compiler_params=pltpu.CompilerParams(has_side_effects=pltpu.SideEffectType.DATAFLOW_SIDE_EFFECTING),
    )(pltpu.with_memory_space_constraint(land, pltpu.HBM), after)


def _cols_wait(name, started, width, pass_on, after):
    send, recv, land_thru, _ = started
    peers = PASS_ON if pass_on else SPREAD

    def body(land_ref, send_r, recv_r, after_ref, land_out):
        x, y, c = _place()
        for n, r in enumerate(peers):
            sent = _col_block(land_ref, _index(_peer(r, x, y, c) if pass_on else (x, y, c)), width)
            came = _col_block(land_ref, _index(_peer(r ^ 1 if pass_on else r, x, y, c)), width)
            cp = pltpu.make_async_remote_copy(
                src_ref=sent, dst_ref=came, send_sem=send_r.at[n], recv_sem=recv_r.at[n],
                device_id=_peer(1, x, y, c) if pass_on else _peer(r, x, y, c),
                device_id_type=pl.DeviceIdType.MESH)
            cp.wait_send()
            cp.wait_recv()

    hbm, sem = pl.BlockSpec(memory_space=pltpu.HBM), pl.BlockSpec(memory_space=pltpu.SEMAPHORE)
    return pl.pallas_call(
        body, name=name, out_shape=(pltpu.HBM(land_thru.shape, land_thru.dtype),),
        in_specs=(hbm, sem, sem, _any_spec()), out_specs=(hbm,), input_output_aliases={0: 0},
        compiler_params=pltpu.CompilerParams(has_side_effects=pltpu.SideEffectType.DATAFLOW_SIDE_EFFECTING),
    )(land_thru, send, recv, after)[0]


def _own_block(blocks, block, me):
    return lax.dynamic_update_index_in_dim(blocks, block, me, 0)


def _adam_math(parts, w, m, v):
    c1 = 1.0 / (1.0 - ADAM_B1 ** ADAM_STEP)
    c2 = 1.0 / (1.0 - ADAM_B2 ** ADAM_STEP)
    g = parts[0].astype(F32)
    for k in range(1, N_DEV):
        g = g + parts[k].astype(F32)
    mn = ADAM_B1 * m + (1.0 - ADAM_B1) * g
    vn = ADAM_B2 * v + (1.0 - ADAM_B2) * (g * g)
    return g, -ADAM_LR * ((mn * c1) / (jnp.sqrt(vn * c2) + ADAM_EPS) + ADAM_WD * w), mn, vn


def _adamw(name, parts, w, m, v):
    rows, cols = w.shape
    br = min(ADAM_ROWS, rows)

    def body(p_ref, w_ref, m_ref, v_ref, g_o, d_o, m_o, v_o):
        g_o[...], d_o[...], m_o[...], v_o[...] = _adam_math(p_ref[...], w_ref[...], m_ref[...], v_ref[...])

    blk = _bs((br, cols), lambda i: (i, 0))
    out = _sds((rows, cols), F32)
    return pl.pallas_call(
        body, name=name, grid=(rows // br,),
        in_specs=[_bs((N_DEV, br, cols), lambda i: (0, i, 0)), blk, blk, blk],
        out_specs=[blk] * 4, out_shape=[out] * 4,
        compiler_params=_params(("parallel",)),
    )(parts, w, m, v)


def _pack(arrs):
    tile = SUBLANES * LANES
    flat = []
    for a in arrs:
        f = a.reshape(-1).astype(F32)
        pad = (-f.shape[0]) % tile
        flat.append(jnp.pad(f, (0, pad)) if pad else f)
    total = sum(f.shape[0] for f in flat)
    tail = (-total) % (ADAM_ROWS * LANES)
    if tail:
        flat.append(jnp.zeros((tail,), F32))
    return jnp.concatenate(flat).reshape(-1, LANES)


def _unpack(buf, like):
    tile = SUBLANES * LANES
    flat = buf.reshape(-1)
    out, off = [], 0
    for a in like:
        n = math.prod(a.shape)
        out.append(flat[off:off + n].reshape(a.shape))
        off += n + ((-n) % tile)
    return out


def _to_block_b(b):
    gl = LANES // SSM_GROUP
    tb = b.reshape(N_SLAB, gl, SSM_STATE, SSM_GROUP).transpose(0, 1, 3, 2)
    eye = jnp.eye(gl, dtype=F32)
    return (tb[:, :, :, None, :] * eye[None, :, None, :, None]).reshape(N_SLAB, LANES, SLAB_STATE)


def _from_block_b(bx):
    gl = LANES // SSM_GROUP
    d = jnp.einsum("jghgp->jgph", bx.reshape(N_SLAB, gl, SSM_GROUP, gl, SSM_STATE))
    return d.reshape(N_SLAB * gl, SSM_STATE, SSM_GROUP)


def _to_block_c(cm):
    gl = LANES // SSM_GROUP
    tc = cm.reshape(N_SLAB, gl, SSM_GROUP, SSM_STATE).transpose(0, 1, 3, 2)
    eye = jnp.eye(gl, dtype=F32)
    return (tc[:, :, :, None, :] * eye[None, :, None, :, None]).reshape(N_SLAB, SLAB_STATE, LANES)


def _from_block_c(cx):
    gl = LANES // SSM_GROUP
    d = jnp.einsum("jgpgh->jghp", cx.reshape(N_SLAB, gl, SSM_STATE, gl, SSM_GROUP))
    return d.reshape(N_SLAB * gl, SSM_GROUP, SSM_STATE)


def kernel(x, norm_mix_g, w_in, ssm_a_re, ssm_a_im, ssm_b_re, ssm_b_im, ssm_c_re, ssm_c_im, ssm_d, ssm_log_dt, ssm_glu_w, ssm_glu_b, sgu_ln_g, sgu_ln_b, sgu_w, sgu_b, out_norm_ssm_g, out_norm_sgu_g, w_out, norm_mlp_g, w_up, w_down, norm_final_g, loss_target, m_norm_mix_g, m_w_in, m_ssm_a_re, m_ssm_a_im, m_ssm_b_re, m_ssm_b_im, m_ssm_c_re, m_ssm_c_im, m_ssm_d, m_ssm_log_dt, m_ssm_glu_w, m_ssm_glu_b, m_sgu_ln_g, m_sgu_ln_b, m_sgu_w, m_sgu_b, m_out_norm_ssm_g, m_out_norm_sgu_g, m_w_out, m_norm_mlp_g, m_w_up, m_w_down, m_norm_final_g, v_norm_mix_g, v_w_in, v_ssm_a_re, v_ssm_a_im, v_ssm_b_re, v_ssm_b_im, v_ssm_c_re, v_ssm_c_im, v_ssm_d, v_ssm_log_dt, v_ssm_glu_w, v_ssm_glu_b, v_sgu_ln_g, v_sgu_ln_b, v_sgu_w, v_sgu_b, v_out_norm_ssm_g, v_out_norm_sgu_g, v_w_out, v_norm_mlp_g, v_w_up, v_w_down, v_norm_final_g):
    given = dict(locals())
    names = ["norm_mix_g", "w_in", "ssm_a_re", "ssm_a_im", "ssm_b_re", "ssm_b_im", "ssm_c_re", "ssm_c_im",
             "ssm_d", "ssm_log_dt", "ssm_glu_w", "ssm_glu_b", "sgu_ln_g", "sgu_ln_b", "sgu_w", "sgu_b",
             "out_norm_ssm_g", "out_norm_sgu_g", "w_out", "norm_mlp_g", "w_up", "w_down", "norm_final_g"]
    big = ["w_in", "ssm_glu_w", "w_out", "w_up", "w_down"]
    small = [n for n in names if n not in big]

    d = D_MODEL
    t = x.shape[1]
    tb = min(1024, t)
    xs = x[0]
    target = loss_target[0]
    nsh_in = w_in.shape[2]
    nsh_up = w_up.shape[2]
    d_ff = nsh_up * N_DEV
    n_in = nsh_in * N_DEV

    me = _index(_place())
    shard_in = w_in[0].astype(BF16)
    land_in = lax.dynamic_update_slice_in_dim(lax.empty((d, n_in), BF16), shard_in, me * nsh_in, axis=1)
    spread = _cols_start("gather_w_in_spread", land_in, nsh_in, False, shard_in)
    gathers = {}

    def start_gather(n, after=None):
        shard = given[n][0].astype(BF16)
        gathers[n] = _send_start("gather_start_" + n, shard, lax.empty((N_DEV,) + shard.shape, BF16), False, after)
        return gathers[n][4]

    def gathered(n, after):
        shard, blocks = _send_wait("gather_wait_" + n, gathers[n], after, False)
        return _own_block(blocks, shard, me)

    nst = N_SLAB * SLAB_STATE
    are, aim = ssm_a_re.reshape(1, nst), ssm_a_im.reshape(1, nst)
    ldt = jnp.repeat(ssm_log_dt[0], SSM_STATE).reshape(1, nst)
    bxr, bxi = _to_block_b(ssm_b_re[0]), _to_block_b(ssm_b_im[0])
    cxr, cxi = _to_block_c(ssm_c_re[0]), _to_block_c(ssm_c_im[0])
    dvec = ssm_d.reshape(1, SSM_WIDTH)

    h1 = _rms_fwd("norm_mix", xs, norm_mix_g, deps=[spread[3]])
    bre, bim, cre, cimn, apr, api, air, aii = _ssm_prep(are, aim, ldt, bxr, bxi, cxr, cxi)
    tabs = (bre, bim, cre, cimn, apr, api, air, aii)
    land_in = _cols_wait("gather_w_in_landed", spread, nsh_in, False, h1)
    passed = _cols_start("gather_w_in_pass", land_in, nsh_in, True, bre)
    wg_in = _cols_wait("gather_w_in_passed", passed, nsh_in, True, passed[3])
    tokens = [start_gather("ssm_glu_w", wg_in), start_gather("w_out", wg_in)]
    bn_i = n_in // 2
    (z,) = _mm("in_proj", "nn", (t // tb, n_in // bn_i, 1),
               (h1, _bs((tb, d), lambda i, j, k: (i, 0))),
               (wg_in, _bs((d, bn_i), lambda i, j, k: (0, j))),
               [(_sds((t, n_in), F32), _bs((tb, bn_i), lambda i, j, k: (i, j)))], deps=tokens)
    tokens = [start_gather("w_up", z), start_gather("w_down", z)]
    y_pre, yg_b, p_re, p_im = _ssm_fwd(z, dvec, *tabs, deps=tokens)

    def glu_ep(acc, yp, b):
        gate = _sigmoid(acc + b)
        return _gelu(yp) * gate, gate

    hw = SSM_WIDTH // 2
    wg_glu = gathered("ssm_glu_w", yg_b).reshape(SSM_WIDTH, SSM_WIDTH)
    tile_g = _bs((tb, hw), lambda i, j, k: (i, j))
    y_ssm, gate = _mm("glu", "nn", (t // tb, 2, 1),
                      (yg_b, _bs((tb, SSM_WIDTH), lambda i, j, k: (i, 0))),
                      (wg_glu, _bs((SSM_WIDTH, hw), lambda i, j, k: (0, j))),
                      [(_sds((t, SSM_WIDTH), F32), tile_g), (_sds((t, SSM_WIDTH), F32), tile_g)],
                      extras=[(y_pre, tile_g), (ssm_glu_b, _bs((1, hw), lambda i, j, k: (0, j)))],
                      epilogue=glu_ep)

    sgu_bexp = jnp.broadcast_to(sgu_b[0][:, :, None], (SGU_HEADS, SGU_CHUNK, SGU_CHUNK))
    y_sgu = _sgu_fwd(z, sgu_ln_g, sgu_ln_b, sgu_w[0], sgu_bexp)
    mixed = _mix_norm(y_ssm, y_sgu, out_norm_ssm_g, out_norm_sgu_g)

    tb2 = min(512, t)
    row2 = _bs((tb2, d), lambda i, j, k: (i, 0))
    vec2 = _bs((1, d), lambda i, j, k: (0, 0))
    vec_sum = (_sds((1, d), F32), vec2)
    wg_out = gathered("w_out", mixed).reshape(d, d)

    def out_ep(acc, r, g):
        x2v = acc + r
        return x2v, _rms_math(x2v, g)

    x2, h2 = _mm("out_proj", "nn", (t // tb2, 1, 1),
                 (mixed, row2), (wg_out, _bs((d, d), lambda i, j, k: (0, 0))),
                 [(_sds((t, d), F32), row2), (_sds((t, d), BF16), row2)],
                 extras=[(xs, row2), (norm_mlp_g, vec2)], epilogue=out_ep, ep_rows=EP_ROWS)

    def up_ep(acc):
        r = jnp.maximum(acc, 0.0)
        return r * r, r

    tile_f = _bs((tb, nsh_up), lambda i, j, k: (i, j))
    wg_up = gathered("w_up", h2)
    f_act, r_act = _mm("mlp_up", "nn", (t // tb, N_DEV, 1),
                       (h2, _bs((tb, d), lambda i, j, k: (i, 0))),
                       (wg_up, _bs((None, d, nsh_up), lambda i, j, k: (j, 0, 0))),
                       [(_sds((t, d_ff), BF16), tile_f), (_sds((t, d_ff), BF16), tile_f)],
                       epilogue=up_ep)
    bk_d, bn_o = 2048, 1024
    tile_o = _bs((tb, bn_o), lambda i, j, k: (i, j))
    wg_down = gathered("w_down", f_act).reshape(d_ff, d)
    (x3,) = _mm("mlp_down", "nn", (t // tb, d // bn_o, d_ff // bk_d),
                (f_act, _bs((tb, bk_d), lambda i, j, k: (i, k))),
                (wg_down, _bs((bk_d, bn_o), lambda i, j, k: (k, j))),
                [(_sds((t, d), F32), tile_o)],
                extras=[(x2, tile_o)], epilogue=lambda acc, r: (acc + r,))
    dx3, dx3_b, g_final, err2 = _final_loss(x3, target, norm_final_g.reshape(1, d))
    loss = lax.psum(0.5 * jnp.sum(err2) / d, MESH_AXES)

    sends = {}

    def send_grad(n, g, land_shape=None):
        sends[n] = _send_start("grad_start_" + n, g, lax.empty(land_shape or g.shape, BF16), True)
        return [sends[n][4]]

    bn_a = 1024
    tile_a = _bs((tb, bn_a), lambda i, j, k: (i, j))
    (da,) = _mm("mlp_down_dx", "nt", (t // tb, d_ff // bn_a, 1),
                (dx3_b, _bs((tb, d), lambda i, j, k: (i, 0))),
                (wg_down, _bs((bn_a, d), lambda i, j, k: (j, 0))),
                [(_sds((t, d_ff), BF16), tile_a)],
                extras=[(r_act, tile_a)], epilogue=lambda acc, r: (acc * (2.0 * r.astype(F32)),))
    sq = 1024
    (gw_down,) = _mm("mlp_down_dw", "tn", (d_ff // sq, 1, t // tb),
                     (f_act, _bs((tb, sq), lambda i, j, k: (k, i))),
                     (dx3_b, _bs((tb, d), lambda i, j, k: (k, 0))),
                     [(_sds((d_ff, d), BF16), _bs((sq, d), lambda i, j, k: (i, 0)))])
    sent = send_grad("w_down", gw_down.reshape(N_DEV, -1, d))
    (dh2,) = _mm("mlp_up_dx", "nt", (t // tb, 1, N_DEV),
                 (da, _bs((tb, nsh_up), lambda i, j, k: (i, k))),
                 (wg_up, _bs((None, d, nsh_up), lambda i, j, k: (k, 0, 0))),
                 [(_sds((t, d), F32), _bs((tb, d), lambda i, j, k: (i, 0)))], deps=sent)

    def norm_bwd_side(dh, xv, dres, g):
        dx, dg = _rms_bwd_math(dh, xv, g)
        dx = dx + dres
        return dx, dx, dg

    up_dw_grid = (1, N_DEV, t // tb)
    side_rows = (t // math.prod(up_dw_grid), d)
    gw_up, dx2, dx2_b, g_norm_mlp = _mm(
        "mlp_up_dw", "tn", up_dw_grid,
        (h2, _bs((tb, d), lambda i, j, k: (k, 0))),
        (da, _bs((tb, nsh_up), lambda i, j, k: (k, j))),
        [(_sds((N_DEV, d, nsh_up), BF16), _bs((None, d, nsh_up), lambda i, j, k: (j, 0, 0)))],
        side=(norm_bwd_side, [(dh2, side_rows), (x2, side_rows), (dx3, side_rows), (norm_mlp_g, (1, d))],
              [(_sds((t, d), F32), side_rows), (_sds((t, d), BF16), side_rows)], [(_sds((1, d), F32), (1, d))]))
    sent = send_grad("w_up", gw_up)

    tk = min(2048, t)
    (gw_out,) = _mm("out_proj_dw", "tn", (d // sq, d // sq, t // tk),
                    (mixed, _bs((tk, sq), lambda i, j, k: (k, i))),
                    (dx2_b, _bs((tk, sq), lambda i, j, k: (k, j))),
                    [(_sds((d, d), BF16), _bs((sq, sq), lambda i, j, k: (i, j)))], deps=sent)
    sent = send_grad("w_out", gw_out.reshape(N_DEV, -1, d))
    half2 = _bs((tb2, SSM_WIDTH), lambda i, j, k: (i, 0))
    vech = _bs((1, SSM_WIDTH), lambda i, j, k: (0, 0))
    half_sum = (_sds((1, SSM_WIDTH), F32), vech)

    def out_dx_ep(acc, ya, yb, ga, gb, yp, gt):
        dya, dga = _rms_bwd_math(acc[:, :SSM_WIDTH], ya, ga)
        dyb, dgb = _rms_bwd_math(acc[:, SSM_WIDTH:], yb, gb)
        dpre = dya * _gelu(yp) * gt * (1.0 - gt)
        return dya, dyb, dpre, dga, dgb, jnp.sum(dpre, axis=0, keepdims=True)

    dy_ssm, dy_sgu, dpre_b, g_onorm_ssm, g_onorm_sgu, g_glu_b = _mm(
        "out_proj_dx", "nt", (t // tb2, 1, 1),
        (dx2_b, row2), (wg_out, _bs((d, d), lambda i, j, k: (0, 0))),
        [(_sds((t, SSM_WIDTH), F32), half2), (_sds((t, SSM_WIDTH), F32), half2),
         (_sds((t, SSM_WIDTH), BF16), half2)],
        extras=[(y_ssm, half2), (y_sgu, half2), (out_norm_ssm_g, vech), (out_norm_sgu_g, vech),
                (y_pre, half2), (gate, half2)],
        epilogue=out_dx_ep, sums=[half_sum, half_sum, half_sum], deps=sent, acc_shape=(tb2, d),
        ep_rows=EP_ROWS)

    (gw_glu,) = _mm("glu_dw", "tn", (1, 1, t // tb),
                    (yg_b, _bs((tb, SSM_WIDTH), lambda i, j, k: (k, 0))),
                    (dpre_b, _bs((tb, SSM_WIDTH), lambda i, j, k: (k, 0))),
                    [(_sds((SSM_WIDTH, SSM_WIDTH), BF16), _bs((SSM_WIDTH, SSM_WIDTH), lambda i, j, k: (0, 0)))])
    sent = send_grad("ssm_glu_w", gw_glu.reshape(N_DEV, -1, SSM_WIDTH))
    (dy_pre,) = _mm("glu_dx", "nt", (t // tb, 2, 1),
                    (dpre_b, _bs((tb, SSM_WIDTH), lambda i, j, k: (i, 0))),
                    (wg_glu, _bs((hw, SSM_WIDTH), lambda i, j, k: (j, 0))),
                    [(_sds((t, SSM_WIDTH), F32), tile_g)],
                    extras=[(dy_ssm, tile_g), (gate, tile_g), (y_pre, tile_g)],
                    epilogue=lambda acc, dy, gt, yp: ((dy * gt + acc) * _gelu_grad(yp),), deps=sent)
    du_b, dbre, dbim, dcre, dcimn, q_re, q_im, dd = _ssm_bwd(dy_pre, z, p_re, p_im, dvec, *tabs)
    dare, daim, dldt, dbxr, dbxi = _ssm_prep_bwd(are, aim, ldt, bxr, bxi, dbre, dbim, q_re, q_im)

    dz_b, g_ln_g, g_ln_b, g_sgu_w, g_sgu_bx = _sgu_bwd(dy_sgu, du_b, z, sgu_ln_g, sgu_ln_b, sgu_w[0], sgu_bexp)
    local_small = {
        "ssm_a_re": dare, "ssm_a_im": daim,
        "ssm_b_re": _from_block_b(dbxr), "ssm_b_im": _from_block_b(dbxi),
        "ssm_c_re": _from_block_c(dcre), "ssm_c_im": -_from_block_c(dcimn),
        "ssm_d": dd, "ssm_log_dt": dldt.reshape(-1, SSM_STATE).sum(axis=-1),
        "ssm_glu_b": g_glu_b, "sgu_ln_g": g_ln_g, "sgu_ln_b": g_ln_b, "sgu_w": g_sgu_w,
        "sgu_b": g_sgu_bx[:, :, 0], "out_norm_ssm_g": g_onorm_ssm, "out_norm_sgu_g": g_onorm_sgu,
        "norm_mlp_g": g_norm_mlp, "norm_final_g": g_final,
    }
    small_early = [n for n in small if n in local_small]
    small_late = [n for n in small if n not in local_small]
    packed = _pack([local_small[n] for n in small_early])
    small_send = _send_start("small_start", packed, lax.empty((N_DEV,) + packed.shape, F32), False)

    def landed_parts(n, after):
        sent_blocks, landed = _send_wait("grad_wait_" + n, sends[n], after, True)
        if sent_blocks.ndim == landed.ndim:
            own = lax.dynamic_index_in_dim(sent_blocks, me, 0, keepdims=False)
        else:
            own = lax.dynamic_slice_in_dim(sent_blocks, me * landed.shape[2], landed.shape[2], axis=1)
        return _own_block(landed, own, me)

    in_dw_grid = (d // sq, n_in // bn_i, t // tb)
    riders = ["w_down", "w_up"]
    side_ins, side_outs = [], []
    for n in riders:
        rows, cols = given[n].shape[1:]
        blk = (rows // math.prod(in_dw_grid), cols)
        side_ins += [(landed_parts(n, dz_b), (N_DEV,) + blk), (given[n][0], blk),
                     (given["m_" + n][0], blk), (given["v_" + n][0], blk)]
        side_outs += [(_sds((rows, cols), F32), blk)] * 4

    def adam_side(*tiles):
        return sum((_adam_math(*tiles[4 * r:4 * r + 4]) for r in range(len(riders))), ())

    gw_in, *rider_res = _mm("in_proj_dw", "tn", in_dw_grid,
                            (h1, _bs((tb, sq), lambda i, j, k: (k, i))),
                            (dz_b, _bs((tb, bn_i), lambda i, j, k: (k, j))),
                            [(_sds((d, n_in), BF16), _bs((sq, bn_i), lambda i, j, k: (i, j)))],
                            deps=[small_send[4]], side=(adam_side, side_ins, side_outs, []))
    sent = send_grad("w_in", gw_in, (N_DEV, d, nsh_in))

    def in_dx_ep(acc, xv, dres, g):
        dx, dg = _rms_bwd_math(acc, xv, g)
        return dx + dres, dg

    grad_x, g_norm_mix = _mm("in_proj_dx", "nt", (t // tb2, 1, n_in // bn_i),
                             (dz_b, _bs((tb2, bn_i), lambda i, j, k: (i, k))),
                             (wg_in, _bs((d, bn_i), lambda i, j, k: (0, k))),
                             [(_sds((t, d), F32), row2)],
                             extras=[(xs, row2), (dx2, row2), (norm_mix_g, vec2)],
                             epilogue=in_dx_ep, sums=[vec_sum], deps=sent, ep_rows=EP_ROWS)

    (late_parts,) = _all_gather("gather_late_grads", [_pack([g_norm_mix])])
    packed, early_parts = _send_wait("small_wait", small_send, grad_x, False)
    early_parts = _own_block(early_parts, packed, me)

    grads, deltas, new_m, new_v = {}, {}, {}, {}
    for r, n in enumerate(riders):
        grads[n], deltas[n], new_m[n], new_v[n] = [a.reshape(given[n].shape) for a in rider_res[4 * r:4 * r + 4]]
    for n in big:
        if n in riders:
            continue
        res = _adamw("adamw_" + n, landed_parts(n, grad_x), given[n][0], given["m_" + n][0], given["v_" + n][0])
        grads[n], deltas[n], new_m[n], new_v[n] = [r.reshape(given[n].shape) for r in res]
    for tag, group, parts in (("early", small_early, early_parts), ("late", small_late, late_parts)):
        like = [given[n] for n in group]
        res = _adamw("adamw_small_" + tag, parts, _pack(like), _pack([given["m_" + n] for n in group]),
                     _pack([given["v_" + n] for n in group]))
        for store, buf in zip((grads, deltas, new_m, new_v), res):
            for n, a in zip(group, _unpack(buf, like)):
                store[n] = a

    return (loss, grad_x.reshape(x.shape), *[grads[n] for n in names], *[deltas[n] for n in names],
            *[new_m[n] for n in names], *[new_v[n] for n in names])
```

```python
import functools
import math

import jax
import jax.numpy as jnp
from jax import lax
from jax.experimental import pallas as pl
from jax.experimental.pallas import tpu as pltpu

F32, BF16 = jnp.float32, jnp.bfloat16
EPS = 1e-6
N_DEV = 8
D_MODEL = 2048
SSM_WIDTH = 1024
SSM_GROUP = 16
SSM_STATE = 64
SGU_HEADS = 8
SGU_CHUNK = 128
SGU_FWD_CHUNKS = 4
LANES = 128
SUBLANES = 8
N_SLAB = SSM_WIDTH // LANES
SLAB_STATE = (LANES // SSM_GROUP) * SSM_STATE
SCAN_BLOCK = 256
SLABS_PER_STEP = 8
VMEM_LIMIT = 56 * 1024 * 1024
ROW_BLOCK = 256
EP_ROWS = 128
ADAM_ROWS = 128
MESH_AXES = ("x", "y", "c")

ADAM_LR, ADAM_B1, ADAM_B2, ADAM_EPS, ADAM_WD, ADAM_STEP = 0.001, 0.9, 0.999, 1e-08, 0.01, 10

_GELU_C0 = math.sqrt(2.0 / math.pi)
_GELU_C1 = 0.044715


def _gelu(v):
    return 0.5 * v * (1.0 + jnp.tanh(_GELU_C0 * (v + _GELU_C1 * v * v * v)))


def _gelu_grad(v):
    th = jnp.tanh(_GELU_C0 * (v + _GELU_C1 * v * v * v))
    return 0.5 * (1.0 + th) + 0.5 * v * (1.0 - th * th) * _GELU_C0 * (1.0 + 3.0 * _GELU_C1 * v * v)


def _sigmoid(v):
    return 1.0 / (1.0 + jnp.exp(-v))


def _params(sem=None):
    return pltpu.CompilerParams(dimension_semantics=sem, vmem_limit_bytes=VMEM_LIMIT)


def _dot(a, b, mode="nn"):
    dims = {"nn": ((1,), (0,)), "nt": ((1,), (1,)), "tn": ((0,), (0,))}[mode]
    return lax.dot_general(a, b, (dims, ((), ())), preferred_element_type=F32)


def _mm(name, mode, grid, a, b, outs, extras=(), epilogue=None, deps=(), sums=(), acc_shape=None,
        ep_rows=None, side=None):
    nk = grid[2]
    n_ex, n_out, n_dep, n_sum = len(extras), len(outs), len(deps), len(sums)
    assert not sums or grid[1] == 1
    if acc_shape is None:
        acc_shape = tuple(d for d in outs[0][1].block_shape if d is not None)
    side_fn, side_ins, side_outs, side_sums = side or (None, (), (), ())
    s_in, s_out, s_sum = len(side_ins), len(side_outs), len(side_sums)

    def body(*refs):
        a_ref, b_ref = refs[0], refs[1]
        ex = refs[2:2 + n_ex]
        pos = 2 + n_ex
        side_in = refs[pos:pos + s_in]
        pos += s_in + n_dep
        out_refs = refs[pos:pos + n_out]
        sum_refs = refs[pos + n_out:pos + n_out + n_sum]
        pos += n_out + n_sum
        side_out = refs[pos:pos + s_out]
        side_sum = refs[pos + s_out:pos + s_out + s_sum]
        acc = refs[-1]
        k = pl.program_id(2)

        @pl.when(k == 0)
        def _():
            acc[...] = jnp.zeros_like(acc)

        if side_sum:
            @pl.when((pl.program_id(0) == 0) & (pl.program_id(1) == 0) & (k == 0))
            def _():
                for o in side_sum:
                    o[...] = jnp.zeros_like(o)

        acc[...] += _dot(a_ref[...], b_ref[...], mode)
        if side_fn is not None:
            res = side_fn(*[r[...] for r in side_in])
            for o, r in zip(side_out, res[:s_out]):
                o[...] = r.astype(o.dtype)
            for o, r in zip(side_sum, res[s_out:]):
                o[...] += r

        def finish(rows):
            args = [e[rows, :] if e.shape[0] == acc_shape[0] else e[...] for e in ex]
            res = acc[rows, :]
            res = (res,) if epilogue is None else epilogue(res, *args)
            for o, r in zip(out_refs, res[:n_out]):
                o[rows, :] = r.astype(o.dtype)
            return tuple(res[n_out:])

        @pl.when(k == nk - 1)
        def _():
            if ep_rows is None:
                terms = finish(slice(None))
            else:
                def chunk(c, tot):
                    rows = pl.ds(pl.multiple_of(c * ep_rows, ep_rows), ep_rows)
                    return tuple(s + r for s, r in zip(tot, finish(rows)))

                zero = tuple(jnp.zeros(o.shape, F32) for o in sum_refs)
                terms = lax.fori_loop(0, acc_shape[0] // ep_rows, chunk, zero)
            for o, r in zip(sum_refs, terms):
                _add_up(o, r, pl.program_id(0) == 0)

    def side_spec(block):
        nd = len(block)
        if nd == 2 and block[0] == 1:
            return _bs(block, lambda i, j, k: (0, 0))
        return _bs(block, lambda i, j, k: (0,) * (nd - 2) + ((i * grid[1] + j) * grid[2] + k, 0))

    sem = ("arbitrary",) * 3 if sums or side else ("parallel", "parallel", "arbitrary")
    res = pl.pallas_call(
        body, name=name, grid=grid,
        in_specs=[a[1], b[1]] + [e[1] for e in extras] + [side_spec(blk) for _, blk in side_ins]
        + [_any_spec()] * n_dep,
        out_specs=[o[1] for o in outs] + [o[1] for o in sums]
        + [side_spec(blk) for _, blk in side_outs] + [side_spec(blk) for _, blk in side_sums],
        out_shape=[o[0] for o in outs] + [o[0] for o in sums]
        + [o[0] for o in side_outs] + [o[0] for o in side_sums],
        scratch_shapes=[pltpu.VMEM(acc_shape, F32)],
        compiler_params=_params(sem),
    )(a[0], b[0], *[e[0] for e in extras], *[x for x, _ in side_ins], *deps)
    return res


def _add_up(ref, term, first):
    @pl.when(first)
    def _():
        ref[...] = term

    @pl.when(jnp.logical_not(first))
    def _():
        ref[...] += term


def _rms_math(xv, g):
    return xv * lax.rsqrt(jnp.mean(xv * xv, axis=-1, keepdims=True) + EPS) * g


def _rms_bwd_math(dy, xv, g):
    r = lax.rsqrt(jnp.mean(xv * xv, axis=-1, keepdims=True) + EPS)
    xhat = xv * r
    dxhat = dy * g
    dx = r * (dxhat - xhat * jnp.mean(dxhat * xhat, axis=-1, keepdims=True))
    return dx, jnp.sum(dy * xhat, axis=0, keepdims=True)


def _sds(shape, dtype):
    return jax.ShapeDtypeStruct(shape, dtype)


def _bs(shape, fn):
    return pl.BlockSpec(shape, fn)


def _rms_fwd(name, x, g, deps=()):
    t, w = x.shape
    br = min(ROW_BLOCK, t)

    def body(*refs):
        x_ref, g_ref, o_ref = refs[0], refs[1], refs[-1]
        xv = x_ref[...]
        r = lax.rsqrt(jnp.mean(xv * xv, axis=-1, keepdims=True) + EPS)
        o_ref[...] = (xv * r * g_ref[...]).astype(BF16)

    return pl.pallas_call(
        body, name=name, grid=(t // br,),
        in_specs=[_bs((br, w), lambda i: (i, 0)), _bs((1, w), lambda i: (0, 0))] + [_any_spec()] * len(deps),
        out_specs=_bs((br, w), lambda i: (i, 0)),
        out_shape=_sds((t, w), BF16),
        compiler_params=_params(("parallel",)),
    )(x, g, *deps)


def _mix_norm(ya, yb, ga, gb):
    t, w = ya.shape
    br = min(ROW_BLOCK, t)

    def body(a_ref, b_ref, ga_ref, gb_ref, o_ref):
        for src, g_ref, col in ((a_ref, ga_ref, 0), (b_ref, gb_ref, w)):
            v = src[...]
            r = lax.rsqrt(jnp.mean(v * v, axis=-1, keepdims=True) + EPS)
            o_ref[:, col:col + w] = (v * r * g_ref[...]).astype(BF16)

    row = _bs((br, w), lambda i: (i, 0))
    vec = _bs((1, w), lambda i: (0, 0))
    return pl.pallas_call(
        body, name="mix_norm", grid=(t // br,),
        in_specs=[row, row, vec, vec],
        out_specs=_bs((br, 2 * w), lambda i: (i, 0)),
        out_shape=_sds((t, 2 * w), BF16),
        compiler_params=_params(("parallel",)),
    )(ya, yb, ga, gb)


def _loss_math(xv, target, g):
    r = lax.rsqrt(jnp.mean(xv * xv, axis=-1, keepdims=True) + EPS)
    xhat = xv * r
    err = xhat * g - target
    dy = err * (1.0 / xv.shape[-1])
    dxhat = dy * g
    dx = r * (dxhat - xhat * jnp.mean(dxhat * xhat, axis=-1, keepdims=True))
    return dx, jnp.sum(dy * xhat, axis=0, keepdims=True), jnp.sum(err * err, axis=0, keepdims=True)


def _final_loss(x3, target, g):
    t, w = x3.shape
    br = min(ROW_BLOCK, t)

    def body(x_ref, tg_ref, g_ref, dx_ref, dxb_ref, dg_ref, l_ref):
        dx, dg, e2 = _loss_math(x_ref[...], tg_ref[...], g_ref[...])
        dx_ref[...] = dx
        dxb_ref[...] = dx.astype(BF16)
        _add_up(dg_ref, dg, pl.program_id(0) == 0)
        _add_up(l_ref, e2, pl.program_id(0) == 0)

    row = _bs((br, w), lambda i: (i, 0))
    vec = _bs((1, w), lambda i: (0, 0))
    return pl.pallas_call(
        body, name="final_loss", grid=(t // br,),
        in_specs=[row, row, vec], out_specs=[row, row, vec, vec],
        out_shape=[_sds((t, w), F32), _sds((t, w), BF16), _sds((1, w), F32), _sds((1, w), F32)],
        compiler_params=_params(("arbitrary",)),
    )(x3, target, g)


def _prep_math(are, aim, ldt, bxr, bxi):
    dt = jnp.exp(ldt)
    er = jnp.exp(are * dt)
    th = aim * dt
    abr, abi = er * jnp.cos(th), er * jnp.sin(th)
    nr, ni = abr - 1.0, abi
    den = are * are + aim * aim
    cr = (nr * are + ni * aim) / den
    ci = (ni * are - nr * aim) / den
    bbr, bbi = [], []
    for j in range(N_SLAB):
        sl = slice(j * SLAB_STATE, (j + 1) * SLAB_STATE)
        bbr.append(cr[:, sl] * bxr[j] - ci[:, sl] * bxi[j])
        bbi.append(cr[:, sl] * bxi[j] + ci[:, sl] * bxr[j])
    return abr, abi, bbr, bbi


def _ssm_prep(are, aim, ldt, bxr, bxi, cxr, cxi):
    nst = N_SLAB * SLAB_STATE

    def body(are_r, aim_r, ldt_r, bxr_r, bxi_r, cxr_r, cxi_r,
             bre_o, bim_o, cre_o, cimn_o, apr_o, api_o, air_o, aii_o):
        abr, abi, bbr, bbi = _prep_math(are_r[...], aim_r[...], ldt_r[...],
                                        [bxr_r[j] for j in range(N_SLAB)], [bxi_r[j] for j in range(N_SLAB)])
        for j in range(N_SLAB):
            bre_o[j] = bbr[j].astype(BF16)
            bim_o[j] = bbi[j].astype(BF16)
        cre_o[...] = cxr_r[...].astype(BF16)
        cimn_o[...] = (-cxi_r[...]).astype(BF16)

        def step(k, cur):
            cr, ci = cur
            den = cr * cr + ci * ci
            apr_o[pl.ds(k, 1), :] = cr
            api_o[pl.ds(k, 1), :] = ci
            air_o[pl.ds(k, 1), :] = cr / den
            aii_o[pl.ds(k, 1), :] = -ci / den
            return cr * abr - ci * abi, cr * abi + ci * abr

        lax.fori_loop(0, SCAN_BLOCK, step, (jnp.ones((1, nst), F32), jnp.zeros((1, nst), F32)))

    tab = _sds((SCAN_BLOCK, nst), F32)
    return pl.pallas_call(
        body, name="ssm_prep",
        out_shape=[_sds(bxr.shape, BF16), _sds(bxr.shape, BF16), _sds(cxr.shape, BF16), _sds(cxr.shape, BF16),
                   tab, tab, tab, tab],
        compiler_params=_params(),
    )(are, aim, ldt, bxr, bxi, cxr, cxi)


def _ssm_prep_bwd(are, aim, ldt, bxr, bxi, dbre, dbim, qr, qi):
    def body(are_r, aim_r, ldt_r, bxr_r, bxi_r, dbre_r, dbim_r, qr_r, qi_r,
             dare_o, daim_o, dldt_o, dbxr_o, dbxi_o):
        prim = (are_r[...], aim_r[...], ldt_r[...],
                [bxr_r[j] for j in range(N_SLAB)], [bxi_r[j] for j in range(N_SLAB)])
        (abr, abi, _, _), vjp = jax.vjp(_prep_math, *prim)
        den = abr * abr + abi * abi
        q_r, q_i = qr_r[...], qi_r[...]
        gar = (q_r * abr - q_i * abi) / den
        gai = (q_r * abi + q_i * abr) / den
        ct = (gar, gai, [dbre_r[j] for j in range(N_SLAB)], [dbim_r[j] for j in range(N_SLAB)])
        dare, daim, dldt, dbxr, dbxi = vjp(ct)
        dare_o[...] = dare
        daim_o[...] = daim
        dldt_o[...] = dldt
        for j in range(N_SLAB):
            dbxr_o[j] = dbxr[j]
            dbxi_o[j] = dbxi[j]

    row = _sds(are.shape, F32)
    return pl.pallas_call(
        body, name="ssm_prep_bwd",
        out_shape=[row, row, row, _sds(bxr.shape, F32), _sds(bxr.shape, F32)],
        compiler_params=_params(),
    )(are, aim, ldt, bxr, bxi, dbre, dbim, qr, qi)


def _tri(lower):
    r = lax.broadcasted_iota(jnp.int32, (SCAN_BLOCK, SCAN_BLOCK), 0)
    c = lax.broadcasted_iota(jnp.int32, (SCAN_BLOCK, SCAN_BLOCK), 1)
    return jnp.where((r >= c) if lower else (r <= c), 1.0, 0.0).astype(BF16)


def _cumsum_mxu(tri, v):
    return _dot(tri, v.astype(BF16))


def _ssm_specs(t, sps):
    nt = t // SCAN_BLOCK
    tab = _bs((SCAN_BLOCK, sps * SLAB_STATE), lambda j, i: (0, j))
    bmat = _bs((sps, LANES, SLAB_STATE), lambda j, i: (j, 0, 0))
    cmat = _bs((sps, SLAB_STATE, LANES), lambda j, i: (j, 0, 0))
    return nt, tab, bmat, cmat


def _slab_slices(s):
    return slice(s * LANES, (s + 1) * LANES), slice(s * SLAB_STATE, (s + 1) * SLAB_STATE)


def _ssm_fwd(z, dvec, bre, bim, cre, cimn, apr, api, air, aii, deps=()):
    t = z.shape[0]
    sps = SLABS_PER_STEP
    nt, tab, bmat, cmat = _ssm_specs(t, sps)
    nst = N_SLAB * SLAB_STATE
    last = SCAN_BLOCK - 1

    def body(*refs):
        u_ref, d_ref, bre_r, bim_r, cre_r, cimn_r, apr_r, api_r, air_r, aii_r = refs[:10]
        y_ref, yg_ref, pr_ref, pi_ref, car_r, car_i = refs[10 + len(deps):]

        @pl.when(pl.program_id(1) == 0)
        def _():
            car_r[...] = jnp.zeros_like(car_r)
            car_i[...] = jnp.zeros_like(car_i)

        tri = _tri(True)
        for s in range(sps):
            ul, sl = _slab_slices(s)
            u = u_ref[:, ul]
            ub = u.astype(BF16)
            bur, bui = _dot(ub, bre_r[s]), _dot(ub, bim_r[s])
            ir, ii = air_r[:, sl], aii_r[:, sl]
            csr = _cumsum_mxu(tri, ir * bur - ii * bui)
            csi = _cumsum_mxu(tri, ir * bui + ii * bur)
            pr, pi = apr_r[:, sl], api_r[:, sl]
            a_r, a_i = apr_r[1:2, sl], api_r[1:2, sl]
            c_r, c_i = car_r[:, sl], car_i[:, sl]
            wr = csr + (a_r * c_r - a_i * c_i)
            wi = csi + (a_r * c_i + a_i * c_r)
            sr = pr * wr - pi * wi
            si = pr * wi + pi * wr
            car_r[:, sl] = sr[last:last + 1, :]
            car_i[:, sl] = si[last:last + 1, :]
            pr_ref[:, sl] = (sr - bur).astype(BF16)
            pi_ref[:, sl] = (si - bui).astype(BF16)
            y = _dot(sr.astype(BF16), cre_r[s]) + _dot(si.astype(BF16), cimn_r[s]) + d_ref[:, ul] * u
            y_ref[:, ul] = y
            yg_ref[:, ul] = _gelu(y).astype(BF16)

    ublk = _bs((SCAN_BLOCK, sps * LANES), lambda j, i: (i, j))
    sblk = _bs((SCAN_BLOCK, sps * SLAB_STATE), lambda j, i: (i, j))
    return pl.pallas_call(
        body, name="ssm_fwd", grid=(N_SLAB // sps, nt),
        in_specs=[ublk, _bs((1, sps * LANES), lambda j, i: (0, j)), bmat, bmat, cmat, cmat, tab, tab, tab, tab]
        + [_any_spec()] * len(deps),
        out_specs=[ublk, ublk, sblk, sblk],
        out_shape=[_sds((t, SSM_WIDTH), F32), _sds((t, SSM_WIDTH), BF16),
                   _sds((t, nst), BF16), _sds((t, nst), BF16)],
        scratch_shapes=[pltpu.VMEM((1, sps * SLAB_STATE), F32), pltpu.VMEM((1, sps * SLAB_STATE), F32)],
        compiler_params=_params(("parallel", "arbitrary")),
    )(z, dvec, bre, bim, cre, cimn, apr, api, air, aii, *deps)


def _ssm_bwd(gy, z, p_re, p_im, dvec, bre, bim, cre, cimn, apr, api, air, aii):
    t = z.shape[0]
    sps = SLABS_PER_STEP
    nt, tab, bmat, cmat = _ssm_specs(t, sps)
    last = SCAN_BLOCK - 1

    def fold(v):
        return v.reshape(SCAN_BLOCK // SUBLANES, SUBLANES, v.shape[-1]).sum(axis=0)

    def body(g_ref, u_ref, pr_ref, pi_ref, d_ref, bre_r, bim_r, cre_r, cimn_r, apr_r, api_r, air_r, aii_r,
             du_ref, dbre_o, dbim_o, dcre_o, dcimn_o, qr_o, qi_o, dd_o, car_r, car_i, qacc_r, qacc_i, dacc):
        i = pl.program_id(1)

        @pl.when(i == 0)
        def _():
            for ref in (car_r, car_i, qacc_r, qacc_i, dacc, dbre_o, dbim_o, dcre_o, dcimn_o):
                ref[...] = jnp.zeros_like(ref)

        tri = _tri(False)
        for s in range(sps):
            ul, sl = _slab_slices(s)
            g = g_ref[:, ul]
            gb = g.astype(BF16)
            u = u_ref[:, ul]
            ub = u.astype(BF16)
            bur, bui = _dot(ub, bre_r[s]), _dot(ub, bim_r[s])
            p_r, p_i = pr_ref[:, sl].astype(F32), pi_ref[:, sl].astype(F32)
            srb, sib = (p_r + bur).astype(BF16), (p_i + bui).astype(BF16)
            dcre_o[s] += _dot(srb, gb, "tn")
            dcimn_o[s] += _dot(sib, gb, "tn")
            dsr, dsi = _dot(gb, cre_r[s], "nt"), _dot(gb, cimn_r[s], "nt")
            pr, pi = apr_r[:, sl], api_r[:, sl]
            csr = _cumsum_mxu(tri, pr * dsr + pi * dsi)
            csi = _cumsum_mxu(tri, pr * dsi - pi * dsr)
            al_r, al_i = apr_r[last:last + 1, sl], api_r[last:last + 1, sl]
            c_r, c_i = car_r[:, sl], car_i[:, sl]
            wr = csr + (al_r * c_r + al_i * c_i)
            wi = csi + (al_r * c_i - al_i * c_r)
            ir, ii = air_r[:, sl], aii_r[:, sl]
            lr = ir * wr + ii * wi
            li = ir * wi - ii * wr
            a_r, a_i = apr_r[1:2, sl], api_r[1:2, sl]
            car_r[:, sl] = a_r * lr[0:1, :] + a_i * li[0:1, :]
            car_i[:, sl] = a_r * li[0:1, :] - a_i * lr[0:1, :]
            lrb, lib = lr.astype(BF16), li.astype(BF16)
            dbre_o[s] += _dot(ub, lrb, "tn")
            dbim_o[s] += _dot(ub, lib, "tn")
            du = d_ref[:, ul] * g + _dot(lrb, bre_r[s], "nt") + _dot(lib, bim_r[s], "nt")
            du_ref[:, ul] = du.astype(BF16)
            qacc_r[:, sl] += fold(lr * p_r + li * p_i)
            qacc_i[:, sl] += fold(li * p_r - lr * p_i)
            dacc[:, ul] += fold(g * u)

        @pl.when(i == nt - 1)
        def _():
            qr_o[...] = jnp.sum(qacc_r[...], axis=0, keepdims=True)
            qi_o[...] = jnp.sum(qacc_i[...], axis=0, keepdims=True)
            dd_o[...] = jnp.sum(dacc[...], axis=0, keepdims=True)

    rev = lambda j, i: (nt - 1 - i, j)
    ublk = _bs((SCAN_BLOCK, sps * LANES), rev)
    sblk = _bs((SCAN_BLOCK, sps * SLAB_STATE), rev)
    qrow = _bs((1, sps * SLAB_STATE), lambda j, i: (0, j))
    urow = _bs((1, sps * LANES), lambda j, i: (0, j))
    nst = N_SLAB * SLAB_STATE
    return pl.pallas_call(
        body, name="ssm_bwd", grid=(N_SLAB // sps, nt),
        in_specs=[ublk, ublk, sblk, sblk, urow, bmat, bmat, cmat, cmat, tab, tab, tab, tab],
        out_specs=[ublk, bmat, bmat, cmat, cmat, qrow, qrow, urow],
        out_shape=[_sds((t, SSM_WIDTH), BF16),
                   _sds((N_SLAB, LANES, SLAB_STATE), F32), _sds((N_SLAB, LANES, SLAB_STATE), F32),
                   _sds((N_SLAB, SLAB_STATE, LANES), F32), _sds((N_SLAB, SLAB_STATE, LANES), F32),
                   _sds((1, nst), F32), _sds((1, nst), F32), _sds((1, SSM_WIDTH), F32)],
        scratch_shapes=[pltpu.VMEM((1, sps * SLAB_STATE), F32), pltpu.VMEM((1, sps * SLAB_STATE), F32),
                        pltpu.VMEM((SUBLANES, sps * SLAB_STATE), F32), pltpu.VMEM((SUBLANES, sps * SLAB_STATE), F32),
                        pltpu.VMEM((SUBLANES, sps * LANES), F32)],
        compiler_params=_params(("parallel", "arbitrary")),
    )(gy, z, p_re, p_im, dvec, bre, bim, cre, cimn, apr, api, air, aii)


def _sgu_mask():
    r = lax.broadcasted_iota(jnp.int32, (SGU_CHUNK, SGU_CHUNK), 0)
    c = lax.broadcasted_iota(jnp.int32, (SGU_CHUNK, SGU_CHUNK), 1)
    return r >= c


def _sgu_common(zu, zv, lng, lnb):
    us, v = _gelu(zu), _gelu(zv)
    mu = jnp.mean(v, axis=-1, keepdims=True)
    vc = v - mu
    rstd = lax.rsqrt(jnp.mean(vc * vc, axis=-1, keepdims=True) + EPS)
    xhat = vc * rstd
    return us, xhat, rstd, xhat * lng + lnb


def _sgu_fwd(z, lng, lnb, w, bexp):
    t = z.shape[0]
    hd = SGU_CHUNK
    rows = min(SGU_FWD_CHUNKS * SGU_CHUNK, t)

    def body(zu_ref, zv_ref, lng_ref, lnb_ref, w_ref, b_ref, y_ref):
        mask = _sgu_mask()
        wts = [jnp.where(mask, w_ref[h], 0.0).astype(BF16) for h in range(SGU_HEADS)]
        for c in range(rows // SGU_CHUNK):
            rs = slice(c * SGU_CHUNK, (c + 1) * SGU_CHUNK)
            us, _, _, vn = _sgu_common(zu_ref[rs, :], zv_ref[rs, :], lng_ref[...], lnb_ref[...])
            vnb = vn.astype(BF16)
            for h in range(SGU_HEADS):
                sl = slice(h * hd, (h + 1) * hd)
                y_ref[rs, sl] = us[:, sl] * (_dot(wts[h], vnb[:, sl]) + b_ref[h])

    row = lambda c: _bs((rows, SSM_WIDTH), lambda i: (i, c))
    vec = _bs((1, SSM_WIDTH), lambda i: (0, 0))
    hmat = _bs((SGU_HEADS, hd, hd), lambda i: (0, 0, 0))
    return pl.pallas_call(
        body, name="sgu_fwd", grid=(t // rows,),
        in_specs=[row(1), row(2), vec, vec, hmat, hmat],
        out_specs=row(0), out_shape=_sds((t, SSM_WIDTH), F32),
        compiler_params=_params(("parallel",)),
    )(z, z, lng, lnb, w, bexp)


def _sgu_bwd(dy, du_ssm, z, lng, lnb, w, bexp, deps=()):
    t = z.shape[0]
    hd = SGU_CHUNK
    nc = t // SGU_CHUNK

    def body(*refs):
        dy_ref, dus_ref, zu_ref, zv_ref, lng_ref, lnb_ref, w_ref, b_ref = refs[:8]
        dz_ref, dlng_o, dlnb_o, dw_o, db_o = refs[8 + len(deps):]
        i = pl.program_id(0)

        @pl.when(i == 0)
        def _():
            for ref in (dlng_o, dlnb_o, dw_o, db_o):
                ref[...] = jnp.zeros_like(ref)

        zu, zv = zu_ref[...], zv_ref[...]
        lng = lng_ref[...]
        us, xhat, rstd, vn = _sgu_common(zu, zv, lng, lnb_ref[...])
        vnb = vn.astype(BF16)
        dyv = dy_ref[...]
        mask = _sgu_mask()
        dus_parts, dvn_parts = [], []
        for h in range(SGU_HEADS):
            sl = slice(h * hd, (h + 1) * hd)
            wt = jnp.where(mask, w_ref[h], 0.0).astype(BF16)
            mixed = _dot(wt, vnb[:, sl]) + b_ref[h]
            dus_parts.append(dyv[:, sl] * mixed)
            dmix = dyv[:, sl] * us[:, sl]
            dmb = dmix.astype(BF16)
            db_o[h] += dmix
            dw_o[h] += _dot(dmb, vnb[:, sl], "nt")
            dvn_parts.append(_dot(wt, dmb, "tn"))
        dus = jnp.concatenate(dus_parts, axis=1)
        dvn = jnp.concatenate(dvn_parts, axis=1)
        dlng_o[...] += jnp.sum(dvn * xhat, axis=0, keepdims=True)
        dlnb_o[...] += jnp.sum(dvn, axis=0, keepdims=True)
        dxh = dvn * lng
        dv = rstd * (dxh - jnp.mean(dxh, axis=-1, keepdims=True)
                     - xhat * jnp.mean(dxh * xhat, axis=-1, keepdims=True))
        dz_ref[:, 0:SSM_WIDTH] = dus_ref[...]
        dz_ref[:, SSM_WIDTH:2 * SSM_WIDTH] = (dus * _gelu_grad(zu)).astype(BF16)
        dz_ref[:, 2 * SSM_WIDTH:] = (dv * _gelu_grad(zv)).astype(BF16)

        @pl.when(i == nc - 1)
        def _():
            for h in range(SGU_HEADS):
                dw_o[h] = jnp.where(mask, dw_o[h], 0.0)
                db_o[h] = jnp.broadcast_to(jnp.sum(db_o[h], axis=1, keepdims=True), (hd, hd))

    row = lambda c: _bs((SGU_CHUNK, SSM_WIDTH), lambda i: (i, c))
    vec = _bs((1, SSM_WIDTH), lambda i: (0, 0))
    hmat = _bs((SGU_HEADS, hd, hd), lambda i: (0, 0, 0))
    return pl.pallas_call(
        body, name="sgu_bwd", grid=(nc,),
        in_specs=[row(0), row(0), row(1), row(2), vec, vec, hmat, hmat] + [_any_spec()] * len(deps),
        out_specs=[_bs((SGU_CHUNK, 3 * SSM_WIDTH), lambda i: (i, 0)), vec, vec, hmat, hmat],
        out_shape=[_sds((t, 3 * SSM_WIDTH), BF16), _sds((1, SSM_WIDTH), F32), _sds((1, SSM_WIDTH), F32),
                   _sds((SGU_HEADS, hd, hd), F32), _sds((SGU_HEADS, hd, hd), F32)],
        compiler_params=_params(("arbitrary",)),
    )(dy, du_ssm, z, z, lng, lnb, w, bexp, *deps)


def _place():
    x, y, c = (lax.axis_index(a) for a in MESH_AXES)
    return x, y, c


def _index(p):
    return 4 * p[0] + 2 * p[1] + p[2]


def _any_spec():
    return pl.BlockSpec(memory_space=pl.ANY)


def _col_block(ref, k, width):
    return ref.at[:, pl.ds(pl.multiple_of(k * width, LANES), width)]


def _all_gather(name, shards, by_columns=False):
    n = len(shards)

    def body(*refs):
        ins, outs = refs[:n], refs[n:2 * n]
        send, recv, loc = refs[2 * n:]
        x, y, c = _place()
        me, sib = (x, y, c), (x, y, 1 - c)
        chips = [(1 - x, y), (x, 1 - y), (1 - x, 1 - y)]

        def blk(w, p):
            if by_columns:
                return _col_block(outs[w], _index(p), shards[w].shape[1])
            return outs[w].at[_index(p)]

        def cp(w, k, block, to, src=None):
            dst = blk(w, block)
            return pltpu.make_async_remote_copy(
                src_ref=dst if src is None else src, dst_ref=dst,
                send_sem=send.at[w * 7 + k], recv_sem=recv.at[w * 7 + k],
                device_id=to, device_id_type=pl.DeviceIdType.MESH)

        mines, sends = [], []
        for w in range(n):
            m = pltpu.make_async_copy(ins[w], blk(w, me), loc.at[w])
            m.start()
            mines.append(m)
            first = [cp(w, 0, me, sib, src=ins[w])]
            first += [cp(w, 1 + j, me, (*chip, c), src=ins[w]) for j, chip in enumerate(chips)]
            for q in first:
                q.start()
            sends += first
        for j, chip in enumerate(chips):
            for w in range(n):
                cp(w, 1 + j, (*chip, c), me).wait_recv()
                q = cp(w, 4 + j, (*chip, c), sib)
                q.start()
                sends.append(q)
        for w in range(n):
            cp(w, 0, sib, me).wait_recv()
            for j, chip in enumerate(chips):
                cp(w, 4 + j, (*chip, 1 - c), me).wait_recv()
        for q in sends:
            q.wait_send()
        for m in mines:
            m.wait()

    return pl.pallas_call(
        body, name=name,
        in_specs=[_any_spec()] * n, out_specs=[_any_spec()] * n,
        out_shape=[_sds((s.shape[0], N_DEV * s.shape[1]) if by_columns else (N_DEV,) + s.shape, s.dtype)
                   for s in shards],
        scratch_shapes=[pltpu.SemaphoreType.DMA((n * 7,)), pltpu.SemaphoreType.DMA((n * 7,)),
                        pltpu.SemaphoreType.DMA((n,))],
        compiler_params=pltpu.CompilerParams(has_side_effects=True),
    )(*shards)


def _peer(r, x, y, c):
    return ((1 - x) if r & 4 else x, (1 - y) if r & 2 else y, (1 - c) if r & 1 else c)


def _sent_block(src_ref, land_ref, k, scatter):
    if not scatter:
        return src_ref
    if len(src_ref.shape) == len(land_ref.shape):
        return src_ref.at[k]
    return _col_block(src_ref, k, land_ref.shape[2])


def _send_start(name, src, land, scatter, after=None):
    n_after = 0 if after is None else 1

    def body(*refs):
        src_ref, land_ref = refs[0], refs[1]
        send, recv, _, _, token = refs[2 + n_after:]
        x, y, c = _place()
        me = _index((x, y, c))
        for r in range(1, N_DEV):
            p = _peer(r, x, y, c)
            pltpu.make_async_remote_copy(
                src_ref=_sent_block(src_ref, land_ref, _index(p), scatter), dst_ref=land_ref.at[me],
                send_sem=send.at[r - 1], recv_sem=recv.at[r - 1],
                device_id=p, device_id_type=pl.DeviceIdType.MESH).start()
        token[...] = jnp.zeros_like(token)

    hbm, sem = pl.BlockSpec(memory_space=pltpu.HBM), pl.BlockSpec(memory_space=pltpu.SEMAPHORE)
    return pl.pallas_call(
        body, name=name,
        out_shape=(pltpu.SemaphoreType.DMA((N_DEV - 1,)), pltpu.SemaphoreType.DMA((N_DEV - 1,)),
                   pltpu.HBM(src.shape, src.dtype), pltpu.HBM(land.shape, land.dtype),
                   _sds((SUBLANES, LANES), F32)),
        in_specs=(hbm, hbm) + (_any_spec(),) * n_after,
        out_specs=(sem, sem, hbm, hbm, pl.BlockSpec(memory_space=pltpu.VMEM)),
        input_output_aliases={0: 2, 1: 3},
        compiler_params=pltpu.CompilerParams(has_side_effects=pltpu.SideEffectType.DATAFLOW_SIDE_EFFECTING),
    )(pltpu.with_memory_space_constraint(src, pltpu.HBM), pltpu.with_memory_space_constraint(land, pltpu.HBM),
      *([] if after is None else [after]))


def _send_wait(name, started, after, scatter):
    send, recv, src_thru, land_thru, _ = started

    def body(src_ref, land_ref, send_r, recv_r, after_ref, src_out, land_out):
        x, y, c = _place()
        for r in range(1, N_DEV):
            p = _peer(r, x, y, c)
            k = _index(p)
            cp = pltpu.make_async_remote_copy(
                src_ref=_sent_block(src_ref, land_ref, k, scatter), dst_ref=land_ref.at[k],
                send_sem=send_r.at[r - 1], recv_sem=recv_r.at[r - 1],
                device_id=p, device_id_type=pl.DeviceIdType.MESH)
            cp.wait_send()
            cp.wait_recv()

    hbm, sem = pl.BlockSpec(memory_space=pltpu.HBM), pl.BlockSpec(memory_space=pltpu.SEMAPHORE)
    return pl.pallas_call(
        body, name=name,
        out_shape=(pltpu.HBM(src_thru.shape, src_thru.dtype), pltpu.HBM(land_thru.shape, land_thru.dtype)),
        in_specs=(hbm, hbm, sem, sem, _any_spec()), out_specs=(hbm, hbm),
        input_output_aliases={0: 0, 1: 1},
        compiler_params=pltpu.CompilerParams(has_side_effects=pltpu.SideEffectType.DATAFLOW_SIDE_EFFECTING),
    )(src_thru, land_thru, send, recv, after)


SPREAD, PASS_ON = (1, 2, 4, 6), (2, 4, 6)


def _cols_start(name, land, width, pass_on, after):
    peers = PASS_ON if pass_on else SPREAD

    def body(land_ref, after_ref, send, recv, land_thru, token):
        x, y, c = _place()
        for n, r in enumerate(peers):
            block = _col_block(land_ref, _index(_peer(r, x, y, c) if pass_on else (x, y, c)), width)
            pltpu.make_async_remote_copy(
                src_ref=block if pass_on else after_ref, dst_ref=block,
                send_sem=send.at[n], recv_sem=recv.at[n],
                device_id=_peer(1, x, y, c) if pass_on else _peer(r, x, y, c),
                device_id_type=pl.DeviceIdType.MESH).start()
        token[...] = jnp.zeros_like(token)

    hbm, sem = pl.BlockSpec(memory_space=pltpu.HBM), pl.BlockSpec(memory_space=pltpu.SEMAPHORE)
    return pl.pallas_call(
        body, name=name,
        out_shape=(pltpu.SemaphoreType.DMA((len(peers),)), pltpu.SemaphoreType.DMA((len(peers),)),
                   pltpu.HBM(land.shape, land.dtype), _sds((SUBLANES, LANES), F32)),
        in_specs=(hbm, _any_spec()), out_specs=(sem, sem, hbm, pl.BlockSpec(memory_space=pltpu.VMEM)),
        input_output_aliases={0: 2},
        compiler_params=pltpu.CompilerParams(has_side_effects=pltpu.SideEffectType.DATAFLOW_SIDE_EFFECTING),
    )(pltpu.with_memory_space_constraint(land, pltpu.HBM), after)


def _cols_wait(name, started, width, pass_on, after):
    send, recv, land_thru, _ = started
    peers = PASS_ON if pass_on else SPREAD

    def body(land_ref, send_r, recv_r, after_ref, land_out):
        x, y, c = _place()
        for n, r in enumerate(peers):
            sent = _col_block(land_ref, _index(_peer(r, x, y, c) if pass_on else (x, y, c)), width)
            came = _col_block(land_ref, _index(_peer(r ^ 1 if pass_on else r, x, y, c)), width)
            cp = pltpu.make_async_remote_copy(
                src_ref=sent, dst_ref=came, send_sem=send_r.at[n], recv_sem=recv_r.at[n],
                device_id=_peer(1, x, y, c) if pass_on else _peer(r, x, y, c),
                device_id_type=pl.DeviceIdType.MESH)
            cp.wait_send()
            cp.wait_recv()

    hbm, sem = pl.BlockSpec(memory_space=pltpu.HBM), pl.BlockSpec(memory_space=pltpu.SEMAPHORE)
    return pl.pallas_call(
        body, name=name, out_shape=(pltpu.HBM(land_thru.shape, land_thru.dtype),),
        in_specs=(hbm, sem, sem, _any_spec()), out_specs=(hbm,), input_output_aliases={0: 0},
        compiler_params=pltpu.CompilerParams(has_side_effects=pltpu.SideEffectType.DATAFLOW_SIDE_EFFECTING),
    )(land_thru, send, recv, after)[0]


def _own_block(blocks, block, me):
    return lax.dynamic_update_index_in_dim(blocks, block, me, 0)


def _adam_math(parts, w, m, v):
    c1 = 1.0 / (1.0 - ADAM_B1 ** ADAM_STEP)
    c2 = 1.0 / (1.0 - ADAM_B2 ** ADAM_STEP)
    g = parts[0].astype(F32)
    for k in range(1, N_DEV):
        g = g + parts[k].astype(F32)
    mn = ADAM_B1 * m + (1.0 - ADAM_B1) * g
    vn = ADAM_B2 * v + (1.0 - ADAM_B2) * (g * g)
    return g, -ADAM_LR * ((mn * c1) / (jnp.sqrt(vn * c2) + ADAM_EPS) + ADAM_WD * w), mn, vn


def _adamw(name, parts, w, m, v):
    rows, cols = w.shape
    br = min(ADAM_ROWS, rows)

    def body(p_ref, w_ref, m_ref, v_ref, g_o, d_o, m_o, v_o):
        g_o[...], d_o[...], m_o[...], v_o[...] = _adam_math(p_ref[...], w_ref[...], m_ref[...], v_ref[...])

    blk = _bs((br, cols), lambda i: (i, 0))
    out = _sds((rows, cols), F32)
    return pl.pallas_call(
        body, name=name, grid=(rows // br,),
        in_specs=[_bs((N_DEV, br, cols), lambda i: (0, i, 0)), blk, blk, blk],
        out_specs=[blk] * 4, out_shape=[out] * 4,
        compiler_params=_params(("parallel",)),
    )(parts, w, m, v)


def _pack(arrs):
    tile = SUBLANES * LANES
    flat = []
    for a in arrs:
        f = a.reshape(-1).astype(F32)
        pad = (-f.shape[0]) % tile
        flat.append(jnp.pad(f, (0, pad)) if pad else f)
    total = sum(f.shape[0] for f in flat)
    tail = (-total) % (ADAM_ROWS * LANES)
    if tail:
        flat.append(jnp.zeros((tail,), F32))
    return jnp.concatenate(flat).reshape(-1, LANES)


def _unpack(buf, like):
    tile = SUBLANES * LANES
    flat = buf.reshape(-1)
    out, off = [], 0
    for a in like:
        n = math.prod(a.shape)
        out.append(flat[off:off + n].reshape(a.shape))
        off += n + ((-n) % tile)
    return out


def _to_block_b(b):
    gl = LANES // SSM_GROUP
    tb = b.reshape(N_SLAB, gl, SSM_STATE, SSM_GROUP).transpose(0, 1, 3, 2)
    eye = jnp.eye(gl, dtype=F32)
    return (tb[:, :, :, None, :] * eye[None, :, None, :, None]).reshape(N_SLAB, LANES, SLAB_STATE)


def _from_block_b(bx):
    gl = LANES // SSM_GROUP
    d = jnp.einsum("jghgp->jgph", bx.reshape(N_SLAB, gl, SSM_GROUP, gl, SSM_STATE))
    return d.reshape(N_SLAB * gl, SSM_STATE, SSM_GROUP)


def _to_block_c(cm):
    gl = LANES // SSM_GROUP
    tc = cm.reshape(N_SLAB, gl, SSM_GROUP, SSM_STATE).transpose(0, 1, 3, 2)
    eye = jnp.eye(gl, dtype=F32)
    return (tc[:, :, :, None, :] * eye[None, :, None, :, None]).reshape(N_SLAB, SLAB_STATE, LANES)


def _from_block_c(cx):
    gl = LANES // SSM_GROUP
    d = jnp.einsum("jgpgh->jghp", cx.reshape(N_SLAB, gl, SSM_STATE, gl, SSM_GROUP))
    return d.reshape(N_SLAB * gl, SSM_GROUP, SSM_STATE)


def kernel(x, norm_mix_g, w_in, ssm_a_re, ssm_a_im, ssm_b_re, ssm_b_im, ssm_c_re, ssm_c_im, ssm_d, ssm_log_dt, ssm_glu_w, ssm_glu_b, sgu_ln_g, sgu_ln_b, sgu_w, sgu_b, out_norm_ssm_g, out_norm_sgu_g, w_out, norm_mlp_g, w_up, w_down, norm_final_g, loss_target, m_norm_mix_g, m_w_in, m_ssm_a_re, m_ssm_a_im, m_ssm_b_re, m_ssm_b_im, m_ssm_c_re, m_ssm_c_im, m_ssm_d, m_ssm_log_dt, m_ssm_glu_w, m_ssm_glu_b, m_sgu_ln_g, m_sgu_ln_b, m_sgu_w, m_sgu_b, m_out_norm_ssm_g, m_out_norm_sgu_g, m_w_out, m_norm_mlp_g, m_w_up, m_w_down, m_norm_final_g, v_norm_mix_g, v_w_in, v_ssm_a_re, v_ssm_a_im, v_ssm_b_re, v_ssm_b_im, v_ssm_c_re, v_ssm_c_im, v_ssm_d, v_ssm_log_dt, v_ssm_glu_w, v_ssm_glu_b, v_sgu_ln_g, v_sgu_ln_b, v_sgu_w, v_sgu_b, v_out_norm_ssm_g, v_out_norm_sgu_g, v_w_out, v_norm_mlp_g, v_w_up, v_w_down, v_norm_final_g):
    given = dict(locals())
    names = ["norm_mix_g", "w_in", "ssm_a_re", "ssm_a_im", "ssm_b_re", "ssm_b_im", "ssm_c_re", "ssm_c_im",
             "ssm_d", "ssm_log_dt", "ssm_glu_w", "ssm_glu_b", "sgu_ln_g", "sgu_ln_b", "sgu_w", "sgu_b",
             "out_norm_ssm_g", "out_norm_sgu_g", "w_out", "norm_mlp_g", "w_up", "w_down", "norm_final_g"]
    big = ["w_in", "ssm_glu_w", "w_out", "w_up", "w_down"]
    small = [n for n in names if n not in big]

    d = D_MODEL
    t = x.shape[1]
    tb = min(1024, t)
    xs = x[0]
    target = loss_target[0]
    nsh_in = w_in.shape[2]
    nsh_up = w_up.shape[2]
    d_ff = nsh_up * N_DEV
    n_in = nsh_in * N_DEV

    me = _index(_place())
    shard_in = w_in[0].astype(BF16)
    spread = _cols_start("gather_w_in_spread", lax.empty((d, n_in), BF16), nsh_in, False, shard_in)
    gathers = {}

    def start_gather(n, after=None):
        shard = given[n][0].astype(BF16)
        gathers[n] = _send_start("gather_start_" + n, shard, lax.empty((N_DEV,) + shard.shape, BF16), False, after)
        return gathers[n][4]

    def gathered(n, after):
        shard, blocks = _send_wait("gather_wait_" + n, gathers[n], after, False)
        return _own_block(blocks, shard, me)

    nst = N_SLAB * SLAB_STATE
    are, aim = ssm_a_re.reshape(1, nst), ssm_a_im.reshape(1, nst)
    ldt = jnp.repeat(ssm_log_dt[0], SSM_STATE).reshape(1, nst)
    bxr, bxi = _to_block_b(ssm_b_re[0]), _to_block_b(ssm_b_im[0])
    cxr, cxi = _to_block_c(ssm_c_re[0]), _to_block_c(ssm_c_im[0])
    dvec = ssm_d.reshape(1, SSM_WIDTH)

    h1 = _rms_fwd("norm_mix", xs, norm_mix_g, deps=[spread[3]])
    bre, bim, cre, cimn, apr, api, air, aii = _ssm_prep(are, aim, ldt, bxr, bxi, cxr, cxi)
    tabs = (bre, bim, cre, cimn, apr, api, air, aii)
    land_in = _cols_wait("gather_w_in_landed", spread, nsh_in, False, h1)
    passed = _cols_start("gather_w_in_pass", land_in, nsh_in, True, bre)
    wg_in = _cols_wait("gather_w_in_passed", passed, nsh_in, True, passed[3])
    wg_in = lax.dynamic_update_slice_in_dim(wg_in, shard_in, me * nsh_in, axis=1)
    tokens = [start_gather("ssm_glu_w", wg_in), start_gather("w_out", wg_in)]
    bn_i = n_in // 2
    (z,) = _mm("in_proj", "nn", (t // tb, n_in // bn_i, 1),
               (h1, _bs((tb, d), lambda i, j, k: (i, 0))),
               (wg_in, _bs((d, bn_i), lambda i, j, k: (0, j))),
               [(_sds((t, n_in), F32), _bs((tb, bn_i), lambda i, j, k: (i, j)))], deps=tokens)
    tokens = [start_gather("w_up", z), start_gather("w_down", z)]
    y_pre, yg_b, p_re, p_im = _ssm_fwd(z, dvec, *tabs, deps=tokens)

    def glu_ep(acc, yp, b):
        gate = _sigmoid(acc + b)
        return _gelu(yp) * gate, gate

    hw = SSM_WIDTH // 2
    wg_glu = gathered("ssm_glu_w", yg_b).reshape(SSM_WIDTH, SSM_WIDTH)
    tile_g = _bs((tb, hw), lambda i, j, k: (i, j))
    y_ssm, gate = _mm("glu", "nn", (t // tb, 2, 1),
                      (yg_b, _bs((tb, SSM_WIDTH), lambda i, j, k: (i, 0))),
                      (wg_glu, _bs((SSM_WIDTH, hw), lambda i, j, k: (0, j))),
                      [(_sds((t, SSM_WIDTH), F32), tile_g), (_sds((t, SSM_WIDTH), F32), tile_g)],
                      extras=[(y_pre, tile_g), (ssm_glu_b, _bs((1, hw), lambda i, j, k: (0, j)))],
                      epilogue=glu_ep)

    sgu_bexp = jnp.broadcast_to(sgu_b[0][:, :, None], (SGU_HEADS, SGU_CHUNK, SGU_CHUNK))
    y_sgu = _sgu_fwd(z, sgu_ln_g, sgu_ln_b, sgu_w[0], sgu_bexp)
    mixed = _mix_norm(y_ssm, y_sgu, out_norm_ssm_g, out_norm_sgu_g)

    tb2 = min(512, t)
    row2 = _bs((tb2, d), lambda i, j, k: (i, 0))
    vec2 = _bs((1, d), lambda i, j, k: (0, 0))
    vec_sum = (_sds((1, d), F32), vec2)
    wg_out = gathered("w_out", mixed).reshape(d, d)

    def out_ep(acc, r, g):
        x2v = acc + r
        return x2v, _rms_math(x2v, g)

    x2, h2 = _mm("out_proj", "nn", (t // tb2, 1, 1),
                 (mixed, row2), (wg_out, _bs((d, d), lambda i, j, k: (0, 0))),
                 [(_sds((t, d), F32), row2), (_sds((t, d), BF16), row2)],
                 extras=[(xs, row2), (norm_mlp_g, vec2)], epilogue=out_ep, ep_rows=EP_ROWS)

    def up_ep(acc):
        r = jnp.maximum(acc, 0.0)
        return r * r, r

    tile_f = _bs((tb, nsh_up), lambda i, j, k: (i, j))
    wg_up = gathered("w_up", h2)
    f_act, r_act = _mm("mlp_up", "nn", (t // tb, N_DEV, 1),
                       (h2, _bs((tb, d), lambda i, j, k: (i, 0))),
                       (wg_up, _bs((None, d, nsh_up), lambda i, j, k: (j, 0, 0))),
                       [(_sds((t, d_ff), BF16), tile_f), (_sds((t, d_ff), BF16), tile_f)],
                       epilogue=up_ep)
    bk_d, bn_o = 2048, 1024
    tile_o = _bs((tb, bn_o), lambda i, j, k: (i, j))
    wg_down = gathered("w_down", f_act).reshape(d_ff, d)
    (x3,) = _mm("mlp_down", "nn", (t // tb, d // bn_o, d_ff // bk_d),
                (f_act, _bs((tb, bk_d), lambda i, j, k: (i, k))),
                (wg_down, _bs((bk_d, bn_o), lambda i, j, k: (k, j))),
                [(_sds((t, d), F32), tile_o)],
                extras=[(x2, tile_o)], epilogue=lambda acc, r: (acc + r,))
    dx3, dx3_b, g_final, err2 = _final_loss(x3, target, norm_final_g.reshape(1, d))
    loss = lax.psum(0.5 * jnp.sum(err2) / d, MESH_AXES)

    sends = {}

    def send_grad(n, g, land_shape=None):
        sends[n] = _send_start("grad_start_" + n, g, lax.empty(land_shape or g.shape, BF16), True)
        return [sends[n][4]]

    bn_a = 1024
    tile_a = _bs((tb, bn_a), lambda i, j, k: (i, j))
    (da,) = _mm("mlp_down_dx", "nt", (t // tb, d_ff // bn_a, 1),
                (dx3_b, _bs((tb, d), lambda i, j, k: (i, 0))),
                (wg_down, _bs((bn_a, d), lambda i, j, k: (j, 0))),
                [(_sds((t, d_ff), BF16), tile_a)],
                extras=[(r_act, tile_a)], epilogue=lambda acc, r: (acc * (2.0 * r.astype(F32)),))
    sq = 1024
    (gw_down,) = _mm("mlp_down_dw", "tn", (d_ff // sq, 1, t // tb),
                     (f_act, _bs((tb, sq), lambda i, j, k: (k, i))),
                     (dx3_b, _bs((tb, d), lambda i, j, k: (k, 0))),
                     [(_sds((d_ff, d), BF16), _bs((sq, d), lambda i, j, k: (i, 0)))])
    sent = send_grad("w_down", gw_down.reshape(N_DEV, -1, d))
    (dh2,) = _mm("mlp_up_dx", "nt", (t // tb, 1, N_DEV),
                 (da, _bs((tb, nsh_up), lambda i, j, k: (i, k))),
                 (wg_up, _bs((None, d, nsh_up), lambda i, j, k: (k, 0, 0))),
                 [(_sds((t, d), F32), _bs((tb, d), lambda i, j, k: (i, 0)))], deps=sent)

    def norm_bwd_side(dh, xv, dres, g):
        dx, dg = _rms_bwd_math(dh, xv, g)
        dx = dx + dres
        return dx, dx, dg

    up_dw_grid = (1, N_DEV, t // tb)
    side_rows = (t // math.prod(up_dw_grid), d)
    gw_up, dx2, dx2_b, g_norm_mlp = _mm(
        "mlp_up_dw", "tn", up_dw_grid,
        (h2, _bs((tb, d), lambda i, j, k: (k, 0))),
        (da, _bs((tb, nsh_up), lambda i, j, k: (k, j))),
        [(_sds((N_DEV, d, nsh_up), BF16), _bs((None, d, nsh_up), lambda i, j, k: (j, 0, 0)))],
        side=(norm_bwd_side, [(dh2, side_rows), (x2, side_rows), (dx3, side_rows), (norm_mlp_g, (1, d))],
              [(_sds((t, d), F32), side_rows), (_sds((t, d), BF16), side_rows)], [(_sds((1, d), F32), (1, d))]))
    sent = send_grad("w_up", gw_up)

    tk = min(2048, t)
    (gw_out,) = _mm("out_proj_dw", "tn", (d // sq, d // sq, t // tk),
                    (mixed, _bs((tk, sq), lambda i, j, k: (k, i))),
                    (dx2_b, _bs((tk, sq), lambda i, j, k: (k, j))),
                    [(_sds((d, d), BF16), _bs((sq, sq), lambda i, j, k: (i, j)))], deps=sent)
    sent = send_grad("w_out", gw_out.reshape(N_DEV, -1, d))
    half2 = _bs((tb2, SSM_WIDTH), lambda i, j, k: (i, 0))
    vech = _bs((1, SSM_WIDTH), lambda i, j, k: (0, 0))
    half_sum = (_sds((1, SSM_WIDTH), F32), vech)

    def out_dx_ep(acc, ya, yb, ga, gb, yp, gt):
        dya, dga = _rms_bwd_math(acc[:, :SSM_WIDTH], ya, ga)
        dyb, dgb = _rms_bwd_math(acc[:, SSM_WIDTH:], yb, gb)
        dpre = dya * _gelu(yp) * gt * (1.0 - gt)
        return dya, dyb, dpre, dga, dgb, jnp.sum(dpre, axis=0, keepdims=True)

    dy_ssm, dy_sgu, dpre_b, g_onorm_ssm, g_onorm_sgu, g_glu_b = _mm(
        "out_proj_dx", "nt", (t // tb2, 1, 1),
        (dx2_b, row2), (wg_out, _bs((d, d), lambda i, j, k: (0, 0))),
        [(_sds((t, SSM_WIDTH), F32), half2), (_sds((t, SSM_WIDTH), F32), half2),
         (_sds((t, SSM_WIDTH), BF16), half2)],
        extras=[(y_ssm, half2), (y_sgu, half2), (out_norm_ssm_g, vech), (out_norm_sgu_g, vech),
                (y_pre, half2), (gate, half2)],
        epilogue=out_dx_ep, sums=[half_sum, half_sum, half_sum], deps=sent, acc_shape=(tb2, d),
        ep_rows=EP_ROWS)

    (gw_glu,) = _mm("glu_dw", "tn", (1, 1, t // tb),
                    (yg_b, _bs((tb, SSM_WIDTH), lambda i, j, k: (k, 0))),
                    (dpre_b, _bs((tb, SSM_WIDTH), lambda i, j, k: (k, 0))),
                    [(_sds((SSM_WIDTH, SSM_WIDTH), BF16), _bs((SSM_WIDTH, SSM_WIDTH), lambda i, j, k: (0, 0)))])
    sent = send_grad("ssm_glu_w", gw_glu.reshape(N_DEV, -1, SSM_WIDTH))
    (dy_pre,) = _mm("glu_dx", "nt", (t // tb, 2, 1),
                    (dpre_b, _bs((tb, SSM_WIDTH), lambda i, j, k: (i, 0))),
                    (wg_glu, _bs((hw, SSM_WIDTH), lambda i, j, k: (j, 0))),
                    [(_sds((t, SSM_WIDTH), F32), tile_g)],
                    extras=[(dy_ssm, tile_g), (gate, tile_g), (y_pre, tile_g)],
                    epilogue=lambda acc, dy, gt, yp: ((dy * gt + acc) * _gelu_grad(yp),), deps=sent)
    du_b, dbre, dbim, dcre, dcimn, q_re, q_im, dd = _ssm_bwd(dy_pre, z, p_re, p_im, dvec, *tabs)
    dare, daim, dldt, dbxr, dbxi = _ssm_prep_bwd(are, aim, ldt, bxr, bxi, dbre, dbim, q_re, q_im)

    dz_b, g_ln_g, g_ln_b, g_sgu_w, g_sgu_bx = _sgu_bwd(dy_sgu, du_b, z, sgu_ln_g, sgu_ln_b, sgu_w[0], sgu_bexp)
    local_small = {
        "ssm_a_re": dare, "ssm_a_im": daim,
        "ssm_b_re": _from_block_b(dbxr), "ssm_b_im": _from_block_b(dbxi),
        "ssm_c_re": _from_block_c(dcre), "ssm_c_im": -_from_block_c(dcimn),
        "ssm_d": dd, "ssm_log_dt": dldt.reshape(-1, SSM_STATE).sum(axis=-1),
        "ssm_glu_b": g_glu_b, "sgu_ln_g": g_ln_g, "sgu_ln_b": g_ln_b, "sgu_w": g_sgu_w,
        "sgu_b": g_sgu_bx[:, :, 0], "out_norm_ssm_g": g_onorm_ssm, "out_norm_sgu_g": g_onorm_sgu,
        "norm_mlp_g": g_norm_mlp, "norm_final_g": g_final,
    }
    small_early = [n for n in small if n in local_small]
    small_late = [n for n in small if n not in local_small]
    packed = _pack([local_small[n] for n in small_early])
    small_send = _send_start("small_start", packed, lax.empty((N_DEV,) + packed.shape, F32), False)

    def landed_parts(n, after):
        sent_blocks, landed = _send_wait("grad_wait_" + n, sends[n], after, True)
        if sent_blocks.ndim == landed.ndim:
            own = lax.dynamic_index_in_dim(sent_blocks, me, 0, keepdims=False)
        else:
            own = lax.dynamic_slice_in_dim(sent_blocks, me * landed.shape[2], landed.shape[2], axis=1)
        return _own_block(landed, own, me)

    in_dw_grid = (d // sq, n_in // bn_i, t // tb)
    riders = ["w_down", "w_up"]
    side_ins, side_outs = [], []
    for n in riders:
        rows, cols = given[n].shape[1:]
        blk = (rows // math.prod(in_dw_grid), cols)
        side_ins += [(landed_parts(n, dz_b), (N_DEV,) + blk), (given[n][0], blk),
                     (given["m_" + n][0], blk), (given["v_" + n][0], blk)]
        side_outs += [(_sds((rows, cols), F32), blk)] * 4

    def adam_side(*tiles):
        return sum((_adam_math(*tiles[4 * r:4 * r + 4]) for r in range(len(riders))), ())

    gw_in, *rider_res = _mm("in_proj_dw", "tn", in_dw_grid,
                            (h1, _bs((tb, sq), lambda i, j, k: (k, i))),
                            (dz_b, _bs((tb, bn_i), lambda i, j, k: (k, j))),
                            [(_sds((d, n_in), BF16), _bs((sq, bn_i), lambda i, j, k: (i, j)))],
                            deps=[small_send[4]], side=(adam_side, side_ins, side_outs, []))
    sent = send_grad("w_in", gw_in, (N_DEV, d, nsh_in))

    def in_dx_ep(acc, xv, dres, g):
        dx, dg = _rms_bwd_math(acc, xv, g)
        return dx + dres, dg

    grad_x, g_norm_mix = _mm("in_proj_dx", "nt", (t // tb2, 1, n_in // bn_i),
                             (dz_b, _bs((tb2, bn_i), lambda i, j, k: (i, k))),
                             (wg_in, _bs((d, bn_i), lambda i, j, k: (0, k))),
                             [(_sds((t, d), F32), row2)],
                             extras=[(xs, row2), (dx2, row2), (norm_mix_g, vec2)],
                             epilogue=in_dx_ep, sums=[vec_sum], deps=sent, ep_rows=EP_ROWS)

    (late_parts,) = _all_gather("gather_late_grads", [_pack([g_norm_mix])])
    packed, early_parts = _send_wait("small_wait", small_send, grad_x, False)
    early_parts = _own_block(early_parts, packed, me)

    grads, deltas, new_m, new_v = {}, {}, {}, {}
    for r, n in enumerate(riders):
        grads[n], deltas[n], new_m[n], new_v[n] = [a.reshape(given[n].shape) for a in rider_res[4 * r:4 * r + 4]]
    for n in big:
        if n in riders:
            continue
        res = _adamw("adamw_" + n, landed_parts(n, grad_x), given[n][0], given["m_" + n][0], given["v_" + n][0])
        grads[n], deltas[n], new_m[n], new_v[n] = [r.reshape(given[n].shape) for r in res]
    for tag, group, parts in (("early", small_early, early_parts), ("late", small_late, late_parts)):
        like = [given[n] for n in group]
        res = _adamw("adamw_small_" + tag, parts, _pack(like), _pack([given["m_" + n] for n in group]),
                     _pack([given["v_" + n] for n in group]))
        for store, buf in zip((grads, deltas, new_m, new_v), res):
            for n, a in zip(group, _unpack(buf, like)):
                store[n] = a

    return (loss, grad_x.reshape(x.shape), *[grads[n] for n in names], *[deltas[n] for n in names],
            *[new_m[n] for n in names], *[new_v[n] for n in names])
```

```python
import functools
import math

import jax
import jax.numpy as jnp
from jax import lax
from jax.experimental import pallas as pl
from jax.experimental.pallas import tpu as pltpu

F32, BF16 = jnp.float32, jnp.bfloat16
EPS = 1e-6
N_DEV = 8
D_MODEL = 2048
SSM_WIDTH = 1024
SSM_GROUP = 16
SSM_STATE = 64
SGU_HEADS = 8
SGU_CHUNK = 128
SGU_FWD_CHUNKS = 4
LANES = 128
SUBLANES = 8
N_SLAB = SSM_WIDTH // LANES
SLAB_STATE = (LANES // SSM_GROUP) * SSM_STATE
SCAN_BLOCK = 256
SLABS_PER_STEP = 8
VMEM_LIMIT = 56 * 1024 * 1024
ROW_BLOCK = 512
EP_ROWS = 128
ADAM_ROWS = 128
MESH_AXES = ("x", "y", "c")

ADAM_LR, ADAM_B1, ADAM_B2, ADAM_EPS, ADAM_WD, ADAM_STEP = 0.001, 0.9, 0.999, 1e-08, 0.01, 10

_GELU_C0 = math.sqrt(2.0 / math.pi)
_GELU_C1 = 0.044715


def _gelu(v):
    return 0.5 * v * (1.0 + jnp.tanh(_GELU_C0 * (v + _GELU_C1 * v * v * v)))


def _gelu_grad(v):
    th = jnp.tanh(_GELU_C0 * (v + _GELU_C1 * v * v * v))
    return 0.5 * (1.0 + th) + 0.5 * v * (1.0 - th * th) * _GELU_C0 * (1.0 + 3.0 * _GELU_C1 * v * v)


def _sigmoid(v):
    return 1.0 / (1.0 + jnp.exp(-v))


def _params(sem=None):
    return pltpu.CompilerParams(dimension_semantics=sem, vmem_limit_bytes=VMEM_LIMIT)


def _dot(a, b, mode="nn"):
    dims = {"nn": ((1,), (0,)), "nt": ((1,), (1,)), "tn": ((0,), (0,))}[mode]
    return lax.dot_general(a, b, (dims, ((), ())), preferred_element_type=F32)


def _mm(name, mode, grid, a, b, outs, extras=(), epilogue=None, deps=(), sums=(), acc_shape=None,
        ep_rows=None, side=None):
    nk = grid[2]
    n_ex, n_out, n_dep, n_sum = len(extras), len(outs), len(deps), len(sums)
    assert not sums or grid[1] == 1
    if acc_shape is None:
        acc_shape = tuple(d for d in outs[0][1].block_shape if d is not None)
    side_fn, side_ins, side_outs, side_sums = side or (None, (), (), ())
    s_in, s_out, s_sum = len(side_ins), len(side_outs), len(side_sums)

    def body(*refs):
        a_ref, b_ref = refs[0], refs[1]
        ex = refs[2:2 + n_ex]
        pos = 2 + n_ex
        side_in = refs[pos:pos + s_in]
        pos += s_in + n_dep
        out_refs = refs[pos:pos + n_out]
        sum_refs = refs[pos + n_out:pos + n_out + n_sum]
        pos += n_out + n_sum
        side_out = refs[pos:pos + s_out]
        side_sum = refs[pos + s_out:pos + s_out + s_sum]
        acc = refs[-1]
        k = pl.program_id(2)

        @pl.when(k == 0)
        def _():
            acc[...] = jnp.zeros_like(acc)

        if side_sum:
            @pl.when((pl.program_id(0) == 0) & (pl.program_id(1) == 0) & (k == 0))
            def _():
                for o in side_sum:
                    o[...] = jnp.zeros_like(o)

        acc[...] += _dot(a_ref[...], b_ref[...], mode)
        if side_fn is not None:
            res = side_fn(*[r[...] for r in side_in])
            for o, r in zip(side_out, res[:s_out]):
                o[...] = r.astype(o.dtype)
            for o, r in zip(side_sum, res[s_out:]):
                o[...] += r

        def finish(rows):
            args = [e[rows, :] if e.shape[0] == acc_shape[0] else e[...] for e in ex]
            res = acc[rows, :]
            res = (res,) if epilogue is None else epilogue(res, *args)
            for o, r in zip(out_refs, res[:n_out]):
                o[rows, :] = r.astype(o.dtype)
            return tuple(res[n_out:])

        @pl.when(k == nk - 1)
        def _():
            if ep_rows is None:
                terms = finish(slice(None))
            else:
                def chunk(c, tot):
                    rows = pl.ds(pl.multiple_of(c * ep_rows, ep_rows), ep_rows)
                    return tuple(s + r for s, r in zip(tot, finish(rows)))

                zero = tuple(jnp.zeros(o.shape, F32) for o in sum_refs)
                terms = lax.fori_loop(0, acc_shape[0] // ep_rows, chunk, zero)
            for o, r in zip(sum_refs, terms):
                _add_up(o, r, pl.program_id(0) == 0)

    def side_spec(block):
        nd = len(block)
        if nd == 2 and block[0] == 1:
            return _bs(block, lambda i, j, k: (0, 0))
        return _bs(block, lambda i, j, k: (0,) * (nd - 2) + ((i * grid[1] + j) * grid[2] + k, 0))

    sem = ("arbitrary",) * 3 if sums or side else ("parallel", "parallel", "arbitrary")
    res = pl.pallas_call(
        body, name=name, grid=grid,
        in_specs=[a[1], b[1]] + [e[1] for e in extras] + [side_spec(blk) for _, blk in side_ins]
        + [_any_spec()] * n_dep,
        out_specs=[o[1] for o in outs] + [o[1] for o in sums]
        + [side_spec(blk) for _, blk in side_outs] + [side_spec(blk) for _, blk in side_sums],
        out_shape=[o[0] for o in outs] + [o[0] for o in sums]
        + [o[0] for o in side_outs] + [o[0] for o in side_sums],
        scratch_shapes=[pltpu.VMEM(acc_shape, F32)],
        compiler_params=_params(sem),
    )(a[0], b[0], *[e[0] for e in extras], *[x for x, _ in side_ins], *deps)
    return res


def _add_up(ref, term, first):
    @pl.when(first)
    def _():
        ref[...] = term

    @pl.when(jnp.logical_not(first))
    def _():
        ref[...] += term


def _rms_math(xv, g):
    return xv * lax.rsqrt(jnp.mean(xv * xv, axis=-1, keepdims=True) + EPS) * g


def _rms_bwd_math(dy, xv, g):
    r = lax.rsqrt(jnp.mean(xv * xv, axis=-1, keepdims=True) + EPS)
    xhat = xv * r
    dxhat = dy * g
    dx = r * (dxhat - xhat * jnp.mean(dxhat * xhat, axis=-1, keepdims=True))
    return dx, jnp.sum(dy * xhat, axis=0, keepdims=True)


def _sds(shape, dtype):
    return jax.ShapeDtypeStruct(shape, dtype)


def _bs(shape, fn):
    return pl.BlockSpec(shape, fn)


def _rms_fwd(name, x, g, deps=()):
    t, w = x.shape
    br = min(ROW_BLOCK, t)

    def body(*refs):
        x_ref, g_ref, o_ref = refs[0], refs[1], refs[-1]
        xv = x_ref[...]
        r = lax.rsqrt(jnp.mean(xv * xv, axis=-1, keepdims=True) + EPS)
        o_ref[...] = (xv * r * g_ref[...]).astype(BF16)

    return pl.pallas_call(
        body, name=name, grid=(t // br,),
        in_specs=[_bs((br, w), lambda i: (i, 0)), _bs((1, w), lambda i: (0, 0))] + [_any_spec()] * len(deps),
        out_specs=_bs((br, w), lambda i: (i, 0)),
        out_shape=_sds((t, w), BF16),
        compiler_params=_params(("parallel",)),
    )(x, g, *deps)


def _mix_norm(ya, yb, ga, gb):
    t, w = ya.shape
    br = min(ROW_BLOCK, t)

    def body(a_ref, b_ref, ga_ref, gb_ref, o_ref):
        for src, g_ref, col in ((a_ref, ga_ref, 0), (b_ref, gb_ref, w)):
            v = src[...]
            r = lax.rsqrt(jnp.mean(v * v, axis=-1, keepdims=True) + EPS)
            o_ref[:, col:col + w] = (v * r * g_ref[...]).astype(BF16)

    row = _bs((br, w), lambda i: (i, 0))
    vec = _bs((1, w), lambda i: (0, 0))
    return pl.pallas_call(
        body, name="mix_norm", grid=(t // br,),
        in_specs=[row, row, vec, vec],
        out_specs=_bs((br, 2 * w), lambda i: (i, 0)),
        out_shape=_sds((t, 2 * w), BF16),
        compiler_params=_params(("parallel",)),
    )(ya, yb, ga, gb)


def _loss_math(xv, target, g):
    r = lax.rsqrt(jnp.mean(xv * xv, axis=-1, keepdims=True) + EPS)
    xhat = xv * r
    err = xhat * g - target
    dy = err * (1.0 / xv.shape[-1])
    dxhat = dy * g
    dx = r * (dxhat - xhat * jnp.mean(dxhat * xhat, axis=-1, keepdims=True))
    return dx, jnp.sum(dy * xhat, axis=0, keepdims=True), jnp.sum(err * err, axis=0, keepdims=True)


def _final_loss(x3, target, g):
    t, w = x3.shape
    br = min(ROW_BLOCK, t)

    def body(x_ref, tg_ref, g_ref, dx_ref, dxb_ref, dg_ref, l_ref):
        dx, dg, e2 = _loss_math(x_ref[...], tg_ref[...], g_ref[...])
        dx_ref[...] = dx
        dxb_ref[...] = dx.astype(BF16)
        _add_up(dg_ref, dg, pl.program_id(0) == 0)
        _add_up(l_ref, e2, pl.program_id(0) == 0)

    row = _bs((br, w), lambda i: (i, 0))
    vec = _bs((1, w), lambda i: (0, 0))
    return pl.pallas_call(
        body, name="final_loss", grid=(t // br,),
        in_specs=[row, row, vec], out_specs=[row, row, vec, vec],
        out_shape=[_sds((t, w), F32), _sds((t, w), BF16), _sds((1, w), F32), _sds((1, w), F32)],
        compiler_params=_params(("arbitrary",)),
    )(x3, target, g)


def _prep_math(are, aim, ldt, bxr, bxi):
    dt = jnp.exp(ldt)
    er = jnp.exp(are * dt)
    th = aim * dt
    abr, abi = er * jnp.cos(th), er * jnp.sin(th)
    nr, ni = abr - 1.0, abi
    den = are * are + aim * aim
    cr = (nr * are + ni * aim) / den
    ci = (ni * are - nr * aim) / den
    bbr, bbi = [], []
    for j in range(N_SLAB):
        sl = slice(j * SLAB_STATE, (j + 1) * SLAB_STATE)
        bbr.append(cr[:, sl] * bxr[j] - ci[:, sl] * bxi[j])
        bbi.append(cr[:, sl] * bxi[j] + ci[:, sl] * bxr[j])
    return abr, abi, bbr, bbi


def _ssm_prep(are, aim, ldt, bxr, bxi, cxr, cxi):
    nst = N_SLAB * SLAB_STATE

    def body(are_r, aim_r, ldt_r, bxr_r, bxi_r, cxr_r, cxi_r,
             bre_o, bim_o, cre_o, cimn_o, apr_o, api_o, air_o, aii_o):
        abr, abi, bbr, bbi = _prep_math(are_r[...], aim_r[...], ldt_r[...],
                                        [bxr_r[j] for j in range(N_SLAB)], [bxi_r[j] for j in range(N_SLAB)])
        for j in range(N_SLAB):
            bre_o[j] = bbr[j].astype(BF16)
            bim_o[j] = bbi[j].astype(BF16)
        cre_o[...] = cxr_r[...].astype(BF16)
        cimn_o[...] = (-cxi_r[...]).astype(BF16)

        def step(k, cur):
            cr, ci = cur
            den = cr * cr + ci * ci
            apr_o[pl.ds(k, 1), :] = cr
            api_o[pl.ds(k, 1), :] = ci
            air_o[pl.ds(k, 1), :] = cr / den
            aii_o[pl.ds(k, 1), :] = -ci / den
            return cr * abr - ci * abi, cr * abi + ci * abr

        lax.fori_loop(0, SCAN_BLOCK, step, (jnp.ones((1, nst), F32), jnp.zeros((1, nst), F32)))

    tab = _sds((SCAN_BLOCK, nst), F32)
    return pl.pallas_call(
        body, name="ssm_prep",
        out_shape=[_sds(bxr.shape, BF16), _sds(bxr.shape, BF16), _sds(cxr.shape, BF16), _sds(cxr.shape, BF16),
                   tab, tab, tab, tab],
        compiler_params=_params(),
    )(are, aim, ldt, bxr, bxi, cxr, cxi)


def _ssm_prep_bwd(are, aim, ldt, bxr, bxi, dbre, dbim, qr, qi):
    def body(are_r, aim_r, ldt_r, bxr_r, bxi_r, dbre_r, dbim_r, qr_r, qi_r,
             dare_o, daim_o, dldt_o, dbxr_o, dbxi_o):
        prim = (are_r[...], aim_r[...], ldt_r[...],
                [bxr_r[j] for j in range(N_SLAB)], [bxi_r[j] for j in range(N_SLAB)])
        (abr, abi, _, _), vjp = jax.vjp(_prep_math, *prim)
        den = abr * abr + abi * abi
        q_r, q_i = qr_r[...], qi_r[...]
        gar = (q_r * abr - q_i * abi) / den
        gai = (q_r * abi + q_i * abr) / den
        ct = (gar, gai, [dbre_r[j] for j in range(N_SLAB)], [dbim_r[j] for j in range(N_SLAB)])
        dare, daim, dldt, dbxr, dbxi = vjp(ct)
        dare_o[...] = dare
        daim_o[...] = daim
        dldt_o[...] = dldt
        for j in range(N_SLAB):
            dbxr_o[j] = dbxr[j]
            dbxi_o[j] = dbxi[j]

    row = _sds(are.shape, F32)
    return pl.pallas_call(
        body, name="ssm_prep_bwd",
        out_shape=[row, row, row, _sds(bxr.shape, F32), _sds(bxr.shape, F32)],
        compiler_params=_params(),
    )(are, aim, ldt, bxr, bxi, dbre, dbim, qr, qi)


def _tri(lower):
    r = lax.broadcasted_iota(jnp.int32, (SCAN_BLOCK, SCAN_BLOCK), 0)
    c = lax.broadcasted_iota(jnp.int32, (SCAN_BLOCK, SCAN_BLOCK), 1)
    return jnp.where((r >= c) if lower else (r <= c), 1.0, 0.0).astype(BF16)


def _cumsum_mxu(tri, v):
    return _dot(tri, v.astype(BF16))


def _ssm_specs(t, sps):
    nt = t // SCAN_BLOCK
    tab = _bs((SCAN_BLOCK, sps * SLAB_STATE), lambda j, i: (0, j))
    bmat = _bs((sps, LANES, SLAB_STATE), lambda j, i: (j, 0, 0))
    cmat = _bs((sps, SLAB_STATE, LANES), lambda j, i: (j, 0, 0))
    return nt, tab, bmat, cmat


def _slab_slices(s):
    return slice(s * LANES, (s + 1) * LANES), slice(s * SLAB_STATE, (s + 1) * SLAB_STATE)


def _ssm_fwd(z, dvec, bre, bim, cre, cimn, apr, api, air, aii, deps=()):
    t = z.shape[0]
    sps = SLABS_PER_STEP
    nt, tab, bmat, cmat = _ssm_specs(t, sps)
    nst = N_SLAB * SLAB_STATE
    last = SCAN_BLOCK - 1

    def body(*refs):
        u_ref, d_ref, bre_r, bim_r, cre_r, cimn_r, apr_r, api_r, air_r, aii_r = refs[:10]
        y_ref, yg_ref, pr_ref, pi_ref, car_r, car_i = refs[10 + len(deps):]

        @pl.when(pl.program_id(1) == 0)
        def _():
            car_r[...] = jnp.zeros_like(car_r)
            car_i[...] = jnp.zeros_like(car_i)

        tri = _tri(True)
        for s in range(sps):
            ul, sl = _slab_slices(s)
            u = u_ref[:, ul]
            ub = u.astype(BF16)
            bur, bui = _dot(ub, bre_r[s]), _dot(ub, bim_r[s])
            ir, ii = air_r[:, sl], aii_r[:, sl]
            csr = _cumsum_mxu(tri, ir * bur - ii * bui)
            csi = _cumsum_mxu(tri, ir * bui + ii * bur)
            pr, pi = apr_r[:, sl], api_r[:, sl]
            a_r, a_i = apr_r[1:2, sl], api_r[1:2, sl]
            c_r, c_i = car_r[:, sl], car_i[:, sl]
            wr = csr + (a_r * c_r - a_i * c_i)
            wi = csi + (a_r * c_i + a_i * c_r)
            sr = pr * wr - pi * wi
            si = pr * wi + pi * wr
            car_r[:, sl] = sr[last:last + 1, :]
            car_i[:, sl] = si[last:last + 1, :]
            pr_ref[:, sl] = (sr - bur).astype(BF16)
            pi_ref[:, sl] = (si - bui).astype(BF16)
            y = _dot(sr.astype(BF16), cre_r[s]) + _dot(si.astype(BF16), cimn_r[s]) + d_ref[:, ul] * u
            y_ref[:, ul] = y
            yg_ref[:, ul] = _gelu(y).astype(BF16)

    ublk = _bs((SCAN_BLOCK, sps * LANES), lambda j, i: (i, j))
    sblk = _bs((SCAN_BLOCK, sps * SLAB_STATE), lambda j, i: (i, j))
    return pl.pallas_call(
        body, name="ssm_fwd", grid=(N_SLAB // sps, nt),
        in_specs=[ublk, _bs((1, sps * LANES), lambda j, i: (0, j)), bmat, bmat, cmat, cmat, tab, tab, tab, tab]
        + [_any_spec()] * len(deps),
        out_specs=[ublk, ublk, sblk, sblk],
        out_shape=[_sds((t, SSM_WIDTH), F32), _sds((t, SSM_WIDTH), BF16),
                   _sds((t, nst), BF16), _sds((t, nst), BF16)],
        scratch_shapes=[pltpu.VMEM((1, sps * SLAB_STATE), F32), pltpu.VMEM((1, sps * SLAB_STATE), F32)],
        compiler_params=_params(("parallel", "arbitrary")),
    )(z, dvec, bre, bim, cre, cimn, apr, api, air, aii, *deps)


def _ssm_bwd(gy, z, p_re, p_im, dvec, bre, bim, cre, cimn, apr, api, air, aii):
    t = z.shape[0]
    sps = SLABS_PER_STEP
    nt, tab, bmat, cmat = _ssm_specs(t, sps)
    last = SCAN_BLOCK - 1

    def fold(v):
        return v.reshape(SCAN_BLOCK // SUBLANES, SUBLANES, v.shape[-1]).sum(axis=0)

    def body(g_ref, u_ref, pr_ref, pi_ref, d_ref, bre_r, bim_r, cre_r, cimn_r, apr_r, api_r, air_r, aii_r,
             du_ref, dbre_o, dbim_o, dcre_o, dcimn_o, qr_o, qi_o, dd_o, car_r, car_i, qacc_r, qacc_i, dacc):
        i = pl.program_id(1)

        @pl.when(i == 0)
        def _():
            for ref in (car_r, car_i, qacc_r, qacc_i, dacc, dbre_o, dbim_o, dcre_o, dcimn_o):
                ref[...] = jnp.zeros_like(ref)

        tri = _tri(False)
        for s in range(sps):
            ul, sl = _slab_slices(s)
            g = g_ref[:, ul]
            gb = g.astype(BF16)
            u = u_ref[:, ul]
            ub = u.astype(BF16)
            bur, bui = _dot(ub, bre_r[s]), _dot(ub, bim_r[s])
            p_r, p_i = pr_ref[:, sl].astype(F32), pi_ref[:, sl].astype(F32)
            srb, sib = (p_r + bur).astype(BF16), (p_i + bui).astype(BF16)
            dcre_o[s] += _dot(srb, gb, "tn")
            dcimn_o[s] += _dot(sib, gb, "tn")
            dsr, dsi = _dot(gb, cre_r[s], "nt"), _dot(gb, cimn_r[s], "nt")
            pr, pi = apr_r[:, sl], api_r[:, sl]
            csr = _cumsum_mxu(tri, pr * dsr + pi * dsi)
            csi = _cumsum_mxu(tri, pr * dsi - pi * dsr)
            al_r, al_i = apr_r[last:last + 1, sl], api_r[last:last + 1, sl]
            c_r, c_i = car_r[:, sl], car_i[:, sl]
            wr = csr + (al_r * c_r + al_i * c_i)
            wi = csi + (al_r * c_i - al_i * c_r)
            ir, ii = air_r[:, sl], aii_r[:, sl]
            lr = ir * wr + ii * wi
            li = ir * wi - ii * wr
            a_r, a_i = apr_r[1:2, sl], api_r[1:2, sl]
            car_r[:, sl] = a_r * lr[0:1, :] + a_i * li[0:1, :]
            car_i[:, sl] = a_r * li[0:1, :] - a_i * lr[0:1, :]
            lrb, lib = lr.astype(BF16), li.astype(BF16)
            dbre_o[s] += _dot(ub, lrb, "tn")
            dbim_o[s] += _dot(ub, lib, "tn")
            du = d_ref[:, ul] * g + _dot(lrb, bre_r[s], "nt") + _dot(lib, bim_r[s], "nt")
            du_ref[:, ul] = du.astype(BF16)
            qacc_r[:, sl] += fold(lr * p_r + li * p_i)
            qacc_i[:, sl] += fold(li * p_r - lr * p_i)
            dacc[:, ul] += fold(g * u)

        @pl.when(i == nt - 1)
        def _():
            qr_o[...] = jnp.sum(qacc_r[...], axis=0, keepdims=True)
            qi_o[...] = jnp.sum(qacc_i[...], axis=0, keepdims=True)
            dd_o[...] = jnp.sum(dacc[...], axis=0, keepdims=True)

    rev = lambda j, i: (nt - 1 - i, j)
    ublk = _bs((SCAN_BLOCK, sps * LANES), rev)
    sblk = _bs((SCAN_BLOCK, sps * SLAB_STATE), rev)
    qrow = _bs((1, sps * SLAB_STATE), lambda j, i: (0, j))
    urow = _bs((1, sps * LANES), lambda j, i: (0, j))
    nst = N_SLAB * SLAB_STATE
    return pl.pallas_call(
        body, name="ssm_bwd", grid=(N_SLAB // sps, nt),
        in_specs=[ublk, ublk, sblk, sblk, urow, bmat, bmat, cmat, cmat, tab, tab, tab, tab],
        out_specs=[ublk, bmat, bmat, cmat, cmat, qrow, qrow, urow],
        out_shape=[_sds((t, SSM_WIDTH), BF16),
                   _sds((N_SLAB, LANES, SLAB_STATE), F32), _sds((N_SLAB, LANES, SLAB_STATE), F32),
                   _sds((N_SLAB, SLAB_STATE, LANES), F32), _sds((N_SLAB, SLAB_STATE, LANES), F32),
                   _sds((1, nst), F32), _sds((1, nst), F32), _sds((1, SSM_WIDTH), F32)],
        scratch_shapes=[pltpu.VMEM((1, sps * SLAB_STATE), F32), pltpu.VMEM((1, sps * SLAB_STATE), F32),
                        pltpu.VMEM((SUBLANES, sps * SLAB_STATE), F32), pltpu.VMEM((SUBLANES, sps * SLAB_STATE), F32),
                        pltpu.VMEM((SUBLANES, sps * LANES), F32)],
        compiler_params=_params(("parallel", "arbitrary")),
    )(gy, z, p_re, p_im, dvec, bre, bim, cre, cimn, apr, api, air, aii)


def _sgu_mask():
    r = lax.broadcasted_iota(jnp.int32, (SGU_CHUNK, SGU_CHUNK), 0)
    c = lax.broadcasted_iota(jnp.int32, (SGU_CHUNK, SGU_CHUNK), 1)
    return r >= c


def _sgu_common(zu, zv, lng, lnb):
    us, v = _gelu(zu), _gelu(zv)
    mu = jnp.mean(v, axis=-1, keepdims=True)
    vc = v - mu
    rstd = lax.rsqrt(jnp.mean(vc * vc, axis=-1, keepdims=True) + EPS)
    xhat = vc * rstd
    return us, xhat, rstd, xhat * lng + lnb


def _sgu_fwd(z, lng, lnb, w, bexp):
    t = z.shape[0]
    hd = SGU_CHUNK
    rows = min(SGU_FWD_CHUNKS * SGU_CHUNK, t)

    def body(zu_ref, zv_ref, lng_ref, lnb_ref, w_ref, b_ref, y_ref):
        mask = _sgu_mask()
        wts = [jnp.where(mask, w_ref[h], 0.0).astype(BF16) for h in range(SGU_HEADS)]
        for c in range(rows // SGU_CHUNK):
            rs = slice(c * SGU_CHUNK, (c + 1) * SGU_CHUNK)
            us, _, _, vn = _sgu_common(zu_ref[rs, :], zv_ref[rs, :], lng_ref[...], lnb_ref[...])
            vnb = vn.astype(BF16)
            for h in range(SGU_HEADS):
                sl = slice(h * hd, (h + 1) * hd)
                y_ref[rs, sl] = us[:, sl] * (_dot(wts[h], vnb[:, sl]) + b_ref[h])

    row = lambda c: _bs((rows, SSM_WIDTH), lambda i: (i, c))
    vec = _bs((1, SSM_WIDTH), lambda i: (0, 0))
    hmat = _bs((SGU_HEADS, hd, hd), lambda i: (0, 0, 0))
    return pl.pallas_call(
        body, name="sgu_fwd", grid=(t // rows,),
        in_specs=[row(1), row(2), vec, vec, hmat, hmat],
        out_specs=row(0), out_shape=_sds((t, SSM_WIDTH), F32),
        compiler_params=_params(("parallel",)),
    )(z, z, lng, lnb, w, bexp)


def _sgu_bwd(dy, du_ssm, z, lng, lnb, w, bexp, deps=()):
    t = z.shape[0]
    hd = SGU_CHUNK
    nc = t // SGU_CHUNK

    def body(*refs):
        dy_ref, dus_ref, zu_ref, zv_ref, lng_ref, lnb_ref, w_ref, b_ref = refs[:8]
        dz_ref, dlng_o, dlnb_o, dw_o, db_o = refs[8 + len(deps):]
        i = pl.program_id(0)

        @pl.when(i == 0)
        def _():
            for ref in (dlng_o, dlnb_o, dw_o, db_o):
                ref[...] = jnp.zeros_like(ref)

        zu, zv = zu_ref[...], zv_ref[...]
        lng = lng_ref[...]
        us, xhat, rstd, vn = _sgu_common(zu, zv, lng, lnb_ref[...])
        vnb = vn.astype(BF16)
        dyv = dy_ref[...]
        mask = _sgu_mask()
        dus_parts, dvn_parts = [], []
        for h in range(SGU_HEADS):
            sl = slice(h * hd, (h + 1) * hd)
            wt = jnp.where(mask, w_ref[h], 0.0).astype(BF16)
            mixed = _dot(wt, vnb[:, sl]) + b_ref[h]
            dus_parts.append(dyv[:, sl] * mixed)
            dmix = dyv[:, sl] * us[:, sl]
            dmb = dmix.astype(BF16)
            db_o[h] += dmix
            dw_o[h] += _dot(dmb, vnb[:, sl], "nt")
            dvn_parts.append(_dot(wt, dmb, "tn"))
        dus = jnp.concatenate(dus_parts, axis=1)
        dvn = jnp.concatenate(dvn_parts, axis=1)
        dlng_o[...] += jnp.sum(dvn * xhat, axis=0, keepdims=True)
        dlnb_o[...] += jnp.sum(dvn, axis=0, keepdims=True)
        dxh = dvn * lng
        dv = rstd * (dxh - jnp.mean(dxh, axis=-1, keepdims=True)
                     - xhat * jnp.mean(dxh * xhat, axis=-1, keepdims=True))
        dz_ref[:, 0:SSM_WIDTH] = dus_ref[...]
        dz_ref[:, SSM_WIDTH:2 * SSM_WIDTH] = (dus * _gelu_grad(zu)).astype(BF16)
        dz_ref[:, 2 * SSM_WIDTH:] = (dv * _gelu_grad(zv)).astype(BF16)

        @pl.when(i == nc - 1)
        def _():
            for h in range(SGU_HEADS):
                dw_o[h] = jnp.where(mask, dw_o[h], 0.0)
                db_o[h] = jnp.broadcast_to(jnp.sum(db_o[h], axis=1, keepdims=True), (hd, hd))

    row = lambda c: _bs((SGU_CHUNK, SSM_WIDTH), lambda i: (i, c))
    vec = _bs((1, SSM_WIDTH), lambda i: (0, 0))
    hmat = _bs((SGU_HEADS, hd, hd), lambda i: (0, 0, 0))
    return pl.pallas_call(
        body, name="sgu_bwd", grid=(nc,),
        in_specs=[row(0), row(0), row(1), row(2), vec, vec, hmat, hmat] + [_any_spec()] * len(deps),
        out_specs=[_bs((SGU_CHUNK, 3 * SSM_WIDTH), lambda i: (i, 0)), vec, vec, hmat, hmat],
        out_shape=[_sds((t, 3 * SSM_WIDTH), BF16), _sds((1, SSM_WIDTH), F32), _sds((1, SSM_WIDTH), F32),
                   _sds((SGU_HEADS, hd, hd), F32), _sds((SGU_HEADS, hd, hd), F32)],
        compiler_params=_params(("arbitrary",)),
    )(dy, du_ssm, z, z, lng, lnb, w, bexp, *deps)


def _place():
    x, y, c = (lax.axis_index(a) for a in MESH_AXES)
    return x, y, c


def _index(p):
    return 4 * p[0] + 2 * p[1] + p[2]


def _any_spec():
    return pl.BlockSpec(memory_space=pl.ANY)


def _col_block(ref, k, width):
    return ref.at[:, pl.ds(pl.multiple_of(k * width, LANES), width)]


def _all_gather(name, shards, by_columns=False):
    n = len(shards)

    def body(*refs):
        ins, outs = refs[:n], refs[n:2 * n]
        send, recv, loc = refs[2 * n:]
        x, y, c = _place()
        me, sib = (x, y, c), (x, y, 1 - c)
        chips = [(1 - x, y), (x, 1 - y), (1 - x, 1 - y)]

        def blk(w, p):
            if by_columns:
                return _col_block(outs[w], _index(p), shards[w].shape[1])
            return outs[w].at[_index(p)]

        def cp(w, k, block, to, src=None):
            dst = blk(w, block)
            return pltpu.make_async_remote_copy(
                src_ref=dst if src is None else src, dst_ref=dst,
                send_sem=send.at[w * 7 + k], recv_sem=recv.at[w * 7 + k],
                device_id=to, device_id_type=pl.DeviceIdType.MESH)

        mines, sends = [], []
        for w in range(n):
            m = pltpu.make_async_copy(ins[w], blk(w, me), loc.at[w])
            m.start()
            mines.append(m)
            first = [cp(w, 0, me, sib, src=ins[w])]
            first += [cp(w, 1 + j, me, (*chip, c), src=ins[w]) for j, chip in enumerate(chips)]
            for q in first:
                q.start()
            sends += first
        for j, chip in enumerate(chips):
            for w in range(n):
                cp(w, 1 + j, (*chip, c), me).wait_recv()
                q = cp(w, 4 + j, (*chip, c), sib)
                q.start()
                sends.append(q)
        for w in range(n):
            cp(w, 0, sib, me).wait_recv()
            for j, chip in enumerate(chips):
                cp(w, 4 + j, (*chip, 1 - c), me).wait_recv()
        for q in sends:
            q.wait_send()
        for m in mines:
            m.wait()

    return pl.pallas_call(
        body, name=name,
        in_specs=[_any_spec()] * n, out_specs=[_any_spec()] * n,
        out_shape=[_sds((s.shape[0], N_DEV * s.shape[1]) if by_columns else (N_DEV,) + s.shape, s.dtype)
                   for s in shards],
        scratch_shapes=[pltpu.SemaphoreType.DMA((n * 7,)), pltpu.SemaphoreType.DMA((n * 7,)),
                        pltpu.SemaphoreType.DMA((n,))],
        compiler_params=pltpu.CompilerParams(has_side_effects=True),
    )(*shards)


def _peer(r, x, y, c):
    return ((1 - x) if r & 4 else x, (1 - y) if r & 2 else y, (1 - c) if r & 1 else c)


def _sent_block(src_ref, land_ref, k, scatter):
    if not scatter:
        return src_ref
    if len(src_ref.shape) == len(land_ref.shape):
        return src_ref.at[k]
    return _col_block(src_ref, k, land_ref.shape[2])


def _send_start(name, src, land, scatter, after=None):
    n_after = 0 if after is None else 1

    def body(*refs):
        src_ref, land_ref = refs[0], refs[1]
        send, recv, _, _, token = refs[2 + n_after:]
        x, y, c = _place()
        me = _index((x, y, c))
        for r in range(1, N_DEV):
            p = _peer(r, x, y, c)
            pltpu.make_async_remote_copy(
                src_ref=_sent_block(src_ref, land_ref, _index(p), scatter), dst_ref=land_ref.at[me],
                send_sem=send.at[r - 1], recv_sem=recv.at[r - 1],
                device_id=p, device_id_type=pl.DeviceIdType.MESH).start()
        token[...] = jnp.zeros_like(token)

    hbm, sem = pl.BlockSpec(memory_space=pltpu.HBM), pl.BlockSpec(memory_space=pltpu.SEMAPHORE)
    return pl.pallas_call(
        body, name=name,
        out_shape=(pltpu.SemaphoreType.DMA((N_DEV - 1,)), pltpu.SemaphoreType.DMA((N_DEV - 1,)),
                   pltpu.HBM(src.shape, src.dtype), pltpu.HBM(land.shape, land.dtype),
                   _sds((SUBLANES, LANES), F32)),
        in_specs=(hbm, hbm) + (_any_spec(),) * n_after,
        out_specs=(sem, sem, hbm, hbm, pl.BlockSpec(memory_space=pltpu.VMEM)),
        input_output_aliases={0: 2, 1: 3},
        compiler_params=pltpu.CompilerParams(has_side_effects=pltpu.SideEffectType.DATAFLOW_SIDE_EFFECTING),
    )(pltpu.with_memory_space_constraint(src, pltpu.HBM), pltpu.with_memory_space_constraint(land, pltpu.HBM),
      *([] if after is None else [after]))


def _send_wait(name, started, after, scatter):
    send, recv, src_thru, land_thru, _ = started

    def body(src_ref, land_ref, send_r, recv_r, after_ref, src_out, land_out):
        x, y, c = _place()
        for r in range(1, N_DEV):
            p = _peer(r, x, y, c)
            k = _index(p)
            cp = pltpu.make_async_remote_copy(
                src_ref=_sent_block(src_ref, land_ref, k, scatter), dst_ref=land_ref.at[k],
                send_sem=send_r.at[r - 1], recv_sem=recv_r.at[r - 1],
                device_id=p, device_id_type=pl.DeviceIdType.MESH)
            cp.wait_send()
            cp.wait_recv()

    hbm, sem = pl.BlockSpec(memory_space=pltpu.HBM), pl.BlockSpec(memory_space=pltpu.SEMAPHORE)
    return pl.pallas_call(
        body, name=name,
        out_shape=(pltpu.HBM(src_thru.shape, src_thru.dtype), pltpu.HBM(land_thru.shape, land_thru.dtype)),
        in_specs=(hbm, hbm, sem, sem, _any_spec()), out_specs=(hbm, hbm),
        input_output_aliases={0: 0, 1: 1},
        compiler_params=pltpu.CompilerParams(has_side_effects=pltpu.SideEffectType.DATAFLOW_SIDE_EFFECTING),
    )(src_thru, land_thru, send, recv, after)


SPREAD, PASS_ON = (1, 2, 4, 6), (2, 4, 6)


def _cols_start(name, land, width, pass_on, after):
    peers = PASS_ON if pass_on else SPREAD

    def body(land_ref, after_ref, send, recv, land_thru, token):
        x, y, c = _place()
        for n, r in enumerate(peers):
            block = _col_block(land_ref, _index(_peer(r, x, y, c) if pass_on else (x, y, c)), width)
            pltpu.make_async_remote_copy(
                src_ref=block if pass_on else after_ref, dst_ref=block,
                send_sem=send.at[n], recv_sem=recv.at[n],
                device_id=_peer(1, x, y, c) if pass_on else _peer(r, x, y, c),
                device_id_type=pl.DeviceIdType.MESH).start()
        token[...] = jnp.zeros_like(token)

    hbm, sem = pl.BlockSpec(memory_space=pltpu.HBM), pl.BlockSpec(memory_space=pltpu.SEMAPHORE)
    return pl.pallas_call(
        body, name=name,
        out_shape=(pltpu.SemaphoreType.DMA((len(peers),)), pltpu.SemaphoreType.DMA((len(peers),)),
                   pltpu.HBM(land.shape, land.dtype), _sds((SUBLANES, LANES), F32)),
        in_specs=(hbm, _any_spec()), out_specs=(sem, sem, hbm, pl.BlockSpec(memory_space=pltpu.VMEM)),
        input_output_aliases={0: 2},
        compiler_params=pltpu.CompilerParams(has_side_effects=pltpu.SideEffectType.DATAFLOW_SIDE_EFFECTING),
    )(pltpu.with_memory_space_constraint(land, pltpu.HBM), after)


def _cols_wait(name, started, width, pass_on, after):
    send, recv, land_thru, _ = started
    peers = PASS_ON if pass_on else SPREAD

    def body(land_ref, send_r, recv_r, after_ref, land_out):
        x, y, c = _place()
        for n, r in enumerate(peers):
            sent = _col_block(land_ref, _index(_peer(r, x, y, c) if pass_on else (x, y, c)), width)
            came = _col_block(land_ref, _index(_peer(r ^ 1 if pass_on else r, x, y, c)), width)
            cp = pltpu.make_async_remote_copy(
                src_ref=sent, dst_ref=came, send_sem=send_r.at[n], recv_sem=recv_r.at[n],
                device_id=_peer(1, x, y, c) if pass_on else _peer(r, x, y, c),
                device_id_type=pl.DeviceIdType.MESH)
            cp.wait_send()
            cp.wait_recv()

    hbm, sem = pl.BlockSpec(memory_space=pltpu.HBM), pl.BlockSpec(memory_space=pltpu.SEMAPHORE)
    return pl.pallas_call(
        body, name=name, out_shape=(pltpu.HBM(land_thru.shape, land_thru.dtype),),
        in_specs=(hbm, sem, sem, _any_spec()), out_specs=(hbm,), input_output_aliases={0: 0},
        compiler_params=pltpu.CompilerParams(has_side_effects=pltpu.SideEffectType.DATAFLOW_SIDE_EFFECTING),
    )(land_thru, send, recv, after)[0]


def _place_cols(land, block):
    def body(blk_ref, land_in, land_out, sem):
        cp = pltpu.make_async_copy(blk_ref, _col_block(land_out, _index(_place()), block.shape[1]), sem)
        cp.start()
        cp.wait()

    return pl.pallas_call(
        body, name="place_w_in", out_shape=_sds(land.shape, land.dtype),
        in_specs=[_any_spec(), _any_spec()], out_specs=_any_spec(),
        scratch_shapes=[pltpu.SemaphoreType.DMA], input_output_aliases={1: 0},
        compiler_params=pltpu.CompilerParams(has_side_effects=True),
    )(block, land)


def _own_block(blocks, block, me):
    return lax.dynamic_update_index_in_dim(blocks, block, me, 0)


def _adam_math(parts, w, m, v):
    c1 = 1.0 / (1.0 - ADAM_B1 ** ADAM_STEP)
    c2 = 1.0 / (1.0 - ADAM_B2 ** ADAM_STEP)
    g = parts[0].astype(F32)
    for k in range(1, N_DEV):
        g = g + parts[k].astype(F32)
    mn = ADAM_B1 * m + (1.0 - ADAM_B1) * g
    vn = ADAM_B2 * v + (1.0 - ADAM_B2) * (g * g)
    return g, -ADAM_LR * ((mn * c1) / (jnp.sqrt(vn * c2) + ADAM_EPS) + ADAM_WD * w), mn, vn


def _adamw(name, parts, w, m, v):
    rows, cols = w.shape
    br = min(ADAM_ROWS, rows)

    def body(p_ref, w_ref, m_ref, v_ref, g_o, d_o, m_o, v_o):
        g_o[...], d_o[...], m_o[...], v_o[...] = _adam_math(p_ref[...], w_ref[...], m_ref[...], v_ref[...])

    blk = _bs((br, cols), lambda i: (i, 0))
    out = _sds((rows, cols), F32)
    return pl.pallas_call(
        body, name=name, grid=(rows // br,),
        in_specs=[_bs((N_DEV, br, cols), lambda i: (0, i, 0)), blk, blk, blk],
        out_specs=[blk] * 4, out_shape=[out] * 4,
        compiler_params=_params(("parallel",)),
    )(parts, w, m, v)


def _pack(arrs):
    tile = SUBLANES * LANES
    flat = []
    for a in arrs:
        f = a.reshape(-1).astype(F32)
        pad = (-f.shape[0]) % tile
        flat.append(jnp.pad(f, (0, pad)) if pad else f)
    total = sum(f.shape[0] for f in flat)
    tail = (-total) % (ADAM_ROWS * LANES)
    if tail:
        flat.append(jnp.zeros((tail,), F32))
    return jnp.concatenate(flat).reshape(-1, LANES)


def _unpack(buf, like):
    tile = SUBLANES * LANES
    flat = buf.reshape(-1)
    out, off = [], 0
    for a in like:
        n = math.prod(a.shape)
        out.append(flat[off:off + n].reshape(a.shape))
        off += n + ((-n) % tile)
    return out


def _to_block_b(b):
    gl = LANES // SSM_GROUP
    tb = b.reshape(N_SLAB, gl, SSM_STATE, SSM_GROUP).transpose(0, 1, 3, 2)
    eye = jnp.eye(gl, dtype=F32)
    return (tb[:, :, :, None, :] * eye[None, :, None, :, None]).reshape(N_SLAB, LANES, SLAB_STATE)


def _from_block_b(bx):
    gl = LANES // SSM_GROUP
    d = jnp.einsum("jghgp->jgph", bx.reshape(N_SLAB, gl, SSM_GROUP, gl, SSM_STATE))
    return d.reshape(N_SLAB * gl, SSM_STATE, SSM_GROUP)


def _to_block_c(cm):
    gl = LANES // SSM_GROUP
    tc = cm.reshape(N_SLAB, gl, SSM_GROUP, SSM_STATE).transpose(0, 1, 3, 2)
    eye = jnp.eye(gl, dtype=F32)
    return (tc[:, :, :, None, :] * eye[None, :, None, :, None]).reshape(N_SLAB, SLAB_STATE, LANES)


def _from_block_c(cx):
    gl = LANES // SSM_GROUP
    d = jnp.einsum("jgpgh->jghp", cx.reshape(N_SLAB, gl, SSM_STATE, gl, SSM_GROUP))
    return d.reshape(N_SLAB * gl, SSM_GROUP, SSM_STATE)


def kernel(x, norm_mix_g, w_in, ssm_a_re, ssm_a_im, ssm_b_re, ssm_b_im, ssm_c_re, ssm_c_im, ssm_d, ssm_log_dt, ssm_glu_w, ssm_glu_b, sgu_ln_g, sgu_ln_b, sgu_w, sgu_b, out_norm_ssm_g, out_norm_sgu_g, w_out, norm_mlp_g, w_up, w_down, norm_final_g, loss_target, m_norm_mix_g, m_w_in, m_ssm_a_re, m_ssm_a_im, m_ssm_b_re, m_ssm_b_im, m_ssm_c_re, m_ssm_c_im, m_ssm_d, m_ssm_log_dt, m_ssm_glu_w, m_ssm_glu_b, m_sgu_ln_g, m_sgu_ln_b, m_sgu_w, m_sgu_b, m_out_norm_ssm_g, m_out_norm_sgu_g, m_w_out, m_norm_mlp_g, m_w_up, m_w_down, m_norm_final_g, v_norm_mix_g, v_w_in, v_ssm_a_re, v_ssm_a_im, v_ssm_b_re, v_ssm_b_im, v_ssm_c_re, v_ssm_c_im, v_ssm_d, v_ssm_log_dt, v_ssm_glu_w, v_ssm_glu_b, v_sgu_ln_g, v_sgu_ln_b, v_sgu_w, v_sgu_b, v_out_norm_ssm_g, v_out_norm_sgu_g, v_w_out, v_norm_mlp_g, v_w_up, v_w_down, v_norm_final_g):
    given = dict(locals())
    names = ["norm_mix_g", "w_in", "ssm_a_re", "ssm_a_im", "ssm_b_re", "ssm_b_im", "ssm_c_re", "ssm_c_im",
             "ssm_d", "ssm_log_dt", "ssm_glu_w", "ssm_glu_b", "sgu_ln_g", "sgu_ln_b", "sgu_w", "sgu_b",
             "out_norm_ssm_g", "out_norm_sgu_g", "w_out", "norm_mlp_g", "w_up", "w_down", "norm_final_g"]
    big = ["w_in", "ssm_glu_w", "w_out", "w_up", "w_down"]
    small = [n for n in names if n not in big]

    d = D_MODEL
    t = x.shape[1]
    tb = min(1024, t)
    xs = x[0]
    target = loss_target[0]
    nsh_in = w_in.shape[2]
    nsh_up = w_up.shape[2]
    d_ff = nsh_up * N_DEV
    n_in = nsh_in * N_DEV

    me = _index(_place())
    shard_in = w_in[0].astype(BF16)
    spread = _cols_start("gather_w_in_spread", lax.empty((d, n_in), BF16), nsh_in, False, shard_in)
    gathers = {}

    def start_gather(n, after=None):
        shard = given[n][0].astype(BF16)
        gathers[n] = _send_start("gather_start_" + n, shard, lax.empty((N_DEV,) + shard.shape, BF16), False, after)
        return gathers[n][4]

    def gathered(n, after):
        shard, blocks = _send_wait("gather_wait_" + n, gathers[n], after, False)
        return _own_block(blocks, shard, me)

    nst = N_SLAB * SLAB_STATE
    are, aim = ssm_a_re.reshape(1, nst), ssm_a_im.reshape(1, nst)
    ldt = jnp.repeat(ssm_log_dt[0], SSM_STATE).reshape(1, nst)
    bxr, bxi = _to_block_b(ssm_b_re[0]), _to_block_b(ssm_b_im[0])
    cxr, cxi = _to_block_c(ssm_c_re[0]), _to_block_c(ssm_c_im[0])
    dvec = ssm_d.reshape(1, SSM_WIDTH)

    h1 = _rms_fwd("norm_mix", xs, norm_mix_g, deps=[spread[3]])
    bre, bim, cre, cimn, apr, api, air, aii = _ssm_prep(are, aim, ldt, bxr, bxi, cxr, cxi)
    tabs = (bre, bim, cre, cimn, apr, api, air, aii)
    land_in = _cols_wait("gather_w_in_landed", spread, nsh_in, False, h1)
    passed = _cols_start("gather_w_in_pass", land_in, nsh_in, True, bre)
    wg_in = _cols_wait("gather_w_in_passed", passed, nsh_in, True, passed[3])
    wg_in = _place_cols(wg_in, shard_in)
    tokens = [start_gather("ssm_glu_w", wg_in), start_gather("w_out", wg_in)]
    bn_i = n_in // 2
    (z,) = _mm("in_proj", "nn", (t // tb, n_in // bn_i, 1),
               (h1, _bs((tb, d), lambda i, j, k: (i, 0))),
               (wg_in, _bs((d, bn_i), lambda i, j, k: (0, j))),
               [(_sds((t, n_in), F32), _bs((tb, bn_i), lambda i, j, k: (i, j)))], deps=tokens)
    tokens = [start_gather("w_up", z), start_gather("w_down", z)]
    y_pre, yg_b, p_re, p_im = _ssm_fwd(z, dvec, *tabs, deps=tokens)

    def glu_ep(acc, yp, b):
        gate = _sigmoid(acc + b)
        return _gelu(yp) * gate, gate

    hw = SSM_WIDTH // 2
    wg_glu = gathered("ssm_glu_w", yg_b).reshape(SSM_WIDTH, SSM_WIDTH)
    tile_g = _bs((tb, hw), lambda i, j, k: (i, j))
    y_ssm, gate = _mm("glu", "nn", (t // tb, 2, 1),
                      (yg_b, _bs((tb, SSM_WIDTH), lambda i, j, k: (i, 0))),
                      (wg_glu, _bs((SSM_WIDTH, hw), lambda i, j, k: (0, j))),
                      [(_sds((t, SSM_WIDTH), F32), tile_g), (_sds((t, SSM_WIDTH), F32), tile_g)],
                      extras=[(y_pre, tile_g), (ssm_glu_b, _bs((1, hw), lambda i, j, k: (0, j)))],
                      epilogue=glu_ep)

    sgu_bexp = jnp.broadcast_to(sgu_b[0][:, :, None], (SGU_HEADS, SGU_CHUNK, SGU_CHUNK))
    y_sgu = _sgu_fwd(z, sgu_ln_g, sgu_ln_b, sgu_w[0], sgu_bexp)
    mixed = _mix_norm(y_ssm, y_sgu, out_norm_ssm_g, out_norm_sgu_g)

    tb2 = min(512, t)
    row2 = _bs((tb2, d), lambda i, j, k: (i, 0))
    vec2 = _bs((1, d), lambda i, j, k: (0, 0))
    vec_sum = (_sds((1, d), F32), vec2)
    wg_out = gathered("w_out", mixed).reshape(d, d)

    def out_ep(acc, r, g):
        x2v = acc + r
        return x2v, _rms_math(x2v, g)

    x2, h2 = _mm("out_proj", "nn", (t // tb2, 1, 1),
                 (mixed, row2), (wg_out, _bs((d, d), lambda i, j, k: (0, 0))),
                 [(_sds((t, d), F32), row2), (_sds((t, d), BF16), row2)],
                 extras=[(xs, row2), (norm_mlp_g, vec2)], epilogue=out_ep, ep_rows=EP_ROWS)

    def up_ep(acc):
        r = jnp.maximum(acc, 0.0)
        return r * r, r

    tile_f = _bs((tb, nsh_up), lambda i, j, k: (i, j))
    wg_up = gathered("w_up", h2)
    f_act, r_act = _mm("mlp_up", "nn", (t // tb, N_DEV, 1),
                       (h2, _bs((tb, d), lambda i, j, k: (i, 0))),
                       (wg_up, _bs((None, d, nsh_up), lambda i, j, k: (j, 0, 0))),
                       [(_sds((t, d_ff), BF16), tile_f), (_sds((t, d_ff), BF16), tile_f)],
                       epilogue=up_ep)
    bk_d, bn_o = 2048, 1024
    tile_o = _bs((tb, bn_o), lambda i, j, k: (i, j))
    wg_down = gathered("w_down", f_act).reshape(d_ff, d)
    (x3,) = _mm("mlp_down", "nn", (t // tb, d // bn_o, d_ff // bk_d),
                (f_act, _bs((tb, bk_d), lambda i, j, k: (i, k))),
                (wg_down, _bs((bk_d, bn_o), lambda i, j, k: (k, j))),
                [(_sds((t, d), F32), tile_o)],
                extras=[(x2, tile_o)], epilogue=lambda acc, r: (acc + r,))
    dx3, dx3_b, g_final, err2 = _final_loss(x3, target, norm_final_g.reshape(1, d))
    loss = lax.psum(0.5 * jnp.sum(err2) / d, MESH_AXES)

    sends = {}

    def send_grad(n, g, land_shape=None):
        sends[n] = _send_start("grad_start_" + n, g, lax.empty(land_shape or g.shape, BF16), True)
        return [sends[n][4]]

    bn_a = 1024
    tile_a = _bs((tb, bn_a), lambda i, j, k: (i, j))
    (da,) = _mm("mlp_down_dx", "nt", (t // tb, d_ff // bn_a, 1),
                (dx3_b, _bs((tb, d), lambda i, j, k: (i, 0))),
                (wg_down, _bs((bn_a, d), lambda i, j, k: (j, 0))),
                [(_sds((t, d_ff), BF16), tile_a)],
                extras=[(r_act, tile_a)], epilogue=lambda acc, r: (acc * (2.0 * r.astype(F32)),))
    sq = 1024
    (gw_down,) = _mm("mlp_down_dw", "tn", (d_ff // sq, 1, t // tb),
                     (f_act, _bs((tb, sq), lambda i, j, k: (k, i))),
                     (dx3_b, _bs((tb, d), lambda i, j, k: (k, 0))),
                     [(_sds((d_ff, d), BF16), _bs((sq, d), lambda i, j, k: (i, 0)))])
    sent = send_grad("w_down", gw_down.reshape(N_DEV, -1, d))
    (dh2,) = _mm("mlp_up_dx", "nt", (t // tb, 1, N_DEV),
                 (da, _bs((tb, nsh_up), lambda i, j, k: (i, k))),
                 (wg_up, _bs((None, d, nsh_up), lambda i, j, k: (k, 0, 0))),
                 [(_sds((t, d), F32), _bs((tb, d), lambda i, j, k: (i, 0)))], deps=sent)

    def norm_bwd_side(dh, xv, dres, g):
        dx, dg = _rms_bwd_math(dh, xv, g)
        dx = dx + dres
        return dx, dx, dg

    up_dw_grid = (1, N_DEV, t // tb)
    side_rows = (t // math.prod(up_dw_grid), d)
    gw_up, dx2, dx2_b, g_norm_mlp = _mm(
        "mlp_up_dw", "tn", up_dw_grid,
        (h2, _bs((tb, d), lambda i, j, k: (k, 0))),
        (da, _bs((tb, nsh_up), lambda i, j, k: (k, j))),
        [(_sds((N_DEV, d, nsh_up), BF16), _bs((None, d, nsh_up), lambda i, j, k: (j, 0, 0)))],
        side=(norm_bwd_side, [(dh2, side_rows), (x2, side_rows), (dx3, side_rows), (norm_mlp_g, (1, d))],
              [(_sds((t, d), F32), side_rows), (_sds((t, d), BF16), side_rows)], [(_sds((1, d), F32), (1, d))]))
    sent = send_grad("w_up", gw_up)

    tk = min(2048, t)
    (gw_out,) = _mm("out_proj_dw", "tn", (d // sq, d // sq, t // tk),
                    (mixed, _bs((tk, sq), lambda i, j, k: (k, i))),
                    (dx2_b, _bs((tk, sq), lambda i, j, k: (k, j))),
                    [(_sds((d, d), BF16), _bs((sq, sq), lambda i, j, k: (i, j)))], deps=sent)
    sent = send_grad("w_out", gw_out.reshape(N_DEV, -1, d))
    half2 = _bs((tb2, SSM_WIDTH), lambda i, j, k: (i, 0))
    vech = _bs((1, SSM_WIDTH), lambda i, j, k: (0, 0))
    half_sum = (_sds((1, SSM_WIDTH), F32), vech)

    def out_dx_ep(acc, ya, yb, ga, gb, yp, gt):
        dya, dga = _rms_bwd_math(acc[:, :SSM_WIDTH], ya, ga)
        dyb, dgb = _rms_bwd_math(acc[:, SSM_WIDTH:], yb, gb)
        dpre = dya * _gelu(yp) * gt * (1.0 - gt)
        return dya, dyb, dpre, dga, dgb, jnp.sum(dpre, axis=0, keepdims=True)

    dy_ssm, dy_sgu, dpre_b, g_onorm_ssm, g_onorm_sgu, g_glu_b = _mm(
        "out_proj_dx", "nt", (t // tb2, 1, 1),
        (dx2_b, row2), (wg_out, _bs((d, d), lambda i, j, k: (0, 0))),
        [(_sds((t, SSM_WIDTH), F32), half2), (_sds((t, SSM_WIDTH), F32), half2),
         (_sds((t, SSM_WIDTH), BF16), half2)],
        extras=[(y_ssm, half2), (y_sgu, half2), (out_norm_ssm_g, vech), (out_norm_sgu_g, vech),
                (y_pre, half2), (gate, half2)],
        epilogue=out_dx_ep, sums=[half_sum, half_sum, half_sum], deps=sent, acc_shape=(tb2, d),
        ep_rows=EP_ROWS)

    (gw_glu,) = _mm("glu_dw", "tn", (1, 1, t // tb),
                    (yg_b, _bs((tb, SSM_WIDTH), lambda i, j, k: (k, 0))),
                    (dpre_b, _bs((tb, SSM_WIDTH), lambda i, j, k: (k, 0))),
                    [(_sds((SSM_WIDTH, SSM_WIDTH), BF16), _bs((SSM_WIDTH, SSM_WIDTH), lambda i, j, k: (0, 0)))])
    sent = send_grad("ssm_glu_w", gw_glu.reshape(N_DEV, -1, SSM_WIDTH))
    (dy_pre,) = _mm("glu_dx", "nt", (t // tb, 2, 1),
                    (dpre_b, _bs((tb, SSM_WIDTH), lambda i, j, k: (i, 0))),
                    (wg_glu, _bs((hw, SSM_WIDTH), lambda i, j, k: (j, 0))),
                    [(_sds((t, SSM_WIDTH), F32), tile_g)],
                    extras=[(dy_ssm, tile_g), (gate, tile_g), (y_pre, tile_g)],
                    epilogue=lambda acc, dy, gt, yp: ((dy * gt + acc) * _gelu_grad(yp),), deps=sent)
    du_b, dbre, dbim, dcre, dcimn, q_re, q_im, dd = _ssm_bwd(dy_pre, z, p_re, p_im, dvec, *tabs)
    dare, daim, dldt, dbxr, dbxi = _ssm_prep_bwd(are, aim, ldt, bxr, bxi, dbre, dbim, q_re, q_im)

    dz_b, g_ln_g, g_ln_b, g_sgu_w, g_sgu_bx = _sgu_bwd(dy_sgu, du_b, z, sgu_ln_g, sgu_ln_b, sgu_w[0], sgu_bexp)
    local_small = {
        "ssm_a_re": dare, "ssm_a_im": daim,
        "ssm_b_re": _from_block_b(dbxr), "ssm_b_im": _from_block_b(dbxi),
        "ssm_c_re": _from_block_c(dcre), "ssm_c_im": -_from_block_c(dcimn),
        "ssm_d": dd, "ssm_log_dt": dldt.reshape(-1, SSM_STATE).sum(axis=-1),
        "ssm_glu_b": g_glu_b, "sgu_ln_g": g_ln_g, "sgu_ln_b": g_ln_b, "sgu_w": g_sgu_w,
        "sgu_b": g_sgu_bx[:, :, 0], "out_norm_ssm_g": g_onorm_ssm, "out_norm_sgu_g": g_onorm_sgu,
        "norm_mlp_g": g_norm_mlp, "norm_final_g": g_final,
    }
    small_early = [n for n in small if n in local_small]
    small_late = [n for n in small if n not in local_small]
    packed = _pack([local_small[n] for n in small_early])
    small_send = _send_start("small_start", packed, lax.empty((N_DEV,) + packed.shape, F32), False)

    def landed_parts(n, after):
        sent_blocks, landed = _send_wait("grad_wait_" + n, sends[n], after, True)
        if sent_blocks.ndim == landed.ndim:
            own = lax.dynamic_index_in_dim(sent_blocks, me, 0, keepdims=False)
        else:
            own = lax.dynamic_slice_in_dim(sent_blocks, me * landed.shape[2], landed.shape[2], axis=1)
        return _own_block(landed, own, me)

    in_dw_grid = (d // sq, n_in // bn_i, t // tb)
    riders = ["w_down", "w_up"]
    side_ins, side_outs = [], []
    for n in riders:
        rows, cols = given[n].shape[1:]
        blk = (rows // math.prod(in_dw_grid), cols)
        side_ins += [(landed_parts(n, dz_b), (N_DEV,) + blk), (given[n][0], blk),
                     (given["m_" + n][0], blk), (given["v_" + n][0], blk)]
        side_outs += [(_sds((rows, cols), F32), blk)] * 4

    def adam_side(*tiles):
        return sum((_adam_math(*tiles[4 * r:4 * r + 4]) for r in range(len(riders))), ())

    gw_in, *rider_res = _mm("in_proj_dw", "tn", in_dw_grid,
                            (h1, _bs((tb, sq), lambda i, j, k: (k, i))),
                            (dz_b, _bs((tb, bn_i), lambda i, j, k: (k, j))),
                            [(_sds((d, n_in), BF16), _bs((sq, bn_i), lambda i, j, k: (i, j)))],
                            deps=[small_send[4]], side=(adam_side, side_ins, side_outs, []))
    sent = send_grad("w_in", gw_in, (N_DEV, d, nsh_in))

    def in_dx_ep(acc, xv, dres, g):
        dx, dg = _rms_bwd_math(acc, xv, g)
        return dx + dres, dg

    grad_x, g_norm_mix = _mm("in_proj_dx", "nt", (t // tb2, 1, n_in // bn_i),
                             (dz_b, _bs((tb2, bn_i), lambda i, j, k: (i, k))),
                             (wg_in, _bs((d, bn_i), lambda i, j, k: (0, k))),
                             [(_sds((t, d), F32), row2)],
                             extras=[(xs, row2), (dx2, row2), (norm_mix_g, vec2)],
                             epilogue=in_dx_ep, sums=[vec_sum], deps=sent, ep_rows=EP_ROWS)

    (late_parts,) = _all_gather("gather_late_grads", [_pack([g_norm_mix])])
    packed, early_parts = _send_wait("small_wait", small_send, grad_x, False)
    early_parts = _own_block(early_parts, packed, me)

    grads, deltas, new_m, new_v = {}, {}, {}, {}
    for r, n in enumerate(riders):
        grads[n], deltas[n], new_m[n], new_v[n] = [a.reshape(given[n].shape) for a in rider_res[4 * r:4 * r + 4]]
    for n in big:
        if n in riders:
            continue
        res = _adamw("adamw_" + n, landed_parts(n, grad_x), given[n][0], given["m_" + n][0], given["v_" + n][0])
        grads[n], deltas[n], new_m[n], new_v[n] = [r.reshape(given[n].shape) for r in res]
    for tag, group, parts in (("early", small_early, early_parts), ("late", small_late, late_parts)):
        like = [given[n] for n in group]
        res = _adamw("adamw_small_" + tag, parts, _pack(like), _pack([given["m_" + n] for n in group]),
                     _pack([given["v_" + n] for n in group]))
        for store, buf in zip((grads, deltas, new_m, new_v), res):
            for n, a in zip(group, _unpack(buf, like)):
                store[n] = a

    return (loss, grad_x.reshape(x.shape), *[grads[n] for n in names], *[deltas[n] for n in names],
            *[new_m[n] for n in names], *[new_v[n] for n in names])
```

```python
import functools
import math

import jax
import jax.numpy as jnp
from jax import lax
from jax.experimental import pallas as pl
from jax.experimental.pallas import tpu as pltpu

F32, BF16 = jnp.float32, jnp.bfloat16
EPS = 1e-6
N_DEV = 8
D_MODEL = 2048
SSM_WIDTH = 1024
SSM_GROUP = 16
SSM_STATE = 64
SGU_HEADS = 8
SGU_CHUNK = 128
SGU_FWD_CHUNKS = 4
LANES = 128
SUBLANES = 8
N_SLAB = SSM_WIDTH // LANES
SLAB_STATE = (LANES // SSM_GROUP) * SSM_STATE
SCAN_BLOCK = 256
SLABS_PER_STEP = 8
VMEM_LIMIT = 56 * 1024 * 1024
ROW_BLOCK = 512
EP_ROWS = 128
ADAM_ROWS = 128
MESH_AXES = ("x", "y", "c")

ADAM_LR, ADAM_B1, ADAM_B2, ADAM_EPS, ADAM_WD, ADAM_STEP = 0.001, 0.9, 0.999, 1e-08, 0.01, 10

_GELU_C0 = math.sqrt(2.0 / math.pi)
_GELU_C1 = 0.044715


def _gelu(v):
    return 0.5 * v * (1.0 + jnp.tanh(_GELU_C0 * (v + _GELU_C1 * v * v * v)))


def _gelu_grad(v):
    th = jnp.tanh(_GELU_C0 * (v + _GELU_C1 * v * v * v))
    return 0.5 * (1.0 + th) + 0.5 * v * (1.0 - th * th) * _GELU_C0 * (1.0 + 3.0 * _GELU_C1 * v * v)


def _sigmoid(v):
    return 1.0 / (1.0 + jnp.exp(-v))


def _params(sem=None):
    return pltpu.CompilerParams(dimension_semantics=sem, vmem_limit_bytes=VMEM_LIMIT)


def _dot(a, b, mode="nn"):
    dims = {"nn": ((1,), (0,)), "nt": ((1,), (1,)), "tn": ((0,), (0,))}[mode]
    return lax.dot_general(a, b, (dims, ((), ())), preferred_element_type=F32)


def _mm(name, mode, grid, a, b, outs, extras=(), epilogue=None, deps=(), sums=(), acc_shape=None,
        ep_rows=None, side=None):
    nk = grid[2]
    n_ex, n_out, n_dep, n_sum = len(extras), len(outs), len(deps), len(sums)
    assert not sums or grid[1] == 1
    if acc_shape is None:
        acc_shape = tuple(d for d in outs[0][1].block_shape if d is not None)
    side_fn, side_ins, side_outs, side_sums = side or (None, (), (), ())
    s_in, s_out, s_sum = len(side_ins), len(side_outs), len(side_sums)

    def body(*refs):
        a_ref, b_ref = refs[0], refs[1]
        ex = refs[2:2 + n_ex]
        pos = 2 + n_ex
        side_in = refs[pos:pos + s_in]
        pos += s_in + n_dep
        out_refs = refs[pos:pos + n_out]
        sum_refs = refs[pos + n_out:pos + n_out + n_sum]
        pos += n_out + n_sum
        side_out = refs[pos:pos + s_out]
        side_sum = refs[pos + s_out:pos + s_out + s_sum]
        acc = refs[-1]
        k = pl.program_id(2)

        @pl.when(k == 0)
        def _():
            acc[...] = jnp.zeros_like(acc)

        if side_sum:
            @pl.when((pl.program_id(0) == 0) & (pl.program_id(1) == 0) & (k == 0))
            def _():
                for o in side_sum:
                    o[...] = jnp.zeros_like(o)

        acc[...] += _dot(a_ref[...], b_ref[...], mode)
        if side_fn is not None:
            res = side_fn(*[r[...] for r in side_in])
            for o, r in zip(side_out, res[:s_out]):
                o[...] = r.astype(o.dtype)
            for o, r in zip(side_sum, res[s_out:]):
                o[...] += r

        def finish(rows):
            args = [e[rows, :] if e.shape[0] == acc_shape[0] else e[...] for e in ex]
            res = acc[rows, :]
            res = (res,) if epilogue is None else epilogue(res, *args)
            for o, r in zip(out_refs, res[:n_out]):
                o[rows, :] = r.astype(o.dtype)
            return tuple(res[n_out:])

        @pl.when(k == nk - 1)
        def _():
            if ep_rows is None:
                terms = finish(slice(None))
            else:
                def chunk(c, tot):
                    rows = pl.ds(pl.multiple_of(c * ep_rows, ep_rows), ep_rows)
                    return tuple(s + r for s, r in zip(tot, finish(rows)))

                zero = tuple(jnp.zeros(o.shape, F32) for o in sum_refs)
                terms = lax.fori_loop(0, acc_shape[0] // ep_rows, chunk, zero)
            for o, r in zip(sum_refs, terms):
                _add_up(o, r, pl.program_id(0) == 0)

    def side_spec(block):
        nd = len(block)
        if nd == 2 and block[0] == 1:
            return _bs(block, lambda i, j, k: (0, 0))
        return _bs(block, lambda i, j, k: (0,) * (nd - 2) + ((i * grid[1] + j) * grid[2] + k, 0))

    sem = ("arbitrary",) * 3 if sums or side else ("parallel", "parallel", "arbitrary")
    res = pl.pallas_call(
        body, name=name, grid=grid,
        in_specs=[a[1], b[1]] + [e[1] for e in extras] + [side_spec(blk) for _, blk in side_ins]
        + [_any_spec()] * n_dep,
        out_specs=[o[1] for o in outs] + [o[1] for o in sums]
        + [side_spec(blk) for _, blk in side_outs] + [side_spec(blk) for _, blk in side_sums],
        out_shape=[o[0] for o in outs] + [o[0] for o in sums]
        + [o[0] for o in side_outs] + [o[0] for o in side_sums],
        scratch_shapes=[pltpu.VMEM(acc_shape, F32)],
        compiler_params=_params(sem),
    )(a[0], b[0], *[e[0] for e in extras], *[x for x, _ in side_ins], *deps)
    return res


def _add_up(ref, term, first):
    @pl.when(first)
    def _():
        ref[...] = term

    @pl.when(jnp.logical_not(first))
    def _():
        ref[...] += term


def _rms_math(xv, g):
    return xv * lax.rsqrt(jnp.mean(xv * xv, axis=-1, keepdims=True) + EPS) * g


def _rms_bwd_math(dy, xv, g):
    r = lax.rsqrt(jnp.mean(xv * xv, axis=-1, keepdims=True) + EPS)
    xhat = xv * r
    dxhat = dy * g
    dx = r * (dxhat - xhat * jnp.mean(dxhat * xhat, axis=-1, keepdims=True))
    return dx, jnp.sum(dy * xhat, axis=0, keepdims=True)


def _sds(shape, dtype):
    return jax.ShapeDtypeStruct(shape, dtype)


def _bs(shape, fn):
    return pl.BlockSpec(shape, fn)


def _rms_fwd(name, x, g, deps=()):
    t, w = x.shape
    br = min(ROW_BLOCK, t)

    def body(*refs):
        x_ref, g_ref, o_ref = refs[0], refs[1], refs[-1]
        xv = x_ref[...]
        r = lax.rsqrt(jnp.mean(xv * xv, axis=-1, keepdims=True) + EPS)
        o_ref[...] = (xv * r * g_ref[...]).astype(BF16)

    return pl.pallas_call(
        body, name=name, grid=(t // br,),
        in_specs=[_bs((br, w), lambda i: (i, 0)), _bs((1, w), lambda i: (0, 0))] + [_any_spec()] * len(deps),
        out_specs=_bs((br, w), lambda i: (i, 0)),
        out_shape=_sds((t, w), BF16),
        compiler_params=_params(("parallel",)),
    )(x, g, *deps)


def _mix_norm(ya, yb, ga, gb):
    t, w = ya.shape
    br = min(ROW_BLOCK, t)

    def body(a_ref, b_ref, ga_ref, gb_ref, o_ref):
        for src, g_ref, col in ((a_ref, ga_ref, 0), (b_ref, gb_ref, w)):
            v = src[...]
            r = lax.rsqrt(jnp.mean(v * v, axis=-1, keepdims=True) + EPS)
            o_ref[:, col:col + w] = (v * r * g_ref[...]).astype(BF16)

    row = _bs((br, w), lambda i: (i, 0))
    vec = _bs((1, w), lambda i: (0, 0))
    return pl.pallas_call(
        body, name="mix_norm", grid=(t // br,),
        in_specs=[row, row, vec, vec],
        out_specs=_bs((br, 2 * w), lambda i: (i, 0)),
        out_shape=_sds((t, 2 * w), BF16),
        compiler_params=_params(("parallel",)),
    )(ya, yb, ga, gb)


def _loss_math(xv, target, g):
    r = lax.rsqrt(jnp.mean(xv * xv, axis=-1, keepdims=True) + EPS)
    xhat = xv * r
    err = xhat * g - target
    dy = err * (1.0 / xv.shape[-1])
    dxhat = dy * g
    dx = r * (dxhat - xhat * jnp.mean(dxhat * xhat, axis=-1, keepdims=True))
    return dx, jnp.sum(dy * xhat, axis=0, keepdims=True), jnp.sum(err * err, axis=0, keepdims=True)


def _final_loss(x3, target, g):
    t, w = x3.shape
    br = min(ROW_BLOCK, t)

    def body(x_ref, tg_ref, g_ref, dx_ref, dxb_ref, dg_ref, l_ref):
        dx, dg, e2 = _loss_math(x_ref[...], tg_ref[...], g_ref[...])
        dx_ref[...] = dx
        dxb_ref[...] = dx.astype(BF16)
        _add_up(dg_ref, dg, pl.program_id(0) == 0)
        _add_up(l_ref, e2, pl.program_id(0) == 0)

    row = _bs((br, w), lambda i: (i, 0))
    vec = _bs((1, w), lambda i: (0, 0))
    return pl.pallas_call(
        body, name="final_loss", grid=(t // br,),
        in_specs=[row, row, vec], out_specs=[row, row, vec, vec],
        out_shape=[_sds((t, w), F32), _sds((t, w), BF16), _sds((1, w), F32), _sds((1, w), F32)],
        compiler_params=_params(("arbitrary",)),
    )(x3, target, g)


def _prep_math(are, aim, ldt, bxr, bxi):
    dt = jnp.exp(ldt)
    er = jnp.exp(are * dt)
    th = aim * dt
    abr, abi = er * jnp.cos(th), er * jnp.sin(th)
    nr, ni = abr - 1.0, abi
    den = are * are + aim * aim
    cr = (nr * are + ni * aim) / den
    ci = (ni * are - nr * aim) / den
    bbr, bbi = [], []
    for j in range(N_SLAB):
        sl = slice(j * SLAB_STATE, (j + 1) * SLAB_STATE)
        bbr.append(cr[:, sl] * bxr[j] - ci[:, sl] * bxi[j])
        bbi.append(cr[:, sl] * bxi[j] + ci[:, sl] * bxr[j])
    return abr, abi, bbr, bbi


def _ssm_prep(are, aim, ldt, bxr, bxi, cxr, cxi):
    nst = N_SLAB * SLAB_STATE

    def body(are_r, aim_r, ldt_r, bxr_r, bxi_r, cxr_r, cxi_r,
             bre_o, bim_o, cre_o, cimn_o, apr_o, api_o, air_o, aii_o):
        abr, abi, bbr, bbi = _prep_math(are_r[...], aim_r[...], ldt_r[...],
                                        [bxr_r[j] for j in range(N_SLAB)], [bxi_r[j] for j in range(N_SLAB)])
        for j in range(N_SLAB):
            bre_o[j] = bbr[j].astype(BF16)
            bim_o[j] = bbi[j].astype(BF16)
        cre_o[...] = cxr_r[...].astype(BF16)
        cimn_o[...] = (-cxi_r[...]).astype(BF16)

        def step(k, cur):
            cr, ci = cur
            den = cr * cr + ci * ci
            apr_o[pl.ds(k, 1), :] = cr
            api_o[pl.ds(k, 1), :] = ci
            air_o[pl.ds(k, 1), :] = cr / den
            aii_o[pl.ds(k, 1), :] = -ci / den
            return cr * abr - ci * abi, cr * abi + ci * abr

        lax.fori_loop(0, SCAN_BLOCK, step, (jnp.ones((1, nst), F32), jnp.zeros((1, nst), F32)))

    tab = _sds((SCAN_BLOCK, nst), F32)
    return pl.pallas_call(
        body, name="ssm_prep",
        out_shape=[_sds(bxr.shape, BF16), _sds(bxr.shape, BF16), _sds(cxr.shape, BF16), _sds(cxr.shape, BF16),
                   tab, tab, tab, tab],
        compiler_params=_params(),
    )(are, aim, ldt, bxr, bxi, cxr, cxi)


def _ssm_prep_bwd(are, aim, ldt, bxr, bxi, dbre, dbim, qr, qi):
    def body(are_r, aim_r, ldt_r, bxr_r, bxi_r, dbre_r, dbim_r, qr_r, qi_r,
             dare_o, daim_o, dldt_o, dbxr_o, dbxi_o):
        prim = (are_r[...], aim_r[...], ldt_r[...],
                [bxr_r[j] for j in range(N_SLAB)], [bxi_r[j] for j in range(N_SLAB)])
        (abr, abi, _, _), vjp = jax.vjp(_prep_math, *prim)
        den = abr * abr + abi * abi
        q_r, q_i = qr_r[...], qi_r[...]
        gar = (q_r * abr - q_i * abi) / den
        gai = (q_r * abi + q_i * abr) / den
        ct = (gar, gai, [dbre_r[j] for j in range(N_SLAB)], [dbim_r[j] for j in range(N_SLAB)])
        dare, daim, dldt, dbxr, dbxi = vjp(ct)
        dare_o[...] = dare
        daim_o[...] = daim
        dldt_o[...] = dldt
        for j in range(N_SLAB):
            dbxr_o[j] = dbxr[j]
            dbxi_o[j] = dbxi[j]

    row = _sds(are.shape, F32)
    return pl.pallas_call(
        body, name="ssm_prep_bwd",
        out_shape=[row, row, row, _sds(bxr.shape, F32), _sds(bxr.shape, F32)],
        compiler_params=_params(),
    )(are, aim, ldt, bxr, bxi, dbre, dbim, qr, qi)


def _tri(lower):
    r = lax.broadcasted_iota(jnp.int32, (SCAN_BLOCK, SCAN_BLOCK), 0)
    c = lax.broadcasted_iota(jnp.int32, (SCAN_BLOCK, SCAN_BLOCK), 1)
    return jnp.where((r >= c) if lower else (r <= c), 1.0, 0.0).astype(BF16)


def _cumsum_mxu(tri, v):
    return _dot(tri, v.astype(BF16))


def _ssm_specs(t, sps):
    nt = t // SCAN_BLOCK
    tab = _bs((SCAN_BLOCK, sps * SLAB_STATE), lambda j, i: (0, j))
    bmat = _bs((sps, LANES, SLAB_STATE), lambda j, i: (j, 0, 0))
    cmat = _bs((sps, SLAB_STATE, LANES), lambda j, i: (j, 0, 0))
    return nt, tab, bmat, cmat


def _slab_slices(s):
    return slice(s * LANES, (s + 1) * LANES), slice(s * SLAB_STATE, (s + 1) * SLAB_STATE)


def _ssm_fwd(z, dvec, bre, bim, cre, cimn, apr, api, air, aii, deps=()):
    t = z.shape[0]
    sps = SLABS_PER_STEP
    nt, tab, bmat, cmat = _ssm_specs(t, sps)
    nst = N_SLAB * SLAB_STATE
    last = SCAN_BLOCK - 1

    def body(*refs):
        u_ref, d_ref, bre_r, bim_r, cre_r, cimn_r, apr_r, api_r, air_r, aii_r = refs[:10]
        y_ref, yg_ref, pr_ref, pi_ref, car_r, car_i = refs[10 + len(deps):]

        @pl.when(pl.program_id(1) == 0)
        def _():
            car_r[...] = jnp.zeros_like(car_r)
            car_i[...] = jnp.zeros_like(car_i)

        tri = _tri(True)
        for s in range(sps):
            ul, sl = _slab_slices(s)
            u = u_ref[:, ul]
            ub = u.astype(BF16)
            bur, bui = _dot(ub, bre_r[s]), _dot(ub, bim_r[s])
            ir, ii = air_r[:, sl], aii_r[:, sl]
            csr = _cumsum_mxu(tri, ir * bur - ii * bui)
            csi = _cumsum_mxu(tri, ir * bui + ii * bur)
            pr, pi = apr_r[:, sl], api_r[:, sl]
            a_r, a_i = apr_r[1:2, sl], api_r[1:2, sl]
            c_r, c_i = car_r[:, sl], car_i[:, sl]
            wr = csr + (a_r * c_r - a_i * c_i)
            wi = csi + (a_r * c_i + a_i * c_r)
            sr = pr * wr - pi * wi
            si = pr * wi + pi * wr
            car_r[:, sl] = sr[last:last + 1, :]
            car_i[:, sl] = si[last:last + 1, :]
            pr_ref[:, sl] = (sr - bur).astype(BF16)
            pi_ref[:, sl] = (si - bui).astype(BF16)
            y = _dot(sr.astype(BF16), cre_r[s]) + _dot(si.astype(BF16), cimn_r[s]) + d_ref[:, ul] * u
            y_ref[:, ul] = y
            yg_ref[:, ul] = _gelu(y).astype(BF16)

    ublk = _bs((SCAN_BLOCK, sps * LANES), lambda j, i: (i, j))
    sblk = _bs((SCAN_BLOCK, sps * SLAB_STATE), lambda j, i: (i, j))
    return pl.pallas_call(
        body, name="ssm_fwd", grid=(N_SLAB // sps, nt),
        in_specs=[ublk, _bs((1, sps * LANES), lambda j, i: (0, j)), bmat, bmat, cmat, cmat, tab, tab, tab, tab]
        + [_any_spec()] * len(deps),
        out_specs=[ublk, ublk, sblk, sblk],
        out_shape=[_sds((t, SSM_WIDTH), F32), _sds((t, SSM_WIDTH), BF16),
                   _sds((t, nst), BF16), _sds((t, nst), BF16)],
        scratch_shapes=[pltpu.VMEM((1, sps * SLAB_STATE), F32), pltpu.VMEM((1, sps * SLAB_STATE), F32)],
        compiler_params=_params(("parallel", "arbitrary")),
    )(z, dvec, bre, bim, cre, cimn, apr, api, air, aii, *deps)


def _ssm_bwd(gy, z, p_re, p_im, dvec, bre, bim, cre, cimn, apr, api, air, aii):
    t = z.shape[0]
    sps = SLABS_PER_STEP
    nt, tab, bmat, cmat = _ssm_specs(t, sps)
    last = SCAN_BLOCK - 1

    def fold(v):
        return v.reshape(SCAN_BLOCK // SUBLANES, SUBLANES, v.shape[-1]).sum(axis=0)

    def body(g_ref, u_ref, pr_ref, pi_ref, d_ref, bre_r, bim_r, cre_r, cimn_r, apr_r, api_r, air_r, aii_r,
             du_ref, dbre_o, dbim_o, dcre_o, dcimn_o, qr_o, qi_o, dd_o, car_r, car_i, qacc_r, qacc_i, dacc):
        i = pl.program_id(1)

        @pl.when(i == 0)
        def _():
            for ref in (car_r, car_i, qacc_r, qacc_i, dacc, dbre_o, dbim_o, dcre_o, dcimn_o):
                ref[...] = jnp.zeros_like(ref)

        tri = _tri(False)
        for s in range(sps):
            ul, sl = _slab_slices(s)
            g = g_ref[:, ul]
            gb = g.astype(BF16)
            u = u_ref[:, ul]
            ub = u.astype(BF16)
            bur, bui = _dot(ub, bre_r[s]), _dot(ub, bim_r[s])
            p_r, p_i = pr_ref[:, sl].astype(F32), pi_ref[:, sl].astype(F32)
            srb, sib = (p_r + bur).astype(BF16), (p_i + bui).astype(BF16)
            dcre_o[s] += _dot(srb, gb, "tn")
            dcimn_o[s] += _dot(sib, gb, "tn")
            dsr, dsi = _dot(gb, cre_r[s], "nt"), _dot(gb, cimn_r[s], "nt")
            pr, pi = apr_r[:, sl], api_r[:, sl]
            csr = _cumsum_mxu(tri, pr * dsr + pi * dsi)
            csi = _cumsum_mxu(tri, pr * dsi - pi * dsr)
            al_r, al_i = apr_r[last:last + 1, sl], api_r[last:last + 1, sl]
            c_r, c_i = car_r[:, sl], car_i[:, sl]
            wr = csr + (al_r * c_r + al_i * c_i)
            wi = csi + (al_r * c_i - al_i * c_r)
            ir, ii = air_r[:, sl], aii_r[:, sl]
            lr = ir * wr + ii * wi
            li = ir * wi - ii * wr
            a_r, a_i = apr_r[1:2, sl], api_r[1:2, sl]
            car_r[:, sl] = a_r * lr[0:1, :] + a_i * li[0:1, :]
            car_i[:, sl] = a_r * li[0:1, :] - a_i * lr[0:1, :]
            lrb, lib = lr.astype(BF16), li.astype(BF16)
            dbre_o[s] += _dot(ub, lrb, "tn")
            dbim_o[s] += _dot(ub, lib, "tn")
            du = d_ref[:, ul] * g + _dot(lrb, bre_r[s], "nt") + _dot(lib, bim_r[s], "nt")
            du_ref[:, ul] = du.astype(BF16)
            qacc_r[:, sl] += fold(lr * p_r + li * p_i)
            qacc_i[:, sl] += fold(li * p_r - lr * p_i)
            dacc[:, ul] += fold(g * u)

        @pl.when(i == nt - 1)
        def _():
            qr_o[...] = jnp.sum(qacc_r[...], axis=0, keepdims=True)
            qi_o[...] = jnp.sum(qacc_i[...], axis=0, keepdims=True)
            dd_o[...] = jnp.sum(dacc[...], axis=0, keepdims=True)

    rev = lambda j, i: (nt - 1 - i, j)
    ublk = _bs((SCAN_BLOCK, sps * LANES), rev)
    sblk = _bs((SCAN_BLOCK, sps * SLAB_STATE), rev)
    qrow = _bs((1, sps * SLAB_STATE), lambda j, i: (0, j))
    urow = _bs((1, sps * LANES), lambda j, i: (0, j))
    nst = N_SLAB * SLAB_STATE
    return pl.pallas_call(
        body, name="ssm_bwd", grid=(N_SLAB // sps, nt),
        in_specs=[ublk, ublk, sblk, sblk, urow, bmat, bmat, cmat, cmat, tab, tab, tab, tab],
        out_specs=[ublk, bmat, bmat, cmat, cmat, qrow, qrow, urow],
        out_shape=[_sds((t, SSM_WIDTH), BF16),
                   _sds((N_SLAB, LANES, SLAB_STATE), F32), _sds((N_SLAB, LANES, SLAB_STATE), F32),
                   _sds((N_SLAB, SLAB_STATE, LANES), F32), _sds((N_SLAB, SLAB_STATE, LANES), F32),
                   _sds((1, nst), F32), _sds((1, nst), F32), _sds((1, SSM_WIDTH), F32)],
        scratch_shapes=[pltpu.VMEM((1, sps * SLAB_STATE), F32), pltpu.VMEM((1, sps * SLAB_STATE), F32),
                        pltpu.VMEM((SUBLANES, sps * SLAB_STATE), F32), pltpu.VMEM((SUBLANES, sps * SLAB_STATE), F32),
                        pltpu.VMEM((SUBLANES, sps * LANES), F32)],
        compiler_params=_params(("parallel", "arbitrary")),
    )(gy, z, p_re, p_im, dvec, bre, bim, cre, cimn, apr, api, air, aii)


def _sgu_mask():
    r = lax.broadcasted_iota(jnp.int32, (SGU_CHUNK, SGU_CHUNK), 0)
    c = lax.broadcasted_iota(jnp.int32, (SGU_CHUNK, SGU_CHUNK), 1)
    return r >= c


def _sgu_common(zu, zv, lng, lnb):
    us, v = _gelu(zu), _gelu(zv)
    mu = jnp.mean(v, axis=-1, keepdims=True)
    vc = v - mu
    rstd = lax.rsqrt(jnp.mean(vc * vc, axis=-1, keepdims=True) + EPS)
    xhat = vc * rstd
    return us, xhat, rstd, xhat * lng + lnb


def _sgu_fwd(z, lng, lnb, w, bexp):
    t = z.shape[0]
    hd = SGU_CHUNK
    rows = min(SGU_FWD_CHUNKS * SGU_CHUNK, t)

    def body(zu_ref, zv_ref, lng_ref, lnb_ref, w_ref, b_ref, y_ref):
        mask = _sgu_mask()
        wts = [jnp.where(mask, w_ref[h], 0.0).astype(BF16) for h in range(SGU_HEADS)]
        for c in range(rows // SGU_CHUNK):
            rs = slice(c * SGU_CHUNK, (c + 1) * SGU_CHUNK)
            us, _, _, vn = _sgu_common(zu_ref[rs, :], zv_ref[rs, :], lng_ref[...], lnb_ref[...])
            vnb = vn.astype(BF16)
            for h in range(SGU_HEADS):
                sl = slice(h * hd, (h + 1) * hd)
                y_ref[rs, sl] = us[:, sl] * (_dot(wts[h], vnb[:, sl]) + b_ref[h])

    row = lambda c: _bs((rows, SSM_WIDTH), lambda i: (i, c))
    vec = _bs((1, SSM_WIDTH), lambda i: (0, 0))
    hmat = _bs((SGU_HEADS, hd, hd), lambda i: (0, 0, 0))
    return pl.pallas_call(
        body, name="sgu_fwd", grid=(t // rows,),
        in_specs=[row(1), row(2), vec, vec, hmat, hmat],
        out_specs=row(0), out_shape=_sds((t, SSM_WIDTH), F32),
        compiler_params=_params(("parallel",)),
    )(z, z, lng, lnb, w, bexp)


def _sgu_bwd(dy, du_ssm, z, lng, lnb, w, bexp, deps=()):
    t = z.shape[0]
    hd = SGU_CHUNK
    nc = t // SGU_CHUNK

    def body(*refs):
        dy_ref, dus_ref, zu_ref, zv_ref, lng_ref, lnb_ref, w_ref, b_ref = refs[:8]
        dz_ref, dlng_o, dlnb_o, dw_o, db_o = refs[8 + len(deps):]
        i = pl.program_id(0)

        @pl.when(i == 0)
        def _():
            for ref in (dlng_o, dlnb_o, dw_o, db_o):
                ref[...] = jnp.zeros_like(ref)

        zu, zv = zu_ref[...], zv_ref[...]
        lng = lng_ref[...]
        us, xhat, rstd, vn = _sgu_common(zu, zv, lng, lnb_ref[...])
        vnb = vn.astype(BF16)
        dyv = dy_ref[...]
        mask = _sgu_mask()
        dus_parts, dvn_parts = [], []
        for h in range(SGU_HEADS):
            sl = slice(h * hd, (h + 1) * hd)
            wt = jnp.where(mask, w_ref[h], 0.0).astype(BF16)
            mixed = _dot(wt, vnb[:, sl]) + b_ref[h]
            dus_parts.append(dyv[:, sl] * mixed)
            dmix = dyv[:, sl] * us[:, sl]
            dmb = dmix.astype(BF16)
            db_o[h] += dmix
            dw_o[h] += _dot(dmb, vnb[:, sl], "nt")
            dvn_parts.append(_dot(wt, dmb, "tn"))
        dus = jnp.concatenate(dus_parts, axis=1)
        dvn = jnp.concatenate(dvn_parts, axis=1)
        dlng_o[...] += jnp.sum(dvn * xhat, axis=0, keepdims=True)
        dlnb_o[...] += jnp.sum(dvn, axis=0, keepdims=True)
        dxh = dvn * lng
        dv = rstd * (dxh - jnp.mean(dxh, axis=-1, keepdims=True)
                     - xhat * jnp.mean(dxh * xhat, axis=-1, keepdims=True))
        dz_ref[:, 0:SSM_WIDTH] = dus_ref[...]
        dz_ref[:, SSM_WIDTH:2 * SSM_WIDTH] = (dus * _gelu_grad(zu)).astype(BF16)
        dz_ref[:, 2 * SSM_WIDTH:] = (dv * _gelu_grad(zv)).astype(BF16)

        @pl.when(i == nc - 1)
        def _():
            for h in range(SGU_HEADS):
                dw_o[h] = jnp.where(mask, dw_o[h], 0.0)
                db_o[h] = jnp.broadcast_to(jnp.sum(db_o[h], axis=1, keepdims=True), (hd, hd))

    row = lambda c: _bs((SGU_CHUNK, SSM_WIDTH), lambda i: (i, c))
    vec = _bs((1, SSM_WIDTH), lambda i: (0, 0))
    hmat = _bs((SGU_HEADS, hd, hd), lambda i: (0, 0, 0))
    return pl.pallas_call(
        body, name="sgu_bwd", grid=(nc,),
        in_specs=[row(0), row(0), row(1), row(2), vec, vec, hmat, hmat] + [_any_spec()] * len(deps),
        out_specs=[_bs((SGU_CHUNK, 3 * SSM_WIDTH), lambda i: (i, 0)), vec, vec, hmat, hmat],
        out_shape=[_sds((t, 3 * SSM_WIDTH), BF16), _sds((1, SSM_WIDTH), F32), _sds((1, SSM_WIDTH), F32),
                   _sds((SGU_HEADS, hd, hd), F32), _sds((SGU_HEADS, hd, hd), F32)],
        compiler_params=_params(("arbitrary",)),
    )(dy, du_ssm, z, z, lng, lnb, w, bexp, *deps)


def _place():
    x, y, c = (lax.axis_index(a) for a in MESH_AXES)
    return x, y, c


def _index(p):
    return 4 * p[0] + 2 * p[1] + p[2]


def _any_spec():
    return pl.BlockSpec(memory_space=pl.ANY)


def _col_block(ref, k, width):
    return ref.at[:, pl.ds(pl.multiple_of(k * width, LANES), width)]


def _all_gather(name, shards, by_columns=False):
    n = len(shards)

    def body(*refs):
        ins, outs = refs[:n], refs[n:2 * n]
        send, recv, loc = refs[2 * n:]
        x, y, c = _place()
        me, sib = (x, y, c), (x, y, 1 - c)
        chips = [(1 - x, y), (x, 1 - y), (1 - x, 1 - y)]

        def blk(w, p):
            if by_columns:
                return _col_block(outs[w], _index(p), shards[w].shape[1])
            return outs[w].at[_index(p)]

        def cp(w, k, block, to, src=None):
            dst = blk(w, block)
            return pltpu.make_async_remote_copy(
                src_ref=dst if src is None else src, dst_ref=dst,
                send_sem=send.at[w * 7 + k], recv_sem=recv.at[w * 7 + k],
                device_id=to, device_id_type=pl.DeviceIdType.MESH)

        mines, sends = [], []
        for w in range(n):
            m = pltpu.make_async_copy(ins[w], blk(w, me), loc.at[w])
            m.start()
            mines.append(m)
            first = [cp(w, 0, me, sib, src=ins[w])]
            first += [cp(w, 1 + j, me, (*chip, c), src=ins[w]) for j, chip in enumerate(chips)]
            for q in first:
                q.start()
            sends += first
        for j, chip in enumerate(chips):
            for w in range(n):
                cp(w, 1 + j, (*chip, c), me).wait_recv()
                q = cp(w, 4 + j, (*chip, c), sib)
                q.start()
                sends.append(q)
        for w in range(n):
            cp(w, 0, sib, me).wait_recv()
            for j, chip in enumerate(chips):
                cp(w, 4 + j, (*chip, 1 - c), me).wait_recv()
        for q in sends:
            q.wait_send()
        for m in mines:
            m.wait()

    return pl.pallas_call(
        body, name=name,
        in_specs=[_any_spec()] * n, out_specs=[_any_spec()] * n,
        out_shape=[_sds((s.shape[0], N_DEV * s.shape[1]) if by_columns else (N_DEV,) + s.shape, s.dtype)
                   for s in shards],
        scratch_shapes=[pltpu.SemaphoreType.DMA((n * 7,)), pltpu.SemaphoreType.DMA((n * 7,)),
                        pltpu.SemaphoreType.DMA((n,))],
        compiler_params=pltpu.CompilerParams(has_side_effects=True),
    )(*shards)


def _peer(r, x, y, c):
    return ((1 - x) if r & 4 else x, (1 - y) if r & 2 else y, (1 - c) if r & 1 else c)


def _sent_block(src_ref, land_ref, k, scatter):
    if not scatter:
        return src_ref
    if len(src_ref.shape) == len(land_ref.shape):
        return src_ref.at[k]
    return _col_block(src_ref, k, land_ref.shape[2])


def _send_start(name, src, land, scatter, after=None):
    n_after = 0 if after is None else 1

    def body(*refs):
        src_ref, land_ref = refs[0], refs[1]
        send, recv, _, _, token = refs[2 + n_after:]
        x, y, c = _place()
        me = _index((x, y, c))
        for r in range(1, N_DEV):
            p = _peer(r, x, y, c)
            pltpu.make_async_remote_copy(
                src_ref=_sent_block(src_ref, land_ref, _index(p), scatter), dst_ref=land_ref.at[me],
                send_sem=send.at[r - 1], recv_sem=recv.at[r - 1],
                device_id=p, device_id_type=pl.DeviceIdType.MESH).start()
        token[...] = jnp.zeros_like(token)

    hbm, sem = pl.BlockSpec(memory_space=pltpu.HBM), pl.BlockSpec(memory_space=pltpu.SEMAPHORE)
    return pl.pallas_call(
        body, name=name,
        out_shape=(pltpu.SemaphoreType.DMA((N_DEV - 1,)), pltpu.SemaphoreType.DMA((N_DEV - 1,)),
                   pltpu.HBM(src.shape, src.dtype), pltpu.HBM(land.shape, land.dtype),
                   _sds((SUBLANES, LANES), F32)),
        in_specs=(hbm, hbm) + (_any_spec(),) * n_after,
        out_specs=(sem, sem, hbm, hbm, pl.BlockSpec(memory_space=pltpu.VMEM)),
        input_output_aliases={0: 2, 1: 3},
        compiler_params=pltpu.CompilerParams(has_side_effects=pltpu.SideEffectType.DATAFLOW_SIDE_EFFECTING),
    )(pltpu.with_memory_space_constraint(src, pltpu.HBM), pltpu.with_memory_space_constraint(land, pltpu.HBM),
      *([] if after is None else [after]))


def _send_wait(name, started, after, scatter):
    send, recv, src_thru, land_thru, _ = started

    def body(src_ref, land_ref, send_r, recv_r, after_ref, src_out, land_out):
        x, y, c = _place()
        for r in range(1, N_DEV):
            p = _peer(r, x, y, c)
            k = _index(p)
            cp = pltpu.make_async_remote_copy(
                src_ref=_sent_block(src_ref, land_ref, k, scatter), dst_ref=land_ref.at[k],
                send_sem=send_r.at[r - 1], recv_sem=recv_r.at[r - 1],
                device_id=p, device_id_type=pl.DeviceIdType.MESH)
            cp.wait_send()
            cp.wait_recv()

    hbm, sem = pl.BlockSpec(memory_space=pltpu.HBM), pl.BlockSpec(memory_space=pltpu.SEMAPHORE)
    return pl.pallas_call(
        body, name=name,
        out_shape=(pltpu.HBM(src_thru.shape, src_thru.dtype), pltpu.HBM(land_thru.shape, land_thru.dtype)),
        in_specs=(hbm, hbm, sem, sem, _any_spec()), out_specs=(hbm, hbm),
        input_output_aliases={0: 0, 1: 1},
        compiler_params=pltpu.CompilerParams(has_side_effects=pltpu.SideEffectType.DATAFLOW_SIDE_EFFECTING),
    )(src_thru, land_thru, send, recv, after)


SPREAD, PASS_ON = (1, 2, 4, 6), (2, 4, 6)


def _cols_start(name, land, width, pass_on, after):
    peers = PASS_ON if pass_on else SPREAD

    def body(land_ref, after_ref, send, recv, land_thru, token):
        x, y, c = _place()
        for n, r in enumerate(peers):
            block = _col_block(land_ref, _index(_peer(r, x, y, c) if pass_on else (x, y, c)), width)
            pltpu.make_async_remote_copy(
                src_ref=block if pass_on else after_ref, dst_ref=block,
                send_sem=send.at[n], recv_sem=recv.at[n],
                device_id=_peer(1, x, y, c) if pass_on else _peer(r, x, y, c),
                device_id_type=pl.DeviceIdType.MESH).start()
        token[...] = jnp.zeros_like(token)

    hbm, sem = pl.BlockSpec(memory_space=pltpu.HBM), pl.BlockSpec(memory_space=pltpu.SEMAPHORE)
    return pl.pallas_call(
        body, name=name,
        out_shape=(pltpu.SemaphoreType.DMA((len(peers),)), pltpu.SemaphoreType.DMA((len(peers),)),
                   pltpu.HBM(land.shape, land.dtype), _sds((SUBLANES, LANES), F32)),
        in_specs=(hbm, _any_spec()), out_specs=(sem, sem, hbm, pl.BlockSpec(memory_space=pltpu.VMEM)),
        input_output_aliases={0: 2},
        compiler_params=pltpu.CompilerParams(has_side_effects=pltpu.SideEffectType.DATAFLOW_SIDE_EFFECTING),
    )(pltpu.with_memory_space_constraint(land, pltpu.HBM), after)


def _cols_wait(name, started, width, pass_on, after):
    send, recv, land_thru, _ = started
    peers = PASS_ON if pass_on else SPREAD

    def body(land_ref, send_r, recv_r, after_ref, land_out):
        x, y, c = _place()
        for n, r in enumerate(peers):
            sent = _col_block(land_ref, _index(_peer(r, x, y, c) if pass_on else (x, y, c)), width)
            came = _col_block(land_ref, _index(_peer(r ^ 1 if pass_on else r, x, y, c)), width)
            cp = pltpu.make_async_remote_copy(
                src_ref=sent, dst_ref=came, send_sem=send_r.at[n], recv_sem=recv_r.at[n],
                device_id=_peer(1, x, y, c) if pass_on else _peer(r, x, y, c),
                device_id_type=pl.DeviceIdType.MESH)
            cp.wait_send()
            cp.wait_recv()

    hbm, sem = pl.BlockSpec(memory_space=pltpu.HBM), pl.BlockSpec(memory_space=pltpu.SEMAPHORE)
    return pl.pallas_call(
        body, name=name, out_shape=(pltpu.HBM(land_thru.shape, land_thru.dtype),),
        in_specs=(hbm, sem, sem, _any_spec()), out_specs=(hbm,), input_output_aliases={0: 0},
        compiler_params=pltpu.CompilerParams(has_side_effects=pltpu.SideEffectType.DATAFLOW_SIDE_EFFECTING),
    )(land_thru, send, recv, after)[0]


def _own_block(blocks, block, me):
    return lax.dynamic_update_index_in_dim(blocks, block, me, 0)


def _adam_math(parts, w, m, v):
    c1 = 1.0 / (1.0 - ADAM_B1 ** ADAM_STEP)
    c2 = 1.0 / (1.0 - ADAM_B2 ** ADAM_STEP)
    g = parts[0].astype(F32)
    for k in range(1, N_DEV):
        g = g + parts[k].astype(F32)
    mn = ADAM_B1 * m + (1.0 - ADAM_B1) * g
    vn = ADAM_B2 * v + (1.0 - ADAM_B2) * (g * g)
    return g, -ADAM_LR * ((mn * c1) / (jnp.sqrt(vn * c2) + ADAM_EPS) + ADAM_WD * w), mn, vn


def _adamw(name, parts, w, m, v):
    rows, cols = w.shape
    br = min(ADAM_ROWS, rows)

    def body(p_ref, w_ref, m_ref, v_ref, g_o, d_o, m_o, v_o):
        g_o[...], d_o[...], m_o[...], v_o[...] = _adam_math(p_ref[...], w_ref[...], m_ref[...], v_ref[...])

    blk = _bs((br, cols), lambda i: (i, 0))
    out = _sds((rows, cols), F32)
    return pl.pallas_call(
        body, name=name, grid=(rows // br,),
        in_specs=[_bs((N_DEV, br, cols), lambda i: (0, i, 0)), blk, blk, blk],
        out_specs=[blk] * 4, out_shape=[out] * 4,
        compiler_params=_params(("parallel",)),
    )(parts, w, m, v)


def _pack(arrs):
    tile = SUBLANES * LANES
    flat = []
    for a in arrs:
        f = a.reshape(-1).astype(F32)
        pad = (-f.shape[0]) % tile
        flat.append(jnp.pad(f, (0, pad)) if pad else f)
    total = sum(f.shape[0] for f in flat)
    tail = (-total) % (ADAM_ROWS * LANES)
    if tail:
        flat.append(jnp.zeros((tail,), F32))
    return jnp.concatenate(flat).reshape(-1, LANES)


def _unpack(buf, like):
    tile = SUBLANES * LANES
    flat = buf.reshape(-1)
    out, off = [], 0
    for a in like:
        n = math.prod(a.shape)
        out.append(flat[off:off + n].reshape(a.shape))
        off += n + ((-n) % tile)
    return out


def _to_block_b(b):
    gl = LANES // SSM_GROUP
    tb = b.reshape(N_SLAB, gl, SSM_STATE, SSM_GROUP).transpose(0, 1, 3, 2)
    eye = jnp.eye(gl, dtype=F32)
    return (tb[:, :, :, None, :] * eye[None, :, None, :, None]).reshape(N_SLAB, LANES, SLAB_STATE)


def _from_block_b(bx):
    gl = LANES // SSM_GROUP
    d = jnp.einsum("jghgp->jgph", bx.reshape(N_SLAB, gl, SSM_GROUP, gl, SSM_STATE))
    return d.reshape(N_SLAB * gl, SSM_STATE, SSM_GROUP)


def _to_block_c(cm):
    gl = LANES // SSM_GROUP
    tc = cm.reshape(N_SLAB, gl, SSM_GROUP, SSM_STATE).transpose(0, 1, 3, 2)
    eye = jnp.eye(gl, dtype=F32)
    return (tc[:, :, :, None, :] * eye[None, :, None, :, None]).reshape(N_SLAB, SLAB_STATE, LANES)


def _from_block_c(cx):
    gl = LANES // SSM_GROUP
    d = jnp.einsum("jgpgh->jghp", cx.reshape(N_SLAB, gl, SSM_STATE, gl, SSM_GROUP))
    return d.reshape(N_SLAB * gl, SSM_GROUP, SSM_STATE)


def kernel(x, norm_mix_g, w_in, ssm_a_re, ssm_a_im, ssm_b_re, ssm_b_im, ssm_c_re, ssm_c_im, ssm_d, ssm_log_dt, ssm_glu_w, ssm_glu_b, sgu_ln_g, sgu_ln_b, sgu_w, sgu_b, out_norm_ssm_g, out_norm_sgu_g, w_out, norm_mlp_g, w_up, w_down, norm_final_g, loss_target, m_norm_mix_g, m_w_in, m_ssm_a_re, m_ssm_a_im, m_ssm_b_re, m_ssm_b_im, m_ssm_c_re, m_ssm_c_im, m_ssm_d, m_ssm_log_dt, m_ssm_glu_w, m_ssm_glu_b, m_sgu_ln_g, m_sgu_ln_b, m_sgu_w, m_sgu_b, m_out_norm_ssm_g, m_out_norm_sgu_g, m_w_out, m_norm_mlp_g, m_w_up, m_w_down, m_norm_final_g, v_norm_mix_g, v_w_in, v_ssm_a_re, v_ssm_a_im, v_ssm_b_re, v_ssm_b_im, v_ssm_c_re, v_ssm_c_im, v_ssm_d, v_ssm_log_dt, v_ssm_glu_w, v_ssm_glu_b, v_sgu_ln_g, v_sgu_ln_b, v_sgu_w, v_sgu_b, v_out_norm_ssm_g, v_out_norm_sgu_g, v_w_out, v_norm_mlp_g, v_w_up, v_w_down, v_norm_final_g):
    given = dict(locals())
    names = ["norm_mix_g", "w_in", "ssm_a_re", "ssm_a_im", "ssm_b_re", "ssm_b_im", "ssm_c_re", "ssm_c_im",
             "ssm_d", "ssm_log_dt", "ssm_glu_w", "ssm_glu_b", "sgu_ln_g", "sgu_ln_b", "sgu_w", "sgu_b",
             "out_norm_ssm_g", "out_norm_sgu_g", "w_out", "norm_mlp_g", "w_up", "w_down", "norm_final_g"]
    big = ["w_in", "ssm_glu_w", "w_out", "w_up", "w_down"]
    small = [n for n in names if n not in big]

    d = D_MODEL
    t = x.shape[1]
    tb = min(1024, t)
    xs = x[0]
    target = loss_target[0]
    nsh_in = w_in.shape[2]
    nsh_up = w_up.shape[2]
    d_ff = nsh_up * N_DEV
    n_in = nsh_in * N_DEV

    me = _index(_place())
    shard_in = w_in[0].astype(BF16)
    spread = _cols_start("gather_w_in_spread", lax.empty((d, n_in), BF16), nsh_in, False, shard_in)
    gathers = {}

    def start_gather(n, after=None):
        shard = given[n][0].astype(BF16)
        gathers[n] = _send_start("gather_start_" + n, shard, lax.empty((N_DEV,) + shard.shape, BF16), False, after)
        return gathers[n][4]

    def gathered(n, after):
        shard, blocks = _send_wait("gather_wait_" + n, gathers[n], after, False)
        return _own_block(blocks, shard, me)

    nst = N_SLAB * SLAB_STATE
    are, aim = ssm_a_re.reshape(1, nst), ssm_a_im.reshape(1, nst)
    ldt = jnp.repeat(ssm_log_dt[0], SSM_STATE).reshape(1, nst)
    bxr, bxi = _to_block_b(ssm_b_re[0]), _to_block_b(ssm_b_im[0])
    cxr, cxi = _to_block_c(ssm_c_re[0]), _to_block_c(ssm_c_im[0])
    dvec = ssm_d.reshape(1, SSM_WIDTH)

    h1 = _rms_fwd("norm_mix", xs, norm_mix_g, deps=[spread[3]])
    bre, bim, cre, cimn, apr, api, air, aii = _ssm_prep(are, aim, ldt, bxr, bxi, cxr, cxi)
    tabs = (bre, bim, cre, cimn, apr, api, air, aii)
    land_in = _cols_wait("gather_w_in_landed", spread, nsh_in, False, h1)
    passed = _cols_start("gather_w_in_pass", land_in, nsh_in, True, bre)
    wg_in = _cols_wait("gather_w_in_passed", passed, nsh_in, True, passed[3])
    wg_in = lax.dynamic_update_slice_in_dim(wg_in, shard_in, me * nsh_in, axis=1)
    tokens = [start_gather("ssm_glu_w", wg_in), start_gather("w_out", wg_in)]
    bn_i = n_in // 2
    (z,) = _mm("in_proj", "nn", (t // tb, n_in // bn_i, 1),
               (h1, _bs((tb, d), lambda i, j, k: (i, 0))),
               (wg_in, _bs((d, bn_i), lambda i, j, k: (0, j))),
               [(_sds((t, n_in), F32), _bs((tb, bn_i), lambda i, j, k: (i, j)))], deps=tokens)
    tokens = [start_gather("w_up", z), start_gather("w_down", z)]
    y_pre, yg_b, p_re, p_im = _ssm_fwd(z, dvec, *tabs, deps=tokens)

    def glu_ep(acc, yp, b):
        gate = _sigmoid(acc + b)
        return _gelu(yp) * gate, gate

    hw = SSM_WIDTH // 2
    wg_glu = gathered("ssm_glu_w", yg_b).reshape(SSM_WIDTH, SSM_WIDTH)
    tile_g = _bs((tb, hw), lambda i, j, k: (i, j))
    y_ssm, gate = _mm("glu", "nn", (t // tb, 2, 1),
                      (yg_b, _bs((tb, SSM_WIDTH), lambda i, j, k: (i, 0))),
                      (wg_glu, _bs((SSM_WIDTH, hw), lambda i, j, k: (0, j))),
                      [(_sds((t, SSM_WIDTH), F32), tile_g), (_sds((t, SSM_WIDTH), F32), tile_g)],
                      extras=[(y_pre, tile_g), (ssm_glu_b, _bs((1, hw), lambda i, j, k: (0, j)))],
                      epilogue=glu_ep)

    sgu_bexp = jnp.broadcast_to(sgu_b[0][:, :, None], (SGU_HEADS, SGU_CHUNK, SGU_CHUNK))
    y_sgu = _sgu_fwd(z, sgu_ln_g, sgu_ln_b, sgu_w[0], sgu_bexp)
    mixed = _mix_norm(y_ssm, y_sgu, out_norm_ssm_g, out_norm_sgu_g)

    tb2 = min(512, t)
    row2 = _bs((tb2, d), lambda i, j, k: (i, 0))
    vec2 = _bs((1, d), lambda i, j, k: (0, 0))
    vec_sum = (_sds((1, d), F32), vec2)
    wg_out = gathered("w_out", mixed).reshape(d, d)

    def out_ep(acc, r, g):
        x2v = acc + r
        return x2v, _rms_math(x2v, g)

    x2, h2 = _mm("out_proj", "nn", (t // tb2, 1, 1),
                 (mixed, row2), (wg_out, _bs((d, d), lambda i, j, k: (0, 0))),
                 [(_sds((t, d), F32), row2), (_sds((t, d), BF16), row2)],
                 extras=[(xs, row2), (norm_mlp_g, vec2)], epilogue=out_ep, ep_rows=EP_ROWS)

    def up_ep(acc):
        r = jnp.maximum(acc, 0.0)
        return r * r, r

    tile_f = _bs((tb, nsh_up), lambda i, j, k: (i, j))
    wg_up = gathered("w_up", h2)
    f_act, r_act = _mm("mlp_up", "nn", (t // tb, N_DEV, 1),
                       (h2, _bs((tb, d), lambda i, j, k: (i, 0))),
                       (wg_up, _bs((None, d, nsh_up), lambda i, j, k: (j, 0, 0))),
                       [(_sds((t, d_ff), BF16), tile_f), (_sds((t, d_ff), BF16), tile_f)],
                       epilogue=up_ep)
    bk_d, bn_o = 2048, 1024
    tile_o = _bs((tb, bn_o), lambda i, j, k: (i, j))
    wg_down = gathered("w_down", f_act).reshape(d_ff, d)
    (x3,) = _mm("mlp_down", "nn", (t // tb, d // bn_o, d_ff // bk_d),
                (f_act, _bs((tb, bk_d), lambda i, j, k: (i, k))),
                (wg_down, _bs((bk_d, bn_o), lambda i, j, k: (k, j))),
                [(_sds((t, d), F32), tile_o)],
                extras=[(x2, tile_o)], epilogue=lambda acc, r: (acc + r,))
    dx3, dx3_b, g_final, err2 = _final_loss(x3, target, norm_final_g.reshape(1, d))
    loss = lax.psum(0.5 * jnp.sum(err2) / d, MESH_AXES)

    sends = {}

    def send_grad(n, g, land_shape=None):
        sends[n] = _send_start("grad_start_" + n, g, lax.empty(land_shape or g.shape, BF16), True)
        return [sends[n][4]]

    bn_a = 1024
    tile_a = _bs((tb, bn_a), lambda i, j, k: (i, j))
    (da,) = _mm("mlp_down_dx", "nt", (t // tb, d_ff // bn_a, 1),
                (dx3_b, _bs((tb, d), lambda i, j, k: (i, 0))),
                (wg_down, _bs((bn_a, d), lambda i, j, k: (j, 0))),
                [(_sds((t, d_ff), BF16), tile_a)],
                extras=[(r_act, tile_a)], epilogue=lambda acc, r: (acc * (2.0 * r.astype(F32)),))
    sq = 1024
    (gw_down,) = _mm("mlp_down_dw", "tn", (d_ff // sq, 1, t // tb),
                     (f_act, _bs((tb, sq), lambda i, j, k: (k, i))),
                     (dx3_b, _bs((tb, d), lambda i, j, k: (k, 0))),
                     [(_sds((d_ff, d), BF16), _bs((sq, d), lambda i, j, k: (i, 0)))])
    sent = send_grad("w_down", gw_down.reshape(N_DEV, -1, d))
    (dh2,) = _mm("mlp_up_dx", "nt", (t // tb, 1, N_DEV),
                 (da, _bs((tb, nsh_up), lambda i, j, k: (i, k))),
                 (wg_up, _bs((None, d, nsh_up), lambda i, j, k: (k, 0, 0))),
                 [(_sds((t, d), F32), _bs((tb, d), lambda i, j, k: (i, 0)))], deps=sent)

    def norm_bwd_side(dh, xv, dres, g):
        dx, dg = _rms_bwd_math(dh, xv, g)
        dx = dx + dres
        return dx, dx, dg

    up_dw_grid = (1, N_DEV, t // tb)
    side_rows = (t // math.prod(up_dw_grid), d)
    gw_up, dx2, dx2_b, g_norm_mlp = _mm(
        "mlp_up_dw", "tn", up_dw_grid,
        (h2, _bs((tb, d), lambda i, j, k: (k, 0))),
        (da, _bs((tb, nsh_up), lambda i, j, k: (k, j))),
        [(_sds((N_DEV, d, nsh_up), BF16), _bs((None, d, nsh_up), lambda i, j, k: (j, 0, 0)))],
        side=(norm_bwd_side, [(dh2, side_rows), (x2, side_rows), (dx3, side_rows), (norm_mlp_g, (1, d))],
              [(_sds((t, d), F32), side_rows), (_sds((t, d), BF16), side_rows)], [(_sds((1, d), F32), (1, d))]))
    sent = send_grad("w_up", gw_up)

    tk = min(2048, t)
    (gw_out,) = _mm("out_proj_dw", "tn", (d // sq, d // sq, t // tk),
                    (mixed, _bs((tk, sq), lambda i, j, k: (k, i))),
                    (dx2_b, _bs((tk, sq), lambda i, j, k: (k, j))),
                    [(_sds((d, d), BF16), _bs((sq, sq), lambda i, j, k: (i, j)))], deps=sent)
    sent = send_grad("w_out", gw_out.reshape(N_DEV, -1, d))
    half2 = _bs((tb2, SSM_WIDTH), lambda i, j, k: (i, 0))
    vech = _bs((1, SSM_WIDTH), lambda i, j, k: (0, 0))
    half_sum = (_sds((1, SSM_WIDTH), F32), vech)

    def out_dx_ep(acc, ya, yb, ga, gb, yp, gt):
        dya, dga = _rms_bwd_math(acc[:, :SSM_WIDTH], ya, ga)
        dyb, dgb = _rms_bwd_math(acc[:, SSM_WIDTH:], yb, gb)
        dpre = dya * _gelu(yp) * gt * (1.0 - gt)
        return dya, dyb, dpre, dga, dgb, jnp.sum(dpre, axis=0, keepdims=True)

    dy_ssm, dy_sgu, dpre_b, g_onorm_ssm, g_onorm_sgu, g_glu_b = _mm(
        "out_proj_dx", "nt", (t // tb2, 1, 1),
        (dx2_b, row2), (wg_out, _bs((d, d), lambda i, j, k: (0, 0))),
        [(_sds((t, SSM_WIDTH), F32), half2), (_sds((t, SSM_WIDTH), F32), half2),
         (_sds((t, SSM_WIDTH), BF16), half2)],
        extras=[(y_ssm, half2), (y_sgu, half2), (out_norm_ssm_g, vech), (out_norm_sgu_g, vech),
                (y_pre, half2), (gate, half2)],
        epilogue=out_dx_ep, sums=[half_sum, half_sum, half_sum], deps=sent, acc_shape=(tb2, d),
        ep_rows=EP_ROWS)

    (gw_glu,) = _mm("glu_dw", "tn", (1, 1, t // tb),
                    (yg_b, _bs((tb, SSM_WIDTH), lambda i, j, k: (k, 0))),
                    (dpre_b, _bs((tb, SSM_WIDTH), lambda i, j, k: (k, 0))),
                    [(_sds((SSM_WIDTH, SSM_WIDTH), BF16), _bs((SSM_WIDTH, SSM_WIDTH), lambda i, j, k: (0, 0)))])
    sent = send_grad("ssm_glu_w", gw_glu.reshape(N_DEV, -1, SSM_WIDTH))
    (dy_pre,) = _mm("glu_dx", "nt", (t // tb, 2, 1),
                    (dpre_b, _bs((tb, SSM_WIDTH), lambda i, j, k: (i, 0))),
                    (wg_glu, _bs((hw, SSM_WIDTH), lambda i, j, k: (j, 0))),
                    [(_sds((t, SSM_WIDTH), F32), tile_g)],
                    extras=[(dy_ssm, tile_g), (gate, tile_g), (y_pre, tile_g)],
                    epilogue=lambda acc, dy, gt, yp: ((dy * gt + acc) * _gelu_grad(yp),), deps=sent)
    du_b, dbre, dbim, dcre, dcimn, q_re, q_im, dd = _ssm_bwd(dy_pre, z, p_re, p_im, dvec, *tabs)
    dare, daim, dldt, dbxr, dbxi = _ssm_prep_bwd(are, aim, ldt, bxr, bxi, dbre, dbim, q_re, q_im)

    dz_b, g_ln_g, g_ln_b, g_sgu_w, g_sgu_bx = _sgu_bwd(dy_sgu, du_b, z, sgu_ln_g, sgu_ln_b, sgu_w[0], sgu_bexp)
    local_small = {
        "ssm_a_re": dare, "ssm_a_im": daim,
        "ssm_b_re": _from_block_b(dbxr), "ssm_b_im": _from_block_b(dbxi),
        "ssm_c_re": _from_block_c(dcre), "ssm_c_im": -_from_block_c(dcimn),
        "ssm_d": dd, "ssm_log_dt": dldt.reshape(-1, SSM_STATE).sum(axis=-1),
        "ssm_glu_b": g_glu_b, "sgu_ln_g": g_ln_g, "sgu_ln_b": g_ln_b, "sgu_w": g_sgu_w,
        "sgu_b": g_sgu_bx[:, :, 0], "out_norm_ssm_g": g_onorm_ssm, "out_norm_sgu_g": g_onorm_sgu,
        "norm_mlp_g": g_norm_mlp, "norm_final_g": g_final,
    }
    small_early = [n for n in small if n in local_small]
    small_late = [n for n in small if n not in local_small]
    packed = _pack([local_small[n] for n in small_early])
    small_send = _send_start("small_start", packed, lax.empty((N_DEV,) + packed.shape, F32), False)

    def landed_parts(n, after):
        sent_blocks, landed = _send_wait("grad_wait_" + n, sends[n], after, True)
        if sent_blocks.ndim == landed.ndim:
            own = lax.dynamic_index_in_dim(sent_blocks, me, 0, keepdims=False)
        else:
            own = lax.dynamic_slice_in_dim(sent_blocks, me * landed.shape[2], landed.shape[2], axis=1)
        return _own_block(landed, own, me)

    in_dw_grid = (d // sq, n_in // bn_i, t // tb)
    riders = ["w_down", "w_up"]
    side_ins, side_outs = [], []
    for n in riders:
        rows, cols = given[n].shape[1:]
        blk = (rows // math.prod(in_dw_grid), cols)
        side_ins += [(landed_parts(n, dz_b), (N_DEV,) + blk), (given[n][0], blk),
                     (given["m_" + n][0], blk), (given["v_" + n][0], blk)]
        side_outs += [(_sds((rows, cols), F32), blk)] * 4

    def adam_side(*tiles):
        return sum((_adam_math(*tiles[4 * r:4 * r + 4]) for r in range(len(riders))), ())

    gw_in, *rider_res = _mm("in_proj_dw", "tn", in_dw_grid,
                            (h1, _bs((tb, sq), lambda i, j, k: (k, i))),
                            (dz_b, _bs((tb, bn_i), lambda i, j, k: (k, j))),
                            [(_sds((d, n_in), BF16), _bs((sq, bn_i), lambda i, j, k: (i, j)))],
                            deps=[small_send[4]], side=(adam_side, side_ins, side_outs, []))
    sent = send_grad("w_in", gw_in, (N_DEV, d, nsh_in))

    def in_dx_ep(acc, xv, dres, g):
        dx, dg = _rms_bwd_math(acc, xv, g)
        return dx + dres, dg

    grad_x, g_norm_mix = _mm("in_proj_dx", "nt", (t // tb2, 1, n_in // bn_i),
                             (dz_b, _bs((tb2, bn_i), lambda i, j, k: (i, k))),
                             (wg_in, _bs((d, bn_i), lambda i, j, k: (0, k))),
                             [(_sds((t, d), F32), row2)],
                             extras=[(xs, row2), (dx2, row2), (norm_mix_g, vec2)],
                             epilogue=in_dx_ep, sums=[vec_sum], deps=sent, ep_rows=EP_ROWS)

    (late_parts,) = _all_gather("gather_late_grads", [_pack([g_norm_mix])])
    packed, early_parts = _send_wait("small_wait", small_send, grad_x, False)
    early_parts = _own_block(early_parts, packed, me)

    grads, deltas, new_m, new_v = {}, {}, {}, {}
    for r, n in enumerate(riders):
        grads[n], deltas[n], new_m[n], new_v[n] = [a.reshape(given[n].shape) for a in rider_res[4 * r:4 * r + 4]]
    for n in big:
        if n in riders:
            continue
        res = _adamw("adamw_" + n, landed_parts(n, grad_x), given[n][0], given["m_" + n][0], given["v_" + n][0])
        grads[n], deltas[n], new_m[n], new_v[n] = [r.reshape(given[n].shape) for r in res]
    for tag, group, parts in (("early", small_early, early_parts), ("late", small_late, late_parts)):
        like = [given[n] for n in group]
        res = _adamw("adamw_small_" + tag, parts, _pack(like), _pack([given["m_" + n] for n in group]),
                     _pack([given["v_" + n] for n in group]))
        for store, buf in zip((grads, deltas, new_m, new_v), res):
            for n, a in zip(group, _unpack(buf, like)):
                store[n] = a

    return (loss, grad_x.reshape(x.shape), *[grads[n] for n in names], *[deltas[n] for n in names],
            *[new_m[n] for n in names], *[new_v[n] for n in names])
```

```python
import functools
import math

import jax
import jax.numpy as jnp
from jax import lax
from jax.experimental import pallas as pl
from jax.experimental.pallas import tpu as pltpu

F32, BF16 = jnp.float32, jnp.bfloat16
EPS = 1e-6
N_DEV = 8
D_MODEL = 2048
SSM_WIDTH = 1024
SSM_GROUP = 16
SSM_STATE = 64
SGU_HEADS = 8
SGU_CHUNK = 128
SGU_FWD_CHUNKS = 4
LANES = 128
SUBLANES = 8
N_SLAB = SSM_WIDTH // LANES
SLAB_STATE = (LANES // SSM_GROUP) * SSM_STATE
SCAN_BLOCK = 256
SLABS_PER_STEP = 8
VMEM_LIMIT = 56 * 1024 * 1024
ROW_BLOCK = 512
RING_SLOTS = 3
EP_ROWS = 128
ADAM_ROWS = 128
MESH_AXES = ("x", "y", "c")

ADAM_LR, ADAM_B1, ADAM_B2, ADAM_EPS, ADAM_WD, ADAM_STEP = 0.001, 0.9, 0.999, 1e-08, 0.01, 10

_GELU_C0 = math.sqrt(2.0 / math.pi)
_GELU_C1 = 0.044715


def _gelu(v):
    return 0.5 * v * (1.0 + jnp.tanh(_GELU_C0 * (v + _GELU_C1 * v * v * v)))


def _gelu_grad(v):
    th = jnp.tanh(_GELU_C0 * (v + _GELU_C1 * v * v * v))
    return 0.5 * (1.0 + th) + 0.5 * v * (1.0 - th * th) * _GELU_C0 * (1.0 + 3.0 * _GELU_C1 * v * v)


def _sigmoid(v):
    return 1.0 / (1.0 + jnp.exp(-v))


def _params(sem=None):
    return pltpu.CompilerParams(dimension_semantics=sem, vmem_limit_bytes=VMEM_LIMIT)


def _dot(a, b, mode="nn"):
    dims = {"nn": ((1,), (0,)), "nt": ((1,), (1,)), "tn": ((0,), (0,))}[mode]
    return lax.dot_general(a, b, (dims, ((), ())), preferred_element_type=F32)


def _mm(name, mode, grid, a, b, outs, extras=(), epilogue=None, deps=(), sums=(), acc_shape=None,
        ep_rows=None, side=None):
    nk = grid[2]
    n_ex, n_out, n_dep, n_sum = len(extras), len(outs), len(deps), len(sums)
    assert not sums or grid[1] == 1
    if acc_shape is None:
        acc_shape = tuple(d for d in outs[0][1].block_shape if d is not None)
    side_fn, side_ins, side_outs, side_sums = side or (None, (), (), ())
    s_in, s_out, s_sum = len(side_ins), len(side_outs), len(side_sums)

    def body(*refs):
        a_ref, b_ref = refs[0], refs[1]
        ex = refs[2:2 + n_ex]
        pos = 2 + n_ex
        side_in = refs[pos:pos + s_in]
        pos += s_in + n_dep
        out_refs = refs[pos:pos + n_out]
        sum_refs = refs[pos + n_out:pos + n_out + n_sum]
        pos += n_out + n_sum
        side_out = refs[pos:pos + s_out]
        side_sum = refs[pos + s_out:pos + s_out + s_sum]
        acc = refs[-1]
        k = pl.program_id(2)

        @pl.when(k == 0)
        def _():
            acc[...] = jnp.zeros_like(acc)

        if side_sum:
            @pl.when((pl.program_id(0) == 0) & (pl.program_id(1) == 0) & (k == 0))
            def _():
                for o in side_sum:
                    o[...] = jnp.zeros_like(o)

        acc[...] += _dot(a_ref[...], b_ref[...], mode)
        if side_fn is not None:
            res = side_fn(*[r[...] for r in side_in])
            for o, r in zip(side_out, res[:s_out]):
                o[...] = r.astype(o.dtype)
            for o, r in zip(side_sum, res[s_out:]):
                o[...] += r

        def finish(rows):
            args = [e[rows, :] if e.shape[0] == acc_shape[0] else e[...] for e in ex]
            res = acc[rows, :]
            res = (res,) if epilogue is None else epilogue(res, *args)
            for o, r in zip(out_refs, res[:n_out]):
                o[rows, :] = r.astype(o.dtype)
            return tuple(res[n_out:])

        @pl.when(k == nk - 1)
        def _():
            if ep_rows is None:
                terms = finish(slice(None))
            else:
                def chunk(c, tot):
                    rows = pl.ds(pl.multiple_of(c * ep_rows, ep_rows), ep_rows)
                    return tuple(s + r for s, r in zip(tot, finish(rows)))

                zero = tuple(jnp.zeros(o.shape, F32) for o in sum_refs)
                terms = lax.fori_loop(0, acc_shape[0] // ep_rows, chunk, zero)
            for o, r in zip(sum_refs, terms):
                _add_up(o, r, pl.program_id(0) == 0)

    def side_spec(block):
        nd = len(block)
        if nd == 2 and block[0] == 1:
            return _bs(block, lambda i, j, k: (0, 0))
        return _bs(block, lambda i, j, k: (0,) * (nd - 2) + ((i * grid[1] + j) * grid[2] + k, 0))

    sem = ("arbitrary",) * 3 if sums or side else ("parallel", "parallel", "arbitrary")
    res = pl.pallas_call(
        body, name=name, grid=grid,
        in_specs=[a[1], b[1]] + [e[1] for e in extras] + [side_spec(blk) for _, blk in side_ins]
        + [_any_spec()] * n_dep,
        out_specs=[o[1] for o in outs] + [o[1] for o in sums]
        + [side_spec(blk) for _, blk in side_outs] + [side_spec(blk) for _, blk in side_sums],
        out_shape=[o[0] for o in outs] + [o[0] for o in sums]
        + [o[0] for o in side_outs] + [o[0] for o in side_sums],
        scratch_shapes=[pltpu.VMEM(acc_shape, F32)],
        compiler_params=_params(sem),
    )(a[0], b[0], *[e[0] for e in extras], *[x for x, _ in side_ins], *deps)
    return res


def _add_up(ref, term, first):
    @pl.when(first)
    def _():
        ref[...] = term

    @pl.when(jnp.logical_not(first))
    def _():
        ref[...] += term


def _rms_math(xv, g):
    return xv * lax.rsqrt(jnp.mean(xv * xv, axis=-1, keepdims=True) + EPS) * g


def _rms_bwd_math(dy, xv, g):
    r = lax.rsqrt(jnp.mean(xv * xv, axis=-1, keepdims=True) + EPS)
    xhat = xv * r
    dxhat = dy * g
    dx = r * (dxhat - xhat * jnp.mean(dxhat * xhat, axis=-1, keepdims=True))
    return dx, jnp.sum(dy * xhat, axis=0, keepdims=True)


def _sds(shape, dtype):
    return jax.ShapeDtypeStruct(shape, dtype)


def _bs(shape, fn):
    return pl.BlockSpec(shape, fn)


def _rms_fwd(name, x, g, deps=()):
    t, w = x.shape
    br = min(ROW_BLOCK, t)

    def body(*refs):
        x_ref, g_ref, o_ref = refs[0], refs[1], refs[-1]
        xv = x_ref[...]
        r = lax.rsqrt(jnp.mean(xv * xv, axis=-1, keepdims=True) + EPS)
        o_ref[...] = (xv * r * g_ref[...]).astype(BF16)

    return pl.pallas_call(
        body, name=name, grid=(t // br,),
        in_specs=[_bs((br, w), lambda i: (i, 0)), _bs((1, w), lambda i: (0, 0))] + [_any_spec()] * len(deps),
        out_specs=_bs((br, w), lambda i: (i, 0)),
        out_shape=_sds((t, w), BF16),
        compiler_params=_params(("parallel",)),
    )(x, g, *deps)


def _mix_norm(ya, yb, ga, gb):
    t, w = ya.shape
    br = min(ROW_BLOCK, t)

    def body(a_ref, b_ref, ga_ref, gb_ref, o_ref):
        for src, g_ref, col in ((a_ref, ga_ref, 0), (b_ref, gb_ref, w)):
            v = src[...]
            r = lax.rsqrt(jnp.mean(v * v, axis=-1, keepdims=True) + EPS)
            o_ref[:, col:col + w] = (v * r * g_ref[...]).astype(BF16)

    row = _bs((br, w), lambda i: (i, 0))
    vec = _bs((1, w), lambda i: (0, 0))
    return pl.pallas_call(
        body, name="mix_norm", grid=(t // br,),
        in_specs=[row, row, vec, vec],
        out_specs=_bs((br, 2 * w), lambda i: (i, 0)),
        out_shape=_sds((t, 2 * w), BF16),
        compiler_params=_params(("parallel",)),
    )(ya, yb, ga, gb)


def _loss_math(xv, target, g):
    r = lax.rsqrt(jnp.mean(xv * xv, axis=-1, keepdims=True) + EPS)
    xhat = xv * r
    err = xhat * g - target
    dy = err * (1.0 / xv.shape[-1])
    dxhat = dy * g
    dx = r * (dxhat - xhat * jnp.mean(dxhat * xhat, axis=-1, keepdims=True))
    return dx, jnp.sum(dy * xhat, axis=0, keepdims=True), jnp.sum(err * err, axis=0, keepdims=True)


def _final_loss(x3, target, g):
    t, w = x3.shape
    br = min(ROW_BLOCK, t)
    n = t // br

    def body(x_hbm, tg_hbm, g_ref, dx_ref, dxb_ref, dg_ref, l_ref, xbuf, tbuf, sems):
        s = pl.program_id(0)

        def fetch(step):
            slot = step % RING_SLOTS
            start = step * br if isinstance(step, int) else pl.multiple_of(step * br, br)
            rows = pl.ds(start, br)
            return (pltpu.make_async_copy(x_hbm.at[rows, :], xbuf.at[slot], sems.at[0, slot]),
                    pltpu.make_async_copy(tg_hbm.at[rows, :], tbuf.at[slot], sems.at[1, slot]))

        @pl.when(s == 0)
        def _():
            for step in range(min(RING_SLOTS - 1, n)):
                for cp in fetch(step):
                    cp.start()

        @pl.when(s + RING_SLOTS - 1 < n)
        def _():
            for cp in fetch(s + RING_SLOTS - 1):
                cp.start()

        for cp in fetch(s):
            cp.wait()
        slot = s % RING_SLOTS
        dx, dg, e2 = _loss_math(xbuf[slot], tbuf[slot], g_ref[...])
        dx_ref[...] = dx
        dxb_ref[...] = dx.astype(BF16)
        _add_up(dg_ref, dg, s == 0)
        _add_up(l_ref, e2, s == 0)

    row = _bs((br, w), lambda i: (i, 0))
    vec = _bs((1, w), lambda i: (0, 0))
    return pl.pallas_call(
        body, name="final_loss", grid=(n,),
        in_specs=[_any_spec(), _any_spec(), vec], out_specs=[row, row, vec, vec],
        out_shape=[_sds((t, w), F32), _sds((t, w), BF16), _sds((1, w), F32), _sds((1, w), F32)],
        scratch_shapes=[pltpu.VMEM((RING_SLOTS, br, w), F32), pltpu.VMEM((RING_SLOTS, br, w), F32),
                        pltpu.SemaphoreType.DMA((2, RING_SLOTS))],
        compiler_params=_params(("arbitrary",)),
    )(x3, target, g)


def _prep_math(are, aim, ldt, bxr, bxi):
    dt = jnp.exp(ldt)
    er = jnp.exp(are * dt)
    th = aim * dt
    abr, abi = er * jnp.cos(th), er * jnp.sin(th)
    nr, ni = abr - 1.0, abi
    den = are * are + aim * aim
    cr = (nr * are + ni * aim) / den
    ci = (ni * are - nr * aim) / den
    bbr, bbi = [], []
    for j in range(N_SLAB):
        sl = slice(j * SLAB_STATE, (j + 1) * SLAB_STATE)
        bbr.append(cr[:, sl] * bxr[j] - ci[:, sl] * bxi[j])
        bbi.append(cr[:, sl] * bxi[j] + ci[:, sl] * bxr[j])
    return abr, abi, bbr, bbi


def _ssm_prep(are, aim, ldt, bxr, bxi, cxr, cxi):
    nst = N_SLAB * SLAB_STATE

    def body(are_r, aim_r, ldt_r, bxr_r, bxi_r, cxr_r, cxi_r,
             bre_o, bim_o, cre_o, cimn_o, apr_o, api_o, air_o, aii_o):
        abr, abi, bbr, bbi = _prep_math(are_r[...], aim_r[...], ldt_r[...],
                                        [bxr_r[j] for j in range(N_SLAB)], [bxi_r[j] for j in range(N_SLAB)])
        for j in range(N_SLAB):
            bre_o[j] = bbr[j].astype(BF16)
            bim_o[j] = bbi[j].astype(BF16)
        cre_o[...] = cxr_r[...].astype(BF16)
        cimn_o[...] = (-cxi_r[...]).astype(BF16)

        def step(k, cur):
            cr, ci = cur
            den = cr * cr + ci * ci
            apr_o[pl.ds(k, 1), :] = cr
            api_o[pl.ds(k, 1), :] = ci
            air_o[pl.ds(k, 1), :] = cr / den
            aii_o[pl.ds(k, 1), :] = -ci / den
            return cr * abr - ci * abi, cr * abi + ci * abr

        lax.fori_loop(0, SCAN_BLOCK, step, (jnp.ones((1, nst), F32), jnp.zeros((1, nst), F32)))

    tab = _sds((SCAN_BLOCK, nst), F32)
    return pl.pallas_call(
        body, name="ssm_prep",
        out_shape=[_sds(bxr.shape, BF16), _sds(bxr.shape, BF16), _sds(cxr.shape, BF16), _sds(cxr.shape, BF16),
                   tab, tab, tab, tab],
        compiler_params=_params(),
    )(are, aim, ldt, bxr, bxi, cxr, cxi)


def _ssm_prep_bwd(are, aim, ldt, bxr, bxi, dbre, dbim, qr, qi):
    def body(are_r, aim_r, ldt_r, bxr_r, bxi_r, dbre_r, dbim_r, qr_r, qi_r,
             dare_o, daim_o, dldt_o, dbxr_o, dbxi_o):
        prim = (are_r[...], aim_r[...], ldt_r[...],
                [bxr_r[j] for j in range(N_SLAB)], [bxi_r[j] for j in range(N_SLAB)])
        (abr, abi, _, _), vjp = jax.vjp(_prep_math, *prim)
        den = abr * abr + abi * abi
        q_r, q_i = qr_r[...], qi_r[...]
        gar = (q_r * abr - q_i * abi) / den
        gai = (q_r * abi + q_i * abr) / den
        ct = (gar, gai, [dbre_r[j] for j in range(N_SLAB)], [dbim_r[j] for j in range(N_SLAB)])
        dare, daim, dldt, dbxr, dbxi = vjp(ct)
        dare_o[...] = dare
        daim_o[...] = daim
        dldt_o[...] = dldt
        for j in range(N_SLAB):
            dbxr_o[j] = dbxr[j]
            dbxi_o[j] = dbxi[j]

    row = _sds(are.shape, F32)
    return pl.pallas_call(
        body, name="ssm_prep_bwd",
        out_shape=[row, row, row, _sds(bxr.shape, F32), _sds(bxr.shape, F32)],
        compiler_params=_params(),
    )(are, aim, ldt, bxr, bxi, dbre, dbim, qr, qi)


def _tri(lower):
    r = lax.broadcasted_iota(jnp.int32, (SCAN_BLOCK, SCAN_BLOCK), 0)
    c = lax.broadcasted_iota(jnp.int32, (SCAN_BLOCK, SCAN_BLOCK), 1)
    return jnp.where((r >= c) if lower else (r <= c), 1.0, 0.0).astype(BF16)


def _cumsum_mxu(tri, v):
    return _dot(tri, v.astype(BF16))


def _ssm_specs(t, sps):
    nt = t // SCAN_BLOCK
    tab = _bs((SCAN_BLOCK, sps * SLAB_STATE), lambda j, i: (0, j))
    bmat = _bs((sps, LANES, SLAB_STATE), lambda j, i: (j, 0, 0))
    cmat = _bs((sps, SLAB_STATE, LANES), lambda j, i: (j, 0, 0))
    return nt, tab, bmat, cmat


def _slab_slices(s):
    return slice(s * LANES, (s + 1) * LANES), slice(s * SLAB_STATE, (s + 1) * SLAB_STATE)


def _ssm_fwd(z, dvec, bre, bim, cre, cimn, apr, api, air, aii, deps=()):
    t = z.shape[0]
    sps = SLABS_PER_STEP
    nt, tab, bmat, cmat = _ssm_specs(t, sps)
    nst = N_SLAB * SLAB_STATE
    last = SCAN_BLOCK - 1

    def body(*refs):
        u_ref, d_ref, bre_r, bim_r, cre_r, cimn_r, apr_r, api_r, air_r, aii_r = refs[:10]
        y_ref, yg_ref, pr_ref, pi_ref, car_r, car_i = refs[10 + len(deps):]

        @pl.when(pl.program_id(1) == 0)
        def _():
            car_r[...] = jnp.zeros_like(car_r)
            car_i[...] = jnp.zeros_like(car_i)

        tri = _tri(True)
        for s in range(sps):
            ul, sl = _slab_slices(s)
            u = u_ref[:, ul]
            ub = u.astype(BF16)
            bur, bui = _dot(ub, bre_r[s]), _dot(ub, bim_r[s])
            ir, ii = air_r[:, sl], aii_r[:, sl]
            csr = _cumsum_mxu(tri, ir * bur - ii * bui)
            csi = _cumsum_mxu(tri, ir * bui + ii * bur)
            pr, pi = apr_r[:, sl], api_r[:, sl]
            a_r, a_i = apr_r[1:2, sl], api_r[1:2, sl]
            c_r, c_i = car_r[:, sl], car_i[:, sl]
            wr = csr + (a_r * c_r - a_i * c_i)
            wi = csi + (a_r * c_i + a_i * c_r)
            sr = pr * wr - pi * wi
            si = pr * wi + pi * wr
            car_r[:, sl] = sr[last:last + 1, :]
            car_i[:, sl] = si[last:last + 1, :]
            pr_ref[:, sl] = (sr - bur).astype(BF16)
            pi_ref[:, sl] = (si - bui).astype(BF16)
            y = _dot(sr.astype(BF16), cre_r[s]) + _dot(si.astype(BF16), cimn_r[s]) + d_ref[:, ul] * u
            y_ref[:, ul] = y
            yg_ref[:, ul] = _gelu(y).astype(BF16)

    ublk = _bs((SCAN_BLOCK, sps * LANES), lambda j, i: (i, j))
    sblk = _bs((SCAN_BLOCK, sps * SLAB_STATE), lambda j, i: (i, j))
    return pl.pallas_call(
        body, name="ssm_fwd", grid=(N_SLAB // sps, nt),
        in_specs=[ublk, _bs((1, sps * LANES), lambda j, i: (0, j)), bmat, bmat, cmat, cmat, tab, tab, tab, tab]
        + [_any_spec()] * len(deps),
        out_specs=[ublk, ublk, sblk, sblk],
        out_shape=[_sds((t, SSM_WIDTH), F32), _sds((t, SSM_WIDTH), BF16),
                   _sds((t, nst), BF16), _sds((t, nst), BF16)],
        scratch_shapes=[pltpu.VMEM((1, sps * SLAB_STATE), F32), pltpu.VMEM((1, sps * SLAB_STATE), F32)],
        compiler_params=_params(("parallel", "arbitrary")),
    )(z, dvec, bre, bim, cre, cimn, apr, api, air, aii, *deps)


def _ssm_bwd(gy, z, p_re, p_im, dvec, bre, bim, cre, cimn, apr, api, air, aii):
    t = z.shape[0]
    sps = SLABS_PER_STEP
    nt, tab, bmat, cmat = _ssm_specs(t, sps)
    last = SCAN_BLOCK - 1

    def fold(v):
        return v.reshape(SCAN_BLOCK // SUBLANES, SUBLANES, v.shape[-1]).sum(axis=0)

    def body(g_ref, u_ref, pr_ref, pi_ref, d_ref, bre_r, bim_r, cre_r, cimn_r, apr_r, api_r, air_r, aii_r,
             du_ref, dbre_o, dbim_o, dcre_o, dcimn_o, qr_o, qi_o, dd_o, car_r, car_i, qacc_r, qacc_i, dacc):
        i = pl.program_id(1)

        @pl.when(i == 0)
        def _():
            for ref in (car_r, car_i, qacc_r, qacc_i, dacc, dbre_o, dbim_o, dcre_o, dcimn_o):
                ref[...] = jnp.zeros_like(ref)

        tri = _tri(False)
        for s in range(sps):
            ul, sl = _slab_slices(s)
            g = g_ref[:, ul]
            gb = g.astype(BF16)
            u = u_ref[:, ul]
            ub = u.astype(BF16)
            bur, bui = _dot(ub, bre_r[s]), _dot(ub, bim_r[s])
            p_r, p_i = pr_ref[:, sl].astype(F32), pi_ref[:, sl].astype(F32)
            srb, sib = (p_r + bur).astype(BF16), (p_i + bui).astype(BF16)
            dcre_o[s] += _dot(srb, gb, "tn")
            dcimn_o[s] += _dot(sib, gb, "tn")
            dsr, dsi = _dot(gb, cre_r[s], "nt"), _dot(gb, cimn_r[s], "nt")
            pr, pi = apr_r[:, sl], api_r[:, sl]
            csr = _cumsum_mxu(tri, pr * dsr + pi * dsi)
            csi = _cumsum_mxu(tri, pr * dsi - pi * dsr)
            al_r, al_i = apr_r[last:last + 1, sl], api_r[last:last + 1, sl]
            c_r, c_i = car_r[:, sl], car_i[:, sl]
            wr = csr + (al_r * c_r + al_i * c_i)
            wi = csi + (al_r * c_i - al_i * c_r)
            ir, ii = air_r[:, sl], aii_r[:, sl]
            lr = ir * wr + ii * wi
            li = ir * wi - ii * wr
            a_r, a_i = apr_r[1:2, sl], api_r[1:2, sl]
            car_r[:, sl] = a_r * lr[0:1, :] + a_i * li[0:1, :]
            car_i[:, sl] = a_r * li[0:1, :] - a_i * lr[0:1, :]
            lrb, lib = lr.astype(BF16), li.astype(BF16)
            dbre_o[s] += _dot(ub, lrb, "tn")
            dbim_o[s] += _dot(ub, lib, "tn")
            du = d_ref[:, ul] * g + _dot(lrb, bre_r[s], "nt") + _dot(lib, bim_r[s], "nt")
            du_ref[:, ul] = du.astype(BF16)
            qacc_r[:, sl] += fold(lr * p_r + li * p_i)
            qacc_i[:, sl] += fold(li * p_r - lr * p_i)
            dacc[:, ul] += fold(g * u)

        @pl.when(i == nt - 1)
        def _():
            qr_o[...] = jnp.sum(qacc_r[...], axis=0, keepdims=True)
            qi_o[...] = jnp.sum(qacc_i[...], axis=0, keepdims=True)
            dd_o[...] = jnp.sum(dacc[...], axis=0, keepdims=True)

    rev = lambda j, i: (nt - 1 - i, j)
    ublk = _bs((SCAN_BLOCK, sps * LANES), rev)
    sblk = _bs((SCAN_BLOCK, sps * SLAB_STATE), rev)
    qrow = _bs((1, sps * SLAB_STATE), lambda j, i: (0, j))
    urow = _bs((1, sps * LANES), lambda j, i: (0, j))
    nst = N_SLAB * SLAB_STATE
    return pl.pallas_call(
        body, name="ssm_bwd", grid=(N_SLAB // sps, nt),
        in_specs=[ublk, ublk, sblk, sblk, urow, bmat, bmat, cmat, cmat, tab, tab, tab, tab],
        out_specs=[ublk, bmat, bmat, cmat, cmat, qrow, qrow, urow],
        out_shape=[_sds((t, SSM_WIDTH), BF16),
                   _sds((N_SLAB, LANES, SLAB_STATE), F32), _sds((N_SLAB, LANES, SLAB_STATE), F32),
                   _sds((N_SLAB, SLAB_STATE, LANES), F32), _sds((N_SLAB, SLAB_STATE, LANES), F32),
                   _sds((1, nst), F32), _sds((1, nst), F32), _sds((1, SSM_WIDTH), F32)],
        scratch_shapes=[pltpu.VMEM((1, sps * SLAB_STATE), F32), pltpu.VMEM((1, sps * SLAB_STATE), F32),
                        pltpu.VMEM((SUBLANES, sps * SLAB_STATE), F32), pltpu.VMEM((SUBLANES, sps * SLAB_STATE), F32),
                        pltpu.VMEM((SUBLANES, sps * LANES), F32)],
        compiler_params=_params(("parallel", "arbitrary")),
    )(gy, z, p_re, p_im, dvec, bre, bim, cre, cimn, apr, api, air, aii)


def _sgu_mask():
    r = lax.broadcasted_iota(jnp.int32, (SGU_CHUNK, SGU_CHUNK), 0)
    c = lax.broadcasted_iota(jnp.int32, (SGU_CHUNK, SGU_CHUNK), 1)
    return r >= c


def _sgu_common(zu, zv, lng, lnb):
    us, v = _gelu(zu), _gelu(zv)
    mu = jnp.mean(v, axis=-1, keepdims=True)
    vc = v - mu
    rstd = lax.rsqrt(jnp.mean(vc * vc, axis=-1, keepdims=True) + EPS)
    xhat = vc * rstd
    return us, xhat, rstd, xhat * lng + lnb


def _sgu_fwd(z, lng, lnb, w, bexp):
    t = z.shape[0]
    hd = SGU_CHUNK
    rows = min(SGU_FWD_CHUNKS * SGU_CHUNK, t)

    def body(zu_ref, zv_ref, lng_ref, lnb_ref, w_ref, b_ref, y_ref):
        mask = _sgu_mask()
        wts = [jnp.where(mask, w_ref[h], 0.0).astype(BF16) for h in range(SGU_HEADS)]
        for c in range(rows // SGU_CHUNK):
            rs = slice(c * SGU_CHUNK, (c + 1) * SGU_CHUNK)
            us, _, _, vn = _sgu_common(zu_ref[rs, :], zv_ref[rs, :], lng_ref[...], lnb_ref[...])
            vnb = vn.astype(BF16)
            for h in range(SGU_HEADS):
                sl = slice(h * hd, (h + 1) * hd)
                y_ref[rs, sl] = us[:, sl] * (_dot(wts[h], vnb[:, sl]) + b_ref[h])

    row = lambda c: _bs((rows, SSM_WIDTH), lambda i: (i, c))
    vec = _bs((1, SSM_WIDTH), lambda i: (0, 0))
    hmat = _bs((SGU_HEADS, hd, hd), lambda i: (0, 0, 0))
    return pl.pallas_call(
        body, name="sgu_fwd", grid=(t // rows,),
        in_specs=[row(1), row(2), vec, vec, hmat, hmat],
        out_specs=row(0), out_shape=_sds((t, SSM_WIDTH), F32),
        compiler_params=_params(("parallel",)),
    )(z, z, lng, lnb, w, bexp)


def _sgu_bwd(dy, du_ssm, z, lng, lnb, w, bexp, deps=()):
    t = z.shape[0]
    hd = SGU_CHUNK
    nc = t // SGU_CHUNK

    def body(*refs):
        dy_ref, dus_ref, zu_ref, zv_ref, lng_ref, lnb_ref, w_ref, b_ref = refs[:8]
        dz_ref, dlng_o, dlnb_o, dw_o, db_o = refs[8 + len(deps):]
        i = pl.program_id(0)

        @pl.when(i == 0)
        def _():
            for ref in (dlng_o, dlnb_o, dw_o, db_o):
                ref[...] = jnp.zeros_like(ref)

        zu, zv = zu_ref[...], zv_ref[...]
        lng = lng_ref[...]
        us, xhat, rstd, vn = _sgu_common(zu, zv, lng, lnb_ref[...])
        vnb = vn.astype(BF16)
        dyv = dy_ref[...]
        mask = _sgu_mask()
        dus_parts, dvn_parts = [], []
        for h in range(SGU_HEADS):
            sl = slice(h * hd, (h + 1) * hd)
            wt = jnp.where(mask, w_ref[h], 0.0).astype(BF16)
            mixed = _dot(wt, vnb[:, sl]) + b_ref[h]
            dus_parts.append(dyv[:, sl] * mixed)
            dmix = dyv[:, sl] * us[:, sl]
            dmb = dmix.astype(BF16)
            db_o[h] += dmix
            dw_o[h] += _dot(dmb, vnb[:, sl], "nt")
            dvn_parts.append(_dot(wt, dmb, "tn"))
        dus = jnp.concatenate(dus_parts, axis=1)
        dvn = jnp.concatenate(dvn_parts, axis=1)
        dlng_o[...] += jnp.sum(dvn * xhat, axis=0, keepdims=True)
        dlnb_o[...] += jnp.sum(dvn, axis=0, keepdims=True)
        dxh = dvn * lng
        dv = rstd * (dxh - jnp.mean(dxh, axis=-1, keepdims=True)
                     - xhat * jnp.mean(dxh * xhat, axis=-1, keepdims=True))
        dz_ref[:, 0:SSM_WIDTH] = dus_ref[...]
        dz_ref[:, SSM_WIDTH:2 * SSM_WIDTH] = (dus * _gelu_grad(zu)).astype(BF16)
        dz_ref[:, 2 * SSM_WIDTH:] = (dv * _gelu_grad(zv)).astype(BF16)

        @pl.when(i == nc - 1)
        def _():
            for h in range(SGU_HEADS):
                dw_o[h] = jnp.where(mask, dw_o[h], 0.0)
                db_o[h] = jnp.broadcast_to(jnp.sum(db_o[h], axis=1, keepdims=True), (hd, hd))

    row = lambda c: _bs((SGU_CHUNK, SSM_WIDTH), lambda i: (i, c))
    vec = _bs((1, SSM_WIDTH), lambda i: (0, 0))
    hmat = _bs((SGU_HEADS, hd, hd), lambda i: (0, 0, 0))
    return pl.pallas_call(
        body, name="sgu_bwd", grid=(nc,),
        in_specs=[row(0), row(0), row(1), row(2), vec, vec, hmat, hmat] + [_any_spec()] * len(deps),
        out_specs=[_bs((SGU_CHUNK, 3 * SSM_WIDTH), lambda i: (i, 0)), vec, vec, hmat, hmat],
        out_shape=[_sds((t, 3 * SSM_WIDTH), BF16), _sds((1, SSM_WIDTH), F32), _sds((1, SSM_WIDTH), F32),
                   _sds((SGU_HEADS, hd, hd), F32), _sds((SGU_HEADS, hd, hd), F32)],
        compiler_params=_params(("arbitrary",)),
    )(dy, du_ssm, z, z, lng, lnb, w, bexp, *deps)


def _place():
    x, y, c = (lax.axis_index(a) for a in MESH_AXES)
    return x, y, c


def _index(p):
    return 4 * p[0] + 2 * p[1] + p[2]


def _any_spec():
    return pl.BlockSpec(memory_space=pl.ANY)


def _col_block(ref, k, width):
    return ref.at[:, pl.ds(pl.multiple_of(k * width, LANES), width)]


def _all_gather(name, shards, by_columns=False):
    n = len(shards)

    def body(*refs):
        ins, outs = refs[:n], refs[n:2 * n]
        send, recv, loc = refs[2 * n:]
        x, y, c = _place()
        me, sib = (x, y, c), (x, y, 1 - c)
        chips = [(1 - x, y), (x, 1 - y), (1 - x, 1 - y)]

        def blk(w, p):
            if by_columns:
                return _col_block(outs[w], _index(p), shards[w].shape[1])
            return outs[w].at[_index(p)]

        def cp(w, k, block, to, src=None):
            dst = blk(w, block)
            return pltpu.make_async_remote_copy(
                src_ref=dst if src is None else src, dst_ref=dst,
                send_sem=send.at[w * 7 + k], recv_sem=recv.at[w * 7 + k],
                device_id=to, device_id_type=pl.DeviceIdType.MESH)

        mines, sends = [], []
        for w in range(n):
            m = pltpu.make_async_copy(ins[w], blk(w, me), loc.at[w])
            m.start()
            mines.append(m)
            first = [cp(w, 0, me, sib, src=ins[w])]
            first += [cp(w, 1 + j, me, (*chip, c), src=ins[w]) for j, chip in enumerate(chips)]
            for q in first:
                q.start()
            sends += first
        for j, chip in enumerate(chips):
            for w in range(n):
                cp(w, 1 + j, (*chip, c), me).wait_recv()
                q = cp(w, 4 + j, (*chip, c), sib)
                q.start()
                sends.append(q)
        for w in range(n):
            cp(w, 0, sib, me).wait_recv()
            for j, chip in enumerate(chips):
                cp(w, 4 + j, (*chip, 1 - c), me).wait_recv()
        for q in sends:
            q.wait_send()
        for m in mines:
            m.wait()

    return pl.pallas_call(
        body, name=name,
        in_specs=[_any_spec()] * n, out_specs=[_any_spec()] * n,
        out_shape=[_sds((s.shape[0], N_DEV * s.shape[1]) if by_columns else (N_DEV,) + s.shape, s.dtype)
                   for s in shards],
        scratch_shapes=[pltpu.SemaphoreType.DMA((n * 7,)), pltpu.SemaphoreType.DMA((n * 7,)),
                        pltpu.SemaphoreType.DMA((n,))],
        compiler_params=pltpu.CompilerParams(has_side_effects=True),
    )(*shards)


def _peer(r, x, y, c):
    return ((1 - x) if r & 4 else x, (1 - y) if r & 2 else y, (1 - c) if r & 1 else c)


def _sent_block(src_ref, land_ref, k, scatter):
    if not scatter:
        return src_ref
    if len(src_ref.shape) == len(land_ref.shape):
        return src_ref.at[k]
    return _col_block(src_ref, k, land_ref.shape[2])


def _send_start(name, src, land, scatter, after=None):
    n_after = 0 if after is None else 1

    def body(*refs):
        src_ref, land_ref = refs[0], refs[1]
        send, recv, _, _, token = refs[2 + n_after:]
        x, y, c = _place()
        me = _index((x, y, c))
        for r in range(1, N_DEV):
            p = _peer(r, x, y, c)
            pltpu.make_async_remote_copy(
                src_ref=_sent_block(src_ref, land_ref, _index(p), scatter), dst_ref=land_ref.at[me],
                send_sem=send.at[r - 1], recv_sem=recv.at[r - 1],
                device_id=p, device_id_type=pl.DeviceIdType.MESH).start()
        token[...] = jnp.zeros_like(token)

    hbm, sem = pl.BlockSpec(memory_space=pltpu.HBM), pl.BlockSpec(memory_space=pltpu.SEMAPHORE)
    return pl.pallas_call(
        body, name=name,
        out_shape=(pltpu.SemaphoreType.DMA((N_DEV - 1,)), pltpu.SemaphoreType.DMA((N_DEV - 1,)),
                   pltpu.HBM(src.shape, src.dtype), pltpu.HBM(land.shape, land.dtype),
                   _sds((SUBLANES, LANES), F32)),
        in_specs=(hbm, hbm) + (_any_spec(),) * n_after,
        out_specs=(sem, sem, hbm, hbm, pl.BlockSpec(memory_space=pltpu.VMEM)),
        input_output_aliases={0: 2, 1: 3},
        compiler_params=pltpu.CompilerParams(has_side_effects=pltpu.SideEffectType.DATAFLOW_SIDE_EFFECTING),
    )(pltpu.with_memory_space_constraint(src, pltpu.HBM), pltpu.with_memory_space_constraint(land, pltpu.HBM),
      *([] if after is None else [after]))


def _send_wait(name, started, after, scatter):
    send, recv, src_thru, land_thru, _ = started

    def body(src_ref, land_ref, send_r, recv_r, after_ref, src_out, land_out):
        x, y, c = _place()
        for r in range(1, N_DEV):
            p = _peer(r, x, y, c)
            k = _index(p)
            cp = pltpu.make_async_remote_copy(
                src_ref=_sent_block(src_ref, land_ref, k, scatter), dst_ref=land_ref.at[k],
                send_sem=send_r.at[r - 1], recv_sem=recv_r.at[r - 1],
                device_id=p, device_id_type=pl.DeviceIdType.MESH)
            cp.wait_send()
            cp.wait_recv()

    hbm, sem = pl.BlockSpec(memory_space=pltpu.HBM), pl.BlockSpec(memory_space=pltpu.SEMAPHORE)
    return pl.pallas_call(
        body, name=name,
        out_shape=(pltpu.HBM(src_thru.shape, src_thru.dtype), pltpu.HBM(land_thru.shape, land_thru.dtype)),
        in_specs=(hbm, hbm, sem, sem, _any_spec()), out_specs=(hbm, hbm),
        input_output_aliases={0: 0, 1: 1},
        compiler_params=pltpu.CompilerParams(has_side_effects=pltpu.SideEffectType.DATAFLOW_SIDE_EFFECTING),
    )(src_thru, land_thru, send, recv, after)


SPREAD, PASS_ON = (1, 2, 4, 6), (2, 4, 6)


def _cols_start(name, land, width, pass_on, after):
    peers = PASS_ON if pass_on else SPREAD

    def body(land_ref, after_ref, send, recv, land_thru, token):
        x, y, c = _place()
        for n, r in enumerate(peers):
            block = _col_block(land_ref, _index(_peer(r, x, y, c) if pass_on else (x, y, c)), width)
            pltpu.make_async_remote_copy(
                src_ref=block if pass_on else after_ref, dst_ref=block,
                send_sem=send.at[n], recv_sem=recv.at[n],
                device_id=_peer(1, x, y, c) if pass_on else _peer(r, x, y, c),
                device_id_type=pl.DeviceIdType.MESH).start()
        token[...] = jnp.zeros_like(token)

    hbm, sem = pl.BlockSpec(memory_space=pltpu.HBM), pl.BlockSpec(memory_space=pltpu.SEMAPHORE)
    return pl.pallas_call(
        body, name=name,
        out_shape=(pltpu.SemaphoreType.DMA((len(peers),)), pltpu.SemaphoreType.DMA((len(peers),)),
                   pltpu.HBM(land.shape, land.dtype), _sds((SUBLANES, LANES), F32)),
        in_specs=(hbm, _any_spec()), out_specs=(sem, sem, hbm, pl.BlockSpec(memory_space=pltpu.VMEM)),
        input_output_aliases={0: 2},
        compiler_params=pltpu.CompilerParams(has_side_effects=pltpu.SideEffectType.DATAFLOW_SIDE_EFFECTING),
    )(pltpu.with_memory_space_constraint(land, pltpu.HBM), after)


def _cols_wait(name, started, width, pass_on, after):
    send, recv, land_thru, _ = started
    peers = PASS_ON if pass_on else SPREAD

    def body(land_ref, send_r, recv_r, after_ref, land_out):
        x, y, c = _place()
        for n, r in enumerate(peers):
            sent = _col_block(land_ref, _index(_peer(r, x, y, c) if pass_on else (x, y, c)), width)
            came = _col_block(land_ref, _index(_peer(r ^ 1 if pass_on else r, x, y, c)), width)
            cp = pltpu.make_async_remote_copy(
                src_ref=sent, dst_ref=came, send_sem=send_r.at[n], recv_sem=recv_r.at[n],
                device_id=_peer(1, x, y, c) if pass_on else _peer(r, x, y, c),
                device_id_type=pl.DeviceIdType.MESH)
            cp.wait_send()
            cp.wait_recv()

    hbm, sem = pl.BlockSpec(memory_space=pltpu.HBM), pl.BlockSpec(memory_space=pltpu.SEMAPHORE)
    return pl.pallas_call(
        body, name=name, out_shape=(pltpu.HBM(land_thru.shape, land_thru.dtype),),
        in_specs=(hbm, sem, sem, _any_spec()), out_specs=(hbm,), input_output_aliases={0: 0},
        compiler_params=pltpu.CompilerParams(has_side_effects=pltpu.SideEffectType.DATAFLOW_SIDE_EFFECTING),
    )(land_thru, send, recv, after)[0]


def _own_block(blocks, block, me):
    return lax.dynamic_update_index_in_dim(blocks, block, me, 0)


def _adam_math(parts, w, m, v):
    c1 = 1.0 / (1.0 - ADAM_B1 ** ADAM_STEP)
    c2 = 1.0 / (1.0 - ADAM_B2 ** ADAM_STEP)
    g = parts[0].astype(F32)
    for k in range(1, N_DEV):
        g = g + parts[k].astype(F32)
    mn = ADAM_B1 * m + (1.0 - ADAM_B1) * g
    vn = ADAM_B2 * v + (1.0 - ADAM_B2) * (g * g)
    return g, -ADAM_LR * ((mn * c1) / (jnp.sqrt(vn * c2) + ADAM_EPS) + ADAM_WD * w), mn, vn


def _adamw(name, parts, w, m, v):
    rows, cols = w.shape
    br = min(ADAM_ROWS, rows)

    def body(p_ref, w_ref, m_ref, v_ref, g_o, d_o, m_o, v_o):
        g_o[...], d_o[...], m_o[...], v_o[...] = _adam_math(p_ref[...], w_ref[...], m_ref[...], v_ref[...])

    blk = _bs((br, cols), lambda i: (i, 0))
    out = _sds((rows, cols), F32)
    return pl.pallas_call(
        body, name=name, grid=(rows // br,),
        in_specs=[_bs((N_DEV, br, cols), lambda i: (0, i, 0)), blk, blk, blk],
        out_specs=[blk] * 4, out_shape=[out] * 4,
        compiler_params=_params(("parallel",)),
    )(parts, w, m, v)


def _pack(arrs):
    tile = SUBLANES * LANES
    flat = []
    for a in arrs:
        f = a.reshape(-1).astype(F32)
        pad = (-f.shape[0]) % tile
        flat.append(jnp.pad(f, (0, pad)) if pad else f)
    total = sum(f.shape[0] for f in flat)
    tail = (-total) % (ADAM_ROWS * LANES)
    if tail:
        flat.append(jnp.zeros((tail,), F32))
    return jnp.concatenate(flat).reshape(-1, LANES)


def _unpack(buf, like):
    tile = SUBLANES * LANES
    flat = buf.reshape(-1)
    out, off = [], 0
    for a in like:
        n = math.prod(a.shape)
        out.append(flat[off:off + n].reshape(a.shape))
        off += n + ((-n) % tile)
    return out


def _to_block_b(b):
    gl = LANES // SSM_GROUP
    tb = b.reshape(N_SLAB, gl, SSM_STATE, SSM_GROUP).transpose(0, 1, 3, 2)
    eye = jnp.eye(gl, dtype=F32)
    return (tb[:, :, :, None, :] * eye[None, :, None, :, None]).reshape(N_SLAB, LANES, SLAB_STATE)


def _from_block_b(bx):
    gl = LANES // SSM_GROUP
    d = jnp.einsum("jghgp->jgph", bx.reshape(N_SLAB, gl, SSM_GROUP, gl, SSM_STATE))
    return d.reshape(N_SLAB * gl, SSM_STATE, SSM_GROUP)


def _to_block_c(cm):
    gl = LANES // SSM_GROUP
    tc = cm.reshape(N_SLAB, gl, SSM_GROUP, SSM_STATE).transpose(0, 1, 3, 2)
    eye = jnp.eye(gl, dtype=F32)
    return (tc[:, :, :, None, :] * eye[None, :, None, :, None]).reshape(N_SLAB, SLAB_STATE, LANES)


def _from_block_c(cx):
    gl = LANES // SSM_GROUP
    d = jnp.einsum("jgpgh->jghp", cx.reshape(N_SLAB, gl, SSM_STATE, gl, SSM_GROUP))
    return d.reshape(N_SLAB * gl, SSM_GROUP, SSM_STATE)


def kernel(x, norm_mix_g, w_in, ssm_a_re, ssm_a_im, ssm_b_re, ssm_b_im, ssm_c_re, ssm_c_im, ssm_d, ssm_log_dt, ssm_glu_w, ssm_glu_b, sgu_ln_g, sgu_ln_b, sgu_w, sgu_b, out_norm_ssm_g, out_norm_sgu_g, w_out, norm_mlp_g, w_up, w_down, norm_final_g, loss_target, m_norm_mix_g, m_w_in, m_ssm_a_re, m_ssm_a_im, m_ssm_b_re, m_ssm_b_im, m_ssm_c_re, m_ssm_c_im, m_ssm_d, m_ssm_log_dt, m_ssm_glu_w, m_ssm_glu_b, m_sgu_ln_g, m_sgu_ln_b, m_sgu_w, m_sgu_b, m_out_norm_ssm_g, m_out_norm_sgu_g, m_w_out, m_norm_mlp_g, m_w_up, m_w_down, m_norm_final_g, v_norm_mix_g, v_w_in, v_ssm_a_re, v_ssm_a_im, v_ssm_b_re, v_ssm_b_im, v_ssm_c_re, v_ssm_c_im, v_ssm_d, v_ssm_log_dt, v_ssm_glu_w, v_ssm_glu_b, v_sgu_ln_g, v_sgu_ln_b, v_sgu_w, v_sgu_b, v_out_norm_ssm_g, v_out_norm_sgu_g, v_w_out, v_norm_mlp_g, v_w_up, v_w_down, v_norm_final_g):
    given = dict(locals())
    names = ["norm_mix_g", "w_in", "ssm_a_re", "ssm_a_im", "ssm_b_re", "ssm_b_im", "ssm_c_re", "ssm_c_im",
             "ssm_d", "ssm_log_dt", "ssm_glu_w", "ssm_glu_b", "sgu_ln_g", "sgu_ln_b", "sgu_w", "sgu_b",
             "out_norm_ssm_g", "out_norm_sgu_g", "w_out", "norm_mlp_g", "w_up", "w_down", "norm_final_g"]
    big = ["w_in", "ssm_glu_w", "w_out", "w_up", "w_down"]
    small = [n for n in names if n not in big]

    d = D_MODEL
    t = x.shape[1]
    tb = min(1024, t)
    xs = x[0]
    target = loss_target[0]
    nsh_in = w_in.shape[2]
    nsh_up = w_up.shape[2]
    d_ff = nsh_up * N_DEV
    n_in = nsh_in * N_DEV

    me = _index(_place())
    shard_in = w_in[0].astype(BF16)
    spread = _cols_start("gather_w_in_spread", lax.empty((d, n_in), BF16), nsh_in, False, shard_in)
    gathers = {}

    def start_gather(n, after=None):
        shard = given[n][0].astype(BF16)
        gathers[n] = _send_start("gather_start_" + n, shard, lax.empty((N_DEV,) + shard.shape, BF16), False, after)
        return gathers[n][4]

    def gathered(n, after):
        shard, blocks = _send_wait("gather_wait_" + n, gathers[n], after, False)
        return _own_block(blocks, shard, me)

    nst = N_SLAB * SLAB_STATE
    are, aim = ssm_a_re.reshape(1, nst), ssm_a_im.reshape(1, nst)
    ldt = jnp.repeat(ssm_log_dt[0], SSM_STATE).reshape(1, nst)
    bxr, bxi = _to_block_b(ssm_b_re[0]), _to_block_b(ssm_b_im[0])
    cxr, cxi = _to_block_c(ssm_c_re[0]), _to_block_c(ssm_c_im[0])
    dvec = ssm_d.reshape(1, SSM_WIDTH)

    h1 = _rms_fwd("norm_mix", xs, norm_mix_g, deps=[spread[3]])
    bre, bim, cre, cimn, apr, api, air, aii = _ssm_prep(are, aim, ldt, bxr, bxi, cxr, cxi)
    tabs = (bre, bim, cre, cimn, apr, api, air, aii)
    land_in = _cols_wait("gather_w_in_landed", spread, nsh_in, False, h1)
    passed = _cols_start("gather_w_in_pass", land_in, nsh_in, True, bre)
    wg_in = _cols_wait("gather_w_in_passed", passed, nsh_in, True, passed[3])
    wg_in = lax.dynamic_update_slice_in_dim(wg_in, shard_in, me * nsh_in, axis=1)
    tokens = [start_gather("ssm_glu_w", wg_in), start_gather("w_out", wg_in)]
    bn_i = n_in // 2
    (z,) = _mm("in_proj", "nn", (t // tb, n_in // bn_i, 1),
               (h1, _bs((tb, d), lambda i, j, k: (i, 0))),
               (wg_in, _bs((d, bn_i), lambda i, j, k: (0, j))),
               [(_sds((t, n_in), F32), _bs((tb, bn_i), lambda i, j, k: (i, j)))], deps=tokens)
    tokens = [start_gather("w_up", z), start_gather("w_down", z)]
    y_pre, yg_b, p_re, p_im = _ssm_fwd(z, dvec, *tabs, deps=tokens)

    def glu_ep(acc, yp, b):
        gate = _sigmoid(acc + b)
        return _gelu(yp) * gate, gate

    hw = SSM_WIDTH // 2
    wg_glu = gathered("ssm_glu_w", yg_b).reshape(SSM_WIDTH, SSM_WIDTH)
    tile_g = _bs((tb, hw), lambda i, j, k: (i, j))
    y_ssm, gate = _mm("glu", "nn", (t // tb, 2, 1),
                      (yg_b, _bs((tb, SSM_WIDTH), lambda i, j, k: (i, 0))),
                      (wg_glu, _bs((SSM_WIDTH, hw), lambda i, j, k: (0, j))),
                      [(_sds((t, SSM_WIDTH), F32), tile_g), (_sds((t, SSM_WIDTH), F32), tile_g)],
                      extras=[(y_pre, tile_g), (ssm_glu_b, _bs((1, hw), lambda i, j, k: (0, j)))],
                      epilogue=glu_ep)

    sgu_bexp = jnp.broadcast_to(sgu_b[0][:, :, None], (SGU_HEADS, SGU_CHUNK, SGU_CHUNK))
    y_sgu = _sgu_fwd(z, sgu_ln_g, sgu_ln_b, sgu_w[0], sgu_bexp)
    mixed = _mix_norm(y_ssm, y_sgu, out_norm_ssm_g, out_norm_sgu_g)

    tb2 = min(512, t)
    row2 = _bs((tb2, d), lambda i, j, k: (i, 0))
    vec2 = _bs((1, d), lambda i, j, k: (0, 0))
    vec_sum = (_sds((1, d), F32), vec2)
    wg_out = gathered("w_out", mixed).reshape(d, d)

    def out_ep(acc, r, g):
        x2v = acc + r
        return x2v, _rms_math(x2v, g)

    x2, h2 = _mm("out_proj", "nn", (t // tb2, 1, 1),
                 (mixed, row2), (wg_out, _bs((d, d), lambda i, j, k: (0, 0))),
                 [(_sds((t, d), F32), row2), (_sds((t, d), BF16), row2)],
                 extras=[(xs, row2), (norm_mlp_g, vec2)], epilogue=out_ep, ep_rows=EP_ROWS)

    def up_ep(acc):
        r = jnp.maximum(acc, 0.0)
        return r * r, r

    tile_f = _bs((tb, nsh_up), lambda i, j, k: (i, j))
    wg_up = gathered("w_up", h2)
    f_act, r_act = _mm("mlp_up", "nn", (t // tb, N_DEV, 1),
                       (h2, _bs((tb, d), lambda i, j, k: (i, 0))),
                       (wg_up, _bs((None, d, nsh_up), lambda i, j, k: (j, 0, 0))),
                       [(_sds((t, d_ff), BF16), tile_f), (_sds((t, d_ff), BF16), tile_f)],
                       epilogue=up_ep)
    bk_d, bn_o = 2048, 1024
    tile_o = _bs((tb, bn_o), lambda i, j, k: (i, j))
    wg_down = gathered("w_down", f_act).reshape(d_ff, d)
    (x3,) = _mm("mlp_down", "nn", (t // tb, d // bn_o, d_ff // bk_d),
                (f_act, _bs((tb, bk_d), lambda i, j, k: (i, k))),
                (wg_down, _bs((bk_d, bn_o), lambda i, j, k: (k, j))),
                [(_sds((t, d), F32), tile_o)],
                extras=[(x2, tile_o)], epilogue=lambda acc, r: (acc + r,))
    dx3, dx3_b, g_final, err2 = _final_loss(x3, target, norm_final_g.reshape(1, d))
    loss = lax.psum(0.5 * jnp.sum(err2) / d, MESH_AXES)

    sends = {}

    def send_grad(n, g, land_shape=None):
        sends[n] = _send_start("grad_start_" + n, g, lax.empty(land_shape or g.shape, BF16), True)
        return [sends[n][4]]

    bn_a = 1024
    tile_a = _bs((tb, bn_a), lambda i, j, k: (i, j))
    (da,) = _mm("mlp_down_dx", "nt", (t // tb, d_ff // bn_a, 1),
                (dx3_b, _bs((tb, d), lambda i, j, k: (i, 0))),
                (wg_down, _bs((bn_a, d), lambda i, j, k: (j, 0))),
                [(_sds((t, d_ff), BF16), tile_a)],
                extras=[(r_act, tile_a)], epilogue=lambda acc, r: (acc * (2.0 * r.astype(F32)),))
    sq = 1024
    (gw_down,) = _mm("mlp_down_dw", "tn", (d_ff // sq, 1, t // tb),
                     (f_act, _bs((tb, sq), lambda i, j, k: (k, i))),
                     (dx3_b, _bs((tb, d), lambda i, j, k: (k, 0))),
                     [(_sds((d_ff, d), BF16), _bs((sq, d), lambda i, j, k: (i, 0)))])
    sent = send_grad("w_down", gw_down.reshape(N_DEV, -1, d))
    (dh2,) = _mm("mlp_up_dx", "nt", (t // tb, 1, N_DEV),
                 (da, _bs((tb, nsh_up), lambda i, j, k: (i, k))),
                 (wg_up, _bs((None, d, nsh_up), lambda i, j, k: (k, 0, 0))),
                 [(_sds((t, d), F32), _bs((tb, d), lambda i, j, k: (i, 0)))], deps=sent)

    def norm_bwd_side(dh, xv, dres, g):
        dx, dg = _rms_bwd_math(dh, xv, g)
        dx = dx + dres
        return dx, dx, dg

    up_dw_grid = (1, N_DEV, t // tb)
    side_rows = (t // math.prod(up_dw_grid), d)
    gw_up, dx2, dx2_b, g_norm_mlp = _mm(
        "mlp_up_dw", "tn", up_dw_grid,
        (h2, _bs((tb, d), lambda i, j, k: (k, 0))),
        (da, _bs((tb, nsh_up), lambda i, j, k: (k, j))),
        [(_sds((N_DEV, d, nsh_up), BF16), _bs((None, d, nsh_up), lambda i, j, k: (j, 0, 0)))],
        side=(norm_bwd_side, [(dh2, side_rows), (x2, side_rows), (dx3, side_rows), (norm_mlp_g, (1, d))],
              [(_sds((t, d), F32), side_rows), (_sds((t, d), BF16), side_rows)], [(_sds((1, d), F32), (1, d))]))
    sent = send_grad("w_up", gw_up)

    tk = min(2048, t)
    (gw_out,) = _mm("out_proj_dw", "tn", (d // sq, d // sq, t // tk),
                    (mixed, _bs((tk, sq), lambda i, j, k: (k, i))),
                    (dx2_b, _bs((tk, sq), lambda i, j, k: (k, j))),
                    [(_sds((d, d), BF16), _bs((sq, sq), lambda i, j, k: (i, j)))], deps=sent)
    sent = send_grad("w_out", gw_out.reshape(N_DEV, -1, d))
    half2 = _bs((tb2, SSM_WIDTH), lambda i, j, k: (i, 0))
    vech = _bs((1, SSM_WIDTH), lambda i, j, k: (0, 0))
    half_sum = (_sds((1, SSM_WIDTH), F32), vech)

    def out_dx_ep(acc, ya, yb, ga, gb, yp, gt):
        dya, dga = _rms_bwd_math(acc[:, :SSM_WIDTH], ya, ga)
        dyb, dgb = _rms_bwd_math(acc[:, SSM_WIDTH:], yb, gb)
        dpre = dya * _gelu(yp) * gt * (1.0 - gt)
        return dya, dyb, dpre, dga, dgb, jnp.sum(dpre, axis=0, keepdims=True)

    dy_ssm, dy_sgu, dpre_b, g_onorm_ssm, g_onorm_sgu, g_glu_b = _mm(
        "out_proj_dx", "nt", (t // tb2, 1, 1),
        (dx2_b, row2), (wg_out, _bs((d, d), lambda i, j, k: (0, 0))),
        [(_sds((t, SSM_WIDTH), F32), half2), (_sds((t, SSM_WIDTH), F32), half2),
         (_sds((t, SSM_WIDTH), BF16), half2)],
        extras=[(y_ssm, half2), (y_sgu, half2), (out_norm_ssm_g, vech), (out_norm_sgu_g, vech),
                (y_pre, half2), (gate, half2)],
        epilogue=out_dx_ep, sums=[half_sum, half_sum, half_sum], deps=sent, acc_shape=(tb2, d),
        ep_rows=EP_ROWS)

    (gw_glu,) = _mm("glu_dw", "tn", (1, 1, t // tb),
                    (yg_b, _bs((tb, SSM_WIDTH), lambda i, j, k: (k, 0))),
                    (dpre_b, _bs((tb, SSM_WIDTH), lambda i, j, k: (k, 0))),
                    [(_sds((SSM_WIDTH, SSM_WIDTH), BF16), _bs((SSM_WIDTH, SSM_WIDTH), lambda i, j, k: (0, 0)))])
    sent = send_grad("ssm_glu_w", gw_glu.reshape(N_DEV, -1, SSM_WIDTH))
    (dy_pre,) = _mm("glu_dx", "nt", (t // tb, 2, 1),
                    (dpre_b, _bs((tb, SSM_WIDTH), lambda i, j, k: (i, 0))),
                    (wg_glu, _bs((hw, SSM_WIDTH), lambda i, j, k: (j, 0))),
                    [(_sds((t, SSM_WIDTH), F32), tile_g)],
                    extras=[(dy_ssm, tile_g), (gate, tile_g), (y_pre, tile_g)],
                    epilogue=lambda acc, dy, gt, yp: ((dy * gt + acc) * _gelu_grad(yp),), deps=sent)
    du_b, dbre, dbim, dcre, dcimn, q_re, q_im, dd = _ssm_bwd(dy_pre, z, p_re, p_im, dvec, *tabs)
    dare, daim, dldt, dbxr, dbxi = _ssm_prep_bwd(are, aim, ldt, bxr, bxi, dbre, dbim, q_re, q_im)

    dz_b, g_ln_g, g_ln_b, g_sgu_w, g_sgu_bx = _sgu_bwd(dy_sgu, du_b, z, sgu_ln_g, sgu_ln_b, sgu_w[0], sgu_bexp)
    local_small = {
        "ssm_a_re": dare, "ssm_a_im": daim,
        "ssm_b_re": _from_block_b(dbxr), "ssm_b_im": _from_block_b(dbxi),
        "ssm_c_re": _from_block_c(dcre), "ssm_c_im": -_from_block_c(dcimn),
        "ssm_d": dd, "ssm_log_dt": dldt.reshape(-1, SSM_STATE).sum(axis=-1),
        "ssm_glu_b": g_glu_b, "sgu_ln_g": g_ln_g, "sgu_ln_b": g_ln_b, "sgu_w": g_sgu_w,
        "sgu_b": g_sgu_bx[:, :, 0], "out_norm_ssm_g": g_onorm_ssm, "out_norm_sgu_g": g_onorm_sgu,
        "norm_mlp_g": g_norm_mlp, "norm_final_g": g_final,
    }
    small_early = [n for n in small if n in local_small]
    small_late = [n for n in small if n not in local_small]
    packed = _pack([local_small[n] for n in small_early])
    small_send = _send_start("small_start", packed, lax.empty((N_DEV,) + packed.shape, F32), False)

    def landed_parts(n, after):
        sent_blocks, landed = _send_wait("grad_wait_" + n, sends[n], after, True)
        if sent_blocks.ndim == landed.ndim:
            own = lax.dynamic_index_in_dim(sent_blocks, me, 0, keepdims=False)
        else:
            own = lax.dynamic_slice_in_dim(sent_blocks, me * landed.shape[2], landed.shape[2], axis=1)
        return _own_block(landed, own, me)

    in_dw_grid = (d // sq, n_in // bn_i, t // tb)
    riders = ["w_down", "w_up"]
    side_ins, side_outs = [], []
    for n in riders:
        rows, cols = given[n].shape[1:]
        blk = (rows // math.prod(in_dw_grid), cols)
        side_ins += [(landed_parts(n, dz_b), (N_DEV,) + blk), (given[n][0], blk),
                     (given["m_" + n][0], blk), (given["v_" + n][0], blk)]
        side_outs += [(_sds((rows, cols), F32), blk)] * 4

    def adam_side(*tiles):
        return sum((_adam_math(*tiles[4 * r:4 * r + 4]) for r in range(len(riders))), ())

    gw_in, *rider_res = _mm("in_proj_dw", "tn", in_dw_grid,
                            (h1, _bs((tb, sq), lambda i, j, k: (k, i))),
                            (dz_b, _bs((tb, bn_i), lambda i, j, k: (k, j))),
                            [(_sds((d, n_in), BF16), _bs((sq, bn_i), lambda i, j, k: (i, j)))],
                            deps=[small_send[4]], side=(adam_side, side_ins, side_outs, []))
    sent = send_grad("w_in", gw_in, (N_DEV, d, nsh_in))

    def in_dx_ep(acc, xv, dres, g):
        dx, dg = _rms_bwd_math(acc, xv, g)
        return dx + dres, dg

    grad_x, g_norm_mix = _mm("in_proj_dx", "nt", (t // tb2, 1, n_in // bn_i),
                             (dz_b, _bs((tb2, bn_i), lambda i, j, k: (i, k))),
                             (wg_in, _bs((d, bn_i), lambda i, j, k: (0, k))),
                             [(_sds((t, d), F32), row2)],
                             extras=[(xs, row2), (dx2, row2), (norm_mix_g, vec2)],
                             epilogue=in_dx_ep, sums=[vec_sum], deps=sent, ep_rows=EP_ROWS)

    (late_parts,) = _all_gather("gather_late_grads", [_pack([g_norm_mix])])
    packed, early_parts = _send_wait("small_wait", small_send, grad_x, False)
    early_parts = _own_block(early_parts, packed, me)

    grads, deltas, new_m, new_v = {}, {}, {}, {}
    for r, n in enumerate(riders):
        grads[n], deltas[n], new_m[n], new_v[n] = [a.reshape(given[n].shape) for a in rider_res[4 * r:4 * r + 4]]
    for n in big:
        if n in riders:
            continue
        res = _adamw("adamw_" + n, landed_parts(n, grad_x), given[n][0], given["m_" + n][0], given["v_" + n][0])
        grads[n], deltas[n], new_m[n], new_v[n] = [r.reshape(given[n].shape) for r in res]
    for tag, group, parts in (("early", small_early, early_parts), ("late", small_late, late_parts)):
        like = [given[n] for n in group]
        res = _adamw("adamw_small_" + tag, parts, _pack(like), _pack([given["m_" + n] for n in group]),
                     _pack([given["v_" + n] for n in group]))
        for store, buf in zip((grads, deltas, new_m, new_v), res):
            for n, a in zip(group, _unpack(buf, like)):
                store[n] = a

    return (loss, grad_x.reshape(x.shape), *[grads[n] for n in names], *[deltas[n] for n in names],
            *[new_m[n] for n in names], *[new_v[n] for n in names])
```

```python
import functools
import math

import jax
import jax.numpy as jnp
from jax import lax
from jax.experimental import pallas as pl
from jax.experimental.pallas import tpu as pltpu

F32, BF16 = jnp.float32, jnp.bfloat16
EPS = 1e-6
N_DEV = 8
D_MODEL = 2048
SSM_WIDTH = 1024
SSM_GROUP = 16
SSM_STATE = 64
SGU_HEADS = 8
SGU_CHUNK = 128
SGU_FWD_CHUNKS = 4
LANES = 128
SUBLANES = 8
N_SLAB = SSM_WIDTH // LANES
SLAB_STATE = (LANES // SSM_GROUP) * SSM_STATE
SCAN_BLOCK = 256
SLABS_PER_STEP = 8
VMEM_LIMIT = 56 * 1024 * 1024
ROW_BLOCK = 512
RING_SLOTS = 3
EP_ROWS = 128
ADAM_ROWS = 128
MESH_AXES = ("x", "y", "c")

ADAM_LR, ADAM_B1, ADAM_B2, ADAM_EPS, ADAM_WD, ADAM_STEP = 0.001, 0.9, 0.999, 1e-08, 0.01, 10

_GELU_C0 = math.sqrt(2.0 / math.pi)
_GELU_C1 = 0.044715


def _gelu(v):
    return 0.5 * v * (1.0 + jnp.tanh(_GELU_C0 * (v + _GELU_C1 * v * v * v)))


def _gelu_grad(v):
    th = jnp.tanh(_GELU_C0 * (v + _GELU_C1 * v * v * v))
    return 0.5 * (1.0 + th) + 0.5 * v * (1.0 - th * th) * _GELU_C0 * (1.0 + 3.0 * _GELU_C1 * v * v)


def _sigmoid(v):
    return 1.0 / (1.0 + jnp.exp(-v))


def _params(sem=None):
    return pltpu.CompilerParams(dimension_semantics=sem, vmem_limit_bytes=VMEM_LIMIT)


def _dot(a, b, mode="nn"):
    dims = {"nn": ((1,), (0,)), "nt": ((1,), (1,)), "tn": ((0,), (0,))}[mode]
    return lax.dot_general(a, b, (dims, ((), ())), preferred_element_type=F32)


def _mm(name, mode, grid, a, b, outs, extras=(), epilogue=None, deps=(), sums=(), acc_shape=None,
        ep_rows=None, side=None):
    nk = grid[2]
    n_ex, n_out, n_dep, n_sum = len(extras), len(outs), len(deps), len(sums)
    assert not sums or grid[1] == 1
    if acc_shape is None:
        acc_shape = tuple(d for d in outs[0][1].block_shape if d is not None)
    side_fn, side_ins, side_outs, side_sums = side or (None, (), (), ())
    s_in, s_out, s_sum = len(side_ins), len(side_outs), len(side_sums)

    def body(*refs):
        a_ref, b_ref = refs[0], refs[1]
        ex = refs[2:2 + n_ex]
        pos = 2 + n_ex
        side_in = refs[pos:pos + s_in]
        pos += s_in + n_dep
        out_refs = refs[pos:pos + n_out]
        sum_refs = refs[pos + n_out:pos + n_out + n_sum]
        pos += n_out + n_sum
        side_out = refs[pos:pos + s_out]
        side_sum = refs[pos + s_out:pos + s_out + s_sum]
        acc = refs[-1]
        k = pl.program_id(2)

        @pl.when(k == 0)
        def _():
            acc[...] = jnp.zeros_like(acc)

        if side_sum:
            @pl.when((pl.program_id(0) == 0) & (pl.program_id(1) == 0) & (k == 0))
            def _():
                for o in side_sum:
                    o[...] = jnp.zeros_like(o)

        acc[...] += _dot(a_ref[...], b_ref[...], mode)
        if side_fn is not None:
            res = side_fn(*[r[...] for r in side_in])
            for o, r in zip(side_out, res[:s_out]):
                o[...] = r.astype(o.dtype)
            for o, r in zip(side_sum, res[s_out:]):
                o[...] += r

        def finish(rows):
            args = [e[rows, :] if e.shape[0] == acc_shape[0] else e[...] for e in ex]
            res = acc[rows, :]
            res = (res,) if epilogue is None else epilogue(res, *args)
            for o, r in zip(out_refs, res[:n_out]):
                o[rows, :] = r.astype(o.dtype)
            return tuple(res[n_out:])

        @pl.when(k == nk - 1)
        def _():
            if ep_rows is None:
                terms = finish(slice(None))
            else:
                def chunk(c, tot):
                    rows = pl.ds(pl.multiple_of(c * ep_rows, ep_rows), ep_rows)
                    return tuple(s + r for s, r in zip(tot, finish(rows)))

                zero = tuple(jnp.zeros(o.shape, F32) for o in sum_refs)
                terms = lax.fori_loop(0, acc_shape[0] // ep_rows, chunk, zero)
            for o, r in zip(sum_refs, terms):
                _add_up(o, r, pl.program_id(0) == 0)

    def side_spec(block):
        nd = len(block)
        if nd == 2 and block[0] == 1:
            return _bs(block, lambda i, j, k: (0, 0))
        return _bs(block, lambda i, j, k: (0,) * (nd - 2) + ((i * grid[1] + j) * grid[2] + k, 0))

    sem = ("arbitrary",) * 3 if sums or side else ("parallel", "parallel", "arbitrary")
    res = pl.pallas_call(
        body, name=name, grid=grid,
        in_specs=[a[1], b[1]] + [e[1] for e in extras] + [side_spec(blk) for _, blk in side_ins]
        + [_any_spec()] * n_dep,
        out_specs=[o[1] for o in outs] + [o[1] for o in sums]
        + [side_spec(blk) for _, blk in side_outs] + [side_spec(blk) for _, blk in side_sums],
        out_shape=[o[0] for o in outs] + [o[0] for o in sums]
        + [o[0] for o in side_outs] + [o[0] for o in side_sums],
        scratch_shapes=[pltpu.VMEM(acc_shape, F32)],
        compiler_params=_params(sem),
    )(a[0], b[0], *[e[0] for e in extras], *[x for x, _ in side_ins], *deps)
    return res


def _add_up(ref, term, first):
    @pl.when(first)
    def _():
        ref[...] = term

    @pl.when(jnp.logical_not(first))
    def _():
        ref[...] += term


def _rms_math(xv, g):
    return xv * lax.rsqrt(jnp.mean(xv * xv, axis=-1, keepdims=True) + EPS) * g


def _rms_bwd_math(dy, xv, g):
    r = lax.rsqrt(jnp.mean(xv * xv, axis=-1, keepdims=True) + EPS)
    xhat = xv * r
    dxhat = dy * g
    dx = r * (dxhat - xhat * jnp.mean(dxhat * xhat, axis=-1, keepdims=True))
    return dx, jnp.sum(dy * xhat, axis=0, keepdims=True)


def _sds(shape, dtype):
    return jax.ShapeDtypeStruct(shape, dtype)


def _bs(shape, fn):
    return pl.BlockSpec(shape, fn)


def _rms_fwd(name, x, g, deps=()):
    t, w = x.shape
    br = min(ROW_BLOCK, t)

    def body(*refs):
        x_ref, g_ref, o_ref = refs[0], refs[1], refs[-1]
        xv = x_ref[...]
        r = lax.rsqrt(jnp.mean(xv * xv, axis=-1, keepdims=True) + EPS)
        o_ref[...] = (xv * r * g_ref[...]).astype(BF16)

    return pl.pallas_call(
        body, name=name, grid=(t // br,),
        in_specs=[_bs((br, w), lambda i: (i, 0)), _bs((1, w), lambda i: (0, 0))] + [_any_spec()] * len(deps),
        out_specs=_bs((br, w), lambda i: (i, 0)),
        out_shape=_sds((t, w), BF16),
        compiler_params=_params(("parallel",)),
    )(x, g, *deps)


def _mix_norm(ya, yb, ga, gb):
    t, w = ya.shape
    br = min(ROW_BLOCK, t)

    def body(a_ref, b_ref, ga_ref, gb_ref, o_ref):
        for src, g_ref, col in ((a_ref, ga_ref, 0), (b_ref, gb_ref, w)):
            v = src[...]
            r = lax.rsqrt(jnp.mean(v * v, axis=-1, keepdims=True) + EPS)
            o_ref[:, col:col + w] = (v * r * g_ref[...]).astype(BF16)

    row = _bs((br, w), lambda i: (i, 0))
    vec = _bs((1, w), lambda i: (0, 0))
    return pl.pallas_call(
        body, name="mix_norm", grid=(t // br,),
        in_specs=[row, row, vec, vec],
        out_specs=_bs((br, 2 * w), lambda i: (i, 0)),
        out_shape=_sds((t, 2 * w), BF16),
        compiler_params=_params(("parallel",)),
    )(ya, yb, ga, gb)


def _loss_math(xv, target, g):
    r = lax.rsqrt(jnp.mean(xv * xv, axis=-1, keepdims=True) + EPS)
    xhat = xv * r
    err = xhat * g - target
    dy = err * (1.0 / xv.shape[-1])
    dxhat = dy * g
    dx = r * (dxhat - xhat * jnp.mean(dxhat * xhat, axis=-1, keepdims=True))
    return dx, jnp.sum(dy * xhat, axis=0, keepdims=True), jnp.sum(err * err, axis=0, keepdims=True)


def _final_loss(x3, target, g):
    t, w = x3.shape
    br = min(ROW_BLOCK, t)
    n = t // br

    def body(x_hbm, tg_hbm, g_ref, dx_ref, dxb_ref, dg_ref, l_ref, xbuf, tbuf, sems):
        s = pl.program_id(0)

        def fetch(step):
            slot = step % RING_SLOTS
            start = step * br if isinstance(step, int) else pl.multiple_of(step * br, br)
            rows = pl.ds(start, br)
            return (pltpu.make_async_copy(x_hbm.at[rows, :], xbuf.at[slot], sems.at[0, slot]),
                    pltpu.make_async_copy(tg_hbm.at[rows, :], tbuf.at[slot], sems.at[1, slot]))

        @pl.when(s == 0)
        def _():
            for step in range(min(RING_SLOTS - 1, n)):
                for thread, cp in enumerate(fetch(step)):
                    cp.start(priority=thread)

        @pl.when(s + RING_SLOTS - 1 < n)
        def _():
            for thread, cp in enumerate(fetch(s + RING_SLOTS - 1)):
                cp.start(priority=thread)

        for cp in fetch(s):
            cp.wait()
        slot = s % RING_SLOTS
        dx, dg, e2 = _loss_math(xbuf[slot], tbuf[slot], g_ref[...])
        dx_ref[...] = dx
        dxb_ref[...] = dx.astype(BF16)
        _add_up(dg_ref, dg, s == 0)
        _add_up(l_ref, e2, s == 0)

    row = _bs((br, w), lambda i: (i, 0))
    vec = _bs((1, w), lambda i: (0, 0))
    return pl.pallas_call(
        body, name="final_loss", grid=(n,),
        in_specs=[_any_spec(), _any_spec(), vec], out_specs=[row, row, vec, vec],
        out_shape=[_sds((t, w), F32), _sds((t, w), BF16), _sds((1, w), F32), _sds((1, w), F32)],
        scratch_shapes=[pltpu.VMEM((RING_SLOTS, br, w), F32), pltpu.VMEM((RING_SLOTS, br, w), F32),
                        pltpu.SemaphoreType.DMA((2, RING_SLOTS))],
        compiler_params=_params(("arbitrary",)),
    )(x3, target, g)


def _prep_math(are, aim, ldt, bxr, bxi):
    dt = jnp.exp(ldt)
    er = jnp.exp(are * dt)
    th = aim * dt
    abr, abi = er * jnp.cos(th), er * jnp.sin(th)
    nr, ni = abr - 1.0, abi
    den = are * are + aim * aim
    cr = (nr * are + ni * aim) / den
    ci = (ni * are - nr * aim) / den
    bbr, bbi = [], []
    for j in range(N_SLAB):
        sl = slice(j * SLAB_STATE, (j + 1) * SLAB_STATE)
        bbr.append(cr[:, sl] * bxr[j] - ci[:, sl] * bxi[j])
        bbi.append(cr[:, sl] * bxi[j] + ci[:, sl] * bxr[j])
    return abr, abi, bbr, bbi


def _ssm_prep(are, aim, ldt, bxr, bxi, cxr, cxi):
    nst = N_SLAB * SLAB_STATE

    def body(are_r, aim_r, ldt_r, bxr_r, bxi_r, cxr_r, cxi_r,
             bre_o, bim_o, cre_o, cimn_o, apr_o, api_o, air_o, aii_o):
        abr, abi, bbr, bbi = _prep_math(are_r[...], aim_r[...], ldt_r[...],
                                        [bxr_r[j] for j in range(N_SLAB)], [bxi_r[j] for j in range(N_SLAB)])
        for j in range(N_SLAB):
            bre_o[j] = bbr[j].astype(BF16)
            bim_o[j] = bbi[j].astype(BF16)
        cre_o[...] = cxr_r[...].astype(BF16)
        cimn_o[...] = (-cxi_r[...]).astype(BF16)

        def step(k, cur):
            cr, ci = cur
            den = cr * cr + ci * ci
            apr_o[pl.ds(k, 1), :] = cr
            api_o[pl.ds(k, 1), :] = ci
            air_o[pl.ds(k, 1), :] = cr / den
            aii_o[pl.ds(k, 1), :] = -ci / den
            return cr * abr - ci * abi, cr * abi + ci * abr

        lax.fori_loop(0, SCAN_BLOCK, step, (jnp.ones((1, nst), F32), jnp.zeros((1, nst), F32)))

    tab = _sds((SCAN_BLOCK, nst), F32)
    return pl.pallas_call(
        body, name="ssm_prep",
        out_shape=[_sds(bxr.shape, BF16), _sds(bxr.shape, BF16), _sds(cxr.shape, BF16), _sds(cxr.shape, BF16),
                   tab, tab, tab, tab],
        compiler_params=_params(),
    )(are, aim, ldt, bxr, bxi, cxr, cxi)


def _ssm_prep_bwd(are, aim, ldt, bxr, bxi, dbre, dbim, qr, qi):
    def body(are_r, aim_r, ldt_r, bxr_r, bxi_r, dbre_r, dbim_r, qr_r, qi_r,
             dare_o, daim_o, dldt_o, dbxr_o, dbxi_o):
        prim = (are_r[...], aim_r[...], ldt_r[...],
                [bxr_r[j] for j in range(N_SLAB)], [bxi_r[j] for j in range(N_SLAB)])
        (abr, abi, _, _), vjp = jax.vjp(_prep_math, *prim)
        den = abr * abr + abi * abi
        q_r, q_i = qr_r[...], qi_r[...]
        gar = (q_r * abr - q_i * abi) / den
        gai = (q_r * abi + q_i * abr) / den
        ct = (gar, gai, [dbre_r[j] for j in range(N_SLAB)], [dbim_r[j] for j in range(N_SLAB)])
        dare, daim, dldt, dbxr, dbxi = vjp(ct)
        dare_o[...] = dare
        daim_o[...] = daim
        dldt_o[...] = dldt
        for j in range(N_SLAB):
            dbxr_o[j] = dbxr[j]
            dbxi_o[j] = dbxi[j]

    row = _sds(are.shape, F32)
    return pl.pallas_call(
        body, name="ssm_prep_bwd",
        out_shape=[row, row, row, _sds(bxr.shape, F32), _sds(bxr.shape, F32)],
        compiler_params=_params(),
    )(are, aim, ldt, bxr, bxi, dbre, dbim, qr, qi)


def _tri(lower):
    r = lax.broadcasted_iota(jnp.int32, (SCAN_BLOCK, SCAN_BLOCK), 0)
    c = lax.broadcasted_iota(jnp.int32, (SCAN_BLOCK, SCAN_BLOCK), 1)
    return jnp.where((r >= c) if lower else (r <= c), 1.0, 0.0).astype(BF16)


def _cumsum_mxu(tri, v):
    return _dot(tri, v.astype(BF16))


def _ssm_specs(t, sps):
    nt = t // SCAN_BLOCK
    tab = _bs((SCAN_BLOCK, sps * SLAB_STATE), lambda j, i: (0, j))
    bmat = _bs((sps, LANES, SLAB_STATE), lambda j, i: (j, 0, 0))
    cmat = _bs((sps, SLAB_STATE, LANES), lambda j, i: (j, 0, 0))
    return nt, tab, bmat, cmat


def _slab_slices(s):
    return slice(s * LANES, (s + 1) * LANES), slice(s * SLAB_STATE, (s + 1) * SLAB_STATE)


def _ssm_fwd(z, dvec, bre, bim, cre, cimn, apr, api, air, aii, deps=()):
    t = z.shape[0]
    sps = SLABS_PER_STEP
    nt, tab, bmat, cmat = _ssm_specs(t, sps)
    nst = N_SLAB * SLAB_STATE
    last = SCAN_BLOCK - 1

    def body(*refs):
        u_ref, d_ref, bre_r, bim_r, cre_r, cimn_r, apr_r, api_r, air_r, aii_r = refs[:10]
        y_ref, yg_ref, pr_ref, pi_ref, car_r, car_i = refs[10 + len(deps):]

        @pl.when(pl.program_id(1) == 0)
        def _():
            car_r[...] = jnp.zeros_like(car_r)
            car_i[...] = jnp.zeros_like(car_i)

        tri = _tri(True)
        for s in range(sps):
            ul, sl = _slab_slices(s)
            u = u_ref[:, ul]
            ub = u.astype(BF16)
            bur, bui = _dot(ub, bre_r[s]), _dot(ub, bim_r[s])
            ir, ii = air_r[:, sl], aii_r[:, sl]
            csr = _cumsum_mxu(tri, ir * bur - ii * bui)
            csi = _cumsum_mxu(tri, ir * bui + ii * bur)
            pr, pi = apr_r[:, sl], api_r[:, sl]
            a_r, a_i = apr_r[1:2, sl], api_r[1:2, sl]
            c_r, c_i = car_r[:, sl], car_i[:, sl]
            wr = csr + (a_r * c_r - a_i * c_i)
            wi = csi + (a_r * c_i + a_i * c_r)
            sr = pr * wr - pi * wi
            si = pr * wi + pi * wr
            car_r[:, sl] = sr[last:last + 1, :]
            car_i[:, sl] = si[last:last + 1, :]
            pr_ref[:, sl] = (sr - bur).astype(BF16)
            pi_ref[:, sl] = (si - bui).astype(BF16)
            y = _dot(sr.astype(BF16), cre_r[s]) + _dot(si.astype(BF16), cimn_r[s]) + d_ref[:, ul] * u
            y_ref[:, ul] = y
            yg_ref[:, ul] = _gelu(y).astype(BF16)

    ublk = _bs((SCAN_BLOCK, sps * LANES), lambda j, i: (i, j))
    sblk = _bs((SCAN_BLOCK, sps * SLAB_STATE), lambda j, i: (i, j))
    return pl.pallas_call(
        body, name="ssm_fwd", grid=(N_SLAB // sps, nt),
        in_specs=[ublk, _bs((1, sps * LANES), lambda j, i: (0, j)), bmat, bmat, cmat, cmat, tab, tab, tab, tab]
        + [_any_spec()] * len(deps),
        out_specs=[ublk, ublk, sblk, sblk],
        out_shape=[_sds((t, SSM_WIDTH), F32), _sds((t, SSM_WIDTH), BF16),
                   _sds((t, nst), BF16), _sds((t, nst), BF16)],
        scratch_shapes=[pltpu.VMEM((1, sps * SLAB_STATE), F32), pltpu.VMEM((1, sps * SLAB_STATE), F32)],
        compiler_params=_params(("parallel", "arbitrary")),
    )(z, dvec, bre, bim, cre, cimn, apr, api, air, aii, *deps)


def _ssm_bwd(gy, z, p_re, p_im, dvec, bre, bim, cre, cimn, apr, api, air, aii):
    t = z.shape[0]
    sps = SLABS_PER_STEP
    nt, tab, bmat, cmat = _ssm_specs(t, sps)
    last = SCAN_BLOCK - 1

    def fold(v):
        return v.reshape(SCAN_BLOCK // SUBLANES, SUBLANES, v.shape[-1]).sum(axis=0)

    def body(g_ref, u_ref, pr_ref, pi_ref, d_ref, bre_r, bim_r, cre_r, cimn_r, apr_r, api_r, air_r, aii_r,
             du_ref, dbre_o, dbim_o, dcre_o, dcimn_o, qr_o, qi_o, dd_o, car_r, car_i, qacc_r, qacc_i, dacc):
        i = pl.program_id(1)

        @pl.when(i == 0)
        def _():
            for ref in (car_r, car_i, qacc_r, qacc_i, dacc, dbre_o, dbim_o, dcre_o, dcimn_o):
                ref[...] = jnp.zeros_like(ref)

        tri = _tri(False)
        for s in range(sps):
            ul, sl = _slab_slices(s)
            g = g_ref[:, ul]
            gb = g.astype(BF16)
            u = u_ref[:, ul]
            ub = u.astype(BF16)
            bur, bui = _dot(ub, bre_r[s]), _dot(ub, bim_r[s])
            p_r, p_i = pr_ref[:, sl].astype(F32), pi_ref[:, sl].astype(F32)
            srb, sib = (p_r + bur).astype(BF16), (p_i + bui).astype(BF16)
            dcre_o[s] += _dot(srb, gb, "tn")
            dcimn_o[s] += _dot(sib, gb, "tn")
            dsr, dsi = _dot(gb, cre_r[s], "nt"), _dot(gb, cimn_r[s], "nt")
            pr, pi = apr_r[:, sl], api_r[:, sl]
            csr = _cumsum_mxu(tri, pr * dsr + pi * dsi)
            csi = _cumsum_mxu(tri, pr * dsi - pi * dsr)
            al_r, al_i = apr_r[last:last + 1, sl], api_r[last:last + 1, sl]
            c_r, c_i = car_r[:, sl], car_i[:, sl]
            wr = csr + (al_r * c_r + al_i * c_i)
            wi = csi + (al_r * c_i - al_i * c_r)
            ir, ii = air_r[:, sl], aii_r[:, sl]
            lr = ir * wr + ii * wi
            li = ir * wi - ii * wr
            a_r, a_i = apr_r[1:2, sl], api_r[1:2, sl]
            car_r[:, sl] = a_r * lr[0:1, :] + a_i * li[0:1, :]
            car_i[:, sl] = a_r * li[0:1, :] - a_i * lr[0:1, :]
            lrb, lib = lr.astype(BF16), li.astype(BF16)
            dbre_o[s] += _dot(ub, lrb, "tn")
            dbim_o[s] += _dot(ub, lib, "tn")
            du = d_ref[:, ul] * g + _dot(lrb, bre_r[s], "nt") + _dot(lib, bim_r[s], "nt")
            du_ref[:, ul] = du.astype(BF16)
            qacc_r[:, sl] += fold(lr * p_r + li * p_i)
            qacc_i[:, sl] += fold(li * p_r - lr * p_i)
            dacc[:, ul] += fold(g * u)

        @pl.when(i == nt - 1)
        def _():
            qr_o[...] = jnp.sum(qacc_r[...], axis=0, keepdims=True)
            qi_o[...] = jnp.sum(qacc_i[...], axis=0, keepdims=True)
            dd_o[...] = jnp.sum(dacc[...], axis=0, keepdims=True)

    rev = lambda j, i: (nt - 1 - i, j)
    ublk = _bs((SCAN_BLOCK, sps * LANES), rev)
    sblk = _bs((SCAN_BLOCK, sps * SLAB_STATE), rev)
    qrow = _bs((1, sps * SLAB_STATE), lambda j, i: (0, j))
    urow = _bs((1, sps * LANES), lambda j, i: (0, j))
    nst = N_SLAB * SLAB_STATE
    return pl.pallas_call(
        body, name="ssm_bwd", grid=(N_SLAB // sps, nt),
        in_specs=[ublk, ublk, sblk, sblk, urow, bmat, bmat, cmat, cmat, tab, tab, tab, tab],
        out_specs=[ublk, bmat, bmat, cmat, cmat, qrow, qrow, urow],
        out_shape=[_sds((t, SSM_WIDTH), BF16),
                   _sds((N_SLAB, LANES, SLAB_STATE), F32), _sds((N_SLAB, LANES, SLAB_STATE), F32),
                   _sds((N_SLAB, SLAB_STATE, LANES), F32), _sds((N_SLAB, SLAB_STATE, LANES), F32),
                   _sds((1, nst), F32), _sds((1, nst), F32), _sds((1, SSM_WIDTH), F32)],
        scratch_shapes=[pltpu.VMEM((1, sps * SLAB_STATE), F32), pltpu.VMEM((1, sps * SLAB_STATE), F32),
                        pltpu.VMEM((SUBLANES, sps * SLAB_STATE), F32), pltpu.VMEM((SUBLANES, sps * SLAB_STATE), F32),
                        pltpu.VMEM((SUBLANES, sps * LANES), F32)],
        compiler_params=_params(("parallel", "arbitrary")),
    )(gy, z, p_re, p_im, dvec, bre, bim, cre, cimn, apr, api, air, aii)


def _sgu_mask():
    r = lax.broadcasted_iota(jnp.int32, (SGU_CHUNK, SGU_CHUNK), 0)
    c = lax.broadcasted_iota(jnp.int32, (SGU_CHUNK, SGU_CHUNK), 1)
    return r >= c


def _sgu_common(zu, zv, lng, lnb):
    us, v = _gelu(zu), _gelu(zv)
    mu = jnp.mean(v, axis=-1, keepdims=True)
    vc = v - mu
    rstd = lax.rsqrt(jnp.mean(vc * vc, axis=-1, keepdims=True) + EPS)
    xhat = vc * rstd
    return us, xhat, rstd, xhat * lng + lnb


def _sgu_fwd(z, lng, lnb, w, bexp):
    t = z.shape[0]
    hd = SGU_CHUNK
    rows = min(SGU_FWD_CHUNKS * SGU_CHUNK, t)

    def body(zu_ref, zv_ref, lng_ref, lnb_ref, w_ref, b_ref, y_ref):
        mask = _sgu_mask()
        wts = [jnp.where(mask, w_ref[h], 0.0).astype(BF16) for h in range(SGU_HEADS)]
        for c in range(rows // SGU_CHUNK):
            rs = slice(c * SGU_CHUNK, (c + 1) * SGU_CHUNK)
            us, _, _, vn = _sgu_common(zu_ref[rs, :], zv_ref[rs, :], lng_ref[...], lnb_ref[...])
            vnb = vn.astype(BF16)
            for h in range(SGU_HEADS):
                sl = slice(h * hd, (h + 1) * hd)
                y_ref[rs, sl] = us[:, sl] * (_dot(wts[h], vnb[:, sl]) + b_ref[h])

    row = lambda c: _bs((rows, SSM_WIDTH), lambda i: (i, c))
    vec = _bs((1, SSM_WIDTH), lambda i: (0, 0))
    hmat = _bs((SGU_HEADS, hd, hd), lambda i: (0, 0, 0))
    return pl.pallas_call(
        body, name="sgu_fwd", grid=(t // rows,),
        in_specs=[row(1), row(2), vec, vec, hmat, hmat],
        out_specs=row(0), out_shape=_sds((t, SSM_WIDTH), F32),
        compiler_params=_params(("parallel",)),
    )(z, z, lng, lnb, w, bexp)


def _sgu_bwd(dy, du_ssm, z, lng, lnb, w, bexp, deps=()):
    t = z.shape[0]
    hd = SGU_CHUNK
    nc = t // SGU_CHUNK

    def body(*refs):
        dy_ref, dus_ref, zu_ref, zv_ref, lng_ref, lnb_ref, w_ref, b_ref = refs[:8]
        dz_ref, dlng_o, dlnb_o, dw_o, db_o = refs[8 + len(deps):]
        i = pl.program_id(0)

        @pl.when(i == 0)
        def _():
            for ref in (dlng_o, dlnb_o, dw_o, db_o):
                ref[...] = jnp.zeros_like(ref)

        zu, zv = zu_ref[...], zv_ref[...]
        lng = lng_ref[...]
        us, xhat, rstd, vn = _sgu_common(zu, zv, lng, lnb_ref[...])
        vnb = vn.astype(BF16)
        dyv = dy_ref[...]
        mask = _sgu_mask()
        dus_parts, dvn_parts = [], []
        for h in range(SGU_HEADS):
            sl = slice(h * hd, (h + 1) * hd)
            wt = jnp.where(mask, w_ref[h], 0.0).astype(BF16)
            mixed = _dot(wt, vnb[:, sl]) + b_ref[h]
            dus_parts.append(dyv[:, sl] * mixed)
            dmix = dyv[:, sl] * us[:, sl]
            dmb = dmix.astype(BF16)
            db_o[h] += dmix
            dw_o[h] += _dot(dmb, vnb[:, sl], "nt")
            dvn_parts.append(_dot(wt, dmb, "tn"))
        dus = jnp.concatenate(dus_parts, axis=1)
        dvn = jnp.concatenate(dvn_parts, axis=1)
        dlng_o[...] += jnp.sum(dvn * xhat, axis=0, keepdims=True)
        dlnb_o[...] += jnp.sum(dvn, axis=0, keepdims=True)
        dxh = dvn * lng
        dv = rstd * (dxh - jnp.mean(dxh, axis=-1, keepdims=True)
                     - xhat * jnp.mean(dxh * xhat, axis=-1, keepdims=True))
        dz_ref[:, 0:SSM_WIDTH] = dus_ref[...]
        dz_ref[:, SSM_WIDTH:2 * SSM_WIDTH] = (dus * _gelu_grad(zu)).astype(BF16)
        dz_ref[:, 2 * SSM_WIDTH:] = (dv * _gelu_grad(zv)).astype(BF16)

        @pl.when(i == nc - 1)
        def _():
            for h in range(SGU_HEADS):
                dw_o[h] = jnp.where(mask, dw_o[h], 0.0)
                db_o[h] = jnp.broadcast_to(jnp.sum(db_o[h], axis=1, keepdims=True), (hd, hd))

    row = lambda c: _bs((SGU_CHUNK, SSM_WIDTH), lambda i: (i, c))
    vec = _bs((1, SSM_WIDTH), lambda i: (0, 0))
    hmat = _bs((SGU_HEADS, hd, hd), lambda i: (0, 0, 0))
    return pl.pallas_call(
        body, name="sgu_bwd", grid=(nc,),
        in_specs=[row(0), row(0), row(1), row(2), vec, vec, hmat, hmat] + [_any_spec()] * len(deps),
        out_specs=[_bs((SGU_CHUNK, 3 * SSM_WIDTH), lambda i: (i, 0)), vec, vec, hmat, hmat],
        out_shape=[_sds((t, 3 * SSM_WIDTH), BF16), _sds((1, SSM_WIDTH), F32), _sds((1, SSM_WIDTH), F32),
                   _sds((SGU_HEADS, hd, hd), F32), _sds((SGU_HEADS, hd, hd), F32)],
        compiler_params=_params(("arbitrary",)),
    )(dy, du_ssm, z, z, lng, lnb, w, bexp, *deps)


def _place():
    x, y, c = (lax.axis_index(a) for a in MESH_AXES)
    return x, y, c


def _index(p):
    return 4 * p[0] + 2 * p[1] + p[2]


def _any_spec():
    return pl.BlockSpec(memory_space=pl.ANY)


def _col_block(ref, k, width):
    return ref.at[:, pl.ds(pl.multiple_of(k * width, LANES), width)]


def _all_gather(name, shards, by_columns=False):
    n = len(shards)

    def body(*refs):
        ins, outs = refs[:n], refs[n:2 * n]
        send, recv, loc = refs[2 * n:]
        x, y, c = _place()
        me, sib = (x, y, c), (x, y, 1 - c)
        chips = [(1 - x, y), (x, 1 - y), (1 - x, 1 - y)]

        def blk(w, p):
            if by_columns:
                return _col_block(outs[w], _index(p), shards[w].shape[1])
            return outs[w].at[_index(p)]

        def cp(w, k, block, to, src=None):
            dst = blk(w, block)
            return pltpu.make_async_remote_copy(
                src_ref=dst if src is None else src, dst_ref=dst,
                send_sem=send.at[w * 7 + k], recv_sem=recv.at[w * 7 + k],
                device_id=to, device_id_type=pl.DeviceIdType.MESH)

        mines, sends = [], []
        for w in range(n):
            m = pltpu.make_async_copy(ins[w], blk(w, me), loc.at[w])
            m.start()
            mines.append(m)
            first = [cp(w, 0, me, sib, src=ins[w])]
            first += [cp(w, 1 + j, me, (*chip, c), src=ins[w]) for j, chip in enumerate(chips)]
            for q in first:
                q.start()
            sends += first
        for j, chip in enumerate(chips):
            for w in range(n):
                cp(w, 1 + j, (*chip, c), me).wait_recv()
                q = cp(w, 4 + j, (*chip, c), sib)
                q.start()
                sends.append(q)
        for w in range(n):
            cp(w, 0, sib, me).wait_recv()
            for j, chip in enumerate(chips):
                cp(w, 4 + j, (*chip, 1 - c), me).wait_recv()
        for q in sends:
            q.wait_send()
        for m in mines:
            m.wait()

    return pl.pallas_call(
        body, name=name,
        in_specs=[_any_spec()] * n, out_specs=[_any_spec()] * n,
        out_shape=[_sds((s.shape[0], N_DEV * s.shape[1]) if by_columns else (N_DEV,) + s.shape, s.dtype)
                   for s in shards],
        scratch_shapes=[pltpu.SemaphoreType.DMA((n * 7,)), pltpu.SemaphoreType.DMA((n * 7,)),
                        pltpu.SemaphoreType.DMA((n,))],
        compiler_params=pltpu.CompilerParams(has_side_effects=True),
    )(*shards)


def _peer(r, x, y, c):
    return ((1 - x) if r & 4 else x, (1 - y) if r & 2 else y, (1 - c) if r & 1 else c)


def _sent_block(src_ref, land_ref, k, scatter):
    if not scatter:
        return src_ref
    if len(src_ref.shape) == len(land_ref.shape):
        return src_ref.at[k]
    return _col_block(src_ref, k, land_ref.shape[2])


def _send_start(name, src, land, scatter, after=None):
    n_after = 0 if after is None else 1

    def body(*refs):
        src_ref, land_ref = refs[0], refs[1]
        send, recv, _, _, token = refs[2 + n_after:]
        x, y, c = _place()
        me = _index((x, y, c))
        for r in range(1, N_DEV):
            p = _peer(r, x, y, c)
            pltpu.make_async_remote_copy(
                src_ref=_sent_block(src_ref, land_ref, _index(p), scatter), dst_ref=land_ref.at[me],
                send_sem=send.at[r - 1], recv_sem=recv.at[r - 1],
                device_id=p, device_id_type=pl.DeviceIdType.MESH).start()
        token[...] = jnp.zeros_like(token)

    hbm, sem = pl.BlockSpec(memory_space=pltpu.HBM), pl.BlockSpec(memory_space=pltpu.SEMAPHORE)
    return pl.pallas_call(
        body, name=name,
        out_shape=(pltpu.SemaphoreType.DMA((N_DEV - 1,)), pltpu.SemaphoreType.DMA((N_DEV - 1,)),
                   pltpu.HBM(src.shape, src.dtype), pltpu.HBM(land.shape, land.dtype),
                   _sds((SUBLANES, LANES), F32)),
        in_specs=(hbm, hbm) + (_any_spec(),) * n_after,
        out_specs=(sem, sem, hbm, hbm, pl.BlockSpec(memory_space=pltpu.VMEM)),
        input_output_aliases={0: 2, 1: 3},
        compiler_params=pltpu.CompilerParams(has_side_effects=pltpu.SideEffectType.DATAFLOW_SIDE_EFFECTING),
    )(pltpu.with_memory_space_constraint(src, pltpu.HBM), pltpu.with_memory_space_constraint(land, pltpu.HBM),
      *([] if after is None else [after]))


def _send_wait(name, started, after, scatter):
    send, recv, src_thru, land_thru, _ = started

    def body(src_ref, land_ref, send_r, recv_r, after_ref, src_out, land_out):
        x, y, c = _place()
        for r in range(1, N_DEV):
            p = _peer(r, x, y, c)
            k = _index(p)
            cp = pltpu.make_async_remote_copy(
                src_ref=_sent_block(src_ref, land_ref, k, scatter), dst_ref=land_ref.at[k],
                send_sem=send_r.at[r - 1], recv_sem=recv_r.at[r - 1],
                device_id=p, device_id_type=pl.DeviceIdType.MESH)
            cp.wait_send()
            cp.wait_recv()

    hbm, sem = pl.BlockSpec(memory_space=pltpu.HBM), pl.BlockSpec(memory_space=pltpu.SEMAPHORE)
    return pl.pallas_call(
        body, name=name,
        out_shape=(pltpu.HBM(src_thru.shape, src_thru.dtype), pltpu.HBM(land_thru.shape, land_thru.dtype)),
        in_specs=(hbm, hbm, sem, sem, _any_spec()), out_specs=(hbm, hbm),
        input_output_aliases={0: 0, 1: 1},
        compiler_params=pltpu.CompilerParams(has_side_effects=pltpu.SideEffectType.DATAFLOW_SIDE_EFFECTING),
    )(src_thru, land_thru, send, recv, after)


SPREAD, PASS_ON = (1, 2, 4, 6), (2, 4, 6)


def _cols_start(name, land, width, pass_on, after):
    peers = PASS_ON if pass_on else SPREAD

    def body(land_ref, after_ref, send, recv, land_thru, token):
        x, y, c = _place()
        for n, r in enumerate(peers):
            block = _col_block(land_ref, _index(_peer(r, x, y, c) if pass_on else (x, y, c)), width)
            pltpu.make_async_remote_copy(
                src_ref=block if pass_on else after_ref, dst_ref=block,
                send_sem=send.at[n], recv_sem=recv.at[n],
                device_id=_peer(1, x, y, c) if pass_on else _peer(r, x, y, c),
                device_id_type=pl.DeviceIdType.MESH).start()
        token[...] = jnp.zeros_like(token)

    hbm, sem = pl.BlockSpec(memory_space=pltpu.HBM), pl.BlockSpec(memory_space=pltpu.SEMAPHORE)
    return pl.pallas_call(
        body, name=name,
        out_shape=(pltpu.SemaphoreType.DMA((len(peers),)), pltpu.SemaphoreType.DMA((len(peers),)),
                   pltpu.HBM(land.shape, land.dtype), _sds((SUBLANES, LANES), F32)),
        in_specs=(hbm, _any_spec()), out_specs=(sem, sem, hbm, pl.BlockSpec(memory_space=pltpu.VMEM)),
        input_output_aliases={0: 2},
        compiler_params=pltpu.CompilerParams(has_side_effects=pltpu.SideEffectType.DATAFLOW_SIDE_EFFECTING),
    )(pltpu.with_memory_space_constraint(land, pltpu.HBM), after)


def _cols_wait(name, started, width, pass_on, after):
    send, recv, land_thru, _ = started
    peers = PASS_ON if pass_on else SPREAD

    def body(land_ref, send_r, recv_r, after_ref, land_out):
        x, y, c = _place()
        for n, r in enumerate(peers):
            sent = _col_block(land_ref, _index(_peer(r, x, y, c) if pass_on else (x, y, c)), width)
            came = _col_block(land_ref, _index(_peer(r ^ 1 if pass_on else r, x, y, c)), width)
            cp = pltpu.make_async_remote_copy(
                src_ref=sent, dst_ref=came, send_sem=send_r.at[n], recv_sem=recv_r.at[n],
                device_id=_peer(1, x, y, c) if pass_on else _peer(r, x, y, c),
                device_id_type=pl.DeviceIdType.MESH)
            cp.wait_send()
            cp.wait_recv()

    hbm, sem = pl.BlockSpec(memory_space=pltpu.HBM), pl.BlockSpec(memory_space=pltpu.SEMAPHORE)
    return pl.pallas_call(
        body, name=name, out_shape=(pltpu.HBM(land_thru.shape, land_thru.dtype),),
        in_specs=(hbm, sem, sem, _any_spec()), out_specs=(hbm,), input_output_aliases={0: 0},
        compiler_params=pltpu.CompilerParams(has_side_effects=pltpu.SideEffectType.DATAFLOW_SIDE_EFFECTING),
    )(land_thru, send, recv, after)[0]


def _own_block(blocks, block, me):
    return lax.dynamic_update_index_in_dim(blocks, block, me, 0)


def _adam_math(parts, w, m, v):
    c1 = 1.0 / (1.0 - ADAM_B1 ** ADAM_STEP)
    c2 = 1.0 / (1.0 - ADAM_B2 ** ADAM_STEP)
    g = parts[0].astype(F32)
    for k in range(1, N_DEV):
        g = g + parts[k].astype(F32)
    mn = ADAM_B1 * m + (1.0 - ADAM_B1) * g
    vn = ADAM_B2 * v + (1.0 - ADAM_B2) * (g * g)
    return g, -ADAM_LR * ((mn * c1) / (jnp.sqrt(vn * c2) + ADAM_EPS) + ADAM_WD * w), mn, vn


def _adamw(name, parts, w, m, v):
    rows, cols = w.shape
    br = min(ADAM_ROWS, rows)

    def body(p_ref, w_ref, m_ref, v_ref, g_o, d_o, m_o, v_o):
        g_o[...], d_o[...], m_o[...], v_o[...] = _adam_math(p_ref[...], w_ref[...], m_ref[...], v_ref[...])

    blk = _bs((br, cols), lambda i: (i, 0))
    out = _sds((rows, cols), F32)
    return pl.pallas_call(
        body, name=name, grid=(rows // br,),
        in_specs=[_bs((N_DEV, br, cols), lambda i: (0, i, 0)), blk, blk, blk],
        out_specs=[blk] * 4, out_shape=[out] * 4,
        compiler_params=_params(("parallel",)),
    )(parts, w, m, v)


def _pack(arrs):
    tile = SUBLANES * LANES
    flat = []
    for a in arrs:
        f = a.reshape(-1).astype(F32)
        pad = (-f.shape[0]) % tile
        flat.append(jnp.pad(f, (0, pad)) if pad else f)
    total = sum(f.shape[0] for f in flat)
    tail = (-total) % (ADAM_ROWS * LANES)
    if tail:
        flat.append(jnp.zeros((tail,), F32))
    return jnp.concatenate(flat).reshape(-1, LANES)


def _unpack(buf, like):
    tile = SUBLANES * LANES
    flat = buf.reshape(-1)
    out, off = [], 0
    for a in like:
        n = math.prod(a.shape)
        out.append(flat[off:off + n].reshape(a.shape))
        off += n + ((-n) % tile)
    return out


def _to_block_b(b):
    gl = LANES // SSM_GROUP
    tb = b.reshape(N_SLAB, gl, SSM_STATE, SSM_GROUP).transpose(0, 1, 3, 2)
    eye = jnp.eye(gl, dtype=F32)
    return (tb[:, :, :, None, :] * eye[None, :, None, :, None]).reshape(N_SLAB, LANES, SLAB_STATE)


def _from_block_b(bx):
    gl = LANES // SSM_GROUP
    d = jnp.einsum("jghgp->jgph", bx.reshape(N_SLAB, gl, SSM_GROUP, gl, SSM_STATE))
    return d.reshape(N_SLAB * gl, SSM_STATE, SSM_GROUP)


def _to_block_c(cm):
    gl = LANES // SSM_GROUP
    tc = cm.reshape(N_SLAB, gl, SSM_GROUP, SSM_STATE).transpose(0, 1, 3, 2)
    eye = jnp.eye(gl, dtype=F32)
    return (tc[:, :, :, None, :] * eye[None, :, None, :, None]).reshape(N_SLAB, SLAB_STATE, LANES)


def _from_block_c(cx):
    gl = LANES // SSM_GROUP
    d = jnp.einsum("jgpgh->jghp", cx.reshape(N_SLAB, gl, SSM_STATE, gl, SSM_GROUP))
    return d.reshape(N_SLAB * gl, SSM_GROUP, SSM_STATE)


def kernel(x, norm_mix_g, w_in, ssm_a_re, ssm_a_im, ssm_b_re, ssm_b_im, ssm_c_re, ssm_c_im, ssm_d, ssm_log_dt, ssm_glu_w, ssm_glu_b, sgu_ln_g, sgu_ln_b, sgu_w, sgu_b, out_norm_ssm_g, out_norm_sgu_g, w_out, norm_mlp_g, w_up, w_down, norm_final_g, loss_target, m_norm_mix_g, m_w_in, m_ssm_a_re, m_ssm_a_im, m_ssm_b_re, m_ssm_b_im, m_ssm_c_re, m_ssm_c_im, m_ssm_d, m_ssm_log_dt, m_ssm_glu_w, m_ssm_glu_b, m_sgu_ln_g, m_sgu_ln_b, m_sgu_w, m_sgu_b, m_out_norm_ssm_g, m_out_norm_sgu_g, m_w_out, m_norm_mlp_g, m_w_up, m_w_down, m_norm_final_g, v_norm_mix_g, v_w_in, v_ssm_a_re, v_ssm_a_im, v_ssm_b_re, v_ssm_b_im, v_ssm_c_re, v_ssm_c_im, v_ssm_d, v_ssm_log_dt, v_ssm_glu_w, v_ssm_glu_b, v_sgu_ln_g, v_sgu_ln_b, v_sgu_w, v_sgu_b, v_out_norm_ssm_g, v_out_norm_sgu_g, v_w_out, v_norm_mlp_g, v_w_up, v_w_down, v_norm_final_g):
    given = dict(locals())
    names = ["norm_mix_g", "w_in", "ssm_a_re", "ssm_a_im", "ssm_b_re", "ssm_b_im", "ssm_c_re", "ssm_c_im",
             "ssm_d", "ssm_log_dt", "ssm_glu_w", "ssm_glu_b", "sgu_ln_g", "sgu_ln_b", "sgu_w", "sgu_b",
             "out_norm_ssm_g", "out_norm_sgu_g", "w_out", "norm_mlp_g", "w_up", "w_down", "norm_final_g"]
    big = ["w_in", "ssm_glu_w", "w_out", "w_up", "w_down"]
    small = [n for n in names if n not in big]

    d = D_MODEL
    t = x.shape[1]
    tb = min(1024, t)
    xs = x[0]
    target = loss_target[0]
    nsh_in = w_in.shape[2]
    nsh_up = w_up.shape[2]
    d_ff = nsh_up * N_DEV
    n_in = nsh_in * N_DEV

    me = _index(_place())
    shard_in = w_in[0].astype(BF16)
    spread = _cols_start("gather_w_in_spread", lax.empty((d, n_in), BF16), nsh_in, False, shard_in)
    gathers = {}

    def start_gather(n, after=None):
        shard = given[n][0].astype(BF16)
        gathers[n] = _send_start("gather_start_" + n, shard, lax.empty((N_DEV,) + shard.shape, BF16), False, after)
        return gathers[n][4]

    def gathered(n, after):
        shard, blocks = _send_wait("gather_wait_" + n, gathers[n], after, False)
        return _own_block(blocks, shard, me)

    nst = N_SLAB * SLAB_STATE
    are, aim = ssm_a_re.reshape(1, nst), ssm_a_im.reshape(1, nst)
    ldt = jnp.repeat(ssm_log_dt[0], SSM_STATE).reshape(1, nst)
    bxr, bxi = _to_block_b(ssm_b_re[0]), _to_block_b(ssm_b_im[0])
    cxr, cxi = _to_block_c(ssm_c_re[0]), _to_block_c(ssm_c_im[0])
    dvec = ssm_d.reshape(1, SSM_WIDTH)

    h1 = _rms_fwd("norm_mix", xs, norm_mix_g, deps=[spread[3]])
    bre, bim, cre, cimn, apr, api, air, aii = _ssm_prep(are, aim, ldt, bxr, bxi, cxr, cxi)
    tabs = (bre, bim, cre, cimn, apr, api, air, aii)
    land_in = _cols_wait("gather_w_in_landed", spread, nsh_in, False, h1)
    passed = _cols_start("gather_w_in_pass", land_in, nsh_in, True, bre)
    wg_in = _cols_wait("gather_w_in_passed", passed, nsh_in, True, passed[3])
    wg_in = lax.dynamic_update_slice_in_dim(wg_in, shard_in, me * nsh_in, axis=1)
    tokens = [start_gather("ssm_glu_w", wg_in), start_gather("w_out", wg_in)]
    bn_i = n_in // 2
    (z,) = _mm("in_proj", "nn", (t // tb, n_in // bn_i, 1),
               (h1, _bs((tb, d), lambda i, j, k: (i, 0))),
               (wg_in, _bs((d, bn_i), lambda i, j, k: (0, j))),
               [(_sds((t, n_in), F32), _bs((tb, bn_i), lambda i, j, k: (i, j)))], deps=tokens)
    tokens = [start_gather("w_up", z), start_gather("w_down", z)]
    y_pre, yg_b, p_re, p_im = _ssm_fwd(z, dvec, *tabs, deps=tokens)

    def glu_ep(acc, yp, b):
        gate = _sigmoid(acc + b)
        return _gelu(yp) * gate, gate

    hw = SSM_WIDTH // 2
    wg_glu = gathered("ssm_glu_w", yg_b).reshape(SSM_WIDTH, SSM_WIDTH)
    tile_g = _bs((tb, hw), lambda i, j, k: (i, j))
    y_ssm, gate = _mm("glu", "nn", (t // tb, 2, 1),
                      (yg_b, _bs((tb, SSM_WIDTH), lambda i, j, k: (i, 0))),
                      (wg_glu, _bs((SSM_WIDTH, hw), lambda i, j, k: (0, j))),
                      [(_sds((t, SSM_WIDTH), F32), tile_g), (_sds((t, SSM_WIDTH), F32), tile_g)],
                      extras=[(y_pre, tile_g), (ssm_glu_b, _bs((1, hw), lambda i, j, k: (0, j)))],
                      epilogue=glu_ep)

    sgu_bexp = jnp.broadcast_to(sgu_b[0][:, :, None], (SGU_HEADS, SGU_CHUNK, SGU_CHUNK))
    y_sgu = _sgu_fwd(z, sgu_ln_g, sgu_ln_b, sgu_w[0], sgu_bexp)
    mixed = _mix_norm(y_ssm, y_sgu, out_norm_ssm_g, out_norm_sgu_g)

    tb2 = min(512, t)
    row2 = _bs((tb2, d), lambda i, j, k: (i, 0))
    vec2 = _bs((1, d), lambda i, j, k: (0, 0))
    vec_sum = (_sds((1, d), F32), vec2)
    wg_out = gathered("w_out", mixed).reshape(d, d)

    def out_ep(acc, r, g):
        x2v = acc + r
        return x2v, _rms_math(x2v, g)

    x2, h2 = _mm("out_proj", "nn", (t // tb2, 1, 1),
                 (mixed, row2), (wg_out, _bs((d, d), lambda i, j, k: (0, 0))),
                 [(_sds((t, d), F32), row2), (_sds((t, d), BF16), row2)],
                 extras=[(xs, row2), (norm_mlp_g, vec2)], epilogue=out_ep, ep_rows=EP_ROWS)

    def up_ep(acc):
        r = jnp.maximum(acc, 0.0)
        return r * r, r

    tile_f = _bs((tb, nsh_up), lambda i, j, k: (i, j))
    wg_up = gathered("w_up", h2)
    f_act, r_act = _mm("mlp_up", "nn", (t // tb, N_DEV, 1),
                       (h2, _bs((tb, d), lambda i, j, k: (i, 0))),
                       (wg_up, _bs((None, d, nsh_up), lambda i, j, k: (j, 0, 0))),
                       [(_sds((t, d_ff), BF16), tile_f), (_sds((t, d_ff), BF16), tile_f)],
                       epilogue=up_ep)
    bk_d, bn_o = 2048, 1024
    tile_o = _bs((tb, bn_o), lambda i, j, k: (i, j))
    wg_down = gathered("w_down", f_act).reshape(d_ff, d)
    (x3,) = _mm("mlp_down", "nn", (t // tb, d // bn_o, d_ff // bk_d),
                (f_act, _bs((tb, bk_d), lambda i, j, k: (i, k))),
                (wg_down, _bs((bk_d, bn_o), lambda i, j, k: (k, j))),
                [(_sds((t, d), F32), tile_o)],
                extras=[(x2, tile_o)], epilogue=lambda acc, r: (acc + r,))
    dx3, dx3_b, g_final, err2 = _final_loss(x3, target, norm_final_g.reshape(1, d))
    loss = lax.psum(0.5 * jnp.sum(err2) / d, MESH_AXES)

    sends = {}

    def send_grad(n, g, land_shape=None):
        sends[n] = _send_start("grad_start_" + n, g, lax.empty(land_shape or g.shape, BF16), True)
        return [sends[n][4]]

    bn_a = 1024
    tile_a = _bs((tb, bn_a), lambda i, j, k: (i, j))
    (da,) = _mm("mlp_down_dx", "nt", (t // tb, d_ff // bn_a, 1),
                (dx3_b, _bs((tb, d), lambda i, j, k: (i, 0))),
                (wg_down, _bs((bn_a, d), lambda i, j, k: (j, 0))),
                [(_sds((t, d_ff), BF16), tile_a)],
                extras=[(r_act, tile_a)], epilogue=lambda acc, r: (acc * (2.0 * r.astype(F32)),))
    sq = 1024
    (gw_down,) = _mm("mlp_down_dw", "tn", (d_ff // sq, 1, t // tb),
                     (f_act, _bs((tb, sq), lambda i, j, k: (k, i))),
                     (dx3_b, _bs((tb, d), lambda i, j, k: (k, 0))),
                     [(_sds((d_ff, d), BF16), _bs((sq, d), lambda i, j, k: (i, 0)))])
    sent = send_grad("w_down", gw_down.reshape(N_DEV, -1, d))
    (dh2,) = _mm("mlp_up_dx", "nt", (t // tb, 1, N_DEV),
                 (da, _bs((tb, nsh_up), lambda i, j, k: (i, k))),
                 (wg_up, _bs((None, d, nsh_up), lambda i, j, k: (k, 0, 0))),
                 [(_sds((t, d), F32), _bs((tb, d), lambda i, j, k: (i, 0)))], deps=sent)

    def norm_bwd_side(dh, xv, dres, g):
        dx, dg = _rms_bwd_math(dh, xv, g)
        dx = dx + dres
        return dx, dx, dg

    up_dw_grid = (1, N_DEV, t // tb)
    side_rows = (t // math.prod(up_dw_grid), d)
    gw_up, dx2, dx2_b, g_norm_mlp = _mm(
        "mlp_up_dw", "tn", up_dw_grid,
        (h2, _bs((tb, d), lambda i, j, k: (k, 0))),
        (da, _bs((tb, nsh_up), lambda i, j, k: (k, j))),
        [(_sds((N_DEV, d, nsh_up), BF16), _bs((None, d, nsh_up), lambda i, j, k: (j, 0, 0)))],
        side=(norm_bwd_side, [(dh2, side_rows), (x2, side_rows), (dx3, side_rows), (norm_mlp_g, (1, d))],
              [(_sds((t, d), F32), side_rows), (_sds((t, d), BF16), side_rows)], [(_sds((1, d), F32), (1, d))]))
    sent = send_grad("w_up", gw_up)

    tk = min(2048, t)
    (gw_out,) = _mm("out_proj_dw", "tn", (d // sq, d // sq, t // tk),
                    (mixed, _bs((tk, sq), lambda i, j, k: (k, i))),
                    (dx2_b, _bs((tk, sq), lambda i, j, k: (k, j))),
                    [(_sds((d, d), BF16), _bs((sq, sq), lambda i, j, k: (i, j)))], deps=sent)
    sent = send_grad("w_out", gw_out.reshape(N_DEV, -1, d))
    half2 = _bs((tb2, SSM_WIDTH), lambda i, j, k: (i, 0))
    vech = _bs((1, SSM_WIDTH), lambda i, j, k: (0, 0))
    half_sum = (_sds((1, SSM_WIDTH), F32), vech)

    def out_dx_ep(acc, ya, yb, ga, gb, yp, gt):
        dya, dga = _rms_bwd_math(acc[:, :SSM_WIDTH], ya, ga)
        dyb, dgb = _rms_bwd_math(acc[:, SSM_WIDTH:], yb, gb)
        dpre = dya * _gelu(yp) * gt * (1.0 - gt)
        return dya, dyb, dpre, dga, dgb, jnp.sum(dpre, axis=0, keepdims=True)

    dy_ssm, dy_sgu, dpre_b, g_onorm_ssm, g_onorm_sgu, g_glu_b = _mm(
        "out_proj_dx", "nt", (t // tb2, 1, 1),
        (dx2_b, row2), (wg_out, _bs((d, d), lambda i, j, k: (0, 0))),
        [(_sds((t, SSM_WIDTH), F32), half2), (_sds((t, SSM_WIDTH), F32), half2),
         (_sds((t, SSM_WIDTH), BF16), half2)],
        extras=[(y_ssm, half2), (y_sgu, half2), (out_norm_ssm_g, vech), (out_norm_sgu_g, vech),
                (y_pre, half2), (gate, half2)],
        epilogue=out_dx_ep, sums=[half_sum, half_sum, half_sum], deps=sent, acc_shape=(tb2, d),
        ep_rows=EP_ROWS)

    (gw_glu,) = _mm("glu_dw", "tn", (1, 1, t // tb),
                    (yg_b, _bs((tb, SSM_WIDTH), lambda i, j, k: (k, 0))),
                    (dpre_b, _bs((tb, SSM_WIDTH), lambda i, j, k: (k, 0))),
                    [(_sds((SSM_WIDTH, SSM_WIDTH), BF16), _bs((SSM_WIDTH, SSM_WIDTH), lambda i, j, k: (0, 0)))])
    sent = send_grad("ssm_glu_w", gw_glu.reshape(N_DEV, -1, SSM_WIDTH))
    (dy_pre,) = _mm("glu_dx", "nt", (t // tb, 2, 1),
                    (dpre_b, _bs((tb, SSM_WIDTH), lambda i, j, k: (i, 0))),
                    (wg_glu, _bs((hw, SSM_WIDTH), lambda i, j, k: (j, 0))),
                    [(_sds((t, SSM_WIDTH), F32), tile_g)],
                    extras=[(dy_ssm, tile_g), (gate, tile_g), (y_pre, tile_g)],
                    epilogue=lambda acc, dy, gt, yp: ((dy * gt + acc) * _gelu_grad(yp),), deps=sent)
    du_b, dbre, dbim, dcre, dcimn, q_re, q_im, dd = _ssm_bwd(dy_pre, z, p_re, p_im, dvec, *tabs)
    dare, daim, dldt, dbxr, dbxi = _ssm_prep_bwd(are, aim, ldt, bxr, bxi, dbre, dbim, q_re, q_im)

    dz_b, g_ln_g, g_ln_b, g_sgu_w, g_sgu_bx = _sgu_bwd(dy_sgu, du_b, z, sgu_ln_g, sgu_ln_b, sgu_w[0], sgu_bexp)
    local_small = {
        "ssm_a_re": dare, "ssm_a_im": daim,
        "ssm_b_re": _from_block_b(dbxr), "ssm_b_im": _from_block_b(dbxi),
        "ssm_c_re": _from_block_c(dcre), "ssm_c_im": -_from_block_c(dcimn),
        "ssm_d": dd, "ssm_log_dt": dldt.reshape(-1, SSM_STATE).sum(axis=-1),
        "ssm_glu_b": g_glu_b, "sgu_ln_g": g_ln_g, "sgu_ln_b": g_ln_b, "sgu_w": g_sgu_w,
        "sgu_b": g_sgu_bx[:, :, 0], "out_norm_ssm_g": g_onorm_ssm, "out_norm_sgu_g": g_onorm_sgu,
        "norm_mlp_g": g_norm_mlp, "norm_final_g": g_final,
    }
    small_early = [n for n in small if n in local_small]
    small_late = [n for n in small if n not in local_small]
    packed = _pack([local_small[n] for n in small_early])
    small_send = _send_start("small_start", packed, lax.empty((N_DEV,) + packed.shape, F32), False)

    def landed_parts(n, after):
        sent_blocks, landed = _send_wait("grad_wait_" + n, sends[n], after, True)
        if sent_blocks.ndim == landed.ndim:
            own = lax.dynamic_index_in_dim(sent_blocks, me, 0, keepdims=False)
        else:
            own = lax.dynamic_slice_in_dim(sent_blocks, me * landed.shape[2], landed.shape[2], axis=1)
        return _own_block(landed, own, me)

    in_dw_grid = (d // sq, n_in // bn_i, t // tb)
    riders = ["w_down", "w_up"]
    side_ins, side_outs = [], []
    for n in riders:
        rows, cols = given[n].shape[1:]
        blk = (rows // math.prod(in_dw_grid), cols)
        side_ins += [(landed_parts(n, dz_b), (N_DEV,) + blk), (given[n][0], blk),
                     (given["m_" + n][0], blk), (given["v_" + n][0], blk)]
        side_outs += [(_sds((rows, cols), F32), blk)] * 4

    def adam_side(*tiles):
        return sum((_adam_math(*tiles[4 * r:4 * r + 4]) for r in range(len(riders))), ())

    gw_in, *rider_res = _mm("in_proj_dw", "tn", in_dw_grid,
                            (h1, _bs((tb, sq), lambda i, j, k: (k, i))),
                            (dz_b, _bs((tb, bn_i), lambda i, j, k: (k, j))),
                            [(_sds((d, n_in), BF16), _bs((sq, bn_i), lambda i, j, k: (i, j)))],
                            deps=[small_send[4]], side=(adam_side, side_ins, side_outs, []))
    sent = send_grad("w_in", gw_in, (N_DEV, d, nsh_in))

    def in_dx_ep(acc, xv, dres, g):
        dx, dg = _rms_bwd_math(acc, xv, g)
        return dx + dres, dg

    grad_x, g_norm_mix = _mm("in_proj_dx", "nt", (t // tb2, 1, n_in // bn_i),
                             (dz_b, _bs((tb2, bn_i), lambda i, j, k: (i, k))),
                             (wg_in, _bs((d, bn_i), lambda i, j, k: (0, k))),
                             [(_sds((t, d), F32), row2)],
                             extras=[(xs, row2), (dx2, row2), (norm_mix_g, vec2)],
                             epilogue=in_dx_ep, sums=[vec_sum], deps=sent, ep_rows=EP_ROWS)

    (late_parts,) = _all_gather("gather_late_grads", [_pack([g_norm_mix])])
    packed, early_parts = _send_wait("small_wait", small_send, grad_x, False)
    early_parts = _own_block(early_parts, packed, me)

    grads, deltas, new_m, new_v = {}, {}, {}, {}
    for r, n in enumerate(riders):
        grads[n], deltas[n], new_m[n], new_v[n] = [a.reshape(given[n].shape) for a in rider_res[4 * r:4 * r + 4]]
    for n in big:
        if n in riders:
            continue
        res = _adamw("adamw_" + n, landed_parts(n, grad_x), given[n][0], given["m_" + n][0], given["v_" + n][0])
        grads[n], deltas[n], new_m[n], new_v[n] = [r.reshape(given[n].shape) for r in res]
    for tag, group, parts in (("early", small_early, early_parts), ("late", small_late, late_parts)):
        like = [given[n] for n in group]
        res = _adamw("adamw_small_" + tag, parts, _pack(like), _pack([given["m_" + n] for n in group]),
                     _pack([given["v_" + n] for n in group]))
        for store, buf in zip((grads, deltas, new_m, new_v), res):
            for n, a in zip(group, _unpack(buf, like)):
                store[n] = a

    return (loss, grad_x.reshape(x.shape), *[grads[n] for n in names], *[deltas[n] for n in names],
            *[new_m[n] for n in names], *[new_v[n] for n in names])
```
